```python
import math
import jax, jax.numpy as jnp
from jax import lax
import numpy as np

D_MODEL = 1024
BATCH = 2
SEQ = 8192
DEPTH = 1
DEC_BATCH = 128
DEC_SEQ = 1
PAST_LEN = 16384
PAGE_SIZE = 128

N_META = 16
N_HEADS = 8
N_KV_HEADS = 2
HEAD_DIM = 64
Q_PER_KV = N_HEADS // N_KV_HEADS
ATTN_WIDTH = N_HEADS * HEAD_DIM
KV_WIDTH = N_KV_HEADS * HEAD_DIM
WINDOW = 128
BLOCK = 128
CONV_CH = 512
CONV_W = 31
REL_BUCKETS = 32
REL_MAX_DIST = 128
D_FF = 2816
EPS = 1e-6
NEG = -1e30
IN_SPLITS = (ATTN_WIDTH, KV_WIDTH, KV_WIDTH, 2 * CONV_CH, D_MODEL, D_MODEL)
IN_WIDTH = sum(IN_SPLITS)

kernel_name = "hybrid_swa_sink_conformer_conv_macaron_step"


def rms_norm(x, g):
    xf = x.astype(jnp.float32)
    y = xf * lax.rsqrt(jnp.mean(xf * xf, axis=-1, keepdims=True) + EPS)
    return (y * g.astype(jnp.float32)).astype(x.dtype)


def layer_norm(x, g, b):
    xf = x.astype(jnp.float32)
    mu = jnp.mean(xf, axis=-1, keepdims=True)
    var = jnp.mean(jnp.square(xf - mu), axis=-1, keepdims=True)
    y = (xf - mu) * lax.rsqrt(var + EPS)
    return (y * g.astype(jnp.float32) + b.astype(jnp.float32)).astype(x.dtype)


def swiglu_ffn(x, g, w1, w3, w2):
    h = rms_norm(x, g)
    return (jax.nn.silu(h @ w1) * (h @ w3)) @ w2


def t5_bucket(dist):
    max_exact = REL_BUCKETS // 2
    d = jnp.maximum(dist, 0)
    ratio = jnp.log(jnp.maximum(d, 1).astype(jnp.float32) / max_exact) / math.log(REL_MAX_DIST / max_exact)
    large = jnp.minimum(max_exact + (ratio * (REL_BUCKETS - max_exact)).astype(jnp.int32), REL_BUCKETS - 1)
    return jnp.where(d < max_exact, d, large)


def band_bias(dist, rel_bias):
    b = rel_bias.astype(jnp.float32)[t5_bucket(dist)]
    b = jnp.transpose(b, (2, 0, 1)).reshape(N_KV_HEADS, Q_PER_KV, *dist.shape)
    ok = (dist >= 0) & (dist < WINDOW)
    return jnp.where(ok, b, NEG)


def sink_attention(q, k, v, bias, sinks):
    s = jnp.einsum('...qkgd,...skd->...kgqs', q, k).astype(jnp.float32) * (HEAD_DIM ** -0.5) + bias
    sink = sinks.astype(jnp.float32).reshape(N_KV_HEADS, Q_PER_KV, 1)
    m = jnp.maximum(jnp.max(s, axis=-1), sink)
    p = jnp.exp(s - m[..., None])
    denom = jnp.sum(p, axis=-1) + jnp.exp(sink - m)
    p = (p / denom[..., None]).astype(v.dtype)
    return jnp.einsum('...kgqs,...skd->...qkgd', p, v)


def pre_mix(h, mix_norm, w_in, q_norm, k_norm):
    u = rms_norm(h, mix_norm)
    z = u @ w_in
    o = [0]
    for w in IN_SPLITS:
        o.append(o[-1] + w)
    lead = h.shape[:-1]
    q = rms_norm(z[..., o[0]:o[1]].reshape(*lead, N_HEADS, HEAD_DIM), q_norm)
    q = q.reshape(*lead, N_KV_HEADS, Q_PER_KV, HEAD_DIM)
    k = rms_norm(z[..., o[1]:o[2]].reshape(*lead, N_KV_HEADS, HEAD_DIM), k_norm)
    v = z[..., o[2]:o[3]].reshape(*lead, N_KV_HEADS, HEAD_DIM)
    a, b = jnp.split(z[..., o[3]:o[4]], 2, axis=-1)
    glu = a * jax.nn.sigmoid(b)
    gate_attn = jax.nn.sigmoid(z[..., o[4]:o[5]])
    gate_conv = jax.nn.sigmoid(z[..., o[5]:o[6]])
    return q, k, v, glu, gate_attn, gate_conv


def conv_branch(c, w_dw, b_dw, conv_ln_g, conv_ln_b, w_conv_out):
    y = lax.conv_general_dilated(c, w_dw[:, None, :].astype(c.dtype), window_strides=(1,), padding='VALID',
                                 dimension_numbers=('NWC', 'WIO', 'NWC'), feature_group_count=CONV_CH) + b_dw
    y = jax.nn.silu(layer_norm(y, conv_ln_g, conv_ln_b))
    return y @ w_conv_out


def post_mix(h, attn_o, conv_o, gate_attn, gate_conv, w_attn_out, w_out):
    a = attn_o.reshape(*attn_o.shape[:-3], ATTN_WIDTH) @ w_attn_out
    return h + (gate_attn * a + gate_conv * conv_o) @ w_out


def setup_inputs(seed: int = 0) -> dict:
    key = jax.random.key(seed)
    ks = jax.random.split(key, 32)
    nrm = lambda k, shape, s: jax.random.normal(k, shape, jnp.float32) * s
    gain = lambda k, n: 1.0 + 0.05 * jax.random.normal(k, (n,), jnp.float32)
    w_buf = min(WINDOW, PAST_LEN)
    return {
        "x_prompt": nrm(ks[0], (BATCH, SEQ, D_MODEL), 1.0),
        "x_sample": nrm(ks[1], (DEC_BATCH, DEC_SEQ, D_MODEL), 1.0),
        "cache_k": nrm(ks[2], (DEC_BATCH, w_buf, N_KV_HEADS, HEAD_DIM), 1.0),
        "cache_v": nrm(ks[3], (DEC_BATCH, w_buf, N_KV_HEADS, HEAD_DIM), 1.0),
        "state_conv": nrm(ks[4], (DEC_BATCH, CONV_W - 1, CONV_CH), 0.5),
        "meta_tokens": nrm(ks[5], (N_META, D_MODEL), 1.0),
        "ffn1_norm": gain(ks[6], D_MODEL),
        "ffn1_w1": nrm(ks[7], (D_MODEL, D_FF), D_MODEL ** -0.5),
        "ffn1_w3": nrm(ks[8], (D_MODEL, D_FF), D_MODEL ** -0.5),
        "ffn1_w2": nrm(ks[9], (D_FF, D_MODEL), D_FF ** -0.5),
        "mix_norm": gain(ks[10], D_MODEL),
        "w_in": nrm(ks[11], (D_MODEL, IN_WIDTH), D_MODEL ** -0.5),
        "q_norm": gain(ks[12], HEAD_DIM),
        "k_norm": gain(ks[13], HEAD_DIM),
        "rel_bias": nrm(ks[14], (REL_BUCKETS, N_HEADS), 0.5),
        "sinks": nrm(ks[15], (N_HEADS,), 0.5),
        "w_attn_out": nrm(ks[16], (ATTN_WIDTH, D_MODEL), ATTN_WIDTH ** -0.5),
        "w_dw": nrm(ks[17], (CONV_W, CONV_CH), CONV_W ** -0.5),
        "b_dw": nrm(ks[18], (CONV_CH,), 0.02),
        "conv_ln_g": gain(ks[19], CONV_CH),
        "conv_ln_b": nrm(ks[20], (CONV_CH,), 0.02),
        "w_conv_out": nrm(ks[21], (CONV_CH, D_MODEL), CONV_CH ** -0.5),
        "w_out": nrm(ks[22], (D_MODEL, D_MODEL), D_MODEL ** -0.5),
        "ffn2_norm": gain(ks[23], D_MODEL),
        "ffn2_w1": nrm(ks[24], (D_MODEL, D_FF), D_MODEL ** -0.5),
        "ffn2_w3": nrm(ks[25], (D_MODEL, D_FF), D_MODEL ** -0.5),
        "ffn2_w2": nrm(ks[26], (D_FF, D_MODEL), D_FF ** -0.5),
    }


def reference(x_prompt, x_sample, cache_k, cache_v, state_conv, meta_tokens,
              ffn1_norm, ffn1_w1, ffn1_w3, ffn1_w2, mix_norm, w_in, q_norm, k_norm,
              rel_bias, sinks, w_attn_out, w_dw, b_dw, conv_ln_g, conv_ln_b, w_conv_out,
              w_out, ffn2_norm, ffn2_w1, ffn2_w3, ffn2_w2):
    B = x_prompt.shape[0]
    x = jnp.concatenate([jnp.broadcast_to(meta_tokens.astype(x_prompt.dtype)[None], (B, N_META, D_MODEL)),
                         x_prompt], axis=1)
    L = x.shape[1]
    for _ in range(DEPTH):
        h = x + 0.5 * swiglu_ffn(x, ffn1_norm, ffn1_w1, ffn1_w3, ffn1_w2)
        q, k, v, glu, ga, gc = pre_mix(h, mix_norm, w_in, q_norm, k_norm)
        lead_pad = (-N_META) % BLOCK
        nb = (L + lead_pad) // BLOCK
        qb = jnp.pad(q, ((0, 0), (lead_pad, 0), (0, 0), (0, 0), (0, 0))).reshape(
            B, nb, BLOCK, N_KV_HEADS, Q_PER_KV, HEAD_DIM)
        kp = jnp.pad(k, ((0, 0), (lead_pad + BLOCK, 0), (0, 0), (0, 0))).reshape(B, nb + 1, BLOCK, N_KV_HEADS, HEAD_DIM)
        vp = jnp.pad(v, ((0, 0), (lead_pad + BLOCK, 0), (0, 0), (0, 0))).reshape(B, nb + 1, BLOCK, N_KV_HEADS, HEAD_DIM)
        kb = jnp.concatenate([kp[:, :-1], kp[:, 1:]], axis=2)
        vb = jnp.concatenate([vp[:, :-1], vp[:, 1:]], axis=2)
        dist = jnp.arange(BLOCK)[:, None] + BLOCK - jnp.arange(2 * BLOCK)[None, :]
        key_pos = (jnp.arange(nb)[:, None] * BLOCK + jnp.arange(2 * BLOCK)[None, :] - BLOCK - lead_pad)
        valid = jnp.where(key_pos >= 0, 0.0, NEG).astype(jnp.float32)
        bias = band_bias(dist, rel_bias)[None] + valid[:, None, None, None, :]
        attn_o = sink_attention(qb, kb, vb, bias, sinks).reshape(
            B, nb * BLOCK, N_KV_HEADS, Q_PER_KV, HEAD_DIM)[:, lead_pad:]
        c = jnp.pad(glu, ((0, 0), (CONV_W - 1, 0), (0, 0)))
        conv_o = conv_branch(c, w_dw, b_dw, conv_ln_g, conv_ln_b, w_conv_out)
        h2 = post_mix(h, attn_o, conv_o, ga, gc, w_attn_out, w_out)
        x = h2 + 0.5 * swiglu_ffn(h2, ffn2_norm, ffn2_w1, ffn2_w3, ffn2_w2)
        new_k_prompt = k[:, L - WINDOW:]
        new_v_prompt = v[:, L - WINDOW:]
        new_conv_prompt = glu[:, L - (CONV_W - 1):]
    y_prompt = x[:, N_META:]

    xs = x_sample
    T = xs.shape[1]
    w_buf = cache_k.shape[1]
    for _ in range(DEPTH):
        h = xs + 0.5 * swiglu_ffn(xs, ffn1_norm, ffn1_w1, ffn1_w3, ffn1_w2)
        q, k, v, glu, ga, gc = pre_mix(h, mix_norm, w_in, q_norm, k_norm)
        k_all = jnp.concatenate([cache_k.astype(k.dtype), k], axis=1)
        v_all = jnp.concatenate([cache_v.astype(v.dtype), v], axis=1)
        dist = jnp.arange(T)[:, None] + w_buf - jnp.arange(w_buf + T)[None, :]
        attn_o = sink_attention(q, k_all, v_all, band_bias(dist, rel_bias), sinks)
        c = jnp.concatenate([state_conv.astype(glu.dtype), glu], axis=1)
        conv_o = conv_branch(c, w_dw, b_dw, conv_ln_g, conv_ln_b, w_conv_out)
        h2 = post_mix(h, attn_o, conv_o, ga, gc, w_attn_out, w_out)
        xs = h2 + 0.5 * swiglu_ffn(h2, ffn2_norm, ffn2_w1, ffn2_w3, ffn2_w2)
        new_k_sample = k_all[:, T:]
        new_v_sample = v_all[:, T:]
        new_conv_sample = c[:, T:]
    y_sample = xs

    return (y_prompt, y_sample, new_k_prompt, new_v_prompt, new_conv_prompt,
            new_k_sample, new_v_sample, new_conv_sample)
```

```python
import functools
import math

import jax
import jax.numpy as jnp
import numpy as np
from jax import lax
from jax.experimental import pallas as pl
from jax.experimental.pallas import tpu as pltpu

F32 = jnp.float32
BF16 = jnp.bfloat16

EPS = 1e-6
NEG = -1e30
WINDOW = 128
REL_MAX_DIST = 128
HEAD_DIM = 64
LANES = 128
ROW_TILE = 256
FF_CHUNK = 256
N_WSTEPS = 16
CONV_ROWS = 64
HALO = 32
SAMPLE_BLOCK = 8
VMEM_LIMIT = 56 * 1024 * 1024


def _dot(a, b):
    return jnp.dot(a, b, preferred_element_type=F32)


def _dot_t(a, b):
    return lax.dot_general(a, b, (((1,), (1,)), ((), ())), preferred_element_type=F32)


def _rms(x, g):
    return x * lax.rsqrt(jnp.mean(x * x, axis=-1, keepdims=True) + EPS) * g


def _pair_rms(x, g2, lo_mask):
    sq = x * x
    lo = jnp.sum(jnp.where(lo_mask, sq, 0.0), axis=-1, keepdims=True) * (1.0 / HEAD_DIM)
    hi = jnp.sum(jnp.where(lo_mask, 0.0, sq), axis=-1, keepdims=True) * (1.0 / HEAD_DIM)
    r = jnp.where(lo_mask, lax.rsqrt(lo + EPS), lax.rsqrt(hi + EPS))
    return x * r * g2


def _swiglu(xn, w1b, w3b, w2b, hid_ref):
    n_ff = w1b.shape[1]
    for c in range(n_ff // FF_CHUNK):
        sl = slice(c * FF_CHUNK, (c + 1) * FF_CHUNK)
        a = _dot(xn, w1b[:, sl])
        b = _dot(xn, w3b[:, sl])
        hid_ref[:, sl] = (a * jax.nn.sigmoid(a) * b).astype(BF16)
    return _dot(hid_ref[...], w2b[...])


def _store_chunk(dst, src, i):
    rows = src.shape[0]
    r = pl.multiple_of(i * rows, 16)
    dst[pl.ds(r, rows), :] = src[...].astype(BF16)


def _ffn_inproj_body(xm_ref, xs_ref, g1_ref, gm_ref, gq_ref, gk_ref, w1c, w3c, w2c, winc,
                     h_ref, q_ref, kv_ref, glu_ref, ga_ref, gc_ref,
                     w1b, w3b, w2b, winb, hid_ref, *, n_main, splits):
    i = pl.program_id(0)

    @pl.when(i < N_WSTEPS)
    def _():
        _store_chunk(w1b, w1c, i)
        _store_chunk(w3b, w3c, i)
        _store_chunk(w2b, w2c, i)
        _store_chunk(winb, winc, i)

    @pl.when(i >= N_WSTEPS)
    def _():
        t = i - N_WSTEPS
        x = jnp.where(t < n_main, xm_ref[...], xs_ref[...])
        xn = _rms(x, g1_ref[...]).astype(BF16)
        h = x + 0.5 * _swiglu(xn, w1b, w3b, w2b, hid_ref)
        h_ref[...] = h
        u = _rms(h, gm_ref[...]).astype(BF16)

        o_q, o_k, o_v, o_a, o_b, o_ga, o_gc, o_end = splits
        lo_mask = lax.broadcasted_iota(jnp.int32, (1, LANES), 1) < HEAD_DIM
        zq = _dot(u, winb[:, o_q:o_k])
        for p in range((o_k - o_q) // LANES):
            sl = slice(p * LANES, (p + 1) * LANES)
            q_ref[:, sl] = (_pair_rms(zq[:, sl], gq_ref[...], lo_mask) * (HEAD_DIM ** -0.5)).astype(BF16)
        zkv = _dot(u, winb[:, o_k:o_a])
        kv_ref[:, :LANES] = _pair_rms(zkv[:, :LANES], gk_ref[...], lo_mask)
        kv_ref[:, LANES:] = zkv[:, LANES:]
        za = _dot(u, winb[:, o_a:o_b])
        zb = _dot(u, winb[:, o_b:o_ga])
        glu_ref[...] = za * jax.nn.sigmoid(zb)
        ga_ref[...] = jax.nn.sigmoid(_dot(u, winb[:, o_ga:o_gc]))
        gc_ref[...] = jax.nn.sigmoid(_dot(u, winb[:, o_gc:o_end]))


def _ffn_inproj(x_main, x_small, g1, gm, gq2, gk2, w1, w3, w2, w_in, splits):
    m_rows, d = x_main.shape
    n_ff = w1.shape[1]
    n_main = m_rows // ROW_TILE
    r_rows = m_rows + ROW_TILE
    t = ROW_TILE
    o_q, o_k, o_v, o_a, o_b, o_ga, o_gc, o_end = splits
    assert o_v - o_k == LANES and o_a - o_v == LANES and (o_k - o_q) % LANES == 0

    def main_idx(i):
        return (jnp.clip(i - N_WSTEPS, 0, n_main - 1), 0)

    def row_idx(i):
        return (jnp.maximum(i - N_WSTEPS, 0), 0)

    def w_idx(i):
        return (jnp.minimum(i, N_WSTEPS - 1), 0)

    const = lambda i: (0, 0)
    outs = [
        jax.ShapeDtypeStruct((r_rows, d), F32),
        jax.ShapeDtypeStruct((r_rows, o_k - o_q), BF16),
        jax.ShapeDtypeStruct((r_rows, o_a - o_k), F32),
        jax.ShapeDtypeStruct((r_rows, o_b - o_a), F32),
        jax.ShapeDtypeStruct((r_rows, o_gc - o_ga), F32),
        jax.ShapeDtypeStruct((r_rows, o_end - o_gc), F32),
    ]
    return pl.pallas_call(
        functools.partial(_ffn_inproj_body, n_main=n_main, splits=splits),
        grid=(N_WSTEPS + n_main + 1,),
        in_specs=[
            pl.BlockSpec((t, d), main_idx),
            pl.BlockSpec((t, d), const),
            pl.BlockSpec((1, d), const),
            pl.BlockSpec((1, d), const),
            pl.BlockSpec((1, LANES), const),
            pl.BlockSpec((1, LANES), const),
            pl.BlockSpec((d // N_WSTEPS, n_ff), w_idx),
            pl.BlockSpec((d // N_WSTEPS, n_ff), w_idx),
            pl.BlockSpec((n_ff // N_WSTEPS, d), w_idx),
            pl.BlockSpec((d // N_WSTEPS, o_end), w_idx),
        ],
        out_specs=[pl.BlockSpec((t, s.shape[1]), row_idx) for s in outs],
        out_shape=outs,
        scratch_shapes=[
            pltpu.VMEM((d, n_ff), BF16),
            pltpu.VMEM((d, n_ff), BF16),
            pltpu.VMEM((n_ff, d), BF16),
            pltpu.VMEM((d, o_end), BF16),
            pltpu.VMEM((t, n_ff), BF16),
        ],
        compiler_params=pltpu.CompilerParams(
            dimension_semantics=("arbitrary",), vmem_limit_bytes=VMEM_LIMIT),
        name="ffn1_inproj",
    )(x_main, x_small, g1, gm, gq2, gk2, w1, w3, w2, w_in)


def _bias_table(bidx, rb_ref, head):
    tab = jnp.full(bidx.shape, NEG, F32)
    for b in range(rb_ref.shape[0]):
        tab = jnp.where(bidx == b, rb_ref[b, head], tab)
    return tab


def _ln_swish(y, g, b):
    mu = jnp.mean(y, axis=-1, keepdims=True)
    yc = y - mu
    var = jnp.mean(yc * yc, axis=-1, keepdims=True)
    z = yc * lax.rsqrt(var + EPS) * g + b
    return z * jax.nn.sigmoid(z)


def _prompt_mix_body(rb_ref, sink_ref, q_ref, kvc_ref, kvp_ref, kvm_ref, gluc_ref, glup_ref, glum_ref,
                     bidx_ref, wdw_ref, bdw_ref, lng_ref, lnb_ref,
                     attn_ref, conv_ref, bias_sc, cb_sc, y_sc, *, tiles_per_seq, n_meta, n_heads):
    i = pl.program_id(0)
    t = q_ref.shape[0]
    n_kv = 2
    grp = n_heads // n_kv
    conv_w = wdw_ref.shape[0]

    @pl.when(i == 0)
    def _():
        for hd in range(n_heads):
            bias_sc[hd] = _bias_table(bidx_ref[...], rb_ref, hd)

    first = (i % tiles_per_seq) == 0
    lo_mask = lax.broadcasted_iota(jnp.int32, (1, LANES), 1) < HEAD_DIM

    kv_prev = jnp.where(first, kvm_ref[...], kvp_ref[...])
    kv = jnp.concatenate([kv_prev, kvc_ref[...]], axis=0)
    k, v = kv[:, :LANES], kv[:, LANES:]
    k_r = pltpu.roll(k, HEAD_DIM, axis=1)
    v_r = pltpu.roll(v, HEAD_DIM, axis=1)
    k_lo = [jnp.where(lo_mask, k, 0.0).astype(BF16), jnp.where(lo_mask, k_r, 0.0).astype(BF16)]
    k_hi = [jnp.where(lo_mask, 0.0, k_r).astype(BF16), jnp.where(lo_mask, 0.0, k).astype(BF16)]
    v_lo = [jnp.where(lo_mask, v, 0.0).astype(BF16), jnp.where(lo_mask, v_r, 0.0).astype(BF16)]
    v_hi = [jnp.where(lo_mask, 0.0, v_r).astype(BF16), jnp.where(lo_mask, 0.0, v).astype(BF16)]
    col = lax.broadcasted_iota(jnp.int32, (1, 2 * WINDOW), 1)
    lead_mask = jnp.where(first & (col < WINDOW - n_meta), NEG, 0.0)

    for qb in range(t // WINDOW):
        rows = slice(qb * WINDOW, (qb + 1) * WINDOW)
        keys = slice(qb * WINDOW, qb * WINDOW + 2 * WINDOW)
        for h in range(n_kv):
            k_st = jnp.concatenate([k_lo[h][keys], k_hi[h][keys]], axis=0)
            v_st = jnp.concatenate([v_lo[h][keys], v_hi[h][keys]], axis=0)
            c0 = h * grp * HEAD_DIM
            qq = jnp.concatenate([q_ref[rows, c0:c0 + LANES], q_ref[rows, c0 + LANES:c0 + 2 * LANES]], axis=0)
            s = _dot_t(qq, k_st)
            p_parts, inv_parts = [], []
            for g2 in range(2):
                p_row, inv_row = [], []
                for par in range(2):
                    hd = h * grp + 2 * g2 + par
                    sq = s[g2 * WINDOW:(g2 + 1) * WINDOW, par * 2 * WINDOW:(par + 1) * 2 * WINDOW] + bias_sc[hd]
                    if qb == 0:
                        sq = sq + lead_mask
                    sink = sink_ref[0, hd]
                    m = jnp.maximum(jnp.max(sq, axis=-1, keepdims=True), sink)
                    p = jnp.exp(sq - m)
                    den = jnp.sum(p, axis=-1, keepdims=True) + jnp.exp(sink - m)
                    p_row.append(p.astype(BF16))
                    inv_row.append(1.0 / den)
                p_parts.append(jnp.concatenate(p_row, axis=1))
                inv_parts.append(jnp.where(lo_mask, inv_row[0], inv_row[1]))
            pm = jnp.concatenate(p_parts, axis=0)
            o = _dot(pm, v_st)
            for g2 in range(2):
                c = c0 + g2 * LANES
                attn_ref[rows, c:c + LANES] = (o[g2 * WINDOW:(g2 + 1) * WINDOW] * inv_parts[g2]).astype(attn_ref.dtype)

    cb_sc[0:HALO, :] = jnp.where(first, glum_ref[...], glup_ref[...])
    cb_sc[HALO:HALO + t, :] = gluc_ref[...]
    base = HALO - (conv_w - 1)
    n_ch = gluc_ref.shape[1]
    for rc in range(t // CONV_ROWS):
        for lc in range(n_ch // LANES):
            ls = slice(lc * LANES, (lc + 1) * LANES)
            acc = jnp.broadcast_to(bdw_ref[:, ls], (CONV_ROWS, LANES))
            for w in range(conv_w):
                r0 = base + rc * CONV_ROWS + w
                acc = acc + cb_sc[r0:r0 + CONV_ROWS, ls] * wdw_ref[w:w + 1, ls]
            y_sc[rc * CONV_ROWS:(rc + 1) * CONV_ROWS, ls] = acc
    conv_ref[...] = _ln_swish(y_sc[...], lng_ref[...], lnb_ref[...]).astype(conv_ref.dtype)


def _prompt_mix(rel_bias, sinks2, q_all, kv_all, glu_all, kv_meta, glu_meta, bidx, w_dw, b_dw, ln_g, ln_b,
                m_rows, seq, n_meta):
    t = ROW_TILE
    n_main = m_rows // t
    n_heads = sinks2.shape[1]
    qw, kvw, ch = q_all.shape[1], kv_all.shape[1], glu_all.shape[1]
    const = lambda i: (0, 0)
    row = lambda i: (i, 0)
    smem = pl.BlockSpec(memory_space=pltpu.SMEM)
    return pl.pallas_call(
        functools.partial(_prompt_mix_body, tiles_per_seq=seq // t, n_meta=n_meta, n_heads=n_heads),
        grid=(n_main,),
        in_specs=[
            smem, smem,
            pl.BlockSpec((t, qw), row),
            pl.BlockSpec((t, kvw), row),
            pl.BlockSpec((WINDOW, kvw), lambda i: (jnp.maximum(i * (t // WINDOW) - 1, 0), 0)),
            pl.BlockSpec((WINDOW, kvw), const),
            pl.BlockSpec((t, ch), row),
            pl.BlockSpec((HALO, ch), lambda i: (jnp.maximum(i * (t // HALO) - 1, 0), 0)),
            pl.BlockSpec((HALO, ch), const),
            pl.BlockSpec((WINDOW, 2 * WINDOW), const),
            pl.BlockSpec(w_dw.shape, const),
            pl.BlockSpec((1, ch), const),
            pl.BlockSpec((1, ch), const),
            pl.BlockSpec((1, ch), const),
        ],
        out_specs=[pl.BlockSpec((t, qw), row), pl.BlockSpec((t, ch), row)],
        out_shape=[jax.ShapeDtypeStruct((m_rows, qw), BF16), jax.ShapeDtypeStruct((m_rows, ch), BF16)],
        scratch_shapes=[
            pltpu.VMEM((n_heads, WINDOW, 2 * WINDOW), F32),
            pltpu.VMEM((HALO + t, ch), F32),
            pltpu.VMEM((t, ch), F32),
        ],
        compiler_params=pltpu.CompilerParams(dimension_semantics=("arbitrary",)),
        name="prompt_mix",
    )(rel_bias, sinks2, q_all, kv_all, kv_all, kv_meta, glu_all, glu_all, glu_meta, bidx,
      w_dw, b_dw, ln_g, ln_b)


def _sample_mix_body(rb_ref, sink_ref, qx_ref, kvn_ref, glun_ref, ck_ref, cv_ref, st_ref, bidx_ref,
                     wdw_ref, bdw_ref, lng_ref, lnb_ref,
                     o_ref, conv_ref, nk_ref, nv_ref, nst_ref, *, n_heads):
    w_buf = ck_ref.shape[1]
    n_st = st_ref.shape[1]
    bias = jnp.concatenate([_bias_table(bidx_ref[...], rb_ref, hd) for hd in range(n_heads)], axis=0)
    hrow = lax.broadcasted_iota(jnp.int32, (n_heads, 1), 0)
    sink = jnp.zeros((n_heads, 1), F32)
    bias_new = jnp.zeros((n_heads, 1), F32)
    for hd in range(n_heads):
        sink = jnp.where(hrow == hd, sink_ref[0, hd], sink)
        bias_new = jnp.where(hrow == hd, rb_ref[0, hd], bias_new)

    for bb in range(qx_ref.shape[0]):
        qb = qx_ref[bb]
        kn = kvn_ref[bb][:, :LANES]
        vn = kvn_ref[bb][:, LANES:]
        ck = ck_ref[bb]
        cv = cv_ref[bb]
        s_c = _dot_t(qb, ck.astype(BF16)) + bias
        s_n = jnp.sum(qb.astype(F32) * kn.astype(BF16).astype(F32), axis=-1, keepdims=True) + bias_new
        m = jnp.maximum(jnp.maximum(jnp.max(s_c, axis=-1, keepdims=True), s_n), sink)
        p_c = jnp.exp(s_c - m)
        p_n = jnp.exp(s_n - m)
        den = jnp.sum(p_c, axis=-1, keepdims=True) + p_n + jnp.exp(sink - m)
        o = _dot(p_c.astype(BF16), cv.astype(BF16))
        o = o + p_n.astype(BF16).astype(F32) * vn.astype(BF16).astype(F32)
        o_ref[bb] = o / den
        nk_ref[bb, 0:w_buf - 1, :] = ck_ref[bb, 1:w_buf, :]
        nk_ref[bb, w_buf - 1:w_buf, :] = kn
        nv_ref[bb, 0:w_buf - 1, :] = cv_ref[bb, 1:w_buf, :]
        nv_ref[bb, w_buf - 1:w_buf, :] = vn
        g_new = glun_ref[bb]
        y = (jnp.sum(st_ref[bb] * wdw_ref[0:n_st, :], axis=0, keepdims=True)
             + g_new * wdw_ref[n_st:n_st + 1, :] + bdw_ref[...])
        conv_ref[bb] = _ln_swish(y, lng_ref[...], lnb_ref[...])
        nst_ref[bb, 0:n_st - 1, :] = st_ref[bb, 1:n_st, :]
        nst_ref[bb, n_st - 1:n_st, :] = g_new


def _sample_mix(rel_bias, sinks2, qx, kv_new, glu_new, cache_k, cache_v, state_conv, bidx_s, w_dw, b_dw, ln_g, ln_b):
    db, w_buf, kw = cache_k.shape
    n_st, ch = state_conv.shape[1], state_conv.shape[2]
    n_heads = sinks2.shape[1]
    sb = SAMPLE_BLOCK
    blk = lambda i: (i, 0, 0)
    const = lambda i: (0, 0)
    smem = pl.BlockSpec(memory_space=pltpu.SMEM)
    return pl.pallas_call(
        functools.partial(_sample_mix_body, n_heads=n_heads),
        grid=(db // sb,),
        in_specs=[
            smem, smem,
            pl.BlockSpec((sb, n_heads, kw), blk),
            pl.BlockSpec((sb, 1, 2 * kw), blk),
            pl.BlockSpec((sb, 1, ch), blk),
            pl.BlockSpec((sb, w_buf, kw), blk),
            pl.BlockSpec((sb, w_buf, kw), blk),
            pl.BlockSpec((sb, n_st, ch), blk),
            pl.BlockSpec((1, w_buf), const),
            pl.BlockSpec(w_dw.shape, const),
            pl.BlockSpec((1, ch), const),
            pl.BlockSpec((1, ch), const),
            pl.BlockSpec((1, ch), const),
        ],
        out_specs=[
            pl.BlockSpec((sb, n_heads, kw), blk),
            pl.BlockSpec((sb, 1, ch), blk),
            pl.BlockSpec((sb, w_buf, kw), blk),
            pl.BlockSpec((sb, w_buf, kw), blk),
            pl.BlockSpec((sb, n_st, ch), blk),
        ],
        out_shape=[
            jax.ShapeDtypeStruct((db, n_heads, kw), F32),
            jax.ShapeDtypeStruct((db, 1, ch), F32),
            jax.ShapeDtypeStruct((db, w_buf, kw), F32),
            jax.ShapeDtypeStruct((db, w_buf, kw), F32),
            jax.ShapeDtypeStruct((db, n_st, ch), F32),
        ],
        compiler_params=pltpu.CompilerParams(dimension_semantics=("arbitrary",)),
        name="sample_mix",
    )(rel_bias, sinks2, qx, kv_new, glu_new, cache_k, cache_v, state_conv, bidx_s, w_dw, b_dw, ln_g, ln_b)


def _post_ffn_body(h_ref, am_ref, as_ref, cm_ref, cs_ref, ga_ref, gc_ref, g2_ref,
                   waoc, wcoc, woutc, w1c, w3c, w2c,
                   ym_ref, ys_ref,
                   waob, wcob, woutb, w1b, w3b, w2b, hid_ref, *, n_main):
    i = pl.program_id(0)

    @pl.when(i < N_WSTEPS)
    def _():
        _store_chunk(waob, waoc, i)
        _store_chunk(wcob, wcoc, i)
        _store_chunk(woutb, woutc, i)
        _store_chunk(w1b, w1c, i)
        _store_chunk(w3b, w3c, i)
        _store_chunk(w2b, w2c, i)

    @pl.when(i >= N_WSTEPS)
    def _():
        t = i - N_WSTEPS
        is_main = t < n_main
        at = jnp.where(is_main, am_ref[...], as_ref[...])
        cv = jnp.where(is_main, cm_ref[...], cs_ref[...])
        a = _dot(at, waob[...])
        c = _dot(cv, wcob[...])
        mix = (ga_ref[...] * a + gc_ref[...] * c).astype(BF16)
        h2 = h_ref[...] + _dot(mix, woutb[...])
        xn = _rms(h2, g2_ref[...]).astype(BF16)
        y = h2 + 0.5 * _swiglu(xn, w1b, w3b, w2b, hid_ref)

        @pl.when(is_main)
        def _():
            ym_ref[...] = y

        @pl.when(jnp.logical_not(is_main))
        def _():
            ys_ref[...] = y


def _post_ffn(h_all, attn_main, attn_small, conv_main, conv_small, ga_all, gc_all, g2,
              w_ao, w_co, w_out, w1, w3, w2):
    m_rows, aw = attn_main.shape
    ch = conv_main.shape[1]
    d = h_all.shape[1]
    n_ff = w1.shape[1]
    t = ROW_TILE
    n_main = m_rows // t

    def main_idx(i):
        return (jnp.clip(i - N_WSTEPS, 0, n_main - 1), 0)

    def row_idx(i):
        return (jnp.maximum(i - N_WSTEPS, 0), 0)

    def w_idx(i):
        return (jnp.minimum(i, N_WSTEPS - 1), 0)

    const = lambda i: (0, 0)
    return pl.pallas_call(
        functools.partial(_post_ffn_body, n_main=n_main),
        grid=(N_WSTEPS + n_main + 1,),
        in_specs=[
            pl.BlockSpec((t, d), row_idx),
            pl.BlockSpec((t, aw), main_idx),
            pl.BlockSpec((t, aw), const),
            pl.BlockSpec((t, ch), main_idx),
            pl.BlockSpec((t, ch), const),
            pl.BlockSpec((t, d), row_idx),
            pl.BlockSpec((t, d), row_idx),
            pl.BlockSpec((1, d), const),
            pl.BlockSpec((aw // N_WSTEPS, d), w_idx),
            pl.BlockSpec((ch // N_WSTEPS, d), w_idx),
            pl.BlockSpec((d // N_WSTEPS, d), w_idx),
            pl.BlockSpec((d // N_WSTEPS, n_ff), w_idx),
            pl.BlockSpec((d // N_WSTEPS, n_ff), w_idx),
            pl.BlockSpec((n_ff // N_WSTEPS, d), w_idx),
        ],
        out_specs=[pl.BlockSpec((t, d), main_idx), pl.BlockSpec((t, d), const)],
        out_shape=[jax.ShapeDtypeStruct((m_rows, d), F32), jax.ShapeDtypeStruct((t, d), F32)],
        scratch_shapes=[
            pltpu.VMEM((aw, d), BF16),
            pltpu.VMEM((ch, d), BF16),
            pltpu.VMEM((d, d), BF16),
            pltpu.VMEM((d, n_ff), BF16),
            pltpu.VMEM((d, n_ff), BF16),
            pltpu.VMEM((n_ff, d), BF16),
            pltpu.VMEM((t, n_ff), BF16),
        ],
        compiler_params=pltpu.CompilerParams(
            dimension_semantics=("arbitrary",), vmem_limit_bytes=VMEM_LIMIT),
        name="post_ffn2",
    )(h_all, attn_main, attn_small, conv_main, conv_small, ga_all, gc_all, g2,
      w_ao, w_co, w_out, w1, w3, w2)


def _t5_bucket(dist, n_buckets):
    max_exact = n_buckets // 2
    d = np.maximum(dist, 0)
    ratio = (np.log(np.maximum(d, 1).astype(np.float32) / np.float32(max_exact))
             / np.float32(math.log(REL_MAX_DIST / max_exact)))
    large = np.minimum(max_exact + (ratio * np.float32(n_buckets - max_exact)).astype(np.int32), n_buckets - 1)
    return np.where(d < max_exact, d, large).astype(np.int32)


def _bucket_or_masked(dist, n_buckets):
    ok = (dist >= 0) & (dist < WINDOW)
    return np.where(ok, _t5_bucket(dist, n_buckets), -1).astype(np.int32)


def kernel(x_prompt, x_sample, cache_k, cache_v, state_conv, meta_tokens, ffn1_norm, ffn1_w1, ffn1_w3, ffn1_w2, mix_norm, w_in, q_norm, k_norm, rel_bias, sinks, w_attn_out, w_dw, b_dw, conv_ln_g, conv_ln_b, w_conv_out, w_out, ffn2_norm, ffn2_w1, ffn2_w3, ffn2_w2):
    n_b, seq, d = x_prompt.shape
    db = x_sample.shape[0]
    n_meta = meta_tokens.shape[0]
    n_heads = sinks.shape[0]
    w_buf, n_kv, hd = cache_k.shape[1], cache_k.shape[2], cache_k.shape[3]
    ch = w_dw.shape[1]
    n_st = state_conv.shape[1]
    n_buckets = rel_bias.shape[0]
    aw, kvw = n_heads * hd, n_kv * hd
    t = ROW_TILE
    m_rows = n_b * seq
    assert hd == HEAD_DIM and kvw == LANES and n_kv == 2 and n_heads == 8 and w_buf == WINDOW
    assert x_sample.shape[1] == 1 and seq % t == 0 and n_meta + db <= t and db % SAMPLE_BLOCK == 0
    assert n_meta <= HALO and n_meta <= WINDOW and n_st == w_dw.shape[0] - 1 and n_st <= HALO
    splits = tuple(int(v) for v in np.cumsum([0, aw, kvw, kvw, ch, ch, d, d]))
    assert splits[-1] == w_in.shape[1]

    row = lambda v: v.reshape(1, -1)
    x_main = x_prompt.reshape(m_rows, d)
    x_small = jnp.concatenate(
        [meta_tokens, x_sample.reshape(db, d), jnp.zeros((t - n_meta - db, d), F32)], axis=0)
    gq2 = row(jnp.concatenate([q_norm, q_norm]))
    gk2 = row(jnp.concatenate([k_norm, k_norm]))

    h_all, q_all, kv_all, glu_all, ga_all, gc_all = _ffn_inproj(
        x_main, x_small, row(ffn1_norm), row(mix_norm), gq2, gk2, ffn1_w1, ffn1_w3, ffn1_w2, w_in, splits)

    sinks2 = row(sinks)
    b_dw2, ln_g2, ln_b2 = row(b_dw), row(conv_ln_g), row(conv_ln_b)

    kv_meta = jnp.concatenate([jnp.zeros((WINDOW - n_meta, 2 * kvw), F32), kv_all[m_rows:m_rows + n_meta]], axis=0)
    glu_meta = jnp.concatenate([jnp.zeros((HALO - n_meta, ch), F32), glu_all[m_rows:m_rows + n_meta]], axis=0)
    dist = np.arange(WINDOW)[:, None] + WINDOW - np.arange(2 * WINDOW)[None, :]
    bidx = jnp.asarray(_bucket_or_masked(dist, n_buckets))
    attn_main, conv_main = _prompt_mix(rel_bias, sinks2, q_all, kv_all, glu_all, kv_meta, glu_meta, bidx,
                                       w_dw, b_dw2, ln_g2, ln_b2, m_rows, seq, n_meta)

    s0 = m_rows + n_meta
    q_s = q_all[s0:s0 + db].reshape(db, n_heads, hd)
    lane_half = (np.arange(kvw)[None, :] // hd) == (np.arange(n_heads)[:, None] // (n_heads // n_kv))
    qx = jnp.where(jnp.asarray(lane_half)[None], jnp.concatenate([q_s, q_s], axis=-1), 0).astype(BF16)
    kv_new = kv_all[s0:s0 + db].reshape(db, 1, 2 * kvw)
    glu_new = glu_all[s0:s0 + db].reshape(db, 1, ch)
    bidx_s = jnp.asarray(_bucket_or_masked(w_buf - np.arange(w_buf)[None, :], n_buckets))
    o_s, conv_s, new_k_s, new_v_s, new_conv_s = _sample_mix(
        rel_bias, sinks2, qx, kv_new, glu_new, cache_k.reshape(db, w_buf, kvw), cache_v.reshape(db, w_buf, kvw),
        state_conv, bidx_s, w_dw, b_dw2, ln_g2, ln_b2)
    o_s = jnp.where(jnp.asarray(lane_half)[None], o_s, 0.0).reshape(db, n_heads, n_kv, hd).sum(axis=2)
    pad_a = jnp.zeros((n_meta, aw), BF16)
    attn_small = jnp.concatenate(
        [pad_a, o_s.reshape(db, aw).astype(BF16), jnp.zeros((t - n_meta - db, aw), BF16)], axis=0)
    conv_small = jnp.concatenate(
        [jnp.zeros((n_meta, ch), BF16), conv_s.reshape(db, ch).astype(BF16), jnp.zeros((t - n_meta - db, ch), BF16)],
        axis=0)

    y_main, y_small = _post_ffn(h_all, attn_main, attn_small, conv_main, conv_small, ga_all, gc_all,
                                row(ffn2_norm), w_attn_out, w_conv_out, w_out, ffn2_w1, ffn2_w3, ffn2_w2)

    kv_p = kv_all[:m_rows].reshape(n_b, seq, 2 * kvw)[:, seq - WINDOW:]
    return (
        y_main.reshape(n_b, seq, d),
        y_small[n_meta:n_meta + db].reshape(db, 1, d),
        kv_p[..., :kvw].reshape(n_b, WINDOW, n_kv, hd),
        kv_p[..., kvw:].reshape(n_b, WINDOW, n_kv, hd),
        glu_all[:m_rows].reshape(n_b, seq, ch)[:, seq - n_st:],
        new_k_s.reshape(db, w_buf, n_kv, hd),
        new_v_s.reshape(db, w_buf, n_kv, hd),
        new_conv_s,
    )
```

```python
import functools
import math

import jax
import jax.numpy as jnp
import numpy as np
from jax import lax
from jax.experimental import pallas as pl
from jax.experimental.pallas import tpu as pltpu

F32 = jnp.float32
BF16 = jnp.bfloat16

EPS = 1e-6
NEG = -1e30
WINDOW = 128
REL_MAX_DIST = 128
HEAD_DIM = 64
LANES = 128
ROW_TILE = 256
FF_CHUNK = 256
N_WSTEPS = 16
HALO = 32
SAMPLE_BLOCK = 8
VMEM_LIMIT = 56 * 1024 * 1024


def _dot(a, b):
    return jnp.dot(a, b, preferred_element_type=F32)


def _dot_t(a, b):
    return lax.dot_general(a, b, (((1,), (1,)), ((), ())), preferred_element_type=F32)


def _rms(x, g):
    return x * lax.rsqrt(jnp.mean(x * x, axis=-1, keepdims=True) + EPS) * g


def _pair_rms(x, g2, lo_mask):
    sq = x * x
    lo = jnp.sum(jnp.where(lo_mask, sq, 0.0), axis=-1, keepdims=True) * (1.0 / HEAD_DIM)
    hi = jnp.sum(jnp.where(lo_mask, 0.0, sq), axis=-1, keepdims=True) * (1.0 / HEAD_DIM)
    r = jnp.where(lo_mask, lax.rsqrt(lo + EPS), lax.rsqrt(hi + EPS))
    return x * r * g2


def _dep_zero(row):
    u = pltpu.bitcast(row, jnp.uint32)
    return pltpu.bitcast((u >> 16) >> 16, F32)


def _swiglu(xn, w1b, w3b, w2b, hid_ref, fillers=()):
    n_ff = w1b.shape[1]
    z = None
    for c in range(n_ff // FF_CHUNK):
        sl = slice(c * FF_CHUNK, (c + 1) * FF_CHUNK)
        a = _dot(xn, w1b[:, sl])
        b = _dot(xn, w3b[:, sl])
        if z is not None:
            b = b + jnp.concatenate([z] * (FF_CHUNK // LANES), axis=1)
        z = _dep_zero(fillers[c]()) if c < len(fillers) else None
        hid_ref[:, sl] = (a * jax.nn.sigmoid(a) * b).astype(BF16)
    out = _dot(hid_ref[...], w2b[...])
    if z is not None:
        out = out + jnp.concatenate([z] * (out.shape[1] // LANES), axis=1)
    return out


def _store_chunk(dst, src, i):
    rows = src.shape[0]
    r = pl.multiple_of(i * rows, 16)
    dst[pl.ds(r, rows), :] = src[...].astype(BF16)


def _bias_table(bidx, rb_ref, head):
    tab = jnp.full(bidx.shape, NEG, F32)
    for b in range(rb_ref.shape[0]):
        tab = jnp.where(bidx == b, rb_ref[b, head], tab)
    return tab


def _ln_swish(y, g, b):
    mu = jnp.mean(y, axis=-1, keepdims=True)
    yc = y - mu
    var = jnp.mean(yc * yc, axis=-1, keepdims=True)
    z = yc * lax.rsqrt(var + EPS) * g + b
    return z * jax.nn.sigmoid(z)


def _ffn_inproj_body(xm_ref, xs_ref, g1_ref, gm_ref, gq_ref, gk_ref, w1c, w3c, w2c, winc,
                     h_ref, q_ref, kv_ref, glu_ref, ga_ref, gc_ref,
                     w1b, w3b, w2b, winb, hid_ref, *, n_main, splits):
    i = pl.program_id(0)

    @pl.when(i < N_WSTEPS)
    def _():
        _store_chunk(w1b, w1c, i)
        _store_chunk(w3b, w3c, i)
        _store_chunk(w2b, w2c, i)
        _store_chunk(winb, winc, i)

    @pl.when(i >= N_WSTEPS)
    def _():
        t = i - N_WSTEPS
        x = jnp.where(t < n_main, xm_ref[...], xs_ref[...])
        xn = _rms(x, g1_ref[...]).astype(BF16)
        h = x + 0.5 * _swiglu(xn, w1b, w3b, w2b, hid_ref)
        h_ref[...] = h
        u = _rms(h, gm_ref[...]).astype(BF16)

        o_q, o_k, o_v, o_a, o_b, o_ga, o_gc, o_end = splits
        lo_mask = lax.broadcasted_iota(jnp.int32, (1, LANES), 1) < HEAD_DIM
        zq = _dot(u, winb[:, o_q:o_k])
        for p in range((o_k - o_q) // LANES):
            sl = slice(p * LANES, (p + 1) * LANES)
            q_ref[:, sl] = (_pair_rms(zq[:, sl], gq_ref[...], lo_mask) * (HEAD_DIM ** -0.5)).astype(BF16)
        zkv = _dot(u, winb[:, o_k:o_a])
        kv_ref[:, :LANES] = _pair_rms(zkv[:, :LANES], gk_ref[...], lo_mask)
        kv_ref[:, LANES:] = zkv[:, LANES:]
        za = _dot(u, winb[:, o_a:o_b])
        zb = _dot(u, winb[:, o_b:o_ga])
        glu_ref[...] = za * jax.nn.sigmoid(zb)
        ga_ref[...] = jax.nn.sigmoid(_dot(u, winb[:, o_ga:o_gc]))
        gc_ref[...] = jax.nn.sigmoid(_dot(u, winb[:, o_gc:o_end]))


def _ffn_inproj(x_main, x_small, g1, gm, gq2, gk2, w1, w3, w2, w_in, splits):
    m_rows, d = x_main.shape
    n_ff = w1.shape[1]
    n_main = m_rows // ROW_TILE
    r_rows = m_rows + ROW_TILE
    t = ROW_TILE
    o_q, o_k, o_v, o_a, o_b, o_ga, o_gc, o_end = splits
    assert o_v - o_k == LANES and o_a - o_v == LANES and (o_k - o_q) % LANES == 0

    def main_idx(i):
        return (jnp.clip(i - N_WSTEPS, 0, n_main - 1), 0)

    def row_idx(i):
        return (jnp.maximum(i - N_WSTEPS, 0), 0)

    def w_idx(i):
        return (jnp.minimum(i, N_WSTEPS - 1), 0)

    const = lambda i: (0, 0)
    outs = [
        jax.ShapeDtypeStruct((r_rows, d), F32),
        jax.ShapeDtypeStruct((r_rows, o_k - o_q), BF16),
        jax.ShapeDtypeStruct((r_rows, o_a - o_k), F32),
        jax.ShapeDtypeStruct((r_rows, o_b - o_a), F32),
        jax.ShapeDtypeStruct((r_rows, o_gc - o_ga), F32),
        jax.ShapeDtypeStruct((r_rows, o_end - o_gc), F32),
    ]
    return pl.pallas_call(
        functools.partial(_ffn_inproj_body, n_main=n_main, splits=splits),
        grid=(N_WSTEPS + n_main + 1,),
        in_specs=[
            pl.BlockSpec((t, d), main_idx),
            pl.BlockSpec((t, d), const),
            pl.BlockSpec((1, d), const),
            pl.BlockSpec((1, d), const),
            pl.BlockSpec((1, LANES), const),
            pl.BlockSpec((1, LANES), const),
            pl.BlockSpec((d // N_WSTEPS, n_ff), w_idx),
            pl.BlockSpec((d // N_WSTEPS, n_ff), w_idx),
            pl.BlockSpec((n_ff // N_WSTEPS, d), w_idx),
            pl.BlockSpec((d // N_WSTEPS, o_end), w_idx),
        ],
        out_specs=[pl.BlockSpec((t, s.shape[1]), row_idx) for s in outs],
        out_shape=outs,
        scratch_shapes=[
            pltpu.VMEM((d, n_ff), BF16),
            pltpu.VMEM((d, n_ff), BF16),
            pltpu.VMEM((n_ff, d), BF16),
            pltpu.VMEM((d, o_end), BF16),
            pltpu.VMEM((t, n_ff), BF16),
        ],
        compiler_params=pltpu.CompilerParams(
            dimension_semantics=("arbitrary",), vmem_limit_bytes=VMEM_LIMIT),
        name="ffn1_inproj",
    )(x_main, x_small, g1, gm, gq2, gk2, w1, w3, w2, w_in)


def _prompt_attn_body(rb_ref, sink_ref, q_ref, kvc_ref, kvp_ref, kvm_ref, bidx_ref,
                      attn_ref, bias_sc, *, tiles_per_seq, n_meta, n_heads):
    i = pl.program_id(0)
    t = q_ref.shape[0]
    n_kv = 2
    grp = n_heads // n_kv

    @pl.when(i == 0)
    def _():
        for hd in range(n_heads):
            bias_sc[hd] = _bias_table(bidx_ref[...], rb_ref, hd)

    first = (i % tiles_per_seq) == 0
    lo_mask = lax.broadcasted_iota(jnp.int32, (1, LANES), 1) < HEAD_DIM
    kv_prev = jnp.where(first, kvm_ref[...], kvp_ref[...])
    kv = jnp.concatenate([kv_prev, kvc_ref[...]], axis=0)
    k, v = kv[:, :LANES], kv[:, LANES:]
    k_r = pltpu.roll(k, HEAD_DIM, axis=1)
    v_r = pltpu.roll(v, HEAD_DIM, axis=1)
    k_lo = [jnp.where(lo_mask, k, 0.0).astype(BF16), jnp.where(lo_mask, k_r, 0.0).astype(BF16)]
    k_hi = [jnp.where(lo_mask, 0.0, k_r).astype(BF16), jnp.where(lo_mask, 0.0, k).astype(BF16)]
    v_lo = [jnp.where(lo_mask, v, 0.0).astype(BF16), jnp.where(lo_mask, v_r, 0.0).astype(BF16)]
    v_hi = [jnp.where(lo_mask, 0.0, v_r).astype(BF16), jnp.where(lo_mask, 0.0, v).astype(BF16)]
    col = lax.broadcasted_iota(jnp.int32, (1, 2 * WINDOW), 1)
    lead_mask = jnp.where(first & (col < WINDOW - n_meta), NEG, 0.0)

    for qb in range(t // WINDOW):
        rows = slice(qb * WINDOW, (qb + 1) * WINDOW)
        keys = slice(qb * WINDOW, qb * WINDOW + 2 * WINDOW)
        for h in range(n_kv):
            k_st = jnp.concatenate([k_lo[h][keys], k_hi[h][keys]], axis=0)
            v_st = jnp.concatenate([v_lo[h][keys], v_hi[h][keys]], axis=0)
            c0 = h * grp * HEAD_DIM
            qq = jnp.concatenate([q_ref[rows, c0:c0 + LANES], q_ref[rows, c0 + LANES:c0 + 2 * LANES]], axis=0)
            s = _dot_t(qq, k_st)
            p_parts, inv_parts = [], []
            for g2 in range(2):
                p_row, inv_row = [], []
                for par in range(2):
                    hd = h * grp + 2 * g2 + par
                    sq = s[g2 * WINDOW:(g2 + 1) * WINDOW, par * 2 * WINDOW:(par + 1) * 2 * WINDOW] + bias_sc[hd]
                    if qb == 0:
                        sq = sq + lead_mask
                    sink = sink_ref[0, hd]
                    m = jnp.maximum(jnp.max(sq, axis=-1, keepdims=True), sink)
                    p = jnp.exp(sq - m)
                    den = jnp.sum(p, axis=-1, keepdims=True) + jnp.exp(sink - m)
                    p_row.append(p.astype(BF16))
                    inv_row.append(1.0 / den)
                p_parts.append(jnp.concatenate(p_row, axis=1))
                inv_parts.append(jnp.where(lo_mask, inv_row[0], inv_row[1]))
            pm = jnp.concatenate(p_parts, axis=0)
            o = _dot(pm, v_st)
            for g2 in range(2):
                c = c0 + g2 * LANES
                attn_ref[rows, c:c + LANES] = (o[g2 * WINDOW:(g2 + 1) * WINDOW] * inv_parts[g2]).astype(attn_ref.dtype)


def _prompt_attn(rel_bias, sinks2, q_all, kv_all, kv_meta, bidx, m_rows, seq, n_meta):
    t = ROW_TILE
    n_main = m_rows // t
    n_heads = sinks2.shape[1]
    qw, kvw = q_all.shape[1], kv_all.shape[1]
    const = lambda i: (0, 0)
    row = lambda i: (i, 0)
    smem = pl.BlockSpec(memory_space=pltpu.SMEM)
    return pl.pallas_call(
        functools.partial(_prompt_attn_body, tiles_per_seq=seq // t, n_meta=n_meta, n_heads=n_heads),
        grid=(n_main,),
        in_specs=[
            smem, smem,
            pl.BlockSpec((t, qw), row),
            pl.BlockSpec((t, kvw), row),
            pl.BlockSpec((WINDOW, kvw), lambda i: (jnp.maximum(i * (t // WINDOW) - 1, 0), 0)),
            pl.BlockSpec((WINDOW, kvw), const),
            pl.BlockSpec((WINDOW, 2 * WINDOW), const),
        ],
        out_specs=pl.BlockSpec((t, qw), row),
        out_shape=jax.ShapeDtypeStruct((m_rows, qw), BF16),
        scratch_shapes=[pltpu.VMEM((n_heads, WINDOW, 2 * WINDOW), F32)],
        compiler_params=pltpu.CompilerParams(dimension_semantics=("arbitrary",)),
        name="prompt_attn",
    )(rel_bias, sinks2, q_all, kv_all, kv_all, kv_meta, bidx)


def _sample_mix_body(rb_ref, sink_ref, qx_ref, kvn_ref, glun_ref, ck_ref, cv_ref, st_ref, bidx_ref,
                     wdw_ref, bdw_ref, lng_ref, lnb_ref,
                     o_ref, conv_ref, nk_ref, nv_ref, nst_ref, *, n_heads):
    w_buf = ck_ref.shape[1]
    n_st = st_ref.shape[1]
    bias = jnp.concatenate([_bias_table(bidx_ref[...], rb_ref, hd) for hd in range(n_heads)], axis=0)
    hrow = lax.broadcasted_iota(jnp.int32, (n_heads, 1), 0)
    sink = jnp.zeros((n_heads, 1), F32)
    bias_new = jnp.zeros((n_heads, 1), F32)
    for hd in range(n_heads):
        sink = jnp.where(hrow == hd, sink_ref[0, hd], sink)
        bias_new = jnp.where(hrow == hd, rb_ref[0, hd], bias_new)

    for bb in range(qx_ref.shape[0]):
        qb = qx_ref[bb]
        kn = kvn_ref[bb][:, :LANES]
        vn = kvn_ref[bb][:, LANES:]
        ck = ck_ref[bb]
        cv = cv_ref[bb]
        s_c = _dot_t(qb, ck.astype(BF16)) + bias
        s_n = jnp.sum(qb.astype(F32) * kn.astype(BF16).astype(F32), axis=-1, keepdims=True) + bias_new
        m = jnp.maximum(jnp.maximum(jnp.max(s_c, axis=-1, keepdims=True), s_n), sink)
        p_c = jnp.exp(s_c - m)
        p_n = jnp.exp(s_n - m)
        den = jnp.sum(p_c, axis=-1, keepdims=True) + p_n + jnp.exp(sink - m)
        o = _dot(p_c.astype(BF16), cv.astype(BF16))
        o = o + p_n.astype(BF16).astype(F32) * vn.astype(BF16).astype(F32)
        o_ref[bb] = o / den
        nk_ref[bb, 0:w_buf - 1, :] = ck_ref[bb, 1:w_buf, :]
        nk_ref[bb, w_buf - 1:w_buf, :] = kn
        nv_ref[bb, 0:w_buf - 1, :] = cv_ref[bb, 1:w_buf, :]
        nv_ref[bb, w_buf - 1:w_buf, :] = vn
        g_new = glun_ref[bb]
        y = (jnp.sum(st_ref[bb] * wdw_ref[0:n_st, :], axis=0, keepdims=True)
             + g_new * wdw_ref[n_st:n_st + 1, :] + bdw_ref[...])
        conv_ref[bb] = _ln_swish(y, lng_ref[...], lnb_ref[...])
        nst_ref[bb, 0:n_st - 1, :] = st_ref[bb, 1:n_st, :]
        nst_ref[bb, n_st - 1:n_st, :] = g_new


def _sample_mix(rel_bias, sinks2, qx, kv_new, glu_new, cache_k, cache_v, state_conv, bidx_s, w_dw, b_dw, ln_g, ln_b):
    db, w_buf, kw = cache_k.shape
    n_st, ch = state_conv.shape[1], state_conv.shape[2]
    n_heads = sinks2.shape[1]
    sb = SAMPLE_BLOCK
    blk = lambda i: (i, 0, 0)
    const = lambda i: (0, 0)
    smem = pl.BlockSpec(memory_space=pltpu.SMEM)
    return pl.pallas_call(
        functools.partial(_sample_mix_body, n_heads=n_heads),
        grid=(db // sb,),
        in_specs=[
            smem, smem,
            pl.BlockSpec((sb, n_heads, kw), blk),
            pl.BlockSpec((sb, 1, 2 * kw), blk),
            pl.BlockSpec((sb, 1, ch), blk),
            pl.BlockSpec((sb, w_buf, kw), blk),
            pl.BlockSpec((sb, w_buf, kw), blk),
            pl.BlockSpec((sb, n_st, ch), blk),
            pl.BlockSpec((1, w_buf), const),
            pl.BlockSpec(w_dw.shape, const),
            pl.BlockSpec((1, ch), const),
            pl.BlockSpec((1, ch), const),
            pl.BlockSpec((1, ch), const),
        ],
        out_specs=[
            pl.BlockSpec((sb, n_heads, kw), blk),
            pl.BlockSpec((sb, 1, ch), blk),
            pl.BlockSpec((sb, w_buf, kw), blk),
            pl.BlockSpec((sb, w_buf, kw), blk),
            pl.BlockSpec((sb, n_st, ch), blk),
        ],
        out_shape=[
            jax.ShapeDtypeStruct((db, n_heads, kw), F32),
            jax.ShapeDtypeStruct((db, 1, ch), F32),
            jax.ShapeDtypeStruct((db, w_buf, kw), F32),
            jax.ShapeDtypeStruct((db, w_buf, kw), F32),
            jax.ShapeDtypeStruct((db, n_st, ch), F32),
        ],
        compiler_params=pltpu.CompilerParams(dimension_semantics=("arbitrary",)),
        name="sample_mix",
    )(rel_bias, sinks2, qx, kv_new, glu_new, cache_k, cache_v, state_conv, bidx_s, w_dw, b_dw, ln_g, ln_b)


def _conv_runs(cb_st, glu_ref, glum_ref, y_sc, wdw_ref, bdw_ref, first, n_runs):
    t = y_sc.shape[0]
    conv_w = wdw_ref.shape[0]
    n_ch = y_sc.shape[1]
    off = HALO - (conv_w - 1)
    n_a = (off + conv_w - 1) // 8 + 1
    units = [(lc, g) for lc in range(n_ch // LANES) for g in range(t // 8)]
    sizes = [len(units) // n_runs + (1 if r < len(units) % n_runs else 0) for r in range(n_runs)]
    staged = []

    def run(mine):
        if not staged:
            cb_st[0:HALO, :] = jnp.where(first, glum_ref[...], cb_st[0:HALO, :])
            cb_st[HALO:HALO + t, :] = glu_ref[...]
            staged.append(True)
        zs = {}

        def z(lc, s, g):
            if (lc, s, g) not in zs:
                ls = slice(lc * LANES, (lc + 1) * LANES)
                acc = None
                for a in range(n_a):
                    w = 8 * a + s - off
                    if 0 <= w < conv_w:
                        term = cb_st[8 * (g + a):8 * (g + a) + 8, ls] * wdw_ref[w:w + 1, ls]
                        acc = term if acc is None else acc + term
                zs[(lc, s, g)] = acc
            return zs[(lc, s, g)]

        dep = None
        for lc, g in mine:
            ls = slice(lc * LANES, (lc + 1) * LANES)
            acc = jnp.broadcast_to(bdw_ref[:, ls], (8, LANES)) + z(lc, 0, g)
            if dep is not None:
                acc = acc + dep
            for s in range(1, 8):
                acc = acc + jnp.concatenate([z(lc, s, g), z(lc, s, g + 1)], axis=0)[s:s + 8, :]
            y_sc[8 * g:8 * g + 8, ls] = acc
            dep = _dep_zero(acc)
        return dep[0:1, :]

    out, k = [], 0
    for n in sizes:
        out.append(functools.partial(run, units[k:k + n]))
        k += n
    return out


def _post_ffn_body(h_ref, am_ref, as_ref, cs_ref, ga_ref, gc_ref, g2_ref, glu_ref, glum_ref,
                   wdw_ref, bdw_ref, lng_ref, lnb_ref,
                   waoc, wcoc, woutc, w1c, w3c, w2c,
                   ym_ref, ys_ref,
                   waob, wcob, woutb, w1b, w3b, w2b, hid_ref, cb_st, y_sc, conv_st, *, n_main, tiles_per_seq):
    i = pl.program_id(0)
    t_rows = y_sc.shape[0]

    @pl.when(i < N_WSTEPS)
    def _():
        _store_chunk(waob, waoc, i)
        _store_chunk(wcob, wcoc, i)
        _store_chunk(woutb, woutc, i)
        _store_chunk(w1b, w1c, i)
        _store_chunk(w3b, w3c, i)
        _store_chunk(w2b, w2c, i)

    @pl.when(i == 0)
    def _():
        cb_st[...] = jnp.zeros_like(cb_st)
        conv_st[...] = jnp.zeros_like(conv_st)

    @pl.when(i >= N_WSTEPS - 1)
    def _():
        t = i - N_WSTEPS
        tn = t + 1
        on_main = t < n_main
        first = (tn < n_main) & (lax.rem(tn, tiles_per_seq) == 0)
        fillers = _conv_runs(cb_st, glu_ref, glum_ref, y_sc, wdw_ref, bdw_ref, first, w1b.shape[1] // FF_CHUNK)

        at = jnp.where(on_main, am_ref[...], as_ref[...])
        cv = jnp.where(on_main, conv_st[...], cs_ref[...])
        a = _dot(at, waob[...])
        c = _dot(cv, wcob[...])
        mix = (ga_ref[...] * a + gc_ref[...] * c).astype(BF16)
        h2 = h_ref[...] + _dot(mix, woutb[...])
        xn = _rms(h2, g2_ref[...]).astype(BF16)
        y = h2 + 0.5 * _swiglu(xn, w1b, w3b, w2b, hid_ref, fillers)

        conv_st[...] = _ln_swish(y_sc[...], lng_ref[...], lnb_ref[...]).astype(conv_st.dtype)
        cb_st[0:HALO, :] = cb_st[t_rows:t_rows + HALO, :]

        @pl.when((t >= 0) & on_main)
        def _():
            ym_ref[...] = y

        @pl.when(t == n_main)
        def _():
            ys_ref[...] = y


def _post_ffn(h_all, attn_main, attn_small, conv_small, ga_all, gc_all, g2, glu_all, glu_meta,
              w_dw, b_dw, ln_g, ln_b, w_ao, w_co, w_out, w1, w3, w2, seq):
    m_rows, aw = attn_main.shape
    ch = glu_all.shape[1]
    d = h_all.shape[1]
    n_ff = w1.shape[1]
    t = ROW_TILE
    n_main = m_rows // t

    def main_idx(i):
        return (jnp.clip(i - N_WSTEPS, 0, n_main - 1), 0)

    def next_idx(i):
        return (jnp.clip(i - N_WSTEPS + 1, 0, n_main - 1), 0)

    def row_idx(i):
        return (jnp.maximum(i - N_WSTEPS, 0), 0)

    def w_idx(i):
        return (jnp.minimum(i, N_WSTEPS - 1), 0)

    const = lambda i: (0, 0)
    return pl.pallas_call(
        functools.partial(_post_ffn_body, n_main=n_main, tiles_per_seq=seq // t),
        grid=(N_WSTEPS + n_main + 1,),
        in_specs=[
            pl.BlockSpec((t, d), row_idx),
            pl.BlockSpec((t, aw), main_idx),
            pl.BlockSpec((t, aw), const),
            pl.BlockSpec((t, ch), const),
            pl.BlockSpec((t, d), row_idx),
            pl.BlockSpec((t, d), row_idx),
            pl.BlockSpec((1, d), const),
            pl.BlockSpec((t, ch), next_idx),
            pl.BlockSpec((HALO, ch), const),
            pl.BlockSpec(w_dw.shape, const),
            pl.BlockSpec((1, ch), const),
            pl.BlockSpec((1, ch), const),
            pl.BlockSpec((1, ch), const),
            pl.BlockSpec((aw // N_WSTEPS, d), w_idx),
            pl.BlockSpec((ch // N_WSTEPS, d), w_idx),
            pl.BlockSpec((d // N_WSTEPS, d), w_idx),
            pl.BlockSpec((d // N_WSTEPS, n_ff), w_idx),
            pl.BlockSpec((d // N_WSTEPS, n_ff), w_idx),
            pl.BlockSpec((n_ff // N_WSTEPS, d), w_idx),
        ],
        out_specs=[pl.BlockSpec((t, d), main_idx), pl.BlockSpec((t, d), const)],
        out_shape=[jax.ShapeDtypeStruct((m_rows, d), F32), jax.ShapeDtypeStruct((t, d), F32)],
        scratch_shapes=[
            pltpu.VMEM((aw, d), BF16),
            pltpu.VMEM((ch, d), BF16),
            pltpu.VMEM((d, d), BF16),
            pltpu.VMEM((d, n_ff), BF16),
            pltpu.VMEM((d, n_ff), BF16),
            pltpu.VMEM((n_ff, d), BF16),
            pltpu.VMEM((t, n_ff), BF16),
            pltpu.VMEM((HALO + t, ch), F32),
            pltpu.VMEM((t, ch), F32),
            pltpu.VMEM((t, ch), BF16),
        ],
        compiler_params=pltpu.CompilerParams(
            dimension_semantics=("arbitrary",), vmem_limit_bytes=VMEM_LIMIT),
        name="post_ffn2",
    )(h_all, attn_main, attn_small, conv_small, ga_all, gc_all, g2, glu_all, glu_meta,
      w_dw, b_dw, ln_g, ln_b, w_ao, w_co, w_out, w1, w3, w2)


def _t5_bucket(dist, n_buckets):
    max_exact = n_buckets // 2
    d = np.maximum(dist, 0)
    ratio = (np.log(np.maximum(d, 1).astype(np.float32) / np.float32(max_exact))
             / np.float32(math.log(REL_MAX_DIST / max_exact)))
    large = np.minimum(max_exact + (ratio * np.float32(n_buckets - max_exact)).astype(np.int32), n_buckets - 1)
    return np.where(d < max_exact, d, large).astype(np.int32)


def _bucket_or_masked(dist, n_buckets):
    ok = (dist >= 0) & (dist < WINDOW)
    return np.where(ok, _t5_bucket(dist, n_buckets), -1).astype(np.int32)


def _rows(x, start, n):
    return lax.slice_in_dim(x, start, start + n, axis=0)


def kernel(x_prompt, x_sample, cache_k, cache_v, state_conv, meta_tokens, ffn1_norm, ffn1_w1, ffn1_w3, ffn1_w2, mix_norm, w_in, q_norm, k_norm, rel_bias, sinks, w_attn_out, w_dw, b_dw, conv_ln_g, conv_ln_b, w_conv_out, w_out, ffn2_norm, ffn2_w1, ffn2_w3, ffn2_w2):
    n_b, seq, d = x_prompt.shape
    db = x_sample.shape[0]
    n_meta = meta_tokens.shape[0]
    n_heads = sinks.shape[0]
    w_buf, n_kv, hd = cache_k.shape[1], cache_k.shape[2], cache_k.shape[3]
    ch = w_dw.shape[1]
    n_st = state_conv.shape[1]
    n_buckets = rel_bias.shape[0]
    aw, kvw = n_heads * hd, n_kv * hd
    t = ROW_TILE
    m_rows = n_b * seq
    assert hd == HEAD_DIM and kvw == LANES and n_kv == 2 and n_heads == 8 and w_buf == WINDOW
    assert x_sample.shape[1] == 1 and seq % t == 0 and n_meta + db <= t and db % SAMPLE_BLOCK == 0
    assert n_meta <= HALO and n_meta <= WINDOW and n_st == w_dw.shape[0] - 1 and n_st <= HALO
    splits = tuple(int(v) for v in np.cumsum([0, aw, kvw, kvw, ch, ch, d, d]))
    assert splits[-1] == w_in.shape[1]

    row = lambda v: v.reshape(1, -1)
    x_main = x_prompt.reshape(m_rows, d)
    x_small = jnp.concatenate(
        [meta_tokens, x_sample.reshape(db, d), jnp.zeros((t - n_meta - db, d), F32)], axis=0)
    gq2 = row(jnp.concatenate([q_norm, q_norm]))
    gk2 = row(jnp.concatenate([k_norm, k_norm]))

    h_all, q_all, kv_all, glu_all, ga_all, gc_all = _ffn_inproj(
        x_main, x_small, row(ffn1_norm), row(mix_norm), gq2, gk2, ffn1_w1, ffn1_w3, ffn1_w2, w_in, splits)

    sinks2 = row(sinks)
    b_dw2, ln_g2, ln_b2 = row(b_dw), row(conv_ln_g), row(conv_ln_b)

    kv_meta = jnp.concatenate([jnp.zeros((WINDOW - n_meta, 2 * kvw), F32), _rows(kv_all, m_rows, n_meta)], axis=0)
    glu_meta = jnp.concatenate([jnp.zeros((HALO - n_meta, ch), F32), _rows(glu_all, m_rows, n_meta)], axis=0)
    dist = np.arange(WINDOW)[:, None] + WINDOW - np.arange(2 * WINDOW)[None, :]
    bidx = jnp.asarray(_bucket_or_masked(dist, n_buckets))
    attn_main = _prompt_attn(rel_bias, sinks2, q_all, kv_all, kv_meta, bidx, m_rows, seq, n_meta)

    s0 = m_rows + n_meta
    q_s = _rows(q_all, s0, db).reshape(db, n_heads, hd)
    lane_half = (np.arange(kvw)[None, :] // hd) == (np.arange(n_heads)[:, None] // (n_heads // n_kv))
    qx = jnp.where(jnp.asarray(lane_half)[None], jnp.concatenate([q_s, q_s], axis=-1), 0).astype(BF16)
    kv_new = _rows(kv_all, s0, db).reshape(db, 1, 2 * kvw)
    glu_new = _rows(glu_all, s0, db).reshape(db, 1, ch)
    bidx_s = jnp.asarray(_bucket_or_masked(w_buf - np.arange(w_buf)[None, :], n_buckets))
    o_s, conv_s, new_k_s, new_v_s, new_conv_s = _sample_mix(
        rel_bias, sinks2, qx, kv_new, glu_new, cache_k.reshape(db, w_buf, kvw), cache_v.reshape(db, w_buf, kvw),
        state_conv, bidx_s, w_dw, b_dw2, ln_g2, ln_b2)
    o_s = jnp.where(jnp.asarray(lane_half)[None], o_s, 0.0).reshape(db, n_heads, n_kv, hd).sum(axis=2)
    attn_small = jnp.concatenate(
        [jnp.zeros((n_meta, aw), BF16), o_s.reshape(db, aw).astype(BF16), jnp.zeros((t - n_meta - db, aw), BF16)],
        axis=0)
    conv_small = jnp.concatenate(
        [jnp.zeros((n_meta, ch), BF16), conv_s.reshape(db, ch).astype(BF16), jnp.zeros((t - n_meta - db, ch), BF16)],
        axis=0)

    y_main, y_small = _post_ffn(h_all, attn_main, attn_small, conv_small, ga_all, gc_all, row(ffn2_norm),
                                glu_all, glu_meta, w_dw, b_dw2, ln_g2, ln_b2,
                                w_attn_out, w_conv_out, w_out, ffn2_w1, ffn2_w3, ffn2_w2, seq)

    kv_t = jnp.stack([_rows(kv_all, (s + 1) * seq - WINDOW, WINDOW) for s in range(n_b)])
    glu_t = jnp.stack([_rows(glu_all, (s + 1) * seq - n_st, n_st) for s in range(n_b)])
    return (
        y_main.reshape(n_b, seq, d),
        _rows(y_small, n_meta, db).reshape(db, 1, d),
        kv_t[..., :kvw].reshape(n_b, WINDOW, n_kv, hd),
        kv_t[..., kvw:].reshape(n_b, WINDOW, n_kv, hd),
        glu_t,
        new_k_s.reshape(db, w_buf, n_kv, hd),
        new_v_s.reshape(db, w_buf, n_kv, hd),
        new_conv_s,
    )
```

```python
import functools
import math

import jax
import jax.numpy as jnp
import numpy as np
from jax import lax
from jax.experimental import pallas as pl
from jax.experimental.pallas import tpu as pltpu

F32 = jnp.float32
BF16 = jnp.bfloat16

EPS = 1e-6
NEG = -1e30
WINDOW = 128
REL_MAX_DIST = 128
HEAD_DIM = 64
LANES = 128
ROW_TILE = 256
FF_CHUNK = 256
N_WSTEPS = 16
ATTN_SPAN = 3
HALO = 32
SAMPLE_BLOCK = 8
VMEM_LIMIT = 56 * 1024 * 1024


def _dot(a, b):
    return jnp.dot(a, b, preferred_element_type=F32)


def _dot_t(a, b):
    return lax.dot_general(a, b, (((1,), (1,)), ((), ())), preferred_element_type=F32)


def _rms(x, g):
    return x * lax.rsqrt(jnp.mean(x * x, axis=-1, keepdims=True) + EPS) * g


def _pair_rms(x, g2, lo_mask):
    sq = x * x
    lo = jnp.sum(jnp.where(lo_mask, sq, 0.0), axis=-1, keepdims=True) * (1.0 / HEAD_DIM)
    hi = jnp.sum(jnp.where(lo_mask, 0.0, sq), axis=-1, keepdims=True) * (1.0 / HEAD_DIM)
    r = jnp.where(lo_mask, lax.rsqrt(lo + EPS), lax.rsqrt(hi + EPS))
    return x * r * g2


def _dep_zero(row):
    u = pltpu.bitcast(row, jnp.uint32)
    return pltpu.bitcast((u >> 16) >> 16, F32)


def _swiglu(xn, w1b, w3b, w2b, hid_ref, fillers=()):
    n_ff = w1b.shape[1]
    n_chunks = n_ff // FF_CHUNK
    due = {}
    for c in range(n_chunks):
        sl = slice(c * FF_CHUNK, (c + 1) * FF_CHUNK)
        a = _dot(xn, w1b[:, sl])
        b = _dot(xn, w3b[:, sl])
        for z in due.pop(c, ()):
            b = b + jnp.concatenate([z] * (FF_CHUNK // LANES), axis=1)
        for issue, when, thunk in fillers:
            if issue == c:
                due.setdefault(min(when, n_chunks), []).append(_dep_zero(thunk()))
        hid_ref[:, sl] = (a * jax.nn.sigmoid(a) * b).astype(BF16)
    out = _dot(hid_ref[...], w2b[...])
    for z in due.pop(n_chunks, ()):
        out = out + jnp.concatenate([z] * (out.shape[1] // LANES), axis=1)
    assert not due
    return out


def _store_chunk(dst, src, i):
    rows = src.shape[0]
    r = pl.multiple_of(i * rows, 16)
    dst[pl.ds(r, rows), :] = src[...].astype(BF16)


def _bias_table(bidx, rb_ref, head):
    tab = jnp.full(bidx.shape, NEG, F32)
    for b in range(rb_ref.shape[0]):
        tab = jnp.where(bidx == b, rb_ref[b, head], tab)
    return tab


def _ln_swish(y, g, b):
    mu = jnp.mean(y, axis=-1, keepdims=True)
    yc = y - mu
    var = jnp.mean(yc * yc, axis=-1, keepdims=True)
    z = yc * lax.rsqrt(var + EPS) * g + b
    return z * jax.nn.sigmoid(z)


def _ffn_inproj_body(xm_ref, xs_ref, g1_ref, gm_ref, gq_ref, gk_ref, w1c, w3c, w2c, winc,
                     h_ref, q_ref, kv_ref, glu_ref, ga_ref, gc_ref,
                     w1b, w3b, w2b, winb, hid_ref, *, n_main, splits):
    i = pl.program_id(0)

    @pl.when(i < N_WSTEPS)
    def _():
        _store_chunk(w1b, w1c, i)
        _store_chunk(w3b, w3c, i)
        _store_chunk(w2b, w2c, i)
        _store_chunk(winb, winc, i)

    @pl.when(i >= N_WSTEPS)
    def _():
        t = i - N_WSTEPS
        x = jnp.where(t < n_main, xm_ref[...], xs_ref[...])
        xn = _rms(x, g1_ref[...]).astype(BF16)
        h = x + 0.5 * _swiglu(xn, w1b, w3b, w2b, hid_ref)
        h_ref[...] = h
        u = _rms(h, gm_ref[...]).astype(BF16)

        o_q, o_k, o_v, o_a, o_b, o_ga, o_gc, o_end = splits
        lo_mask = lax.broadcasted_iota(jnp.int32, (1, LANES), 1) < HEAD_DIM
        zq = _dot(u, winb[:, o_q:o_k])
        for p in range((o_k - o_q) // LANES):
            sl = slice(p * LANES, (p + 1) * LANES)
            q_ref[:, sl] = (_pair_rms(zq[:, sl], gq_ref[...], lo_mask) * (HEAD_DIM ** -0.5)).astype(BF16)
        zkv = _dot(u, winb[:, o_k:o_a])
        kv_ref[:, :LANES] = _pair_rms(zkv[:, :LANES], gk_ref[...], lo_mask)
        kv_ref[:, LANES:] = zkv[:, LANES:]
        za = _dot(u, winb[:, o_a:o_b])
        zb = _dot(u, winb[:, o_b:o_ga])
        glu_ref[...] = za * jax.nn.sigmoid(zb)
        ga_ref[...] = jax.nn.sigmoid(_dot(u, winb[:, o_ga:o_gc]))
        gc_ref[...] = jax.nn.sigmoid(_dot(u, winb[:, o_gc:o_end]))


def _ffn_inproj(x_main, x_small, g1, gm, gq2, gk2, w1, w3, w2, w_in, splits):
    m_rows, d = x_main.shape
    n_ff = w1.shape[1]
    n_main = m_rows // ROW_TILE
    r_rows = m_rows + ROW_TILE
    t = ROW_TILE
    o_q, o_k, o_v, o_a, o_b, o_ga, o_gc, o_end = splits
    assert o_v - o_k == LANES and o_a - o_v == LANES and (o_k - o_q) % LANES == 0

    def main_idx(i):
        return (jnp.clip(i - N_WSTEPS, 0, n_main - 1), 0)

    def row_idx(i):
        return (jnp.maximum(i - N_WSTEPS, 0), 0)

    def w_idx(i):
        return (jnp.minimum(i, N_WSTEPS - 1), 0)

    const = lambda i: (0, 0)
    outs = [
        jax.ShapeDtypeStruct((r_rows, d), F32),
        jax.ShapeDtypeStruct((r_rows, o_k - o_q), BF16),
        jax.ShapeDtypeStruct((r_rows, o_a - o_k), F32),
        jax.ShapeDtypeStruct((r_rows, o_b - o_a), F32),
        jax.ShapeDtypeStruct((r_rows, o_gc - o_ga), F32),
        jax.ShapeDtypeStruct((r_rows, o_end - o_gc), F32),
    ]
    return pl.pallas_call(
        functools.partial(_ffn_inproj_body, n_main=n_main, splits=splits),
        grid=(N_WSTEPS + n_main + 1,),
        in_specs=[
            pl.BlockSpec((t, d), main_idx),
            pl.BlockSpec((t, d), const),
            pl.BlockSpec((1, d), const),
            pl.BlockSpec((1, d), const),
            pl.BlockSpec((1, LANES), const),
            pl.BlockSpec((1, LANES), const),
            pl.BlockSpec((d // N_WSTEPS, n_ff), w_idx),
            pl.BlockSpec((d // N_WSTEPS, n_ff), w_idx),
            pl.BlockSpec((n_ff // N_WSTEPS, d), w_idx),
            pl.BlockSpec((d // N_WSTEPS, o_end), w_idx),
        ],
        out_specs=[pl.BlockSpec((t, s.shape[1]), row_idx) for s in outs],
        out_shape=outs,
        scratch_shapes=[
            pltpu.VMEM((d, n_ff), BF16),
            pltpu.VMEM((d, n_ff), BF16),
            pltpu.VMEM((n_ff, d), BF16),
            pltpu.VMEM((d, o_end), BF16),
            pltpu.VMEM((t, n_ff), BF16),
        ],
        compiler_params=pltpu.CompilerParams(
            dimension_semantics=("arbitrary",), vmem_limit_bytes=VMEM_LIMIT),
        name="ffn1_inproj",
    )(x_main, x_small, g1, gm, gq2, gk2, w1, w3, w2, w_in)


def _sample_mix_body(rb_ref, sink_ref, qx_ref, kvn_ref, glun_ref, ck_ref, cv_ref, st_ref, bidx_ref,
                     wdw_ref, bdw_ref, lng_ref, lnb_ref,
                     o_ref, conv_ref, nk_ref, nv_ref, nst_ref, *, n_heads):
    w_buf = ck_ref.shape[1]
    n_st = st_ref.shape[1]
    bias = jnp.concatenate([_bias_table(bidx_ref[...], rb_ref, hd) for hd in range(n_heads)], axis=0)
    hrow = lax.broadcasted_iota(jnp.int32, (n_heads, 1), 0)
    sink = jnp.zeros((n_heads, 1), F32)
    bias_new = jnp.zeros((n_heads, 1), F32)
    for hd in range(n_heads):
        sink = jnp.where(hrow == hd, sink_ref[0, hd], sink)
        bias_new = jnp.where(hrow == hd, rb_ref[0, hd], bias_new)

    for bb in range(qx_ref.shape[0]):
        qb = qx_ref[bb]
        kn = kvn_ref[bb][:, :LANES]
        vn = kvn_ref[bb][:, LANES:]
        ck = ck_ref[bb]
        cv = cv_ref[bb]
        s_c = _dot_t(qb, ck.astype(BF16)) + bias
        s_n = jnp.sum(qb.astype(F32) * kn.astype(BF16).astype(F32), axis=-1, keepdims=True) + bias_new
        m = jnp.maximum(jnp.maximum(jnp.max(s_c, axis=-1, keepdims=True), s_n), sink)
        p_c = jnp.exp(s_c - m)
        p_n = jnp.exp(s_n - m)
        den = jnp.sum(p_c, axis=-1, keepdims=True) + p_n + jnp.exp(sink - m)
        o = _dot(p_c.astype(BF16), cv.astype(BF16))
        o = o + p_n.astype(BF16).astype(F32) * vn.astype(BF16).astype(F32)
        o_ref[bb] = o / den
        nk_ref[bb, 0:w_buf - 1, :] = ck_ref[bb, 1:w_buf, :]
        nk_ref[bb, w_buf - 1:w_buf, :] = kn
        nv_ref[bb, 0:w_buf - 1, :] = cv_ref[bb, 1:w_buf, :]
        nv_ref[bb, w_buf - 1:w_buf, :] = vn
        g_new = glun_ref[bb]
        y = (jnp.sum(st_ref[bb] * wdw_ref[0:n_st, :], axis=0, keepdims=True)
             + g_new * wdw_ref[n_st:n_st + 1, :] + bdw_ref[...])
        conv_ref[bb] = _ln_swish(y, lng_ref[...], lnb_ref[...])
        nst_ref[bb, 0:n_st - 1, :] = st_ref[bb, 1:n_st, :]
        nst_ref[bb, n_st - 1:n_st, :] = g_new


def _sample_mix(rel_bias, sinks2, qx, kv_new, glu_new, cache_k, cache_v, state_conv, bidx_s, w_dw, b_dw, ln_g, ln_b):
    db, w_buf, kw = cache_k.shape
    n_st, ch = state_conv.shape[1], state_conv.shape[2]
    n_heads = sinks2.shape[1]
    sb = SAMPLE_BLOCK
    blk = lambda i: (i, 0, 0)
    const = lambda i: (0, 0)
    smem = pl.BlockSpec(memory_space=pltpu.SMEM)
    return pl.pallas_call(
        functools.partial(_sample_mix_body, n_heads=n_heads),
        grid=(db // sb,),
        in_specs=[
            smem, smem,
            pl.BlockSpec((sb, n_heads, kw), blk),
            pl.BlockSpec((sb, 1, 2 * kw), blk),
            pl.BlockSpec((sb, 1, ch), blk),
            pl.BlockSpec((sb, w_buf, kw), blk),
            pl.BlockSpec((sb, w_buf, kw), blk),
            pl.BlockSpec((sb, n_st, ch), blk),
            pl.BlockSpec((1, w_buf), const),
            pl.BlockSpec(w_dw.shape, const),
            pl.BlockSpec((1, ch), const),
            pl.BlockSpec((1, ch), const),
            pl.BlockSpec((1, ch), const),
        ],
        out_specs=[
            pl.BlockSpec((sb, n_heads, kw), blk),
            pl.BlockSpec((sb, 1, ch), blk),
            pl.BlockSpec((sb, w_buf, kw), blk),
            pl.BlockSpec((sb, w_buf, kw), blk),
            pl.BlockSpec((sb, n_st, ch), blk),
        ],
        out_shape=[
            jax.ShapeDtypeStruct((db, n_heads, kw), F32),
            jax.ShapeDtypeStruct((db, 1, ch), F32),
            jax.ShapeDtypeStruct((db, w_buf, kw), F32),
            jax.ShapeDtypeStruct((db, w_buf, kw), F32),
            jax.ShapeDtypeStruct((db, n_st, ch), F32),
        ],
        compiler_params=pltpu.CompilerParams(dimension_semantics=("arbitrary",)),
        name="sample_mix",
    )(rel_bias, sinks2, qx, kv_new, glu_new, cache_k, cache_v, state_conv, bidx_s, w_dw, b_dw, ln_g, ln_b)


def _fold_rows(x):
    return jnp.max(x.reshape(x.shape[0] // 8, 8, LANES), axis=0)[0:1, :]


def _attention_pieces(q_ref, kvc_ref, kvp_ref, kvm_ref, bias_sc, sink_ref, attn_st, first, n_meta, n_heads):
    t = q_ref.shape[0]
    n_kv = 2
    grp = n_heads // n_kv
    cache = {}

    def prep():
        if not cache:
            lo_mask = lax.broadcasted_iota(jnp.int32, (1, LANES), 1) < HEAD_DIM
            kv_prev = jnp.where(first, kvm_ref[...], kvp_ref[...])
            kv = jnp.concatenate([kv_prev, kvc_ref[...]], axis=0)
            k, v = kv[:, :LANES], kv[:, LANES:]
            k_r = pltpu.roll(k, HEAD_DIM, axis=1)
            v_r = pltpu.roll(v, HEAD_DIM, axis=1)
            cache["k_lo"] = [jnp.where(lo_mask, k, 0.0).astype(BF16), jnp.where(lo_mask, k_r, 0.0).astype(BF16)]
            cache["k_hi"] = [jnp.where(lo_mask, 0.0, k_r).astype(BF16), jnp.where(lo_mask, 0.0, k).astype(BF16)]
            cache["v_lo"] = [jnp.where(lo_mask, v, 0.0).astype(BF16), jnp.where(lo_mask, v_r, 0.0).astype(BF16)]
            cache["v_hi"] = [jnp.where(lo_mask, 0.0, v_r).astype(BF16), jnp.where(lo_mask, 0.0, v).astype(BF16)]
            col = lax.broadcasted_iota(jnp.int32, (1, 2 * WINDOW), 1)
            cache["lead_mask"] = jnp.where(first & (col < WINDOW - n_meta), NEG, 0.0)
            cache["lo_mask"] = lo_mask
        return cache

    def piece(qb, h):
        c = prep()
        lo_mask = c["lo_mask"]
        rows = slice(qb * WINDOW, (qb + 1) * WINDOW)
        keys = slice(qb * WINDOW, qb * WINDOW + 2 * WINDOW)
        k_st = jnp.concatenate([c["k_lo"][h][keys], c["k_hi"][h][keys]], axis=0)
        v_st = jnp.concatenate([c["v_lo"][h][keys], c["v_hi"][h][keys]], axis=0)
        c0 = h * grp * HEAD_DIM
        qq = jnp.concatenate([q_ref[rows, c0:c0 + LANES], q_ref[rows, c0 + LANES:c0 + 2 * LANES]], axis=0)
        s = _dot_t(qq, k_st)
        p_parts, inv_parts = [], []
        for g2 in range(2):
            p_row, inv_row = [], []
            for par in range(2):
                hd = h * grp + 2 * g2 + par
                sq = s[g2 * WINDOW:(g2 + 1) * WINDOW, par * 2 * WINDOW:(par + 1) * 2 * WINDOW] + bias_sc[hd]
                if qb == 0:
                    sq = sq + c["lead_mask"]
                sink = sink_ref[0, hd]
                m = jnp.maximum(jnp.max(sq, axis=-1, keepdims=True), sink)
                p = jnp.exp(sq - m)
                den = jnp.sum(p, axis=-1, keepdims=True) + jnp.exp(sink - m)
                p_row.append(p.astype(BF16))
                inv_row.append(1.0 / den)
            p_parts.append(jnp.concatenate(p_row, axis=1))
            inv_parts.append(jnp.where(lo_mask, inv_row[0], inv_row[1]))
        pm = jnp.concatenate(p_parts, axis=0)
        o = _dot(pm, v_st)
        outs = []
        for g2 in range(2):
            cc = c0 + g2 * LANES
            outs.append(o[g2 * WINDOW:(g2 + 1) * WINDOW] * inv_parts[g2])
            attn_st[rows, cc:cc + LANES] = outs[-1].astype(attn_st.dtype)
        return _fold_rows(jnp.maximum(outs[0], outs[1]))

    return [functools.partial(piece, qb, h) for qb in range(t // WINDOW) for h in range(n_kv)]


def _conv_runs(cb_st, glu_ref, glum_ref, y_sc, wdw_ref, bdw_ref, first, n_runs):
    t = y_sc.shape[0]
    conv_w = wdw_ref.shape[0]
    n_ch = y_sc.shape[1]
    off = HALO - (conv_w - 1)
    n_a = (off + conv_w - 1) // 8 + 1
    units = [(lc, g) for lc in range(n_ch // LANES) for g in range(t // 8)]
    sizes = [len(units) // n_runs + (1 if r < len(units) % n_runs else 0) for r in range(n_runs)]
    staged = []

    def run(mine):
        if not staged:
            cb_st[0:HALO, :] = jnp.where(first, glum_ref[...], cb_st[0:HALO, :])
            cb_st[HALO:HALO + t, :] = glu_ref[...]
            staged.append(True)
        zs = {}

        def z(lc, s, g):
            if (lc, s, g) not in zs:
                ls = slice(lc * LANES, (lc + 1) * LANES)
                acc = None
                for a in range(n_a):
                    w = 8 * a + s - off
                    if 0 <= w < conv_w:
                        term = cb_st[8 * (g + a):8 * (g + a) + 8, ls] * wdw_ref[w:w + 1, ls]
                        acc = term if acc is None else acc + term
                zs[(lc, s, g)] = acc
            return zs[(lc, s, g)]

        dep = None
        for lc, g in mine:
            ls = slice(lc * LANES, (lc + 1) * LANES)
            acc = jnp.broadcast_to(bdw_ref[:, ls], (8, LANES)) + z(lc, 0, g)
            if dep is not None:
                acc = acc + dep
            for s in range(1, 8):
                acc = acc + jnp.concatenate([z(lc, s, g), z(lc, s, g + 1)], axis=0)[s:s + 8, :]
            y_sc[8 * g:8 * g + 8, ls] = acc
            dep = _dep_zero(acc)
        return dep[0:1, :]

    out, k = [], 0
    for n in sizes:
        out.append(functools.partial(run, units[k:k + n]))
        k += n
    return out


def _post_ffn_body(rb_ref, sink_ref, h_ref, as_ref, cs_ref, ga_ref, gc_ref, g2_ref,
                   q_ref, kvc_ref, kvp_ref, kvm_ref, bidx_ref, glu_ref, glum_ref,
                   wdw_ref, bdw_ref, lng_ref, lnb_ref,
                   waoc, wcoc, woutc, w1c, w3c, w2c,
                   ym_ref, ys_ref,
                   waob, wcob, woutb, w1b, w3b, w2b, hid_ref, cb_st, y_sc, conv_st, attn_st, bias_sc,
                   *, n_main, tiles_per_seq, n_meta, n_heads):
    i = pl.program_id(0)
    t_rows = y_sc.shape[0]

    @pl.when(i < N_WSTEPS)
    def _():
        _store_chunk(waob, waoc, i)
        _store_chunk(wcob, wcoc, i)
        _store_chunk(woutb, woutc, i)
        _store_chunk(w1b, w1c, i)
        _store_chunk(w3b, w3c, i)
        _store_chunk(w2b, w2c, i)

    @pl.when(i == 0)
    def _():
        cb_st[...] = jnp.zeros_like(cb_st)
        conv_st[...] = jnp.zeros_like(conv_st)
        attn_st[...] = jnp.zeros_like(attn_st)
        for hd in range(n_heads):
            bias_sc[hd] = _bias_table(bidx_ref[...], rb_ref, hd)

    @pl.when(i >= N_WSTEPS - 1)
    def _():
        t = i - N_WSTEPS
        tn = t + 1
        on_main = t < n_main
        first = (tn < n_main) & (lax.rem(tn, tiles_per_seq) == 0)
        n_chunks = w1b.shape[1] // FF_CHUNK
        conv_f = _conv_runs(cb_st, glu_ref, glum_ref, y_sc, wdw_ref, bdw_ref, first, n_chunks)
        attn_f = _attention_pieces(q_ref, kvc_ref, kvp_ref, kvm_ref, bias_sc, sink_ref, attn_st, first,
                                   n_meta, n_heads)
        fillers = [(c, c + 1, f) for c, f in enumerate(conv_f)]
        fillers += [(2 * p, 2 * p + ATTN_SPAN, f) for p, f in enumerate(attn_f)]

        at = jnp.where(on_main, attn_st[...], as_ref[...])
        cv = jnp.where(on_main, conv_st[...], cs_ref[...])
        a = _dot(at, waob[...])
        c = _dot(cv, wcob[...])
        mix = (ga_ref[...] * a + gc_ref[...] * c).astype(BF16)
        h2 = h_ref[...] + _dot(mix, woutb[...])
        xn = _rms(h2, g2_ref[...]).astype(BF16)
        y = h2 + 0.5 * _swiglu(xn, w1b, w3b, w2b, hid_ref, fillers)

        conv_st[...] = _ln_swish(y_sc[...], lng_ref[...], lnb_ref[...]).astype(conv_st.dtype)
        cb_st[0:HALO, :] = cb_st[t_rows:t_rows + HALO, :]

        @pl.when((t >= 0) & on_main)
        def _():
            ym_ref[...] = y

        @pl.when(t == n_main)
        def _():
            ys_ref[...] = y


def _post_ffn(rel_bias, sinks2, h_all, attn_small, conv_small, ga_all, gc_all, g2,
              q_all, kv_all, kv_meta, bidx, glu_all, glu_meta,
              w_dw, b_dw, ln_g, ln_b, w_ao, w_co, w_out, w1, w3, w2, m_rows, seq, n_meta):
    aw, kvw2, ch = q_all.shape[1], kv_all.shape[1], glu_all.shape[1]
    d = h_all.shape[1]
    n_ff = w1.shape[1]
    n_heads = sinks2.shape[1]
    t = ROW_TILE
    n_main = m_rows // t

    def main_idx(i):
        return (jnp.clip(i - N_WSTEPS, 0, n_main - 1), 0)

    def next_idx(i):
        return (jnp.clip(i - N_WSTEPS + 1, 0, n_main - 1), 0)

    def next_prev_keys_idx(i):
        return (jnp.maximum(next_idx(i)[0] * (t // WINDOW) - 1, 0), 0)

    def row_idx(i):
        return (jnp.maximum(i - N_WSTEPS, 0), 0)

    def w_idx(i):
        return (jnp.minimum(i, N_WSTEPS - 1), 0)

    const = lambda i: (0, 0)
    smem = pl.BlockSpec(memory_space=pltpu.SMEM)
    return pl.pallas_call(
        functools.partial(_post_ffn_body, n_main=n_main, tiles_per_seq=seq // t, n_meta=n_meta, n_heads=n_heads),
        grid=(N_WSTEPS + n_main + 1,),
        in_specs=[
            smem, smem,
            pl.BlockSpec((t, d), row_idx),
            pl.BlockSpec((t, aw), const),
            pl.BlockSpec((t, ch), const),
            pl.BlockSpec((t, d), row_idx),
            pl.BlockSpec((t, d), row_idx),
            pl.BlockSpec((1, d), const),
            pl.BlockSpec((t, aw), next_idx),
            pl.BlockSpec((t, kvw2), next_idx),
            pl.BlockSpec((WINDOW, kvw2), next_prev_keys_idx),
            pl.BlockSpec((WINDOW, kvw2), const),
            pl.BlockSpec((WINDOW, 2 * WINDOW), const),
            pl.BlockSpec((t, ch), next_idx),
            pl.BlockSpec((HALO, ch), const),
            pl.BlockSpec(w_dw.shape, const),
            pl.BlockSpec((1, ch), const),
            pl.BlockSpec((1, ch), const),
            pl.BlockSpec((1, ch), const),
            pl.BlockSpec((aw // N_WSTEPS, d), w_idx),
            pl.BlockSpec((ch // N_WSTEPS, d), w_idx),
            pl.BlockSpec((d // N_WSTEPS, d), w_idx),
            pl.BlockSpec((d // N_WSTEPS, n_ff), w_idx),
            pl.BlockSpec((d // N_WSTEPS, n_ff), w_idx),
            pl.BlockSpec((n_ff // N_WSTEPS, d), w_idx),
        ],
        out_specs=[pl.BlockSpec((t, d), main_idx), pl.BlockSpec((t, d), const)],
        out_shape=[jax.ShapeDtypeStruct((m_rows, d), F32), jax.ShapeDtypeStruct((t, d), F32)],
        scratch_shapes=[
            pltpu.VMEM((aw, d), BF16),
            pltpu.VMEM((ch, d), BF16),
            pltpu.VMEM((d, d), BF16),
            pltpu.VMEM((d, n_ff), BF16),
            pltpu.VMEM((d, n_ff), BF16),
            pltpu.VMEM((n_ff, d), BF16),
            pltpu.VMEM((t, n_ff), BF16),
            pltpu.VMEM((HALO + t, ch), F32),
            pltpu.VMEM((t, ch), F32),
            pltpu.VMEM((t, ch), BF16),
            pltpu.VMEM((t, aw), BF16),
            pltpu.VMEM((n_heads, WINDOW, 2 * WINDOW), F32),
        ],
        compiler_params=pltpu.CompilerParams(
            dimension_semantics=("arbitrary",), vmem_limit_bytes=VMEM_LIMIT),
        name="post_ffn2",
    )(rel_bias, sinks2, h_all, attn_small, conv_small, ga_all, gc_all, g2,
      q_all, kv_all, kv_all, kv_meta, bidx, glu_all, glu_meta,
      w_dw, b_dw, ln_g, ln_b, w_ao, w_co, w_out, w1, w3, w2)


def _t5_bucket(dist, n_buckets):
    max_exact = n_buckets // 2
    d = np.maximum(dist, 0)
    ratio = (np.log(np.maximum(d, 1).astype(np.float32) / np.float32(max_exact))
             / np.float32(math.log(REL_MAX_DIST / max_exact)))
    large = np.minimum(max_exact + (ratio * np.float32(n_buckets - max_exact)).astype(np.int32), n_buckets - 1)
    return np.where(d < max_exact, d, large).astype(np.int32)


def _bucket_or_masked(dist, n_buckets):
    ok = (dist >= 0) & (dist < WINDOW)
    return np.where(ok, _t5_bucket(dist, n_buckets), -1).astype(np.int32)


def _rows(x, start, n):
    return lax.slice_in_dim(x, start, start + n, axis=0)


def kernel(x_prompt, x_sample, cache_k, cache_v, state_conv, meta_tokens, ffn1_norm, ffn1_w1, ffn1_w3, ffn1_w2, mix_norm, w_in, q_norm, k_norm, rel_bias, sinks, w_attn_out, w_dw, b_dw, conv_ln_g, conv_ln_b, w_conv_out, w_out, ffn2_norm, ffn2_w1, ffn2_w3, ffn2_w2):
    n_b, seq, d = x_prompt.shape
    db = x_sample.shape[0]
    n_meta = meta_tokens.shape[0]
    n_heads = sinks.shape[0]
    w_buf, n_kv, hd = cache_k.shape[1], cache_k.shape[2], cache_k.shape[3]
    ch = w_dw.shape[1]
    n_st = state_conv.shape[1]
    n_buckets = rel_bias.shape[0]
    aw, kvw = n_heads * hd, n_kv * hd
    t = ROW_TILE
    m_rows = n_b * seq
    assert hd == HEAD_DIM and kvw == LANES and n_kv == 2 and n_heads == 8 and w_buf == WINDOW
    assert x_sample.shape[1] == 1 and seq % t == 0 and n_meta + db <= t and db % SAMPLE_BLOCK == 0
    assert n_meta <= HALO and n_meta <= WINDOW and n_st == w_dw.shape[0] - 1 and n_st <= HALO
    splits = tuple(int(v) for v in np.cumsum([0, aw, kvw, kvw, ch, ch, d, d]))
    assert splits[-1] == w_in.shape[1]

    row = lambda v: v.reshape(1, -1)
    x_main = x_prompt.reshape(m_rows, d)
    x_small = jnp.concatenate(
        [meta_tokens, x_sample.reshape(db, d), jnp.zeros((t - n_meta - db, d), F32)], axis=0)
    gq2 = row(jnp.concatenate([q_norm, q_norm]))
    gk2 = row(jnp.concatenate([k_norm, k_norm]))

    h_all, q_all, kv_all, glu_all, ga_all, gc_all = _ffn_inproj(
        x_main, x_small, row(ffn1_norm), row(mix_norm), gq2, gk2, ffn1_w1, ffn1_w3, ffn1_w2, w_in, splits)

    sinks2 = row(sinks)
    b_dw2, ln_g2, ln_b2 = row(b_dw), row(conv_ln_g), row(conv_ln_b)

    kv_meta = jnp.concatenate([jnp.zeros((WINDOW - n_meta, 2 * kvw), F32), _rows(kv_all, m_rows, n_meta)], axis=0)
    glu_meta = jnp.concatenate([jnp.zeros((HALO - n_meta, ch), F32), _rows(glu_all, m_rows, n_meta)], axis=0)
    dist = np.arange(WINDOW)[:, None] + WINDOW - np.arange(2 * WINDOW)[None, :]
    bidx = jnp.asarray(_bucket_or_masked(dist, n_buckets))

    s0 = m_rows + n_meta
    q_s = _rows(q_all, s0, db).reshape(db, n_heads, hd)
    lane_half = (np.arange(kvw)[None, :] // hd) == (np.arange(n_heads)[:, None] // (n_heads // n_kv))
    qx = jnp.where(jnp.asarray(lane_half)[None], jnp.concatenate([q_s, q_s], axis=-1), 0).astype(BF16)
    kv_new = _rows(kv_all, s0, db).reshape(db, 1, 2 * kvw)
    glu_new = _rows(glu_all, s0, db).reshape(db, 1, ch)
    bidx_s = jnp.asarray(_bucket_or_masked(w_buf - np.arange(w_buf)[None, :], n_buckets))
    o_s, conv_s, new_k_s, new_v_s, new_conv_s = _sample_mix(
        rel_bias, sinks2, qx, kv_new, glu_new, cache_k.reshape(db, w_buf, kvw), cache_v.reshape(db, w_buf, kvw),
        state_conv, bidx_s, w_dw, b_dw2, ln_g2, ln_b2)
    o_s = jnp.where(jnp.asarray(lane_half)[None], o_s, 0.0).reshape(db, n_heads, n_kv, hd).sum(axis=2)
    attn_small = jnp.concatenate(
        [jnp.zeros((n_meta, aw), BF16), o_s.reshape(db, aw).astype(BF16), jnp.zeros((t - n_meta - db, aw), BF16)],
        axis=0)
    conv_small = jnp.concatenate(
        [jnp.zeros((n_meta, ch), BF16), conv_s.reshape(db, ch).astype(BF16), jnp.zeros((t - n_meta - db, ch), BF16)],
        axis=0)

    y_main, y_small = _post_ffn(rel_bias, sinks2, h_all, attn_small, conv_small, ga_all, gc_all, row(ffn2_norm),
                                q_all, kv_all, kv_meta, bidx, glu_all, glu_meta, w_dw, b_dw2, ln_g2, ln_b2,
                                w_attn_out, w_conv_out, w_out, ffn2_w1, ffn2_w3, ffn2_w2, m_rows, seq, n_meta)

    kv_t = jnp.stack([_rows(kv_all, (s + 1) * seq - WINDOW, WINDOW) for s in range(n_b)])
    glu_t = jnp.stack([_rows(glu_all, (s + 1) * seq - n_st, n_st) for s in range(n_b)])
    return (
        y_main.reshape(n_b, seq, d),
        _rows(y_small, n_meta, db).reshape(db, 1, d),
        kv_t[..., :kvw].reshape(n_b, WINDOW, n_kv, hd),
        kv_t[..., kvw:].reshape(n_b, WINDOW, n_kv, hd),
        glu_t,
        new_k_s.reshape(db, w_buf, n_kv, hd),
        new_v_s.reshape(db, w_buf, n_kv, hd),
        new_conv_s,
    )
```

```python
import functools
import math

import jax
import jax.numpy as jnp
import numpy as np
from jax import lax
from jax.experimental import pallas as pl
from jax.experimental.pallas import tpu as pltpu

F32 = jnp.float32
BF16 = jnp.bfloat16

EPS = 1e-6
NEG = -1e30
WINDOW = 128
REL_MAX_DIST = 128
HEAD_DIM = 64
LANES = 128
ROW_TILE = 256
ATTN_TILE = 512
FF_CHUNK = 256
N_WSTEPS = 16
HALO = 32
SAMPLE_BLOCK = 8
VMEM_LIMIT = 56 * 1024 * 1024


def _dot(a, b):
    return jnp.dot(a, b, preferred_element_type=F32)


def _dot_t(a, b):
    return lax.dot_general(a, b, (((1,), (1,)), ((), ())), preferred_element_type=F32)


def _rms(x, g):
    return x * lax.rsqrt(jnp.mean(x * x, axis=-1, keepdims=True) + EPS) * g


def _pair_rms(x, g2, lo_mask):
    sq = x * x
    lo = jnp.sum(jnp.where(lo_mask, sq, 0.0), axis=-1, keepdims=True) * (1.0 / HEAD_DIM)
    hi = jnp.sum(jnp.where(lo_mask, 0.0, sq), axis=-1, keepdims=True) * (1.0 / HEAD_DIM)
    r = jnp.where(lo_mask, lax.rsqrt(lo + EPS), lax.rsqrt(hi + EPS))
    return x * r * g2


def _dep_zero(row):
    u = pltpu.bitcast(row, jnp.uint32)
    return pltpu.bitcast((u >> 16) >> 16, F32)


def _swiglu(xn, w1b, w3b, w2b, hid_ref, fillers=()):
    n_ff = w1b.shape[1]
    n_chunks = n_ff // FF_CHUNK
    due = {}
    for c in range(n_chunks):
        sl = slice(c * FF_CHUNK, (c + 1) * FF_CHUNK)
        a = _dot(xn, w1b[:, sl])
        b = _dot(xn, w3b[:, sl])
        for z in due.pop(c, ()):
            b = b + jnp.concatenate([z] * (FF_CHUNK // LANES), axis=1)
        for issue, when, thunk in fillers:
            if issue == c:
                due.setdefault(min(when, n_chunks), []).append(_dep_zero(thunk()))
        hid_ref[:, sl] = (a * jax.nn.sigmoid(a) * b).astype(BF16)
    out = _dot(hid_ref[...], w2b[...])
    for z in due.pop(n_chunks, ()):
        out = out + jnp.concatenate([z] * (out.shape[1] // LANES), axis=1)
    assert not due
    return out


def _store_chunk(dst, src, i):
    rows = src.shape[0]
    r = pl.multiple_of(i * rows, 16)
    dst[pl.ds(r, rows), :] = src[...].astype(BF16)


def _bias_table(bidx, rb_ref, head):
    tab = jnp.full(bidx.shape, NEG, F32)
    for b in range(rb_ref.shape[0]):
        tab = jnp.where(bidx == b, rb_ref[b, head], tab)
    return tab


def _ln_swish(y, g, b):
    mu = jnp.mean(y, axis=-1, keepdims=True)
    yc = y - mu
    var = jnp.mean(yc * yc, axis=-1, keepdims=True)
    z = yc * lax.rsqrt(var + EPS) * g + b
    return z * jax.nn.sigmoid(z)


def _ffn_inproj_body(xc_ref, xn_ref, xs_ref, g1_ref, gm_ref, gq_ref, gk_ref, w1c, w3c, w2c, winc,
                     h_ref, q_ref, kv_ref, glu_ref, ga_ref, gc_ref,
                     w1b, w3b, w2b, winb, hid_ref, xn_st, u_st, *, n_main, splits):
    i = pl.program_id(0)

    @pl.when(i < N_WSTEPS)
    def _():
        _store_chunk(w1b, w1c, i)
        _store_chunk(w3b, w3c, i)
        _store_chunk(w2b, w2c, i)
        _store_chunk(winb, winc, i)

    @pl.when(i == 0)
    def _():
        xn_st[...] = jnp.zeros_like(xn_st)
        u_st[...] = jnp.zeros_like(u_st)

    @pl.when(i >= N_WSTEPS - 1)
    def _():
        s = i - N_WSTEPS

        u = u_st[...]
        o_q, o_k, o_v, o_a, o_b, o_ga, o_gc, o_end = splits
        lo_mask = lax.broadcasted_iota(jnp.int32, (1, LANES), 1) < HEAD_DIM
        zq = _dot(u, winb[:, o_q:o_k])
        for p in range((o_k - o_q) // LANES):
            sl = slice(p * LANES, (p + 1) * LANES)
            q_ref[:, sl] = (_pair_rms(zq[:, sl], gq_ref[...], lo_mask) * (HEAD_DIM ** -0.5)).astype(BF16)
        zkv = _dot(u, winb[:, o_k:o_a])
        kv_ref[:, :LANES] = _pair_rms(zkv[:, :LANES], gk_ref[...], lo_mask)
        kv_ref[:, LANES:] = zkv[:, LANES:]
        za = _dot(u, winb[:, o_a:o_b])
        zb = _dot(u, winb[:, o_b:o_ga])
        glu_ref[...] = za * jax.nn.sigmoid(zb)
        ga_ref[...] = jax.nn.sigmoid(_dot(u, winb[:, o_ga:o_gc]))
        gc_ref[...] = jax.nn.sigmoid(_dot(u, winb[:, o_gc:o_end]))

        x = jnp.where(s < n_main, xc_ref[...], xs_ref[...])
        h = x + 0.5 * _swiglu(xn_st[...], w1b, w3b, w2b, hid_ref)
        h_ref[...] = h
        u_st[...] = _rms(h, gm_ref[...]).astype(BF16)

        x_next = jnp.where(s + 1 < n_main, xn_ref[...], xs_ref[...])
        xn_st[...] = _rms(x_next, g1_ref[...]).astype(BF16)


def _ffn_inproj(x_main, x_small, g1, gm, gq2, gk2, w1, w3, w2, w_in, splits):
    m_rows, d = x_main.shape
    n_ff = w1.shape[1]
    n_main = m_rows // ROW_TILE
    r_rows = m_rows + ROW_TILE
    t = ROW_TILE
    o_q, o_k, o_v, o_a, o_b, o_ga, o_gc, o_end = splits
    assert o_v - o_k == LANES and o_a - o_v == LANES and (o_k - o_q) % LANES == 0

    def cur_idx(i):
        return (jnp.clip(i - N_WSTEPS, 0, n_main - 1), 0)

    def next_idx(i):
        return (jnp.clip(i - N_WSTEPS + 1, 0, n_main - 1), 0)

    def h_idx(i):
        return (jnp.clip(i - N_WSTEPS, 0, n_main), 0)

    def proj_idx(i):
        return (jnp.clip(i - N_WSTEPS - 1, 0, n_main), 0)

    def w_idx(i):
        return (jnp.minimum(i, N_WSTEPS - 1), 0)

    const = lambda i: (0, 0)
    outs = [
        (jax.ShapeDtypeStruct((r_rows, d), F32), h_idx),
        (jax.ShapeDtypeStruct((r_rows, o_k - o_q), BF16), proj_idx),
        (jax.ShapeDtypeStruct((r_rows, o_a - o_k), F32), proj_idx),
        (jax.ShapeDtypeStruct((r_rows, o_b - o_a), F32), proj_idx),
        (jax.ShapeDtypeStruct((r_rows, o_gc - o_ga), F32), proj_idx),
        (jax.ShapeDtypeStruct((r_rows, o_end - o_gc), F32), proj_idx),
    ]
    return pl.pallas_call(
        functools.partial(_ffn_inproj_body, n_main=n_main, splits=splits),
        grid=(N_WSTEPS + n_main + 2,),
        in_specs=[
            pl.BlockSpec((t, d), cur_idx),
            pl.BlockSpec((t, d), next_idx),
            pl.BlockSpec((t, d), const),
            pl.BlockSpec((1, d), const),
            pl.BlockSpec((1, d), const),
            pl.BlockSpec((1, LANES), const),
            pl.BlockSpec((1, LANES), const),
            pl.BlockSpec((d // N_WSTEPS, n_ff), w_idx),
            pl.BlockSpec((d // N_WSTEPS, n_ff), w_idx),
            pl.BlockSpec((n_ff // N_WSTEPS, d), w_idx),
            pl.BlockSpec((d // N_WSTEPS, o_end), w_idx),
        ],
        out_specs=[pl.BlockSpec((t, s.shape[1]), idx) for s, idx in outs],
        out_shape=[s for s, _ in outs],
        scratch_shapes=[
            pltpu.VMEM((d, n_ff), BF16),
            pltpu.VMEM((d, n_ff), BF16),
            pltpu.VMEM((n_ff, d), BF16),
            pltpu.VMEM((d, o_end), BF16),
            pltpu.VMEM((t, n_ff), BF16),
            pltpu.VMEM((t, d), BF16),
            pltpu.VMEM((t, d), BF16),
        ],
        compiler_params=pltpu.CompilerParams(
            dimension_semantics=("arbitrary",), vmem_limit_bytes=VMEM_LIMIT),
        name="ffn1_inproj",
    )(x_main, x_main, x_small, g1, gm, gq2, gk2, w1, w3, w2, w_in)


def _prompt_attn_body(rb_ref, sink_ref, q_ref, kvc_ref, kvp_ref, kvm_ref, bidx_ref,
                      attn_ref, bias_sc, *, tiles_per_seq, n_meta, n_heads):
    i = pl.program_id(0)
    t = q_ref.shape[0]
    n_kv = 2
    grp = n_heads // n_kv

    @pl.when(i == 0)
    def _():
        for hd in range(n_heads):
            bias_sc[hd] = _bias_table(bidx_ref[...], rb_ref, hd)

    first = (i % tiles_per_seq) == 0
    lo_mask = lax.broadcasted_iota(jnp.int32, (1, LANES), 1) < HEAD_DIM
    kv_prev = jnp.where(first, kvm_ref[...], kvp_ref[...])
    kv = jnp.concatenate([kv_prev, kvc_ref[...]], axis=0)
    k, v = kv[:, :LANES], kv[:, LANES:]
    k_r = pltpu.roll(k, HEAD_DIM, axis=1)
    v_r = pltpu.roll(v, HEAD_DIM, axis=1)
    k_lo = [jnp.where(lo_mask, k, 0.0).astype(BF16), jnp.where(lo_mask, k_r, 0.0).astype(BF16)]
    k_hi = [jnp.where(lo_mask, 0.0, k_r).astype(BF16), jnp.where(lo_mask, 0.0, k).astype(BF16)]
    v_lo = [jnp.where(lo_mask, v, 0.0).astype(BF16), jnp.where(lo_mask, v_r, 0.0).astype(BF16)]
    v_hi = [jnp.where(lo_mask, 0.0, v_r).astype(BF16), jnp.where(lo_mask, 0.0, v).astype(BF16)]
    col = lax.broadcasted_iota(jnp.int32, (1, 2 * WINDOW), 1)
    lead_mask = jnp.where(first & (col < WINDOW - n_meta), NEG, 0.0)

    for qb in range(t // WINDOW):
        rows = slice(qb * WINDOW, (qb + 1) * WINDOW)
        keys = slice(qb * WINDOW, qb * WINDOW + 2 * WINDOW)
        for h in range(n_kv):
            k_st = jnp.concatenate([k_lo[h][keys], k_hi[h][keys]], axis=0)
            v_st = jnp.concatenate([v_lo[h][keys], v_hi[h][keys]], axis=0)
            c0 = h * grp * HEAD_DIM
            qq = jnp.concatenate([q_ref[rows, c0:c0 + LANES], q_ref[rows, c0 + LANES:c0 + 2 * LANES]], axis=0)
            s = _dot_t(qq, k_st)
            p_parts, inv_parts = [], []
            for g2 in range(2):
                p_row, inv_row = [], []
                for par in range(2):
                    hd = h * grp + 2 * g2 + par
                    sq = s[g2 * WINDOW:(g2 + 1) * WINDOW, par * 2 * WINDOW:(par + 1) * 2 * WINDOW] + bias_sc[hd]
                    if qb == 0:
                        sq = sq + lead_mask
                    sink = sink_ref[0, hd]
                    m = jnp.maximum(jnp.max(sq, axis=-1, keepdims=True), sink)
                    p = jnp.exp(sq - m)
                    den = jnp.sum(p, axis=-1, keepdims=True) + jnp.exp(sink - m)
                    p_row.append(p.astype(BF16))
                    inv_row.append(1.0 / den)
                p_parts.append(jnp.concatenate(p_row, axis=1))
                inv_parts.append(jnp.where(lo_mask, inv_row[0], inv_row[1]))
            pm = jnp.concatenate(p_parts, axis=0)
            o = _dot(pm, v_st)
            for g2 in range(2):
                c = c0 + g2 * LANES
                attn_ref[rows, c:c + LANES] = (o[g2 * WINDOW:(g2 + 1) * WINDOW] * inv_parts[g2]).astype(attn_ref.dtype)


def _prompt_attn(rel_bias, sinks2, q_all, kv_all, kv_meta, bidx, m_rows, seq, n_meta):
    t = ATTN_TILE
    n_heads = sinks2.shape[1]
    qw, kvw = q_all.shape[1], kv_all.shape[1]
    const = lambda i: (0, 0)
    row = lambda i: (i, 0)
    smem = pl.BlockSpec(memory_space=pltpu.SMEM)
    return pl.pallas_call(
        functools.partial(_prompt_attn_body, tiles_per_seq=seq // t, n_meta=n_meta, n_heads=n_heads),
        grid=(m_rows // t,),
        in_specs=[
            smem, smem,
            pl.BlockSpec((t, qw), row),
            pl.BlockSpec((t, kvw), row),
            pl.BlockSpec((WINDOW, kvw), lambda i: (jnp.maximum(i * (t // WINDOW) - 1, 0), 0)),
            pl.BlockSpec((WINDOW, kvw), const),
            pl.BlockSpec((WINDOW, 2 * WINDOW), const),
        ],
        out_specs=pl.BlockSpec((t, qw), row),
        out_shape=jax.ShapeDtypeStruct((m_rows, qw), BF16),
        scratch_shapes=[pltpu.VMEM((n_heads, WINDOW, 2 * WINDOW), F32)],
        compiler_params=pltpu.CompilerParams(dimension_semantics=("arbitrary",)),
        name="prompt_attn",
    )(rel_bias, sinks2, q_all, kv_all, kv_all, kv_meta, bidx)


def _sample_mix_body(rb_ref, sink_ref, qx_ref, kvn_ref, glun_ref, ck_ref, cv_ref, st_ref, bidx_ref,
                     wdw_ref, bdw_ref, lng_ref, lnb_ref,
                     o_ref, conv_ref, nk_ref, nv_ref, nst_ref, *, n_heads):
    w_buf = ck_ref.shape[1]
    n_st = st_ref.shape[1]
    bias = jnp.concatenate([_bias_table(bidx_ref[...], rb_ref, hd) for hd in range(n_heads)], axis=0)
    hrow = lax.broadcasted_iota(jnp.int32, (n_heads, 1), 0)
    sink = jnp.zeros((n_heads, 1), F32)
    bias_new = jnp.zeros((n_heads, 1), F32)
    for hd in range(n_heads):
        sink = jnp.where(hrow == hd, sink_ref[0, hd], sink)
        bias_new = jnp.where(hrow == hd, rb_ref[0, hd], bias_new)

    for bb in range(qx_ref.shape[0]):
        qb = qx_ref[bb]
        kn = kvn_ref[bb][:, :LANES]
        vn = kvn_ref[bb][:, LANES:]
        ck = ck_ref[bb]
        cv = cv_ref[bb]
        s_c = _dot_t(qb, ck.astype(BF16)) + bias
        s_n = jnp.sum(qb.astype(F32) * kn.astype(BF16).astype(F32), axis=-1, keepdims=True) + bias_new
        m = jnp.maximum(jnp.maximum(jnp.max(s_c, axis=-1, keepdims=True), s_n), sink)
        p_c = jnp.exp(s_c - m)
        p_n = jnp.exp(s_n - m)
        den = jnp.sum(p_c, axis=-1, keepdims=True) + p_n + jnp.exp(sink - m)
        o = _dot(p_c.astype(BF16), cv.astype(BF16))
        o = o + p_n.astype(BF16).astype(F32) * vn.astype(BF16).astype(F32)
        o_ref[bb] = o / den
        nk_ref[bb, 0:w_buf - 1, :] = ck_ref[bb, 1:w_buf, :]
        nk_ref[bb, w_buf - 1:w_buf, :] = kn
        nv_ref[bb, 0:w_buf - 1, :] = cv_ref[bb, 1:w_buf, :]
        nv_ref[bb, w_buf - 1:w_buf, :] = vn
        g_new = glun_ref[bb]
        y = (jnp.sum(st_ref[bb] * wdw_ref[0:n_st, :], axis=0, keepdims=True)
             + g_new * wdw_ref[n_st:n_st + 1, :] + bdw_ref[...])
        conv_ref[bb] = _ln_swish(y, lng_ref[...], lnb_ref[...])
        nst_ref[bb, 0:n_st - 1, :] = st_ref[bb, 1:n_st, :]
        nst_ref[bb, n_st - 1:n_st, :] = g_new


def _sample_mix(rel_bias, sinks2, qx, kv_new, glu_new, cache_k, cache_v, state_conv, bidx_s, w_dw, b_dw, ln_g, ln_b):
    db, w_buf, kw = cache_k.shape
    n_st, ch = state_conv.shape[1], state_conv.shape[2]
    n_heads = sinks2.shape[1]
    sb = SAMPLE_BLOCK
    blk = lambda i: (i, 0, 0)
    const = lambda i: (0, 0)
    smem = pl.BlockSpec(memory_space=pltpu.SMEM)
    return pl.pallas_call(
        functools.partial(_sample_mix_body, n_heads=n_heads),
        grid=(db // sb,),
        in_specs=[
            smem, smem,
            pl.BlockSpec((sb, n_heads, kw), blk),
            pl.BlockSpec((sb, 1, 2 * kw), blk),
            pl.BlockSpec((sb, 1, ch), blk),
            pl.BlockSpec((sb, w_buf, kw), blk),
            pl.BlockSpec((sb, w_buf, kw), blk),
            pl.BlockSpec((sb, n_st, ch), blk),
            pl.BlockSpec((1, w_buf), const),
            pl.BlockSpec(w_dw.shape, const),
            pl.BlockSpec((1, ch), const),
            pl.BlockSpec((1, ch), const),
            pl.BlockSpec((1, ch), const),
        ],
        out_specs=[
            pl.BlockSpec((sb, n_heads, kw), blk),
            pl.BlockSpec((sb, 1, ch), blk),
            pl.BlockSpec((sb, w_buf, kw), blk),
            pl.BlockSpec((sb, w_buf, kw), blk),
            pl.BlockSpec((sb, n_st, ch), blk),
        ],
        out_shape=[
            jax.ShapeDtypeStruct((db, n_heads, kw), F32),
            jax.ShapeDtypeStruct((db, 1, ch), F32),
            jax.ShapeDtypeStruct((db, w_buf, kw), F32),
            jax.ShapeDtypeStruct((db, w_buf, kw), F32),
            jax.ShapeDtypeStruct((db, n_st, ch), F32),
        ],
        compiler_params=pltpu.CompilerParams(dimension_semantics=("arbitrary",)),
        name="sample_mix",
    )(rel_bias, sinks2, qx, kv_new, glu_new, cache_k, cache_v, state_conv, bidx_s, w_dw, b_dw, ln_g, ln_b)


def _conv_runs(cb_st, glu_ref, glum_ref, y_sc, wdw_ref, bdw_ref, first, n_runs):
    t = y_sc.shape[0]
    conv_w = wdw_ref.shape[0]
    n_ch = y_sc.shape[1]
    off = HALO - (conv_w - 1)
    n_a = (off + conv_w - 1) // 8 + 1
    units = [(lc, g) for lc in range(n_ch // LANES) for g in range(t // 8)]
    sizes = [len(units) // n_runs + (1 if r < len(units) % n_runs else 0) for r in range(n_runs)]
    staged = []

    def run(mine):
        if not staged:
            cb_st[0:HALO, :] = jnp.where(first, glum_ref[...], cb_st[0:HALO, :])
            cb_st[HALO:HALO + t, :] = glu_ref[...]
            staged.append(True)
        zs = {}

        def z(lc, s, g):
            if (lc, s, g) not in zs:
                ls = slice(lc * LANES, (lc + 1) * LANES)
                acc = None
                for a in range(n_a):
                    w = 8 * a + s - off
                    if 0 <= w < conv_w:
                        term = cb_st[8 * (g + a):8 * (g + a) + 8, ls] * wdw_ref[w:w + 1, ls]
                        acc = term if acc is None else acc + term
                zs[(lc, s, g)] = acc
            return zs[(lc, s, g)]

        dep = None
        for lc, g in mine:
            ls = slice(lc * LANES, (lc + 1) * LANES)
            acc = jnp.broadcast_to(bdw_ref[:, ls], (8, LANES)) + z(lc, 0, g)
            if dep is not None:
                acc = acc + dep
            for s in range(1, 8):
                acc = acc + jnp.concatenate([z(lc, s, g), z(lc, s, g + 1)], axis=0)[s:s + 8, :]
            y_sc[8 * g:8 * g + 8, ls] = acc
            dep = _dep_zero(acc)
        return dep[0:1, :]

    out, k = [], 0
    for n in sizes:
        out.append(functools.partial(run, units[k:k + n]))
        k += n
    return out


def _post_ffn_body(h_ref, am_ref, as_ref, cs_ref, ga_ref, gc_ref, g2_ref, glu_ref, glum_ref,
                   wdw_ref, bdw_ref, lng_ref, lnb_ref,
                   waoc, wcoc, woutc, w1c, w3c, w2c,
                   ym_ref, ys_ref,
                   waob, wcob, woutb, w1b, w3b, w2b, hid_ref, cb_st, y_sc, conv_st, *, n_main, tiles_per_seq):
    i = pl.program_id(0)
    t_rows = y_sc.shape[0]

    @pl.when(i < N_WSTEPS)
    def _():
        _store_chunk(waob, waoc, i)
        _store_chunk(wcob, wcoc, i)
        _store_chunk(woutb, woutc, i)
        _store_chunk(w1b, w1c, i)
        _store_chunk(w3b, w3c, i)
        _store_chunk(w2b, w2c, i)

    @pl.when(i == 0)
    def _():
        cb_st[...] = jnp.zeros_like(cb_st)
        conv_st[...] = jnp.zeros_like(conv_st)

    @pl.when(i >= N_WSTEPS - 1)
    def _():
        t = i - N_WSTEPS
        tn = t + 1
        on_main = t < n_main
        first = (tn < n_main) & (lax.rem(tn, tiles_per_seq) == 0)
        conv_f = _conv_runs(cb_st, glu_ref, glum_ref, y_sc, wdw_ref, bdw_ref, first, w1b.shape[1] // FF_CHUNK)
        fillers = [(c, c + 1, f) for c, f in enumerate(conv_f)]

        at = jnp.where(on_main, am_ref[...], as_ref[...])
        cv = jnp.where(on_main, conv_st[...], cs_ref[...])
        a = _dot(at, waob[...])
        c = _dot(cv, wcob[...])
        mix = (ga_ref[...] * a + gc_ref[...] * c).astype(BF16)
        h2 = h_ref[...] + _dot(mix, woutb[...])
        xn = _rms(h2, g2_ref[...]).astype(BF16)
        y = h2 + 0.5 * _swiglu(xn, w1b, w3b, w2b, hid_ref, fillers)

        conv_st[...] = _ln_swish(y_sc[...], lng_ref[...], lnb_ref[...]).astype(conv_st.dtype)
        cb_st[0:HALO, :] = cb_st[t_rows:t_rows + HALO, :]

        @pl.when((t >= 0) & on_main)
        def _():
            ym_ref[...] = y

        @pl.when(t == n_main)
        def _():
            ys_ref[...] = y


def _post_ffn(h_all, attn_main, attn_small, conv_small, ga_all, gc_all, g2, glu_all, glu_meta,
              w_dw, b_dw, ln_g, ln_b, w_ao, w_co, w_out, w1, w3, w2, seq):
    m_rows, aw = attn_main.shape
    ch = glu_all.shape[1]
    d = h_all.shape[1]
    n_ff = w1.shape[1]
    t = ROW_TILE
    n_main = m_rows // t

    def main_idx(i):
        return (jnp.clip(i - N_WSTEPS, 0, n_main - 1), 0)

    def next_idx(i):
        return (jnp.clip(i - N_WSTEPS + 1, 0, n_main - 1), 0)

    def row_idx(i):
        return (jnp.maximum(i - N_WSTEPS, 0), 0)

    def w_idx(i):
        return (jnp.minimum(i, N_WSTEPS - 1), 0)

    const = lambda i: (0, 0)
    return pl.pallas_call(
        functools.partial(_post_ffn_body, n_main=n_main, tiles_per_seq=seq // t),
        grid=(N_WSTEPS + n_main + 1,),
        in_specs=[
            pl.BlockSpec((t, d), row_idx),
            pl.BlockSpec((t, aw), main_idx),
            pl.BlockSpec((t, aw), const),
            pl.BlockSpec((t, ch), const),
            pl.BlockSpec((t, d), row_idx),
            pl.BlockSpec((t, d), row_idx),
            pl.BlockSpec((1, d), const),
            pl.BlockSpec((t, ch), next_idx),
            pl.BlockSpec((HALO, ch), const),
            pl.BlockSpec(w_dw.shape, const),
            pl.BlockSpec((1, ch), const),
            pl.BlockSpec((1, ch), const),
            pl.BlockSpec((1, ch), const),
            pl.BlockSpec((aw // N_WSTEPS, d), w_idx),
            pl.BlockSpec((ch // N_WSTEPS, d), w_idx),
            pl.BlockSpec((d // N_WSTEPS, d), w_idx),
            pl.BlockSpec((d // N_WSTEPS, n_ff), w_idx),
            pl.BlockSpec((d // N_WSTEPS, n_ff), w_idx),
            pl.BlockSpec((n_ff // N_WSTEPS, d), w_idx),
        ],
        out_specs=[pl.BlockSpec((t, d), main_idx), pl.BlockSpec((t, d), const)],
        out_shape=[jax.ShapeDtypeStruct((m_rows, d), F32), jax.ShapeDtypeStruct((t, d), F32)],
        scratch_shapes=[
            pltpu.VMEM((aw, d), BF16),
            pltpu.VMEM((ch, d), BF16),
            pltpu.VMEM((d, d), BF16),
            pltpu.VMEM((d, n_ff), BF16),
            pltpu.VMEM((d, n_ff), BF16),
            pltpu.VMEM((n_ff, d), BF16),
            pltpu.VMEM((t, n_ff), BF16),
            pltpu.VMEM((HALO + t, ch), F32),
            pltpu.VMEM((t, ch), F32),
            pltpu.VMEM((t, ch), BF16),
        ],
        compiler_params=pltpu.CompilerParams(
            dimension_semantics=("arbitrary",), vmem_limit_bytes=VMEM_LIMIT),
        name="post_ffn2",
    )(h_all, attn_main, attn_small, conv_small, ga_all, gc_all, g2, glu_all, glu_meta,
      w_dw, b_dw, ln_g, ln_b, w_ao, w_co, w_out, w1, w3, w2)


def _t5_bucket(dist, n_buckets):
    max_exact = n_buckets // 2
    d = np.maximum(dist, 0)
    ratio = (np.log(np.maximum(d, 1).astype(np.float32) / np.float32(max_exact))
             / np.float32(math.log(REL_MAX_DIST / max_exact)))
    large = np.minimum(max_exact + (ratio * np.float32(n_buckets - max_exact)).astype(np.int32), n_buckets - 1)
    return np.where(d < max_exact, d, large).astype(np.int32)


def _bucket_or_masked(dist, n_buckets):
    ok = (dist >= 0) & (dist < WINDOW)
    return np.where(ok, _t5_bucket(dist, n_buckets), -1).astype(np.int32)


def _rows(x, start, n):
    return lax.slice_in_dim(x, start, start + n, axis=0)


def kernel(x_prompt, x_sample, cache_k, cache_v, state_conv, meta_tokens, ffn1_norm, ffn1_w1, ffn1_w3, ffn1_w2, mix_norm, w_in, q_norm, k_norm, rel_bias, sinks, w_attn_out, w_dw, b_dw, conv_ln_g, conv_ln_b, w_conv_out, w_out, ffn2_norm, ffn2_w1, ffn2_w3, ffn2_w2):
    n_b, seq, d = x_prompt.shape
    db = x_sample.shape[0]
    n_meta = meta_tokens.shape[0]
    n_heads = sinks.shape[0]
    w_buf, n_kv, hd = cache_k.shape[1], cache_k.shape[2], cache_k.shape[3]
    ch = w_dw.shape[1]
    n_st = state_conv.shape[1]
    n_buckets = rel_bias.shape[0]
    aw, kvw = n_heads * hd, n_kv * hd
    t = ROW_TILE
    m_rows = n_b * seq
    assert hd == HEAD_DIM and kvw == LANES and n_kv == 2 and n_heads == 8 and w_buf == WINDOW
    assert x_sample.shape[1] == 1 and seq % t == 0 and seq % ATTN_TILE == 0
    assert n_meta + db <= t and db % SAMPLE_BLOCK == 0
    assert n_meta <= HALO and n_meta <= WINDOW and n_st == w_dw.shape[0] - 1 and n_st <= HALO
    splits = tuple(int(v) for v in np.cumsum([0, aw, kvw, kvw, ch, ch, d, d]))
    assert splits[-1] == w_in.shape[1]

    row = lambda v: v.reshape(1, -1)
    x_main = x_prompt.reshape(m_rows, d)
    x_small = jnp.concatenate(
        [meta_tokens, x_sample.reshape(db, d), jnp.zeros((t - n_meta - db, d), F32)], axis=0)
    gq2 = row(jnp.concatenate([q_norm, q_norm]))
    gk2 = row(jnp.concatenate([k_norm, k_norm]))

    h_all, q_all, kv_all, glu_all, ga_all, gc_all = _ffn_inproj(
        x_main, x_small, row(ffn1_norm), row(mix_norm), gq2, gk2, ffn1_w1, ffn1_w3, ffn1_w2, w_in, splits)

    sinks2 = row(sinks)
    b_dw2, ln_g2, ln_b2 = row(b_dw), row(conv_ln_g), row(conv_ln_b)

    kv_meta = jnp.concatenate([jnp.zeros((WINDOW - n_meta, 2 * kvw), F32), _rows(kv_all, m_rows, n_meta)], axis=0)
    glu_meta = jnp.concatenate([jnp.zeros((HALO - n_meta, ch), F32), _rows(glu_all, m_rows, n_meta)], axis=0)
    dist = np.arange(WINDOW)[:, None] + WINDOW - np.arange(2 * WINDOW)[None, :]
    bidx = jnp.asarray(_bucket_or_masked(dist, n_buckets))
    attn_main = _prompt_attn(rel_bias, sinks2, q_all, kv_all, kv_meta, bidx, m_rows, seq, n_meta)

    s0 = m_rows + n_meta
    q_s = _rows(q_all, s0, db).reshape(db, n_heads, hd)
    lane_half = (np.arange(kvw)[None, :] // hd) == (np.arange(n_heads)[:, None] // (n_heads // n_kv))
    qx = jnp.where(jnp.asarray(lane_half)[None], jnp.concatenate([q_s, q_s], axis=-1), 0).astype(BF16)
    kv_new = _rows(kv_all, s0, db).reshape(db, 1, 2 * kvw)
    glu_new = _rows(glu_all, s0, db).reshape(db, 1, ch)
    bidx_s = jnp.asarray(_bucket_or_masked(w_buf - np.arange(w_buf)[None, :], n_buckets))
    o_s, conv_s, new_k_s, new_v_s, new_conv_s = _sample_mix(
        rel_bias, sinks2, qx, kv_new, glu_new, cache_k.reshape(db, w_buf, kvw), cache_v.reshape(db, w_buf, kvw),
        state_conv, bidx_s, w_dw, b_dw2, ln_g2, ln_b2)
    o_s = jnp.where(jnp.asarray(lane_half)[None], o_s, 0.0).reshape(db, n_heads, n_kv, hd).sum(axis=2)
    attn_small = jnp.concatenate(
        [jnp.zeros((n_meta, aw), BF16), o_s.reshape(db, aw).astype(BF16), jnp.zeros((t - n_meta - db, aw), BF16)],
        axis=0)
    conv_small = jnp.concatenate(
        [jnp.zeros((n_meta, ch), BF16), conv_s.reshape(db, ch).astype(BF16), jnp.zeros((t - n_meta - db, ch), BF16)],
        axis=0)

    y_main, y_small = _post_ffn(h_all, attn_main, attn_small, conv_small, ga_all, gc_all, row(ffn2_norm),
                                glu_all, glu_meta, w_dw, b_dw2, ln_g2, ln_b2,
                                w_attn_out, w_conv_out, w_out, ffn2_w1, ffn2_w3, ffn2_w2, seq)

    kv_t = jnp.stack([_rows(kv_all, (s + 1) * seq - WINDOW, WINDOW) for s in range(n_b)])
    glu_t = jnp.stack([_rows(glu_all, (s + 1) * seq - n_st, n_st) for s in range(n_b)])
    return (
        y_main.reshape(n_b, seq, d),
        _rows(y_small, n_meta, db).reshape(db, 1, d),
        kv_t[..., :kvw].reshape(n_b, WINDOW, n_kv, hd),
        kv_t[..., kvw:].reshape(n_b, WINDOW, n_kv, hd),
        glu_t,
        new_k_s.reshape(db, w_buf, n_kv, hd),
        new_v_s.reshape(db, w_buf, n_kv, hd),
        new_conv_s,
    )
```

```python
import functools
import math

import jax
import jax.numpy as jnp
import numpy as np
from jax import lax
from jax.experimental import pallas as pl
from jax.experimental.pallas import tpu as pltpu

F32 = jnp.float32
BF16 = jnp.bfloat16

EPS = 1e-6
NEG = -1e30
WINDOW = 128
REL_MAX_DIST = 128
HEAD_DIM = 64
LANES = 128
ROW_TILE = 256
ATTN_TILE = 512
FF_CHUNK = 256
N_WSTEPS = 16
HALO = 32
SAMPLE_BLOCK = 8
VMEM_LIMIT = 56 * 1024 * 1024


def _dot(a, b):
    return jnp.dot(a, b, preferred_element_type=F32)


def _dot_t(a, b):
    return lax.dot_general(a, b, (((1,), (1,)), ((), ())), preferred_element_type=F32)


def _rms(x, g):
    return x * lax.rsqrt(jnp.mean(x * x, axis=-1, keepdims=True) + EPS) * g


def _pair_rms(x, g2, lo_mask):
    sq = x * x
    lo = jnp.sum(jnp.where(lo_mask, sq, 0.0), axis=-1, keepdims=True) * (1.0 / HEAD_DIM)
    hi = jnp.sum(jnp.where(lo_mask, 0.0, sq), axis=-1, keepdims=True) * (1.0 / HEAD_DIM)
    r = jnp.where(lo_mask, lax.rsqrt(lo + EPS), lax.rsqrt(hi + EPS))
    return x * r * g2


def _dep_zero(row):
    u = pltpu.bitcast(row, jnp.uint32)
    return pltpu.bitcast((u >> 16) >> 16, F32)


def _swiglu(xn, w1b, w3b, w2b, hid_ref, fillers=()):
    n_ff = w1b.shape[1]
    n_chunks = n_ff // FF_CHUNK
    due = {}
    for c in range(n_chunks):
        sl = slice(c * FF_CHUNK, (c + 1) * FF_CHUNK)
        a = _dot(xn, w1b[:, sl])
        b = _dot(xn, w3b[:, sl])
        for z in due.pop(c, ()):
            b = b + jnp.concatenate([z] * (FF_CHUNK // LANES), axis=1)
        for issue, when, thunk in fillers:
            if issue == c:
                due.setdefault(min(when, n_chunks), []).append(_dep_zero(thunk()))
        hid_ref[:, sl] = (a * jax.nn.sigmoid(a) * b).astype(BF16)
    out = _dot(hid_ref[...], w2b[...])
    for z in due.pop(n_chunks, ()):
        out = out + jnp.concatenate([z] * (out.shape[1] // LANES), axis=1)
    assert not due
    return out


def _store_chunk(dst, src, i):
    rows = src.shape[0]
    r = pl.multiple_of(i * rows, 16)
    dst[pl.ds(r, rows), :] = src[...].astype(BF16)


def _bias_table(bidx, rb_ref, head):
    tab = jnp.full(bidx.shape, NEG, F32)
    for b in range(rb_ref.shape[0]):
        tab = jnp.where(bidx == b, rb_ref[b, head], tab)
    return tab


def _ln_swish(y, g, b):
    mu = jnp.mean(y, axis=-1, keepdims=True)
    yc = y - mu
    var = jnp.mean(yc * yc, axis=-1, keepdims=True)
    z = yc * lax.rsqrt(var + EPS) * g + b
    return z * jax.nn.sigmoid(z)


def _ffn_inproj_body(xc_ref, xn_ref, xs_ref, g1_ref, gm_ref, gq_ref, gk_ref, w1c, w3c, w2c, winc,
                     h_ref, q_ref, kv_ref, glu_ref, ga_ref, gc_ref,
                     w1b, w3b, w2b, winb, hid_ref, xn_st, u_st, *, n_main, splits):
    i = pl.program_id(0)

    @pl.when(i < N_WSTEPS)
    def _():
        _store_chunk(w1b, w1c, i)
        _store_chunk(w3b, w3c, i)
        _store_chunk(w2b, w2c, i)
        _store_chunk(winb, winc, i)

    s = i - N_WSTEPS

    def project():
        u = u_st[...]
        o_q, o_k, o_v, o_a, o_b, o_ga, o_gc, o_end = splits
        lo_mask = lax.broadcasted_iota(jnp.int32, (1, LANES), 1) < HEAD_DIM
        zq = _dot(u, winb[:, o_q:o_k])
        for p in range((o_k - o_q) // LANES):
            sl = slice(p * LANES, (p + 1) * LANES)
            q_ref[:, sl] = (_pair_rms(zq[:, sl], gq_ref[...], lo_mask) * (HEAD_DIM ** -0.5)).astype(BF16)
        zkv = _dot(u, winb[:, o_k:o_a])
        kv_ref[:, :LANES] = _pair_rms(zkv[:, :LANES], gk_ref[...], lo_mask)
        kv_ref[:, LANES:] = zkv[:, LANES:]
        za = _dot(u, winb[:, o_a:o_b])
        zb = _dot(u, winb[:, o_b:o_ga])
        glu_ref[...] = za * jax.nn.sigmoid(zb)
        ga_ref[...] = jax.nn.sigmoid(_dot(u, winb[:, o_ga:o_gc]))
        gc_ref[...] = jax.nn.sigmoid(_dot(u, winb[:, o_gc:o_end]))

    def normalise_next():
        x_next = jnp.where(s + 1 < n_main, xn_ref[...], xs_ref[...])
        xn_st[...] = _rms(x_next, g1_ref[...]).astype(BF16)

    @pl.when(i == 0)
    def _():
        u_st[...] = jnp.zeros_like(u_st)

    @pl.when(s == -1)
    def _():
        normalise_next()

    @pl.when((s >= 0) & (s <= n_main))
    def _():
        project()
        x = jnp.where(s < n_main, xc_ref[...], xs_ref[...])
        h = x + 0.5 * _swiglu(xn_st[...], w1b, w3b, w2b, hid_ref)
        h_ref[...] = h
        u_st[...] = _rms(h, gm_ref[...]).astype(BF16)
        normalise_next()

    @pl.when(s == n_main + 1)
    def _():
        project()


def _ffn_inproj(x_main, x_small, g1, gm, gq2, gk2, w1, w3, w2, w_in, splits):
    m_rows, d = x_main.shape
    n_ff = w1.shape[1]
    n_main = m_rows // ROW_TILE
    r_rows = m_rows + ROW_TILE
    t = ROW_TILE
    o_q, o_k, o_v, o_a, o_b, o_ga, o_gc, o_end = splits
    assert o_v - o_k == LANES and o_a - o_v == LANES and (o_k - o_q) % LANES == 0

    def cur_idx(i):
        return (jnp.clip(i - N_WSTEPS, 0, n_main - 1), 0)

    def next_idx(i):
        return (jnp.clip(i - N_WSTEPS + 1, 0, n_main - 1), 0)

    def h_idx(i):
        return (jnp.clip(i - N_WSTEPS, 0, n_main), 0)

    def proj_idx(i):
        return (jnp.clip(i - N_WSTEPS - 1, 0, n_main), 0)

    def w_idx(i):
        return (jnp.minimum(i, N_WSTEPS - 1), 0)

    const = lambda i: (0, 0)
    outs = [
        (jax.ShapeDtypeStruct((r_rows, d), F32), h_idx),
        (jax.ShapeDtypeStruct((r_rows, o_k - o_q), BF16), proj_idx),
        (jax.ShapeDtypeStruct((r_rows, o_a - o_k), F32), proj_idx),
        (jax.ShapeDtypeStruct((r_rows, o_b - o_a), F32), proj_idx),
        (jax.ShapeDtypeStruct((r_rows, o_gc - o_ga), F32), proj_idx),
        (jax.ShapeDtypeStruct((r_rows, o_end - o_gc), F32), proj_idx),
    ]
    return pl.pallas_call(
        functools.partial(_ffn_inproj_body, n_main=n_main, splits=splits),
        grid=(N_WSTEPS + n_main + 2,),
        in_specs=[
            pl.BlockSpec((t, d), cur_idx),
            pl.BlockSpec((t, d), next_idx),
            pl.BlockSpec((t, d), const),
            pl.BlockSpec((1, d), const),
            pl.BlockSpec((1, d), const),
            pl.BlockSpec((1, LANES), const),
            pl.BlockSpec((1, LANES), const),
            pl.BlockSpec((d // N_WSTEPS, n_ff), w_idx),
            pl.BlockSpec((d // N_WSTEPS, n_ff), w_idx),
            pl.BlockSpec((n_ff // N_WSTEPS, d), w_idx),
            pl.BlockSpec((d // N_WSTEPS, o_end), w_idx),
        ],
        out_specs=[pl.BlockSpec((t, s.shape[1]), idx) for s, idx in outs],
        out_shape=[s for s, _ in outs],
        scratch_shapes=[
            pltpu.VMEM((d, n_ff), BF16),
            pltpu.VMEM((d, n_ff), BF16),
            pltpu.VMEM((n_ff, d), BF16),
            pltpu.VMEM((d, o_end), BF16),
            pltpu.VMEM((t, n_ff), BF16),
            pltpu.VMEM((t, d), BF16),
            pltpu.VMEM((t, d), BF16),
        ],
        compiler_params=pltpu.CompilerParams(
            dimension_semantics=("arbitrary",), vmem_limit_bytes=VMEM_LIMIT),
        name="ffn1_inproj",
    )(x_main, x_main, x_small, g1, gm, gq2, gk2, w1, w3, w2, w_in)


def _prompt_attn_body(rb_ref, sink_ref, q_ref, kvc_ref, kvp_ref, kvm_ref, bidx_ref,
                      attn_ref, bias_sc, *, tiles_per_seq, n_meta, n_heads):
    i = pl.program_id(0)
    t = q_ref.shape[0]
    n_kv = 2
    grp = n_heads // n_kv

    @pl.when(i == 0)
    def _():
        col = lax.broadcasted_iota(jnp.int32, (WINDOW, 2 * WINDOW), 1)
        for hd in range(n_heads):
            tab = _bias_table(bidx_ref[...], rb_ref, hd)
            bias_sc[0, hd] = tab
            bias_sc[1, hd] = jnp.where(col < WINDOW - n_meta, NEG, tab)

    first = (i % tiles_per_seq) == 0
    lead = jnp.where(first, 1, 0)
    lo_mask = lax.broadcasted_iota(jnp.int32, (1, LANES), 1) < HEAD_DIM
    kv_prev = jnp.where(first, kvm_ref[...], kvp_ref[...])
    kv = jnp.concatenate([kv_prev, kvc_ref[...]], axis=0)
    k, v = kv[:, :LANES], kv[:, LANES:]
    kb, vb = k.astype(BF16), v.astype(BF16)
    kr = pltpu.roll(k, HEAD_DIM, axis=1).astype(BF16)
    vr = pltpu.roll(v, HEAD_DIM, axis=1).astype(BF16)
    zero = jnp.zeros((), BF16)
    k_lo = [jnp.where(lo_mask, kb, zero), jnp.where(lo_mask, kr, zero)]
    k_hi = [jnp.where(lo_mask, zero, kr), jnp.where(lo_mask, zero, kb)]
    v_lo = [jnp.where(lo_mask, vb, zero), jnp.where(lo_mask, vr, zero)]
    v_hi = [jnp.where(lo_mask, zero, vr), jnp.where(lo_mask, zero, vb)]

    for qb in range(t // WINDOW):
        rows = slice(qb * WINDOW, (qb + 1) * WINDOW)
        keys = slice(qb * WINDOW, qb * WINDOW + 2 * WINDOW)
        tab = lead if qb == 0 else 0
        for h in range(n_kv):
            k_st = jnp.concatenate([k_lo[h][keys], k_hi[h][keys]], axis=0)
            v_st = jnp.concatenate([v_lo[h][keys], v_hi[h][keys]], axis=0)
            c0 = h * grp * HEAD_DIM
            qq = jnp.concatenate([q_ref[rows, c0:c0 + LANES], q_ref[rows, c0 + LANES:c0 + 2 * LANES]], axis=0)
            s = _dot_t(qq, k_st)
            p_parts, inv_parts = [], []
            for g2 in range(2):
                p_row, inv_row = [], []
                for par in range(2):
                    hd = h * grp + 2 * g2 + par
                    sq = s[g2 * WINDOW:(g2 + 1) * WINDOW, par * 2 * WINDOW:(par + 1) * 2 * WINDOW] + bias_sc[tab, hd]
                    sink = sink_ref[0, hd]
                    m = jnp.maximum(jnp.max(sq, axis=-1, keepdims=True), sink)
                    p = jnp.exp(sq - m)
                    den = jnp.sum(p, axis=-1, keepdims=True) + jnp.exp(sink - m)
                    p_row.append(p.astype(BF16))
                    inv_row.append(1.0 / den)
                p_parts.append(jnp.concatenate(p_row, axis=1))
                inv_parts.append(jnp.where(lo_mask, inv_row[0], inv_row[1]))
            pm = jnp.concatenate(p_parts, axis=0)
            o = _dot(pm, v_st)
            for g2 in range(2):
                c = c0 + g2 * LANES
                attn_ref[rows, c:c + LANES] = (o[g2 * WINDOW:(g2 + 1) * WINDOW] * inv_parts[g2]).astype(attn_ref.dtype)


def _prompt_attn(rel_bias, sinks2, q_all, kv_all, kv_meta, bidx, m_rows, seq, n_meta):
    t = ATTN_TILE
    n_heads = sinks2.shape[1]
    qw, kvw = q_all.shape[1], kv_all.shape[1]
    const = lambda i: (0, 0)
    row = lambda i: (i, 0)
    smem = pl.BlockSpec(memory_space=pltpu.SMEM)
    return pl.pallas_call(
        functools.partial(_prompt_attn_body, tiles_per_seq=seq // t, n_meta=n_meta, n_heads=n_heads),
        grid=(m_rows // t,),
        in_specs=[
            smem, smem,
            pl.BlockSpec((t, qw), row),
            pl.BlockSpec((t, kvw), row),
            pl.BlockSpec((WINDOW, kvw), lambda i: (jnp.maximum(i * (t // WINDOW) - 1, 0), 0)),
            pl.BlockSpec((WINDOW, kvw), const),
            pl.BlockSpec((WINDOW, 2 * WINDOW), const),
        ],
        out_specs=pl.BlockSpec((t, qw), row),
        out_shape=jax.ShapeDtypeStruct((m_rows, qw), BF16),
        scratch_shapes=[pltpu.VMEM((2, n_heads, WINDOW, 2 * WINDOW), F32)],
        compiler_params=pltpu.CompilerParams(dimension_semantics=("arbitrary",)),
        name="prompt_attn",
    )(rel_bias, sinks2, q_all, kv_all, kv_all, kv_meta, bidx)


def _sample_mix_body(rb_ref, sink_ref, qx_ref, kvn_ref, glun_ref, ck_ref, cv_ref, st_ref, bidx_ref,
                     wdw_ref, bdw_ref, lng_ref, lnb_ref,
                     o_ref, conv_ref, nk_ref, nv_ref, nst_ref, *, n_heads):
    w_buf = ck_ref.shape[1]
    n_st = st_ref.shape[1]
    bias = jnp.concatenate([_bias_table(bidx_ref[...], rb_ref, hd) for hd in range(n_heads)], axis=0)
    hrow = lax.broadcasted_iota(jnp.int32, (n_heads, 1), 0)
    sink = jnp.zeros((n_heads, 1), F32)
    bias_new = jnp.zeros((n_heads, 1), F32)
    for hd in range(n_heads):
        sink = jnp.where(hrow == hd, sink_ref[0, hd], sink)
        bias_new = jnp.where(hrow == hd, rb_ref[0, hd], bias_new)

    for bb in range(qx_ref.shape[0]):
        qb = qx_ref[bb]
        kn = kvn_ref[bb][:, :LANES]
        vn = kvn_ref[bb][:, LANES:]
        ck = ck_ref[bb]
        cv = cv_ref[bb]
        s_c = _dot_t(qb, ck.astype(BF16)) + bias
        s_n = jnp.sum(qb.astype(F32) * kn.astype(BF16).astype(F32), axis=-1, keepdims=True) + bias_new
        m = jnp.maximum(jnp.maximum(jnp.max(s_c, axis=-1, keepdims=True), s_n), sink)
        p_c = jnp.exp(s_c - m)
        p_n = jnp.exp(s_n - m)
        den = jnp.sum(p_c, axis=-1, keepdims=True) + p_n + jnp.exp(sink - m)
        o = _dot(p_c.astype(BF16), cv.astype(BF16))
        o = o + p_n.astype(BF16).astype(F32) * vn.astype(BF16).astype(F32)
        o_ref[bb] = o / den
        nk_ref[bb, 0:w_buf - 1, :] = ck_ref[bb, 1:w_buf, :]
        nk_ref[bb, w_buf - 1:w_buf, :] = kn
        nv_ref[bb, 0:w_buf - 1, :] = cv_ref[bb, 1:w_buf, :]
        nv_ref[bb, w_buf - 1:w_buf, :] = vn
        g_new = glun_ref[bb]
        y = (jnp.sum(st_ref[bb] * wdw_ref[0:n_st, :], axis=0, keepdims=True)
             + g_new * wdw_ref[n_st:n_st + 1, :] + bdw_ref[...])
        conv_ref[bb] = _ln_swish(y, lng_ref[...], lnb_ref[...])
        nst_ref[bb, 0:n_st - 1, :] = st_ref[bb, 1:n_st, :]
        nst_ref[bb, n_st - 1:n_st, :] = g_new


def _sample_mix(rel_bias, sinks2, qx, kv_new, glu_new, cache_k, cache_v, state_conv, bidx_s, w_dw, b_dw, ln_g, ln_b):
    db, w_buf, kw = cache_k.shape
    n_st, ch = state_conv.shape[1], state_conv.shape[2]
    n_heads = sinks2.shape[1]
    sb = SAMPLE_BLOCK
    blk = lambda i: (i, 0, 0)
    const = lambda i: (0, 0)
    smem = pl.BlockSpec(memory_space=pltpu.SMEM)
    return pl.pallas_call(
        functools.partial(_sample_mix_body, n_heads=n_heads),
        grid=(db // sb,),
        in_specs=[
            smem, smem,
            pl.BlockSpec((sb, n_heads, kw), blk),
            pl.BlockSpec((sb, 1, 2 * kw), blk),
            pl.BlockSpec((sb, 1, ch), blk),
            pl.BlockSpec((sb, w_buf, kw), blk),
            pl.BlockSpec((sb, w_buf, kw), blk),
            pl.BlockSpec((sb, n_st, ch), blk),
            pl.BlockSpec((1, w_buf), const),
            pl.BlockSpec(w_dw.shape, const),
            pl.BlockSpec((1, ch), const),
            pl.BlockSpec((1, ch), const),
            pl.BlockSpec((1, ch), const),
        ],
        out_specs=[
            pl.BlockSpec((sb, n_heads, kw), blk),
            pl.BlockSpec((sb, 1, ch), blk),
            pl.BlockSpec((sb, w_buf, kw), blk),
            pl.BlockSpec((sb, w_buf, kw), blk),
            pl.BlockSpec((sb, n_st, ch), blk),
        ],
        out_shape=[
            jax.ShapeDtypeStruct((db, n_heads, kw), F32),
            jax.ShapeDtypeStruct((db, 1, ch), F32),
            jax.ShapeDtypeStruct((db, w_buf, kw), F32),
            jax.ShapeDtypeStruct((db, w_buf, kw), F32),
            jax.ShapeDtypeStruct((db, n_st, ch), F32),
        ],
        compiler_params=pltpu.CompilerParams(dimension_semantics=("arbitrary",)),
        name="sample_mix",
    )(rel_bias, sinks2, qx, kv_new, glu_new, cache_k, cache_v, state_conv, bidx_s, w_dw, b_dw, ln_g, ln_b)


def _conv_runs(cb_st, glu_ref, glum_ref, y_sc, wdw_ref, bdw_ref, first, n_runs):
    t = y_sc.shape[0]
    conv_w = wdw_ref.shape[0]
    n_ch = y_sc.shape[1]
    off = HALO - (conv_w - 1)
    n_a = (off + conv_w - 1) // 8 + 1
    units = [(lc, g) for lc in range(n_ch // LANES) for g in range(t // 8)]
    sizes = [len(units) // n_runs + (1 if r < len(units) % n_runs else 0) for r in range(n_runs)]
    staged = []

    def run(mine):
        if not staged:
            cb_st[0:HALO, :] = jnp.where(first, glum_ref[...], cb_st[0:HALO, :])
            cb_st[HALO:HALO + t, :] = glu_ref[...]
            staged.append(True)
        zs = {}

        def z(lc, s, g):
            if (lc, s, g) not in zs:
                ls = slice(lc * LANES, (lc + 1) * LANES)
                acc = None
                for a in range(n_a):
                    w = 8 * a + s - off
                    if 0 <= w < conv_w:
                        term = cb_st[8 * (g + a):8 * (g + a) + 8, ls] * wdw_ref[w:w + 1, ls]
                        acc = term if acc is None else acc + term
                zs[(lc, s, g)] = acc
            return zs[(lc, s, g)]

        dep = None
        for lc, g in mine:
            ls = slice(lc * LANES, (lc + 1) * LANES)
            acc = jnp.broadcast_to(bdw_ref[:, ls], (8, LANES)) + z(lc, 0, g)
            if dep is not None:
                acc = acc + dep
            for s in range(1, 8):
                acc = acc + jnp.concatenate([z(lc, s, g), z(lc, s, g + 1)], axis=0)[s:s + 8, :]
            y_sc[8 * g:8 * g + 8, ls] = acc
            dep = _dep_zero(acc)
        return dep[0:1, :]

    out, k = [], 0
    for n in sizes:
        out.append(functools.partial(run, units[k:k + n]))
        k += n
    return out


def _post_ffn_body(h_ref, am_ref, as_ref, cs_ref, ga_ref, gc_ref, g2_ref, glu_ref, glum_ref,
                   wdw_ref, bdw_ref, lng_ref, lnb_ref,
                   waoc, wcoc, woutc, w1c, w3c, w2c,
                   ym_ref, ys_ref,
                   waob, wcob, woutb, w1b, w3b, w2b, hid_ref, cb_st, y_sc, conv_st, *, n_main, tiles_per_seq):
    i = pl.program_id(0)
    t_rows = y_sc.shape[0]

    @pl.when(i < N_WSTEPS)
    def _():
        _store_chunk(waob, waoc, i)
        _store_chunk(wcob, wcoc, i)
        _store_chunk(woutb, woutc, i)
        _store_chunk(w1b, w1c, i)
        _store_chunk(w3b, w3c, i)
        _store_chunk(w2b, w2c, i)

    @pl.when(i == 0)
    def _():
        cb_st[...] = jnp.zeros_like(cb_st)

    t = i - N_WSTEPS
    tn = t + 1
    first = (tn < n_main) & (lax.rem(tn, tiles_per_seq) == 0)
    n_chunks = w1b.shape[1] // FF_CHUNK

    def finish_conv():
        conv_st[...] = _ln_swish(y_sc[...], lng_ref[...], lnb_ref[...]).astype(conv_st.dtype)
        cb_st[0:HALO, :] = cb_st[t_rows:t_rows + HALO, :]

    @pl.when(t == -1)
    def _():
        for f in _conv_runs(cb_st, glu_ref, glum_ref, y_sc, wdw_ref, bdw_ref, first, n_chunks):
            f()
        finish_conv()

    @pl.when(t >= 0)
    def _():
        on_main = t < n_main
        conv_f = _conv_runs(cb_st, glu_ref, glum_ref, y_sc, wdw_ref, bdw_ref, first, n_chunks)
        fillers = [(c, c + 1, f) for c, f in enumerate(conv_f)]

        at = jnp.where(on_main, am_ref[...], as_ref[...])
        cv = jnp.where(on_main, conv_st[...], cs_ref[...])
        a = _dot(at, waob[...])
        c = _dot(cv, wcob[...])
        mix = (ga_ref[...] * a + gc_ref[...] * c).astype(BF16)
        h2 = h_ref[...] + _dot(mix, woutb[...])
        xn = _rms(h2, g2_ref[...]).astype(BF16)
        y = h2 + 0.5 * _swiglu(xn, w1b, w3b, w2b, hid_ref, fillers)
        finish_conv()

        @pl.when(on_main)
        def _():
            ym_ref[...] = y

        @pl.when(t == n_main)
        def _():
            ys_ref[...] = y


def _post_ffn(h_all, attn_main, attn_small, conv_small, ga_all, gc_all, g2, glu_all, glu_meta,
              w_dw, b_dw, ln_g, ln_b, w_ao, w_co, w_out, w1, w3, w2, seq):
    m_rows, aw = attn_main.shape
    ch = glu_all.shape[1]
    d = h_all.shape[1]
    n_ff = w1.shape[1]
    t = ROW_TILE
    n_main = m_rows // t

    def main_idx(i):
        return (jnp.clip(i - N_WSTEPS, 0, n_main - 1), 0)

    def next_idx(i):
        return (jnp.clip(i - N_WSTEPS + 1, 0, n_main - 1), 0)

    def row_idx(i):
        return (jnp.maximum(i - N_WSTEPS, 0), 0)

    def w_idx(i):
        return (jnp.minimum(i, N_WSTEPS - 1), 0)

    const = lambda i: (0, 0)
    return pl.pallas_call(
        functools.partial(_post_ffn_body, n_main=n_main, tiles_per_seq=seq // t),
        grid=(N_WSTEPS + n_main + 1,),
        in_specs=[
            pl.BlockSpec((t, d), row_idx),
            pl.BlockSpec((t, aw), main_idx),
            pl.BlockSpec((t, aw), const),
            pl.BlockSpec((t, ch), const),
            pl.BlockSpec((t, d), row_idx),
            pl.BlockSpec((t, d), row_idx),
            pl.BlockSpec((1, d), const),
            pl.BlockSpec((t, ch), next_idx),
            pl.BlockSpec((HALO, ch), const),
            pl.BlockSpec(w_dw.shape, const),
            pl.BlockSpec((1, ch), const),
            pl.BlockSpec((1, ch), const),
            pl.BlockSpec((1, ch), const),
            pl.BlockSpec((aw // N_WSTEPS, d), w_idx),
            pl.BlockSpec((ch // N_WSTEPS, d), w_idx),
            pl.BlockSpec((d // N_WSTEPS, d), w_idx),
            pl.BlockSpec((d // N_WSTEPS, n_ff), w_idx),
            pl.BlockSpec((d // N_WSTEPS, n_ff), w_idx),
            pl.BlockSpec((n_ff // N_WSTEPS, d), w_idx),
        ],
        out_specs=[pl.BlockSpec((t, d), main_idx), pl.BlockSpec((t, d), const)],
        out_shape=[jax.ShapeDtypeStruct((m_rows, d), F32), jax.ShapeDtypeStruct((t, d), F32)],
        scratch_shapes=[
            pltpu.VMEM((aw, d), BF16),
            pltpu.VMEM((ch, d), BF16),
            pltpu.VMEM((d, d), BF16),
            pltpu.VMEM((d, n_ff), BF16),
            pltpu.VMEM((d, n_ff), BF16),
            pltpu.VMEM((n_ff, d), BF16),
            pltpu.VMEM((t, n_ff), BF16),
            pltpu.VMEM((HALO + t, ch), F32),
            pltpu.VMEM((t, ch), F32),
            pltpu.VMEM((t, ch), BF16),
        ],
        compiler_params=pltpu.CompilerParams(
            dimension_semantics=("arbitrary",), vmem_limit_bytes=VMEM_LIMIT),
        name="post_ffn2",
    )(h_all, attn_main, attn_small, conv_small, ga_all, gc_all, g2, glu_all, glu_meta,
      w_dw, b_dw, ln_g, ln_b, w_ao, w_co, w_out, w1, w3, w2)


def _t5_bucket(dist, n_buckets):
    max_exact = n_buckets // 2
    d = np.maximum(dist, 0)
    ratio = (np.log(np.maximum(d, 1).astype(np.float32) / np.float32(max_exact))
             / np.float32(math.log(REL_MAX_DIST / max_exact)))
    large = np.minimum(max_exact + (ratio * np.float32(n_buckets - max_exact)).astype(np.int32), n_buckets - 1)
    return np.where(d < max_exact, d, large).astype(np.int32)


def _bucket_or_masked(dist, n_buckets):
    ok = (dist >= 0) & (dist < WINDOW)
    return np.where(ok, _t5_bucket(dist, n_buckets), -1).astype(np.int32)


def _rows(x, start, n):
    return lax.slice_in_dim(x, start, start + n, axis=0)


def kernel(x_prompt, x_sample, cache_k, cache_v, state_conv, meta_tokens, ffn1_norm, ffn1_w1, ffn1_w3, ffn1_w2, mix_norm, w_in, q_norm, k_norm, rel_bias, sinks, w_attn_out, w_dw, b_dw, conv_ln_g, conv_ln_b, w_conv_out, w_out, ffn2_norm, ffn2_w1, ffn2_w3, ffn2_w2):
    n_b, seq, d = x_prompt.shape
    db = x_sample.shape[0]
    n_meta = meta_tokens.shape[0]
    n_heads = sinks.shape[0]
    w_buf, n_kv, hd = cache_k.shape[1], cache_k.shape[2], cache_k.shape[3]
    ch = w_dw.shape[1]
    n_st = state_conv.shape[1]
    n_buckets = rel_bias.shape[0]
    aw, kvw = n_heads * hd, n_kv * hd
    t = ROW_TILE
    m_rows = n_b * seq
    assert hd == HEAD_DIM and kvw == LANES and n_kv == 2 and n_heads == 8 and w_buf == WINDOW
    assert x_sample.shape[1] == 1 and seq % t == 0 and seq % ATTN_TILE == 0
    assert n_meta + db <= t and db % SAMPLE_BLOCK == 0
    assert n_meta <= HALO and n_meta <= WINDOW and n_st == w_dw.shape[0] - 1 and n_st <= HALO
    splits = tuple(int(v) for v in np.cumsum([0, aw, kvw, kvw, ch, ch, d, d]))
    assert splits[-1] == w_in.shape[1]

    row = lambda v: v.reshape(1, -1)
    x_main = x_prompt.reshape(m_rows, d)
    x_small = jnp.concatenate(
        [meta_tokens, x_sample.reshape(db, d), jnp.zeros((t - n_meta - db, d), F32)], axis=0)
    gq2 = row(jnp.concatenate([q_norm, q_norm]))
    gk2 = row(jnp.concatenate([k_norm, k_norm]))

    h_all, q_all, kv_all, glu_all, ga_all, gc_all = _ffn_inproj(
        x_main, x_small, row(ffn1_norm), row(mix_norm), gq2, gk2, ffn1_w1, ffn1_w3, ffn1_w2, w_in, splits)

    sinks2 = row(sinks)
    b_dw2, ln_g2, ln_b2 = row(b_dw), row(conv_ln_g), row(conv_ln_b)

    kv_meta = jnp.concatenate([jnp.zeros((WINDOW - n_meta, 2 * kvw), F32), _rows(kv_all, m_rows, n_meta)], axis=0)
    glu_meta = jnp.concatenate([jnp.zeros((HALO - n_meta, ch), F32), _rows(glu_all, m_rows, n_meta)], axis=0)
    dist = np.arange(WINDOW)[:, None] + WINDOW - np.arange(2 * WINDOW)[None, :]
    bidx = jnp.asarray(_bucket_or_masked(dist, n_buckets))
    attn_main = _prompt_attn(rel_bias, sinks2, q_all, kv_all, kv_meta, bidx, m_rows, seq, n_meta)

    s0 = m_rows + n_meta
    q_s = _rows(q_all, s0, db).reshape(db, n_heads, hd)
    lane_half = (np.arange(kvw)[None, :] // hd) == (np.arange(n_heads)[:, None] // (n_heads // n_kv))
    qx = jnp.where(jnp.asarray(lane_half)[None], jnp.concatenate([q_s, q_s], axis=-1), 0).astype(BF16)
    kv_new = _rows(kv_all, s0, db).reshape(db, 1, 2 * kvw)
    glu_new = _rows(glu_all, s0, db).reshape(db, 1, ch)
    bidx_s = jnp.asarray(_bucket_or_masked(w_buf - np.arange(w_buf)[None, :], n_buckets))
    o_s, conv_s, new_k_s, new_v_s, new_conv_s = _sample_mix(
        rel_bias, sinks2, qx, kv_new, glu_new, cache_k.reshape(db, w_buf, kvw), cache_v.reshape(db, w_buf, kvw),
        state_conv, bidx_s, w_dw, b_dw2, ln_g2, ln_b2)
    o_s = jnp.where(jnp.asarray(lane_half)[None], o_s, 0.0).reshape(db, n_heads, n_kv, hd).sum(axis=2)
    attn_small = jnp.concatenate(
        [jnp.zeros((n_meta, aw), BF16), o_s.reshape(db, aw).astype(BF16), jnp.zeros((t - n_meta - db, aw), BF16)],
        axis=0)
    conv_small = jnp.concatenate(
        [jnp.zeros((n_meta, ch), BF16), conv_s.reshape(db, ch).astype(BF16), jnp.zeros((t - n_meta - db, ch), BF16)],
        axis=0)

    y_main, y_small = _post_ffn(h_all, attn_main, attn_small, conv_small, ga_all, gc_all, row(ffn2_norm),
                                glu_all, glu_meta, w_dw, b_dw2, ln_g2, ln_b2,
                                w_attn_out, w_conv_out, w_out, ffn2_w1, ffn2_w3, ffn2_w2, seq)

    kv_t = jnp.stack([_rows(kv_all, (s + 1) * seq - WINDOW, WINDOW) for s in range(n_b)])
    glu_t = jnp.stack([_rows(glu_all, (s + 1) * seq - n_st, n_st) for s in range(n_b)])
    return (
        y_main.reshape(n_b, seq, d),
        _rows(y_small, n_meta, db).reshape(db, 1, d),
        kv_t[..., :kvw].reshape(n_b, WINDOW, n_kv, hd),
        kv_t[..., kvw:].reshape(n_b, WINDOW, n_kv, hd),
        glu_t,
        new_k_s.reshape(db, w_buf, n_kv, hd),
        new_v_s.reshape(db, w_buf, n_kv, hd),
        new_conv_s,
    )
```

```python
import functools
import math

import jax
import jax.numpy as jnp
import numpy as np
from jax import lax
from jax.experimental import pallas as pl
from jax.experimental.pallas import tpu as pltpu

F32 = jnp.float32
BF16 = jnp.bfloat16

EPS = 1e-6
NEG = -1e30
WINDOW = 128
REL_MAX_DIST = 128
HEAD_DIM = 64
LANES = 128
ROW_TILE = 256
ATTN_TILE = 512
FF_CHUNK = 256
N_WSTEPS = 16
HALO = 32
SAMPLE_BLOCK = 16
VMEM_LIMIT = 56 * 1024 * 1024


def _dot(a, b):
    return jnp.dot(a, b, preferred_element_type=F32)


def _dot_t(a, b):
    return lax.dot_general(a, b, (((1,), (1,)), ((), ())), preferred_element_type=F32)


def _rms(x, g):
    return x * lax.rsqrt(jnp.mean(x * x, axis=-1, keepdims=True) + EPS) * g


def _pair_rms(x, g2, lo_mask):
    sq = x * x
    lo = jnp.sum(jnp.where(lo_mask, sq, 0.0), axis=-1, keepdims=True) * (1.0 / HEAD_DIM)
    hi = jnp.sum(jnp.where(lo_mask, 0.0, sq), axis=-1, keepdims=True) * (1.0 / HEAD_DIM)
    r = jnp.where(lo_mask, lax.rsqrt(lo + EPS), lax.rsqrt(hi + EPS))
    return x * r * g2


def _dep_zero(row):
    u = pltpu.bitcast(row, jnp.uint32)
    return pltpu.bitcast((u >> 16) >> 16, F32)


def _swiglu(xn, w1b, w3b, w2b, hid_ref, fillers=()):
    n_ff = w1b.shape[1]
    n_chunks = n_ff // FF_CHUNK
    due = {}
    for c in range(n_chunks):
        sl = slice(c * FF_CHUNK, (c + 1) * FF_CHUNK)
        a = _dot(xn, w1b[:, sl])
        b = _dot(xn, w3b[:, sl])
        for z in due.pop(c, ()):
            b = b + jnp.concatenate([z] * (FF_CHUNK // LANES), axis=1)
        for issue, when, thunk in fillers:
            if issue == c:
                due.setdefault(min(when, n_chunks), []).append(_dep_zero(thunk()))
        hid_ref[:, sl] = (a * jax.nn.sigmoid(a) * b).astype(BF16)
    out = _dot(hid_ref[...], w2b[...])
    for z in due.pop(n_chunks, ()):
        out = out + jnp.concatenate([z] * (out.shape[1] // LANES), axis=1)
    assert not due
    return out


def _store_chunk(dst, src, i):
    rows = src.shape[0]
    r = pl.multiple_of(i * rows, 16)
    dst[pl.ds(r, rows), :] = src[...].astype(BF16)


def _bias_table(bidx, rb_ref, head):
    tab = jnp.full(bidx.shape, NEG, F32)
    for b in range(rb_ref.shape[0]):
        tab = jnp.where(bidx == b, rb_ref[b, head], tab)
    return tab


def _ln_swish(y, g, b):
    mu = jnp.mean(y, axis=-1, keepdims=True)
    yc = y - mu
    var = jnp.mean(yc * yc, axis=-1, keepdims=True)
    z = yc * lax.rsqrt(var + EPS) * g + b
    return z * jax.nn.sigmoid(z)


def _ffn_inproj_body(xc_ref, xn_ref, xs_ref, g1_ref, gm_ref, gq_ref, gk_ref, w1c, w3c, w2c, winc,
                     h_ref, q_ref, kv_ref, glu_ref, ga_ref, gc_ref,
                     w1b, w3b, w2b, winb, hid_ref, xn_st, u_st, *, n_main, splits):
    i = pl.program_id(0)

    @pl.when(i < N_WSTEPS)
    def _():
        _store_chunk(w1b, w1c, i)
        _store_chunk(w3b, w3c, i)
        _store_chunk(w2b, w2c, i)
        _store_chunk(winb, winc, i)

    s = i - N_WSTEPS

    def project():
        u = u_st[...]
        o_q, o_k, o_v, o_a, o_b, o_ga, o_gc, o_end = splits
        lo_mask = lax.broadcasted_iota(jnp.int32, (1, LANES), 1) < HEAD_DIM
        zq = _dot(u, winb[:, o_q:o_k])
        for p in range((o_k - o_q) // LANES):
            sl = slice(p * LANES, (p + 1) * LANES)
            q_ref[:, sl] = (_pair_rms(zq[:, sl], gq_ref[...], lo_mask) * (HEAD_DIM ** -0.5)).astype(BF16)
        zkv = _dot(u, winb[:, o_k:o_a])
        kv_ref[:, :LANES] = _pair_rms(zkv[:, :LANES], gk_ref[...], lo_mask)
        kv_ref[:, LANES:] = zkv[:, LANES:]
        za = _dot(u, winb[:, o_a:o_b])
        zb = _dot(u, winb[:, o_b:o_ga])
        glu_ref[...] = za * jax.nn.sigmoid(zb)
        ga_ref[...] = jax.nn.sigmoid(_dot(u, winb[:, o_ga:o_gc]))
        gc_ref[...] = jax.nn.sigmoid(_dot(u, winb[:, o_gc:o_end]))

    def normalise_next():
        x_next = jnp.where(s + 1 < n_main, xn_ref[...], xs_ref[...])
        xn_st[...] = _rms(x_next, g1_ref[...]).astype(BF16)

    @pl.when(i == 0)
    def _():
        u_st[...] = jnp.zeros_like(u_st)

    @pl.when(s == -1)
    def _():
        normalise_next()

    @pl.when((s >= 0) & (s <= n_main))
    def _():
        project()
        x = jnp.where(s < n_main, xc_ref[...], xs_ref[...])
        h = x + 0.5 * _swiglu(xn_st[...], w1b, w3b, w2b, hid_ref)
        h_ref[...] = h
        u_st[...] = _rms(h, gm_ref[...]).astype(BF16)
        normalise_next()

    @pl.when(s == n_main + 1)
    def _():
        project()


def _ffn_inproj(x_main, x_small, g1, gm, gq2, gk2, w1, w3, w2, w_in, splits):
    m_rows, d = x_main.shape
    n_ff = w1.shape[1]
    n_main = m_rows // ROW_TILE
    r_rows = m_rows + ROW_TILE
    t = ROW_TILE
    o_q, o_k, o_v, o_a, o_b, o_ga, o_gc, o_end = splits
    assert o_v - o_k == LANES and o_a - o_v == LANES and (o_k - o_q) % LANES == 0

    def cur_idx(i):
        return (jnp.clip(i - N_WSTEPS, 0, n_main - 1), 0)

    def next_idx(i):
        return (jnp.clip(i - N_WSTEPS + 1, 0, n_main - 1), 0)

    def h_idx(i):
        return (jnp.clip(i - N_WSTEPS, 0, n_main), 0)

    def proj_idx(i):
        return (jnp.clip(i - N_WSTEPS - 1, 0, n_main), 0)

    def w_idx(i):
        return (jnp.minimum(i, N_WSTEPS - 1), 0)

    const = lambda i: (0, 0)
    outs = [
        (jax.ShapeDtypeStruct((r_rows, d), F32), h_idx),
        (jax.ShapeDtypeStruct((r_rows, o_k - o_q), BF16), proj_idx),
        (jax.ShapeDtypeStruct((r_rows, o_a - o_k), F32), proj_idx),
        (jax.ShapeDtypeStruct((r_rows, o_b - o_a), F32), proj_idx),
        (jax.ShapeDtypeStruct((r_rows, o_gc - o_ga), F32), proj_idx),
        (jax.ShapeDtypeStruct((r_rows, o_end - o_gc), F32), proj_idx),
    ]
    return pl.pallas_call(
        functools.partial(_ffn_inproj_body, n_main=n_main, splits=splits),
        grid=(N_WSTEPS + n_main + 2,),
        in_specs=[
            pl.BlockSpec((t, d), cur_idx),
            pl.BlockSpec((t, d), next_idx),
            pl.BlockSpec((t, d), const),
            pl.BlockSpec((1, d), const),
            pl.BlockSpec((1, d), const),
            pl.BlockSpec((1, LANES), const),
            pl.BlockSpec((1, LANES), const),
            pl.BlockSpec((d // N_WSTEPS, n_ff), w_idx),
            pl.BlockSpec((d // N_WSTEPS, n_ff), w_idx),
            pl.BlockSpec((n_ff // N_WSTEPS, d), w_idx),
            pl.BlockSpec((d // N_WSTEPS, o_end), w_idx),
        ],
        out_specs=[pl.BlockSpec((t, s.shape[1]), idx) for s, idx in outs],
        out_shape=[s for s, _ in outs],
        scratch_shapes=[
            pltpu.VMEM((d, n_ff), BF16),
            pltpu.VMEM((d, n_ff), BF16),
            pltpu.VMEM((n_ff, d), BF16),
            pltpu.VMEM((d, o_end), BF16),
            pltpu.VMEM((t, n_ff), BF16),
            pltpu.VMEM((t, d), BF16),
            pltpu.VMEM((t, d), BF16),
        ],
        compiler_params=pltpu.CompilerParams(
            dimension_semantics=("arbitrary",), vmem_limit_bytes=VMEM_LIMIT),
        name="ffn1_inproj",
    )(x_main, x_main, x_small, g1, gm, gq2, gk2, w1, w3, w2, w_in)


def _prompt_attn_body(rb_ref, sink_ref, q_ref, kvc_ref, kvp_ref, kvm_ref, bidx_ref, *rest,
                      tiles_per_seq, n_meta, n_heads, n_weights):
    w_refs, attn_ref, wb_refs, bias_sc = rest[:n_weights], rest[n_weights], rest[n_weights + 1:-1], rest[-1]
    for w_ref, wb_ref in zip(w_refs, wb_refs, strict=True):
        wb_ref[...] = w_ref[...].astype(BF16)

    i = pl.program_id(0)
    t = q_ref.shape[0]
    n_kv = 2
    grp = n_heads // n_kv

    @pl.when(i == 0)
    def _():
        col = lax.broadcasted_iota(jnp.int32, (WINDOW, 2 * WINDOW), 1)
        for hd in range(n_heads):
            tab = _bias_table(bidx_ref[...], rb_ref, hd)
            bias_sc[0, hd] = tab
            bias_sc[1, hd] = jnp.where(col < WINDOW - n_meta, NEG, tab)

    first = (i % tiles_per_seq) == 0
    lead = jnp.where(first, 1, 0)
    lo_mask = lax.broadcasted_iota(jnp.int32, (1, LANES), 1) < HEAD_DIM
    kv_prev = jnp.where(first, kvm_ref[...], kvp_ref[...])
    kv = jnp.concatenate([kv_prev, kvc_ref[...]], axis=0)
    k, v = kv[:, :LANES], kv[:, LANES:]
    kb, vb = k.astype(BF16), v.astype(BF16)
    kr = pltpu.roll(k, HEAD_DIM, axis=1).astype(BF16)
    vr = pltpu.roll(v, HEAD_DIM, axis=1).astype(BF16)
    zero = jnp.zeros((), BF16)
    k_lo = [jnp.where(lo_mask, kb, zero), jnp.where(lo_mask, kr, zero)]
    k_hi = [jnp.where(lo_mask, zero, kr), jnp.where(lo_mask, zero, kb)]
    v_lo = [jnp.where(lo_mask, vb, zero), jnp.where(lo_mask, vr, zero)]
    v_hi = [jnp.where(lo_mask, zero, vr), jnp.where(lo_mask, zero, vb)]

    for qb in range(t // WINDOW):
        rows = slice(qb * WINDOW, (qb + 1) * WINDOW)
        keys = slice(qb * WINDOW, qb * WINDOW + 2 * WINDOW)
        tab = lead if qb == 0 else 0
        for h in range(n_kv):
            k_st = jnp.concatenate([k_lo[h][keys], k_hi[h][keys]], axis=0)
            v_st = jnp.concatenate([v_lo[h][keys], v_hi[h][keys]], axis=0)
            c0 = h * grp * HEAD_DIM
            qq = jnp.concatenate([q_ref[rows, c0:c0 + LANES], q_ref[rows, c0 + LANES:c0 + 2 * LANES]], axis=0)
            s = _dot_t(qq, k_st)
            p_parts, inv_parts = [], []
            for g2 in range(2):
                p_row, inv_row = [], []
                for par in range(2):
                    hd = h * grp + 2 * g2 + par
                    sq = s[g2 * WINDOW:(g2 + 1) * WINDOW, par * 2 * WINDOW:(par + 1) * 2 * WINDOW] + bias_sc[tab, hd]
                    sink = sink_ref[0, hd]
                    m = jnp.maximum(jnp.max(sq, axis=-1, keepdims=True), sink)
                    p = jnp.exp(sq - m)
                    den = jnp.sum(p, axis=-1, keepdims=True) + jnp.exp(sink - m)
                    p_row.append(p.astype(BF16))
                    inv_row.append(1.0 / den)
                p_parts.append(jnp.concatenate(p_row, axis=1))
                inv_parts.append(jnp.where(lo_mask, inv_row[0], inv_row[1]))
            pm = jnp.concatenate(p_parts, axis=0)
            o = _dot(pm, v_st)
            for g2 in range(2):
                c = c0 + g2 * LANES
                attn_ref[rows, c:c + LANES] = (o[g2 * WINDOW:(g2 + 1) * WINDOW] * inv_parts[g2]).astype(attn_ref.dtype)


def _prompt_attn(rel_bias, sinks2, q_all, kv_all, kv_meta, bidx, weights, m_rows, seq, n_meta):
    t = ATTN_TILE
    n_steps = m_rows // t
    n_heads = sinks2.shape[1]
    qw, kvw = q_all.shape[1], kv_all.shape[1]
    const = lambda i: (0, 0)
    row = lambda i: (i, 0)
    smem = pl.BlockSpec(memory_space=pltpu.SMEM)

    def w_spec(w):
        share = 1 if (w.shape[0] // n_steps) % 16 == 0 else 2
        assert w.shape[0] % (n_steps // share) == 0 and (w.shape[0] * share // n_steps) % 16 == 0
        return pl.BlockSpec((w.shape[0] * share // n_steps, w.shape[1]), lambda i: (i // share, 0))

    w_specs = [w_spec(w) for w in weights]
    outs = pl.pallas_call(
        functools.partial(_prompt_attn_body, tiles_per_seq=seq // t, n_meta=n_meta, n_heads=n_heads,
                          n_weights=len(weights)),
        grid=(n_steps,),
        in_specs=[
            smem, smem,
            pl.BlockSpec((t, qw), row),
            pl.BlockSpec((t, kvw), row),
            pl.BlockSpec((WINDOW, kvw), lambda i: (jnp.maximum(i * (t // WINDOW) - 1, 0), 0)),
            pl.BlockSpec((WINDOW, kvw), const),
            pl.BlockSpec((WINDOW, 2 * WINDOW), const),
        ] + w_specs,
        out_specs=[pl.BlockSpec((t, qw), row)] + w_specs,
        out_shape=[jax.ShapeDtypeStruct((m_rows, qw), BF16)] + [jax.ShapeDtypeStruct(w.shape, BF16) for w in weights],
        scratch_shapes=[pltpu.VMEM((2, n_heads, WINDOW, 2 * WINDOW), F32)],
        compiler_params=pltpu.CompilerParams(dimension_semantics=("arbitrary",)),
        name="prompt_attn",
    )(rel_bias, sinks2, q_all, kv_all, kv_all, kv_meta, bidx, *weights)
    return outs[0], outs[1:]


def _sample_mix_body(rb_ref, sink_ref, qx_ref, kvn_ref, glun_ref, ck_ref, cv_ref, st_ref, bidx_ref,
                     wdw_ref, bdw_ref, lng_ref, lnb_ref,
                     o_ref, conv_ref, nk_ref, nv_ref, nst_ref, bias_sc, *, n_heads):
    sb, w_buf = ck_ref.shape[0], ck_ref.shape[2]
    n_st = st_ref.shape[0]
    n_rows = sb * n_heads

    @pl.when(pl.program_id(0) == 0)
    def _():
        own = jnp.concatenate([_bias_table(bidx_ref[...], rb_ref, hd) for hd in range(n_heads)], axis=0)
        tiled = jnp.concatenate([jnp.concatenate([own] * sb, axis=1)] * sb, axis=0)
        row_seq = lax.broadcasted_iota(jnp.int32, tiled.shape, 0) // n_heads
        col_seq = lax.broadcasted_iota(jnp.int32, tiled.shape, 1) // w_buf
        bias_sc[...] = jnp.where(row_seq == col_seq, tiled, NEG)

    hrow = lax.broadcasted_iota(jnp.int32, (n_heads, 1), 0)
    sink = jnp.zeros((n_heads, 1), F32)
    bias_new = jnp.zeros((n_heads, 1), F32)
    for hd in range(n_heads):
        sink = jnp.where(hrow == hd, sink_ref[0, hd], sink)
        bias_new = jnp.where(hrow == hd, rb_ref[0, hd], bias_new)
    sink = jnp.concatenate([sink] * sb, axis=0)
    bias_new = jnp.concatenate([bias_new] * sb, axis=0)

    q = qx_ref[...].reshape(n_rows, LANES)
    kvn = kvn_ref[...]
    kn_rows = jnp.broadcast_to(kvn[:, :, :LANES], (sb, n_heads, LANES)).reshape(n_rows, LANES).astype(BF16)
    vn_rows = jnp.broadcast_to(kvn[:, :, LANES:], (sb, n_heads, LANES)).reshape(n_rows, LANES).astype(BF16)
    ck_all = jnp.concatenate([ck_ref[bb] for bb in range(sb)], axis=1).astype(BF16)
    cv_all = jnp.concatenate([cv_ref[bb] for bb in range(sb)], axis=1).astype(BF16)
    s_c = _dot(q, ck_all) + bias_sc[...]
    s_n = jnp.sum(q.astype(F32) * kn_rows.astype(F32), axis=-1, keepdims=True) + bias_new
    m = jnp.maximum(jnp.maximum(jnp.max(s_c, axis=-1, keepdims=True), s_n), sink)
    p_c = jnp.exp(s_c - m)
    p_n = jnp.exp(s_n - m)
    den = jnp.sum(p_c, axis=-1, keepdims=True) + p_n + jnp.exp(sink - m)
    o = _dot_t(p_c.astype(BF16), cv_all) + p_n.astype(BF16).astype(F32) * vn_rows.astype(F32)
    o_ref[...] = (o / den).reshape(sb, n_heads, LANES)

    for bb in range(sb):
        kn = kvn_ref[bb][:, :LANES]
        vn = kvn_ref[bb][:, LANES:]
        nk_ref[bb, 0:w_buf - 1, :] = ck_ref[bb].T[1:w_buf, :]
        nk_ref[bb, w_buf - 1:w_buf, :] = kn
        nv_ref[bb, 0:w_buf - 1, :] = cv_ref[bb].T[1:w_buf, :]
        nv_ref[bb, w_buf - 1:w_buf, :] = vn
        g_new = glun_ref[bb]
        st = st_ref[:, bb, :]
        y = (jnp.sum(st * wdw_ref[0:n_st, :], axis=0, keepdims=True)
             + g_new * wdw_ref[n_st:n_st + 1, :] + bdw_ref[...])
        conv_ref[bb] = _ln_swish(y, lng_ref[...], lnb_ref[...])
        nst_ref[bb, 0:n_st - 1, :] = st[1:n_st, :]
        nst_ref[bb, n_st - 1:n_st, :] = g_new


def _sample_mix(rel_bias, sinks2, qx, kv_new, glu_new, cache_kt, cache_vt, state_t, bidx_s, w_dw, b_dw, ln_g, ln_b):
    db, kw, w_buf = cache_kt.shape
    n_st, ch = state_t.shape[0], state_t.shape[2]
    n_heads = sinks2.shape[1]
    sb = SAMPLE_BLOCK
    blk = lambda i: (i, 0, 0)
    const = lambda i: (0, 0)
    smem = pl.BlockSpec(memory_space=pltpu.SMEM)
    return pl.pallas_call(
        functools.partial(_sample_mix_body, n_heads=n_heads),
        grid=(db // sb,),
        in_specs=[
            smem, smem,
            pl.BlockSpec((sb, n_heads, kw), blk),
            pl.BlockSpec((sb, 1, 2 * kw), blk),
            pl.BlockSpec((sb, 1, ch), blk),
            pl.BlockSpec((sb, kw, w_buf), blk),
            pl.BlockSpec((sb, kw, w_buf), blk),
            pl.BlockSpec((n_st, sb, ch), lambda i: (0, i, 0)),
            pl.BlockSpec((1, w_buf), const),
            pl.BlockSpec(w_dw.shape, const),
            pl.BlockSpec((1, ch), const),
            pl.BlockSpec((1, ch), const),
            pl.BlockSpec((1, ch), const),
        ],
        out_specs=[
            pl.BlockSpec((sb, n_heads, kw), blk),
            pl.BlockSpec((sb, 1, ch), blk),
            pl.BlockSpec((sb, w_buf, kw), blk),
            pl.BlockSpec((sb, w_buf, kw), blk),
            pl.BlockSpec((sb, n_st, ch), blk),
        ],
        out_shape=[
            jax.ShapeDtypeStruct((db, n_heads, kw), F32),
            jax.ShapeDtypeStruct((db, 1, ch), F32),
            jax.ShapeDtypeStruct((db, w_buf, kw), F32),
            jax.ShapeDtypeStruct((db, w_buf, kw), F32),
            jax.ShapeDtypeStruct((db, n_st, ch), F32),
        ],
        scratch_shapes=[pltpu.VMEM((sb * n_heads, sb * w_buf), F32)],
        compiler_params=pltpu.CompilerParams(dimension_semantics=("arbitrary",)),
        name="sample_mix",
    )(rel_bias, sinks2, qx, kv_new, glu_new, cache_kt, cache_vt, state_t, bidx_s, w_dw, b_dw, ln_g, ln_b)


def _conv_runs(cb_st, glu_ref, glum_ref, y_sc, wdw_ref, bdw_ref, first, n_runs):
    t = y_sc.shape[0]
    conv_w = wdw_ref.shape[0]
    n_ch = y_sc.shape[1]
    off = HALO - (conv_w - 1)
    n_a = (off + conv_w - 1) // 8 + 1
    units = [(lc, g) for lc in range(n_ch // LANES) for g in range(t // 8)]
    sizes = [len(units) // n_runs + (1 if r < len(units) % n_runs else 0) for r in range(n_runs)]
    staged = []

    def run(mine):
        if not staged:
            cb_st[0:HALO, :] = jnp.where(first, glum_ref[...], cb_st[0:HALO, :])
            cb_st[HALO:HALO + t, :] = glu_ref[...]
            staged.append(True)
        zs = {}

        def z(lc, s, g):
            if (lc, s, g) not in zs:
                ls = slice(lc * LANES, (lc + 1) * LANES)
                acc = None
                for a in range(n_a):
                    w = 8 * a + s - off
                    if 0 <= w < conv_w:
                        term = cb_st[8 * (g + a):8 * (g + a) + 8, ls] * wdw_ref[w:w + 1, ls]
                        acc = term if acc is None else acc + term
                zs[(lc, s, g)] = acc
            return zs[(lc, s, g)]

        dep = None
        for lc, g in mine:
            ls = slice(lc * LANES, (lc + 1) * LANES)
            acc = jnp.broadcast_to(bdw_ref[:, ls], (8, LANES)) + z(lc, 0, g)
            if dep is not None:
                acc = acc + dep
            for s in range(1, 8):
                acc = acc + jnp.concatenate([z(lc, s, g), z(lc, s, g + 1)], axis=0)[s:s + 8, :]
            y_sc[8 * g:8 * g + 8, ls] = acc
            dep = _dep_zero(acc)
        return dep[0:1, :]

    out, k = [], 0
    for n in sizes:
        out.append(functools.partial(run, units[k:k + n]))
        k += n
    return out


def _post_ffn_body(h_ref, am_ref, as_ref, cs_ref, ga_ref, gc_ref, g2_ref, glu_ref, glum_ref,
                   wdw_ref, bdw_ref, lng_ref, lnb_ref,
                   waob, wcob, woutb, w1b, w3b, w2b,
                   ym_ref, ys_ref,
                   hid_ref, cb_st, y_sc, conv_st, *, n_main, tiles_per_seq):
    i = pl.program_id(0)
    t_rows = y_sc.shape[0]

    @pl.when(i == 0)
    def _():
        cb_st[...] = jnp.zeros_like(cb_st)

    t = i - 1
    tn = t + 1
    first = (tn < n_main) & (lax.rem(tn, tiles_per_seq) == 0)
    n_chunks = w1b.shape[1] // FF_CHUNK

    def finish_conv():
        conv_st[...] = _ln_swish(y_sc[...], lng_ref[...], lnb_ref[...]).astype(conv_st.dtype)
        cb_st[0:HALO, :] = cb_st[t_rows:t_rows + HALO, :]

    @pl.when(t == -1)
    def _():
        for f in _conv_runs(cb_st, glu_ref, glum_ref, y_sc, wdw_ref, bdw_ref, first, n_chunks):
            f()
        finish_conv()

    @pl.when(t >= 0)
    def _():
        on_main = t < n_main
        conv_f = _conv_runs(cb_st, glu_ref, glum_ref, y_sc, wdw_ref, bdw_ref, first, n_chunks)
        fillers = [(c, c + 1, f) for c, f in enumerate(conv_f)]

        at = jnp.where(on_main, am_ref[...], as_ref[...])
        cv = jnp.where(on_main, conv_st[...], cs_ref[...])
        a = _dot(at, waob[...])
        c = _dot(cv, wcob[...])
        mix = (ga_ref[...] * a + gc_ref[...] * c).astype(BF16)
        h2 = h_ref[...] + _dot(mix, woutb[...])
        xn = _rms(h2, g2_ref[...]).astype(BF16)
        y = h2 + 0.5 * _swiglu(xn, w1b, w3b, w2b, hid_ref, fillers)
        finish_conv()

        @pl.when(on_main)
        def _():
            ym_ref[...] = y

        @pl.when(t == n_main)
        def _():
            ys_ref[...] = y


def _post_ffn(h_all, attn_main, attn_small, conv_small, ga_all, gc_all, g2, glu_all, glu_meta,
              w_dw, b_dw, ln_g, ln_b, w_ao, w_co, w_out, w1, w3, w2, seq):
    m_rows, aw = attn_main.shape
    ch = glu_all.shape[1]
    d = h_all.shape[1]
    n_ff = w1.shape[1]
    t = ROW_TILE
    n_main = m_rows // t

    def main_idx(i):
        return (jnp.clip(i - 1, 0, n_main - 1), 0)

    def next_idx(i):
        return (jnp.minimum(i, n_main - 1), 0)

    def row_idx(i):
        return (jnp.maximum(i - 1, 0), 0)

    const = lambda i: (0, 0)
    resident = lambda w: pl.BlockSpec(w.shape, const, pipeline_mode=pl.Buffered(1))
    return pl.pallas_call(
        functools.partial(_post_ffn_body, n_main=n_main, tiles_per_seq=seq // t),
        grid=(n_main + 2,),
        in_specs=[
            pl.BlockSpec((t, d), row_idx),
            pl.BlockSpec((t, aw), main_idx),
            pl.BlockSpec((t, aw), const),
            pl.BlockSpec((t, ch), const),
            pl.BlockSpec((t, d), row_idx),
            pl.BlockSpec((t, d), row_idx),
            pl.BlockSpec((1, d), const),
            pl.BlockSpec((t, ch), next_idx),
            pl.BlockSpec((HALO, ch), const),
            pl.BlockSpec(w_dw.shape, const),
            pl.BlockSpec((1, ch), const),
            pl.BlockSpec((1, ch), const),
            pl.BlockSpec((1, ch), const),
            resident(w_ao), resident(w_co), resident(w_out), resident(w1), resident(w3), resident(w2),
        ],
        out_specs=[pl.BlockSpec((t, d), main_idx), pl.BlockSpec((t, d), const)],
        out_shape=[jax.ShapeDtypeStruct((m_rows, d), F32), jax.ShapeDtypeStruct((t, d), F32)],
        scratch_shapes=[
            pltpu.VMEM((t, n_ff), BF16),
            pltpu.VMEM((HALO + t, ch), F32),
            pltpu.VMEM((t, ch), F32),
            pltpu.VMEM((t, ch), BF16),
        ],
        compiler_params=pltpu.CompilerParams(
            dimension_semantics=("arbitrary",), vmem_limit_bytes=VMEM_LIMIT),
        name="post_ffn2",
    )(h_all, attn_main, attn_small, conv_small, ga_all, gc_all, g2, glu_all, glu_meta,
      w_dw, b_dw, ln_g, ln_b, w_ao, w_co, w_out, w1, w3, w2)


def _t5_bucket(dist, n_buckets):
    max_exact = n_buckets // 2
    d = np.maximum(dist, 0)
    ratio = (np.log(np.maximum(d, 1).astype(np.float32) / np.float32(max_exact))
             / np.float32(math.log(REL_MAX_DIST / max_exact)))
    large = np.minimum(max_exact + (ratio * np.float32(n_buckets - max_exact)).astype(np.int32), n_buckets - 1)
    return np.where(d < max_exact, d, large).astype(np.int32)


def _bucket_or_masked(dist, n_buckets):
    ok = (dist >= 0) & (dist < WINDOW)
    return np.where(ok, _t5_bucket(dist, n_buckets), -1).astype(np.int32)


def _rows(x, start, n):
    return lax.slice_in_dim(x, start, start + n, axis=0)


def kernel(x_prompt, x_sample, cache_k, cache_v, state_conv, meta_tokens, ffn1_norm, ffn1_w1, ffn1_w3, ffn1_w2, mix_norm, w_in, q_norm, k_norm, rel_bias, sinks, w_attn_out, w_dw, b_dw, conv_ln_g, conv_ln_b, w_conv_out, w_out, ffn2_norm, ffn2_w1, ffn2_w3, ffn2_w2):
    n_b, seq, d = x_prompt.shape
    db = x_sample.shape[0]
    n_meta = meta_tokens.shape[0]
    n_heads = sinks.shape[0]
    w_buf, n_kv, hd = cache_k.shape[1], cache_k.shape[2], cache_k.shape[3]
    ch = w_dw.shape[1]
    n_st = state_conv.shape[1]
    n_buckets = rel_bias.shape[0]
    aw, kvw = n_heads * hd, n_kv * hd
    t = ROW_TILE
    m_rows = n_b * seq
    assert hd == HEAD_DIM and kvw == LANES and n_kv == 2 and n_heads == 8 and w_buf == WINDOW
    assert x_sample.shape[1] == 1 and seq % t == 0 and seq % ATTN_TILE == 0
    assert n_meta + db <= t and db % SAMPLE_BLOCK == 0
    assert n_meta <= HALO and n_meta <= WINDOW and n_st == w_dw.shape[0] - 1 and n_st <= HALO
    splits = tuple(int(v) for v in np.cumsum([0, aw, kvw, kvw, ch, ch, d, d]))
    assert splits[-1] == w_in.shape[1]

    row = lambda v: v.reshape(1, -1)
    x_main = x_prompt.reshape(m_rows, d)
    x_small = jnp.concatenate(
        [meta_tokens, x_sample.reshape(db, d), jnp.zeros((t - n_meta - db, d), F32)], axis=0)
    gq2 = row(jnp.concatenate([q_norm, q_norm]))
    gk2 = row(jnp.concatenate([k_norm, k_norm]))

    h_all, q_all, kv_all, glu_all, ga_all, gc_all = _ffn_inproj(
        x_main, x_small, row(ffn1_norm), row(mix_norm), gq2, gk2, ffn1_w1, ffn1_w3, ffn1_w2, w_in, splits)

    sinks2 = row(sinks)
    b_dw2, ln_g2, ln_b2 = row(b_dw), row(conv_ln_g), row(conv_ln_b)

    kv_meta = jnp.concatenate([jnp.zeros((WINDOW - n_meta, 2 * kvw), F32), _rows(kv_all, m_rows, n_meta)], axis=0)
    glu_meta = jnp.concatenate([jnp.zeros((HALO - n_meta, ch), F32), _rows(glu_all, m_rows, n_meta)], axis=0)
    dist = np.arange(WINDOW)[:, None] + WINDOW - np.arange(2 * WINDOW)[None, :]
    bidx = jnp.asarray(_bucket_or_masked(dist, n_buckets))
    attn_main, w_bf16 = _prompt_attn(
        rel_bias, sinks2, q_all, kv_all, kv_meta, bidx,
        (w_attn_out, w_conv_out, w_out, ffn2_w1, ffn2_w3, ffn2_w2), m_rows, seq, n_meta)

    s0 = m_rows + n_meta
    q_s = _rows(q_all, s0, db).reshape(db, n_heads, hd)
    lane_half = (np.arange(kvw)[None, :] // hd) == (np.arange(n_heads)[:, None] // (n_heads // n_kv))
    qx = jnp.where(jnp.asarray(lane_half)[None], jnp.concatenate([q_s, q_s], axis=-1), 0).astype(BF16)
    kv_new = _rows(kv_all, s0, db).reshape(db, 1, 2 * kvw)
    glu_new = _rows(glu_all, s0, db).reshape(db, 1, ch)
    bidx_s = jnp.asarray(_bucket_or_masked(w_buf - np.arange(w_buf)[None, :], n_buckets))
    o_s, conv_s, new_k_s, new_v_s, new_conv_s = _sample_mix(
        rel_bias, sinks2, qx, kv_new, glu_new,
        jnp.transpose(cache_k, (0, 2, 3, 1)).reshape(db, kvw, w_buf),
        jnp.transpose(cache_v, (0, 2, 3, 1)).reshape(db, kvw, w_buf),
        jnp.transpose(state_conv, (1, 0, 2)), bidx_s, w_dw, b_dw2, ln_g2, ln_b2)
    o_s = jnp.where(jnp.asarray(lane_half)[None], o_s, 0.0).reshape(db, n_heads, n_kv, hd).sum(axis=2)
    attn_small = jnp.concatenate(
        [jnp.zeros((n_meta, aw), BF16), o_s.reshape(db, aw).astype(BF16), jnp.zeros((t - n_meta - db, aw), BF16)],
        axis=0)
    conv_small = jnp.concatenate(
        [jnp.zeros((n_meta, ch), BF16), conv_s.reshape(db, ch).astype(BF16), jnp.zeros((t - n_meta - db, ch), BF16)],
        axis=0)

    y_main, y_small = _post_ffn(h_all, attn_main, attn_small, conv_small, ga_all, gc_all, row(ffn2_norm),
                                glu_all, glu_meta, w_dw, b_dw2, ln_g2, ln_b2, *w_bf16, seq)

    kv_t = jnp.stack([_rows(kv_all, (s + 1) * seq - WINDOW, WINDOW) for s in range(n_b)])
    glu_t = jnp.stack([_rows(glu_all, (s + 1) * seq - n_st, n_st) for s in range(n_b)])
    return (
        y_main.reshape(n_b, seq, d),
        _rows(y_small, n_meta, db).reshape(db, 1, d),
        kv_t[..., :kvw].reshape(n_b, WINDOW, n_kv, hd),
        kv_t[..., kvw:].reshape(n_b, WINDOW, n_kv, hd),
        glu_t,
        new_k_s.reshape(db, w_buf, n_kv, hd),
        new_v_s.reshape(db, w_buf, n_kv, hd),
        new_conv_s,
    )
```

```python
import functools
import math

import jax
import jax.numpy as jnp
import numpy as np
from jax import lax
from jax.experimental import pallas as pl
from jax.experimental.pallas import tpu as pltpu

F32 = jnp.float32
BF16 = jnp.bfloat16

EPS = 1e-6
NEG = -1e30
WINDOW = 128
REL_MAX_DIST = 128
HEAD_DIM = 64
LANES = 128
ROW_TILE = 256
ATTN_TILE = 1024
FF_CHUNK = 256
N_WSTEPS = 8
HALO = 32
SAMPLE_BLOCK = 16
VMEM_LIMIT = 56 * 1024 * 1024


def _dot(a, b):
    return jnp.dot(a, b, preferred_element_type=F32)


def _dot_t(a, b):
    return lax.dot_general(a, b, (((1,), (1,)), ((), ())), preferred_element_type=F32)


def _rms(x, g):
    return x * lax.rsqrt(jnp.mean(x * x, axis=-1, keepdims=True) + EPS) * g


def _pair_rms(x, g2, lo_mask):
    sq = x * x
    lo = jnp.sum(jnp.where(lo_mask, sq, 0.0), axis=-1, keepdims=True) * (1.0 / HEAD_DIM)
    hi = jnp.sum(jnp.where(lo_mask, 0.0, sq), axis=-1, keepdims=True) * (1.0 / HEAD_DIM)
    r = jnp.where(lo_mask, lax.rsqrt(lo + EPS), lax.rsqrt(hi + EPS))
    return x * r * g2


def _dep_zero(row):
    u = pltpu.bitcast(row, jnp.uint32)
    return pltpu.bitcast((u >> 16) >> 16, F32)


def _swiglu(xn, w1b, w3b, w2b, hid_ref, fillers=()):
    n_ff = w1b.shape[1]
    n_chunks = n_ff // FF_CHUNK
    due = {}
    for c in range(n_chunks):
        sl = slice(c * FF_CHUNK, (c + 1) * FF_CHUNK)
        a = _dot(xn, w1b[:, sl])
        b = _dot(xn, w3b[:, sl])
        for z in due.pop(c, ()):
            b = b + jnp.concatenate([z] * (FF_CHUNK // LANES), axis=1)
        for issue, when, thunk in fillers:
            if issue == c:
                due.setdefault(min(when, n_chunks), []).append(_dep_zero(thunk()))
        hid_ref[:, sl] = (a * jax.nn.sigmoid(a) * b).astype(BF16)
    out = _dot(hid_ref[...], w2b[...])
    for z in due.pop(n_chunks, ()):
        out = out + jnp.concatenate([z] * (out.shape[1] // LANES), axis=1)
    assert not due
    return out


def _store_chunk(dst, src, i):
    rows = src.shape[0]
    r = pl.multiple_of(i * rows, 16)
    dst[pl.ds(r, rows), :] = src[...].astype(BF16)


def _bias_table(bidx, rb_ref, head):
    tab = jnp.full(bidx.shape, NEG, F32)
    for b in range(rb_ref.shape[0]):
        tab = jnp.where(bidx == b, rb_ref[b, head], tab)
    return tab


def _ln_swish(y, g, b):
    mu = jnp.mean(y, axis=-1, keepdims=True)
    yc = y - mu
    var = jnp.mean(yc * yc, axis=-1, keepdims=True)
    z = yc * lax.rsqrt(var + EPS) * g + b
    return z * jax.nn.sigmoid(z)


def _ffn_inproj_body(xc_ref, xn_ref, xs_ref, g1_ref, gm_ref, gq_ref, gk_ref, w1c, w3c, w2c, winc,
                     h_ref, q_ref, kv_ref, glu_ref, ga_ref, gc_ref,
                     w1b, w3b, w2b, winb, hid_ref, xn_st, u_st, *, n_main, splits):
    i = pl.program_id(0)

    @pl.when(i < N_WSTEPS)
    def _():
        _store_chunk(w1b, w1c, i)
        _store_chunk(w3b, w3c, i)
        _store_chunk(w2b, w2c, i)
        _store_chunk(winb, winc, i)

    s = i - N_WSTEPS

    def project():
        u = u_st[...]
        o_q, o_k, o_v, o_a, o_b, o_ga, o_gc, o_end = splits
        lo_mask = lax.broadcasted_iota(jnp.int32, (1, LANES), 1) < HEAD_DIM
        zq = _dot(u, winb[:, o_q:o_k])
        for p in range((o_k - o_q) // LANES):
            sl = slice(p * LANES, (p + 1) * LANES)
            q_ref[:, sl] = (_pair_rms(zq[:, sl], gq_ref[...], lo_mask) * (HEAD_DIM ** -0.5)).astype(BF16)
        zkv = _dot(u, winb[:, o_k:o_a])
        kv_ref[:, :LANES] = _pair_rms(zkv[:, :LANES], gk_ref[...], lo_mask)
        kv_ref[:, LANES:] = zkv[:, LANES:]
        za = _dot(u, winb[:, o_a:o_b])
        zb = _dot(u, winb[:, o_b:o_ga])
        glu_ref[...] = za * jax.nn.sigmoid(zb)
        ga_ref[...] = jax.nn.sigmoid(_dot(u, winb[:, o_ga:o_gc]))
        gc_ref[...] = jax.nn.sigmoid(_dot(u, winb[:, o_gc:o_end]))

    def normalise_next():
        x_next = jnp.where(s + 1 < n_main, xn_ref[...], xs_ref[...])
        xn_st[...] = _rms(x_next, g1_ref[...]).astype(BF16)

    @pl.when(i == 0)
    def _():
        u_st[...] = jnp.zeros_like(u_st)

    @pl.when(s == -1)
    def _():
        normalise_next()

    @pl.when((s >= 0) & (s <= n_main))
    def _():
        project()
        x = jnp.where(s < n_main, xc_ref[...], xs_ref[...])
        h = x + 0.5 * _swiglu(xn_st[...], w1b, w3b, w2b, hid_ref)
        h_ref[...] = h
        u_st[...] = _rms(h, gm_ref[...]).astype(BF16)
        normalise_next()

    @pl.when(s == n_main + 1)
    def _():
        project()


def _ffn_inproj(x_main, x_small, g1, gm, gq2, gk2, w1, w3, w2, w_in, splits):
    m_rows, d = x_main.shape
    n_ff = w1.shape[1]
    n_main = m_rows // ROW_TILE
    r_rows = m_rows + ROW_TILE
    t = ROW_TILE
    o_q, o_k, o_v, o_a, o_b, o_ga, o_gc, o_end = splits
    assert o_v - o_k == LANES and o_a - o_v == LANES and (o_k - o_q) % LANES == 0

    def cur_idx(i):
        return (jnp.clip(i - N_WSTEPS, 0, n_main - 1), 0)

    def next_idx(i):
        return (jnp.clip(i - N_WSTEPS + 1, 0, n_main - 1), 0)

    def h_idx(i):
        return (jnp.clip(i - N_WSTEPS, 0, n_main), 0)

    def proj_idx(i):
        return (jnp.clip(i - N_WSTEPS - 1, 0, n_main), 0)

    def w_idx(i):
        return (jnp.minimum(i, N_WSTEPS - 1), 0)

    const = lambda i: (0, 0)
    outs = [
        (jax.ShapeDtypeStruct((r_rows, d), F32), h_idx),
        (jax.ShapeDtypeStruct((r_rows, o_k - o_q), BF16), proj_idx),
        (jax.ShapeDtypeStruct((r_rows, o_a - o_k), F32), proj_idx),
        (jax.ShapeDtypeStruct((r_rows, o_b - o_a), F32), proj_idx),
        (jax.ShapeDtypeStruct((r_rows, o_gc - o_ga), F32), proj_idx),
        (jax.ShapeDtypeStruct((r_rows, o_end - o_gc), F32), proj_idx),
    ]
    return pl.pallas_call(
        functools.partial(_ffn_inproj_body, n_main=n_main, splits=splits),
        grid=(N_WSTEPS + n_main + 2,),
        in_specs=[
            pl.BlockSpec((t, d), cur_idx),
            pl.BlockSpec((t, d), next_idx),
            pl.BlockSpec((t, d), const),
            pl.BlockSpec((1, d), const),
            pl.BlockSpec((1, d), const),
            pl.BlockSpec((1, LANES), const),
            pl.BlockSpec((1, LANES), const),
            pl.BlockSpec((d // N_WSTEPS, n_ff), w_idx),
            pl.BlockSpec((d // N_WSTEPS, n_ff), w_idx),
            pl.BlockSpec((n_ff // N_WSTEPS, d), w_idx),
            pl.BlockSpec((d // N_WSTEPS, o_end), w_idx),
        ],
        out_specs=[pl.BlockSpec((t, s.shape[1]), idx) for s, idx in outs],
        out_shape=[s for s, _ in outs],
        scratch_shapes=[
            pltpu.VMEM((d, n_ff), BF16),
            pltpu.VMEM((d, n_ff), BF16),
            pltpu.VMEM((n_ff, d), BF16),
            pltpu.VMEM((d, o_end), BF16),
            pltpu.VMEM((t, n_ff), BF16),
            pltpu.VMEM((t, d), BF16),
            pltpu.VMEM((t, d), BF16),
        ],
        compiler_params=pltpu.CompilerParams(
            dimension_semantics=("arbitrary",), vmem_limit_bytes=VMEM_LIMIT),
        name="ffn1_inproj",
    )(x_main, x_main, x_small, g1, gm, gq2, gk2, w1, w3, w2, w_in)


def _prompt_attn_body(rb_ref, sink_ref, q_ref, kvc_ref, kvp_ref, kvm_ref, bidx_ref, *rest,
                      tiles_per_seq, n_meta, n_heads, n_weights):
    w_refs, attn_ref, wb_refs, bias_sc = rest[:n_weights], rest[n_weights], rest[n_weights + 1:-1], rest[-1]
    for w_ref, wb_ref in zip(w_refs, wb_refs, strict=True):
        wb_ref[...] = w_ref[...].astype(BF16)

    i = pl.program_id(0)
    t = q_ref.shape[0]
    n_kv = 2
    grp = n_heads // n_kv

    @pl.when(i == 0)
    def _():
        col = lax.broadcasted_iota(jnp.int32, (WINDOW, 2 * WINDOW), 1)
        for hd in range(n_heads):
            tab = _bias_table(bidx_ref[...], rb_ref, hd)
            bias_sc[0, hd] = tab
            bias_sc[1, hd] = jnp.where(col < WINDOW - n_meta, NEG, tab)

    first = (i % tiles_per_seq) == 0
    lead = jnp.where(first, 1, 0)
    lo_mask = lax.broadcasted_iota(jnp.int32, (1, LANES), 1) < HEAD_DIM
    kv_prev = jnp.where(first, kvm_ref[...], kvp_ref[...])
    kv = jnp.concatenate([kv_prev, kvc_ref[...]], axis=0)
    k, v = kv[:, :LANES], kv[:, LANES:]
    kb, vb = k.astype(BF16), v.astype(BF16)
    kr = pltpu.roll(k, HEAD_DIM, axis=1).astype(BF16)
    vr = pltpu.roll(v, HEAD_DIM, axis=1).astype(BF16)
    zero = jnp.zeros((), BF16)
    k_lo = [jnp.where(lo_mask, kb, zero), jnp.where(lo_mask, kr, zero)]
    k_hi = [jnp.where(lo_mask, zero, kr), jnp.where(lo_mask, zero, kb)]
    v_lo = [jnp.where(lo_mask, vb, zero), jnp.where(lo_mask, vr, zero)]
    v_hi = [jnp.where(lo_mask, zero, vr), jnp.where(lo_mask, zero, vb)]

    for qb in range(t // WINDOW):
        rows = slice(qb * WINDOW, (qb + 1) * WINDOW)
        keys = slice(qb * WINDOW, qb * WINDOW + 2 * WINDOW)
        tab = lead if qb == 0 else 0
        for h in range(n_kv):
            k_st = jnp.concatenate([k_lo[h][keys], k_hi[h][keys]], axis=0)
            v_st = jnp.concatenate([v_lo[h][keys], v_hi[h][keys]], axis=0)
            c0 = h * grp * HEAD_DIM
            qq = jnp.concatenate([q_ref[rows, c0:c0 + LANES], q_ref[rows, c0 + LANES:c0 + 2 * LANES]], axis=0)
            s = _dot_t(qq, k_st)
            p_parts, inv_parts = [], []
            for g2 in range(2):
                p_row, inv_row = [], []
                for par in range(2):
                    hd = h * grp + 2 * g2 + par
                    sq = s[g2 * WINDOW:(g2 + 1) * WINDOW, par * 2 * WINDOW:(par + 1) * 2 * WINDOW] + bias_sc[tab, hd]
                    sink = sink_ref[0, hd]
                    m = jnp.maximum(jnp.max(sq, axis=-1, keepdims=True), sink)
                    p = jnp.exp(sq - m)
                    den = jnp.sum(p, axis=-1, keepdims=True) + jnp.exp(sink - m)
                    p_row.append(p.astype(BF16))
                    inv_row.append(1.0 / den)
                p_parts.append(jnp.concatenate(p_row, axis=1))
                inv_parts.append(jnp.where(lo_mask, inv_row[0], inv_row[1]))
            pm = jnp.concatenate(p_parts, axis=0)
            o = _dot(pm, v_st)
            for g2 in range(2):
                c = c0 + g2 * LANES
                attn_ref[rows, c:c + LANES] = (o[g2 * WINDOW:(g2 + 1) * WINDOW] * inv_parts[g2]).astype(attn_ref.dtype)


def _prompt_attn(rel_bias, sinks2, q_all, kv_all, kv_meta, bidx, weights, m_rows, seq, n_meta):
    t = ATTN_TILE
    n_steps = m_rows // t
    n_heads = sinks2.shape[1]
    qw, kvw = q_all.shape[1], kv_all.shape[1]
    const = lambda i: (0, 0)
    row = lambda i: (i, 0)
    smem = pl.BlockSpec(memory_space=pltpu.SMEM)

    def w_spec(w):
        share = 1 if (w.shape[0] // n_steps) % 16 == 0 else 2
        assert w.shape[0] % (n_steps // share) == 0 and (w.shape[0] * share // n_steps) % 16 == 0
        return pl.BlockSpec((w.shape[0] * share // n_steps, w.shape[1]), lambda i: (i // share, 0))

    w_specs = [w_spec(w) for w in weights]
    outs = pl.pallas_call(
        functools.partial(_prompt_attn_body, tiles_per_seq=seq // t, n_meta=n_meta, n_heads=n_heads,
                          n_weights=len(weights)),
        grid=(n_steps,),
        in_specs=[
            smem, smem,
            pl.BlockSpec((t, qw), row),
            pl.BlockSpec((t, kvw), row),
            pl.BlockSpec((WINDOW, kvw), lambda i: (jnp.maximum(i * (t // WINDOW) - 1, 0), 0)),
            pl.BlockSpec((WINDOW, kvw), const),
            pl.BlockSpec((WINDOW, 2 * WINDOW), const),
        ] + w_specs,
        out_specs=[pl.BlockSpec((t, qw), row)] + w_specs,
        out_shape=[jax.ShapeDtypeStruct((m_rows, qw), BF16)] + [jax.ShapeDtypeStruct(w.shape, BF16) for w in weights],
        scratch_shapes=[pltpu.VMEM((2, n_heads, WINDOW, 2 * WINDOW), F32)],
        compiler_params=pltpu.CompilerParams(dimension_semantics=("arbitrary",)),
        name="prompt_attn",
    )(rel_bias, sinks2, q_all, kv_all, kv_all, kv_meta, bidx, *weights)
    return outs[0], outs[1:]


def _sample_mix_body(rb_ref, sink_ref, qx_ref, kvn_ref, glun_ref, ck_ref, cv_ref, st_ref, bidx_ref,
                     wdw_ref, bdw_ref, lng_ref, lnb_ref,
                     o_ref, conv_ref, nk_ref, nv_ref, nst_ref, bias_sc, *, n_heads):
    sb, w_buf = ck_ref.shape[0], ck_ref.shape[2]
    n_st = st_ref.shape[0]
    n_rows = sb * n_heads

    @pl.when(pl.program_id(0) == 0)
    def _():
        own = jnp.concatenate([_bias_table(bidx_ref[...], rb_ref, hd) for hd in range(n_heads)], axis=0)
        tiled = jnp.concatenate([jnp.concatenate([own] * sb, axis=1)] * sb, axis=0)
        row_seq = lax.broadcasted_iota(jnp.int32, tiled.shape, 0) // n_heads
        col_seq = lax.broadcasted_iota(jnp.int32, tiled.shape, 1) // w_buf
        bias_sc[...] = jnp.where(row_seq == col_seq, tiled, NEG)

    hrow = lax.broadcasted_iota(jnp.int32, (n_heads, 1), 0)
    sink = jnp.zeros((n_heads, 1), F32)
    bias_new = jnp.zeros((n_heads, 1), F32)
    for hd in range(n_heads):
        sink = jnp.where(hrow == hd, sink_ref[0, hd], sink)
        bias_new = jnp.where(hrow == hd, rb_ref[0, hd], bias_new)
    sink = jnp.concatenate([sink] * sb, axis=0)
    bias_new = jnp.concatenate([bias_new] * sb, axis=0)

    q = qx_ref[...].reshape(n_rows, LANES)
    kvn = kvn_ref[...]
    kn_rows = jnp.broadcast_to(kvn[:, :, :LANES], (sb, n_heads, LANES)).reshape(n_rows, LANES).astype(BF16)
    vn_rows = jnp.broadcast_to(kvn[:, :, LANES:], (sb, n_heads, LANES)).reshape(n_rows, LANES).astype(BF16)
    ck_all = jnp.concatenate([ck_ref[bb] for bb in range(sb)], axis=1).astype(BF16)
    cv_all = jnp.concatenate([cv_ref[bb] for bb in range(sb)], axis=1).astype(BF16)
    s_c = _dot(q, ck_all) + bias_sc[...]
    s_n = jnp.sum(q.astype(F32) * kn_rows.astype(F32), axis=-1, keepdims=True) + bias_new
    m = jnp.maximum(jnp.maximum(jnp.max(s_c, axis=-1, keepdims=True), s_n), sink)
    p_c = jnp.exp(s_c - m)
    p_n = jnp.exp(s_n - m)
    den = jnp.sum(p_c, axis=-1, keepdims=True) + p_n + jnp.exp(sink - m)
    o = _dot_t(p_c.astype(BF16), cv_all) + p_n.astype(BF16).astype(F32) * vn_rows.astype(F32)
    o_ref[...] = (o / den).reshape(sb, n_heads, LANES)

    for bb in range(sb):
        kn = kvn_ref[bb][:, :LANES]
        vn = kvn_ref[bb][:, LANES:]
        nk_ref[bb, 0:w_buf - 1, :] = ck_ref[bb].T[1:w_buf, :]
        nk_ref[bb, w_buf - 1:w_buf, :] = kn
        nv_ref[bb, 0:w_buf - 1, :] = cv_ref[bb].T[1:w_buf, :]
        nv_ref[bb, w_buf - 1:w_buf, :] = vn
        g_new = glun_ref[bb]
        st = st_ref[:, bb, :]
        y = (jnp.sum(st * wdw_ref[0:n_st, :], axis=0, keepdims=True)
             + g_new * wdw_ref[n_st:n_st + 1, :] + bdw_ref[...])
        conv_ref[bb] = _ln_swish(y, lng_ref[...], lnb_ref[...])
        nst_ref[bb, 0:n_st - 1, :] = st[1:n_st, :]
        nst_ref[bb, n_st - 1:n_st, :] = g_new


def _sample_mix(rel_bias, sinks2, qx, kv_new, glu_new, cache_kt, cache_vt, state_t, bidx_s, w_dw, b_dw, ln_g, ln_b):
    db, kw, w_buf = cache_kt.shape
    n_st, ch = state_t.shape[0], state_t.shape[2]
    n_heads = sinks2.shape[1]
    sb = SAMPLE_BLOCK
    blk = lambda i: (i, 0, 0)
    const = lambda i: (0, 0)
    smem = pl.BlockSpec(memory_space=pltpu.SMEM)
    return pl.pallas_call(
        functools.partial(_sample_mix_body, n_heads=n_heads),
        grid=(db // sb,),
        in_specs=[
            smem, smem,
            pl.BlockSpec((sb, n_heads, kw), blk),
            pl.BlockSpec((sb, 1, 2 * kw), blk),
            pl.BlockSpec((sb, 1, ch), blk),
            pl.BlockSpec((sb, kw, w_buf), blk),
            pl.BlockSpec((sb, kw, w_buf), blk),
            pl.BlockSpec((n_st, sb, ch), lambda i: (0, i, 0)),
            pl.BlockSpec((1, w_buf), const),
            pl.BlockSpec(w_dw.shape, const),
            pl.BlockSpec((1, ch), const),
            pl.BlockSpec((1, ch), const),
            pl.BlockSpec((1, ch), const),
        ],
        out_specs=[
            pl.BlockSpec((sb, n_heads, kw), blk),
            pl.BlockSpec((sb, 1, ch), blk),
            pl.BlockSpec((sb, w_buf, kw), blk),
            pl.BlockSpec((sb, w_buf, kw), blk),
            pl.BlockSpec((sb, n_st, ch), blk),
        ],
        out_shape=[
            jax.ShapeDtypeStruct((db, n_heads, kw), F32),
            jax.ShapeDtypeStruct((db, 1, ch), F32),
            jax.ShapeDtypeStruct((db, w_buf, kw), F32),
            jax.ShapeDtypeStruct((db, w_buf, kw), F32),
            jax.ShapeDtypeStruct((db, n_st, ch), F32),
        ],
        scratch_shapes=[pltpu.VMEM((sb * n_heads, sb * w_buf), F32)],
        compiler_params=pltpu.CompilerParams(dimension_semantics=("arbitrary",)),
        name="sample_mix",
    )(rel_bias, sinks2, qx, kv_new, glu_new, cache_kt, cache_vt, state_t, bidx_s, w_dw, b_dw, ln_g, ln_b)


def _conv_runs(cb_st, glu_ref, glum_ref, y_sc, wdw_ref, bdw_ref, first, n_runs):
    t = y_sc.shape[0]
    conv_w = wdw_ref.shape[0]
    n_ch = y_sc.shape[1]
    off = HALO - (conv_w - 1)
    n_a = (off + conv_w - 1) // 8 + 1
    units = [(lc, g) for lc in range(n_ch // LANES) for g in range(t // 8)]
    sizes = [len(units) // n_runs + (1 if r < len(units) % n_runs else 0) for r in range(n_runs)]
    staged = []

    def run(mine):
        if not staged:
            cb_st[0:HALO, :] = jnp.where(first, glum_ref[...], cb_st[0:HALO, :])
            cb_st[HALO:HALO + t, :] = glu_ref[...]
            staged.append(True)
        zs = {}

        def z(lc, s, g):
            if (lc, s, g) not in zs:
                ls = slice(lc * LANES, (lc + 1) * LANES)
                acc = None
                for a in range(n_a):
                    w = 8 * a + s - off
                    if 0 <= w < conv_w:
                        term = cb_st[8 * (g + a):8 * (g + a) + 8, ls] * wdw_ref[w:w + 1, ls]
                        acc = term if acc is None else acc + term
                zs[(lc, s, g)] = acc
            return zs[(lc, s, g)]

        dep = None
        for lc, g in mine:
            ls = slice(lc * LANES, (lc + 1) * LANES)
            acc = jnp.broadcast_to(bdw_ref[:, ls], (8, LANES)) + z(lc, 0, g)
            if dep is not None:
                acc = acc + dep
            for s in range(1, 8):
                acc = acc + jnp.concatenate([z(lc, s, g), z(lc, s, g + 1)], axis=0)[s:s + 8, :]
            y_sc[8 * g:8 * g + 8, ls] = acc
            dep = _dep_zero(acc)
        return dep[0:1, :]

    out, k = [], 0
    for n in sizes:
        out.append(functools.partial(run, units[k:k + n]))
        k += n
    return out


def _post_ffn_body(h_ref, am_ref, as_ref, cs_ref, ga_ref, gc_ref, g2_ref, glu_ref, glum_ref,
                   wdw_ref, bdw_ref, lng_ref, lnb_ref,
                   waob, wcob, woutb, w1b, w3b, w2b,
                   ym_ref, ys_ref,
                   hid_ref, cb_st, y_sc, conv_st, *, n_main, tiles_per_seq):
    i = pl.program_id(0)
    t_rows = y_sc.shape[0]

    @pl.when(i == 0)
    def _():
        cb_st[...] = jnp.zeros_like(cb_st)

    t = i - 1
    tn = t + 1
    first = (tn < n_main) & (lax.rem(tn, tiles_per_seq) == 0)
    n_chunks = w1b.shape[1] // FF_CHUNK

    def finish_conv():
        conv_st[...] = _ln_swish(y_sc[...], lng_ref[...], lnb_ref[...]).astype(conv_st.dtype)
        cb_st[0:HALO, :] = cb_st[t_rows:t_rows + HALO, :]

    @pl.when(t == -1)
    def _():
        for f in _conv_runs(cb_st, glu_ref, glum_ref, y_sc, wdw_ref, bdw_ref, first, n_chunks):
            f()
        finish_conv()

    @pl.when(t >= 0)
    def _():
        on_main = t < n_main
        conv_f = _conv_runs(cb_st, glu_ref, glum_ref, y_sc, wdw_ref, bdw_ref, first, n_chunks)
        fillers = [(c, c + 1, f) for c, f in enumerate(conv_f)]

        at = jnp.where(on_main, am_ref[...], as_ref[...])
        cv = jnp.where(on_main, conv_st[...], cs_ref[...])
        a = _dot(at, waob[...])
        c = _dot(cv, wcob[...])
        mix = (ga_ref[...] * a + gc_ref[...] * c).astype(BF16)
        h2 = h_ref[...] + _dot(mix, woutb[...])
        xn = _rms(h2, g2_ref[...]).astype(BF16)
        y = h2 + 0.5 * _swiglu(xn, w1b, w3b, w2b, hid_ref, fillers)
        finish_conv()

        @pl.when(on_main)
        def _():
            ym_ref[...] = y

        @pl.when(t == n_main)
        def _():
            ys_ref[...] = y


def _post_ffn(h_all, attn_main, attn_small, conv_small, ga_all, gc_all, g2, glu_all, glu_meta,
              w_dw, b_dw, ln_g, ln_b, w_ao, w_co, w_out, w1, w3, w2, seq):
    m_rows, aw = attn_main.shape
    ch = glu_all.shape[1]
    d = h_all.shape[1]
    n_ff = w1.shape[1]
    t = ROW_TILE
    n_main = m_rows // t

    def main_idx(i):
        return (jnp.clip(i - 1, 0, n_main - 1), 0)

    def next_idx(i):
        return (jnp.minimum(i, n_main - 1), 0)

    def row_idx(i):
        return (jnp.maximum(i - 1, 0), 0)

    const = lambda i: (0, 0)
    resident = lambda w: pl.BlockSpec(w.shape, const, pipeline_mode=pl.Buffered(1))
    return pl.pallas_call(
        functools.partial(_post_ffn_body, n_main=n_main, tiles_per_seq=seq // t),
        grid=(n_main + 2,),
        in_specs=[
            pl.BlockSpec((t, d), row_idx),
            pl.BlockSpec((t, aw), main_idx),
            pl.BlockSpec((t, aw), const),
            pl.BlockSpec((t, ch), const),
            pl.BlockSpec((t, d), row_idx),
            pl.BlockSpec((t, d), row_idx),
            pl.BlockSpec((1, d), const),
            pl.BlockSpec((t, ch), next_idx),
            pl.BlockSpec((HALO, ch), const),
            pl.BlockSpec(w_dw.shape, const),
            pl.BlockSpec((1, ch), const),
            pl.BlockSpec((1, ch), const),
            pl.BlockSpec((1, ch), const),
            resident(w_ao), resident(w_co), resident(w_out), resident(w1), resident(w3), resident(w2),
        ],
        out_specs=[pl.BlockSpec((t, d), main_idx), pl.BlockSpec((t, d), const)],
        out_shape=[jax.ShapeDtypeStruct((m_rows, d), F32), jax.ShapeDtypeStruct((t, d), F32)],
        scratch_shapes=[
            pltpu.VMEM((t, n_ff), BF16),
            pltpu.VMEM((HALO + t, ch), F32),
            pltpu.VMEM((t, ch), F32),
            pltpu.VMEM((t, ch), BF16),
        ],
        compiler_params=pltpu.CompilerParams(
            dimension_semantics=("arbitrary",), vmem_limit_bytes=VMEM_LIMIT),
        name="post_ffn2",
    )(h_all, attn_main, attn_small, conv_small, ga_all, gc_all, g2, glu_all, glu_meta,
      w_dw, b_dw, ln_g, ln_b, w_ao, w_co, w_out, w1, w3, w2)


def _t5_bucket(dist, n_buckets):
    max_exact = n_buckets // 2
    d = np.maximum(dist, 0)
    ratio = (np.log(np.maximum(d, 1).astype(np.float32) / np.float32(max_exact))
             / np.float32(math.log(REL_MAX_DIST / max_exact)))
    large = np.minimum(max_exact + (ratio * np.float32(n_buckets - max_exact)).astype(np.int32), n_buckets - 1)
    return np.where(d < max_exact, d, large).astype(np.int32)


def _bucket_or_masked(dist, n_buckets):
    ok = (dist >= 0) & (dist < WINDOW)
    return np.where(ok, _t5_bucket(dist, n_buckets), -1).astype(np.int32)


def _rows(x, start, n):
    return lax.slice_in_dim(x, start, start + n, axis=0)


def kernel(x_prompt, x_sample, cache_k, cache_v, state_conv, meta_tokens, ffn1_norm, ffn1_w1, ffn1_w3, ffn1_w2, mix_norm, w_in, q_norm, k_norm, rel_bias, sinks, w_attn_out, w_dw, b_dw, conv_ln_g, conv_ln_b, w_conv_out, w_out, ffn2_norm, ffn2_w1, ffn2_w3, ffn2_w2):
    n_b, seq, d = x_prompt.shape
    db = x_sample.shape[0]
    n_meta = meta_tokens.shape[0]
    n_heads = sinks.shape[0]
    w_buf, n_kv, hd = cache_k.shape[1], cache_k.shape[2], cache_k.shape[3]
    ch = w_dw.shape[1]
    n_st = state_conv.shape[1]
    n_buckets = rel_bias.shape[0]
    aw, kvw = n_heads * hd, n_kv * hd
    t = ROW_TILE
    m_rows = n_b * seq
    assert hd == HEAD_DIM and kvw == LANES and n_kv == 2 and n_heads == 8 and w_buf == WINDOW
    assert x_sample.shape[1] == 1 and seq % t == 0 and seq % ATTN_TILE == 0
    assert n_meta + db <= t and db % SAMPLE_BLOCK == 0
    assert n_meta <= HALO and n_meta <= WINDOW and n_st == w_dw.shape[0] - 1 and n_st <= HALO
    splits = tuple(int(v) for v in np.cumsum([0, aw, kvw, kvw, ch, ch, d, d]))
    assert splits[-1] == w_in.shape[1]

    row = lambda v: v.reshape(1, -1)
    x_main = x_prompt.reshape(m_rows, d)
    x_small = jnp.concatenate(
        [meta_tokens, x_sample.reshape(db, d), jnp.zeros((t - n_meta - db, d), F32)], axis=0)
    gq2 = row(jnp.concatenate([q_norm, q_norm]))
    gk2 = row(jnp.concatenate([k_norm, k_norm]))

    h_all, q_all, kv_all, glu_all, ga_all, gc_all = _ffn_inproj(
        x_main, x_small, row(ffn1_norm), row(mix_norm), gq2, gk2, ffn1_w1, ffn1_w3, ffn1_w2, w_in, splits)

    sinks2 = row(sinks)
    b_dw2, ln_g2, ln_b2 = row(b_dw), row(conv_ln_g), row(conv_ln_b)

    kv_meta = jnp.concatenate([jnp.zeros((WINDOW - n_meta, 2 * kvw), F32), _rows(kv_all, m_rows, n_meta)], axis=0)
    glu_meta = jnp.concatenate([jnp.zeros((HALO - n_meta, ch), F32), _rows(glu_all, m_rows, n_meta)], axis=0)
    dist = np.arange(WINDOW)[:, None] + WINDOW - np.arange(2 * WINDOW)[None, :]
    bidx = jnp.asarray(_bucket_or_masked(dist, n_buckets))
    attn_main, w_bf16 = _prompt_attn(
        rel_bias, sinks2, q_all, kv_all, kv_meta, bidx,
        (w_attn_out, w_conv_out, w_out, ffn2_w1, ffn2_w3, ffn2_w2), m_rows, seq, n_meta)

    s0 = m_rows + n_meta
    q_s = _rows(q_all, s0, db).reshape(db, n_heads, hd)
    lane_half = (np.arange(kvw)[None, :] // hd) == (np.arange(n_heads)[:, None] // (n_heads // n_kv))
    qx = jnp.where(jnp.asarray(lane_half)[None], jnp.concatenate([q_s, q_s], axis=-1), 0).astype(BF16)
    kv_new = _rows(kv_all, s0, db).reshape(db, 1, 2 * kvw)
    glu_new = _rows(glu_all, s0, db).reshape(db, 1, ch)
    bidx_s = jnp.asarray(_bucket_or_masked(w_buf - np.arange(w_buf)[None, :], n_buckets))
    o_s, conv_s, new_k_s, new_v_s, new_conv_s = _sample_mix(
        rel_bias, sinks2, qx, kv_new, glu_new,
        jnp.transpose(cache_k, (0, 2, 3, 1)).reshape(db, kvw, w_buf),
        jnp.transpose(cache_v, (0, 2, 3, 1)).reshape(db, kvw, w_buf),
        jnp.transpose(state_conv, (1, 0, 2)), bidx_s, w_dw, b_dw2, ln_g2, ln_b2)
    o_s = jnp.where(jnp.asarray(lane_half)[None], o_s, 0.0).reshape(db, n_heads, n_kv, hd).sum(axis=2)
    attn_small = jnp.concatenate(
        [jnp.zeros((n_meta, aw), BF16), o_s.reshape(db, aw).astype(BF16), jnp.zeros((t - n_meta - db, aw), BF16)],
        axis=0)
    conv_small = jnp.concatenate(
        [jnp.zeros((n_meta, ch), BF16), conv_s.reshape(db, ch).astype(BF16), jnp.zeros((t - n_meta - db, ch), BF16)],
        axis=0)

    y_main, y_small = _post_ffn(h_all, attn_main, attn_small, conv_small, ga_all, gc_all, row(ffn2_norm),
                                glu_all, glu_meta, w_dw, b_dw2, ln_g2, ln_b2, *w_bf16, seq)

    kv_t = jnp.stack([_rows(kv_all, (s + 1) * seq - WINDOW, WINDOW) for s in range(n_b)])
    glu_t = jnp.stack([_rows(glu_all, (s + 1) * seq - n_st, n_st) for s in range(n_b)])
    return (
        y_main.reshape(n_b, seq, d),
        _rows(y_small, n_meta, db).reshape(db, 1, d),
        kv_t[..., :kvw].reshape(n_b, WINDOW, n_kv, hd),
        kv_t[..., kvw:].reshape(n_b, WINDOW, n_kv, hd),
        glu_t,
        new_k_s.reshape(db, w_buf, n_kv, hd),
        new_v_s.reshape(db, w_buf, n_kv, hd),
        new_conv_s,
    )
```

```python
import functools
import math

import jax
import jax.numpy as jnp
import numpy as np
from jax import lax
from jax.experimental import pallas as pl
from jax.experimental.pallas import tpu as pltpu

F32 = jnp.float32
BF16 = jnp.bfloat16

EPS = 1e-6
NEG = -1e30
WINDOW = 128
REL_MAX_DIST = 128
HEAD_DIM = 64
LANES = 128
ROW_TILE = 256
ATTN_TILE = 2048
FF_CHUNK = 256
N_WSTEPS = 8
HALO = 32
SAMPLE_BLOCK = 16
VMEM_LIMIT = 56 * 1024 * 1024


def _dot(a, b):
    return jnp.dot(a, b, preferred_element_type=F32)


def _dot_t(a, b):
    return lax.dot_general(a, b, (((1,), (1,)), ((), ())), preferred_element_type=F32)


def _rms(x, g):
    return x * lax.rsqrt(jnp.mean(x * x, axis=-1, keepdims=True) + EPS) * g


def _pair_rms(x, g2, lo_mask):
    sq = x * x
    lo = jnp.sum(jnp.where(lo_mask, sq, 0.0), axis=-1, keepdims=True) * (1.0 / HEAD_DIM)
    hi = jnp.sum(jnp.where(lo_mask, 0.0, sq), axis=-1, keepdims=True) * (1.0 / HEAD_DIM)
    r = jnp.where(lo_mask, lax.rsqrt(lo + EPS), lax.rsqrt(hi + EPS))
    return x * r * g2


def _dep_zero(row):
    u = pltpu.bitcast(row, jnp.uint32)
    return pltpu.bitcast((u >> 16) >> 16, F32)


def _swiglu(xn, w1b, w3b, w2b, hid_ref, fillers=()):
    n_ff = w1b.shape[1]
    n_chunks = n_ff // FF_CHUNK
    due = {}
    for c in range(n_chunks):
        sl = slice(c * FF_CHUNK, (c + 1) * FF_CHUNK)
        a = _dot(xn, w1b[:, sl])
        b = _dot(xn, w3b[:, sl])
        for z in due.pop(c, ()):
            b = b + jnp.concatenate([z] * (FF_CHUNK // LANES), axis=1)
        for issue, when, thunk in fillers:
            if issue == c:
                due.setdefault(min(when, n_chunks), []).append(_dep_zero(thunk()))
        hid_ref[:, sl] = (a * jax.nn.sigmoid(a) * b).astype(BF16)
    out = _dot(hid_ref[...], w2b[...])
    for z in due.pop(n_chunks, ()):
        out = out + jnp.concatenate([z] * (out.shape[1] // LANES), axis=1)
    assert not due
    return out


def _store_chunk(dst, src, i):
    rows = src.shape[0]
    r = pl.multiple_of(i * rows, 16)
    dst[pl.ds(r, rows), :] = src[...].astype(BF16)


def _bias_table(bidx, rb_ref, head):
    tab = jnp.full(bidx.shape, NEG, F32)
    for b in range(rb_ref.shape[0]):
        tab = jnp.where(bidx == b, rb_ref[b, head], tab)
    return tab


def _ln_swish(y, g, b):
    mu = jnp.mean(y, axis=-1, keepdims=True)
    yc = y - mu
    var = jnp.mean(yc * yc, axis=-1, keepdims=True)
    z = yc * lax.rsqrt(var + EPS) * g + b
    return z * jax.nn.sigmoid(z)


def _ffn_inproj_body(xc_ref, xn_ref, xs_ref, g1_ref, gm_ref, gq_ref, gk_ref, w1c, w3c, w2c, winc,
                     h_ref, q_ref, kv_ref, glu_ref, ga_ref, gc_ref,
                     w1b, w3b, w2b, winb, hid_ref, xn_st, u_st, *, n_main, splits):
    i = pl.program_id(0)

    @pl.when(i < N_WSTEPS)
    def _():
        _store_chunk(w1b, w1c, i)
        _store_chunk(w3b, w3c, i)
        _store_chunk(w2b, w2c, i)
        _store_chunk(winb, winc, i)

    s = i - N_WSTEPS

    def project():
        u = u_st[...]
        o_q, o_k, o_v, o_a, o_b, o_ga, o_gc, o_end = splits
        lo_mask = lax.broadcasted_iota(jnp.int32, (1, LANES), 1) < HEAD_DIM
        zq = _dot(u, winb[:, o_q:o_k])
        for p in range((o_k - o_q) // LANES):
            sl = slice(p * LANES, (p + 1) * LANES)
            q_ref[:, sl] = (_pair_rms(zq[:, sl], gq_ref[...], lo_mask) * (HEAD_DIM ** -0.5)).astype(BF16)
        zkv = _dot(u, winb[:, o_k:o_a])
        kv_ref[:, :LANES] = _pair_rms(zkv[:, :LANES], gk_ref[...], lo_mask)
        kv_ref[:, LANES:] = zkv[:, LANES:]
        za = _dot(u, winb[:, o_a:o_b])
        zb = _dot(u, winb[:, o_b:o_ga])
        glu_ref[...] = za * jax.nn.sigmoid(zb)
        ga_ref[...] = jax.nn.sigmoid(_dot(u, winb[:, o_ga:o_gc])).astype(ga_ref.dtype)
        gc_ref[...] = jax.nn.sigmoid(_dot(u, winb[:, o_gc:o_end])).astype(gc_ref.dtype)

    def normalise_next():
        x_next = jnp.where(s + 1 < n_main, xn_ref[...], xs_ref[...])
        xn_st[...] = _rms(x_next, g1_ref[...]).astype(BF16)

    @pl.when(i == 0)
    def _():
        u_st[...] = jnp.zeros_like(u_st)

    @pl.when(s == -1)
    def _():
        normalise_next()

    @pl.when((s >= 0) & (s <= n_main))
    def _():
        project()
        x = jnp.where(s < n_main, xc_ref[...], xs_ref[...])
        h = x + 0.5 * _swiglu(xn_st[...], w1b, w3b, w2b, hid_ref)
        h_ref[...] = h
        u_st[...] = _rms(h, gm_ref[...]).astype(BF16)
        normalise_next()

    @pl.when(s == n_main + 1)
    def _():
        project()


def _ffn_inproj(x_main, x_small, g1, gm, gq2, gk2, w1, w3, w2, w_in, splits):
    m_rows, d = x_main.shape
    n_ff = w1.shape[1]
    n_main = m_rows // ROW_TILE
    r_rows = m_rows + ROW_TILE
    t = ROW_TILE
    o_q, o_k, o_v, o_a, o_b, o_ga, o_gc, o_end = splits
    assert o_v - o_k == LANES and o_a - o_v == LANES and (o_k - o_q) % LANES == 0

    def cur_idx(i):
        return (jnp.clip(i - N_WSTEPS, 0, n_main - 1), 0)

    def next_idx(i):
        return (jnp.clip(i - N_WSTEPS + 1, 0, n_main - 1), 0)

    def h_idx(i):
        return (jnp.clip(i - N_WSTEPS, 0, n_main), 0)

    def proj_idx(i):
        return (jnp.clip(i - N_WSTEPS - 1, 0, n_main), 0)

    def w_idx(i):
        return (jnp.minimum(i, N_WSTEPS - 1), 0)

    const = lambda i: (0, 0)
    outs = [
        (jax.ShapeDtypeStruct((r_rows, d), F32), h_idx),
        (jax.ShapeDtypeStruct((r_rows, o_k - o_q), BF16), proj_idx),
        (jax.ShapeDtypeStruct((r_rows, o_a - o_k), F32), proj_idx),
        (jax.ShapeDtypeStruct((r_rows, o_b - o_a), F32), proj_idx),
        (jax.ShapeDtypeStruct((r_rows, o_gc - o_ga), BF16), proj_idx),
        (jax.ShapeDtypeStruct((r_rows, o_end - o_gc), BF16), proj_idx),
    ]
    return pl.pallas_call(
        functools.partial(_ffn_inproj_body, n_main=n_main, splits=splits),
        grid=(N_WSTEPS + n_main + 2,),
        in_specs=[
            pl.BlockSpec((t, d), cur_idx),
            pl.BlockSpec((t, d), next_idx),
            pl.BlockSpec((t, d), const),
            pl.BlockSpec((1, d), const),
            pl.BlockSpec((1, d), const),
            pl.BlockSpec((1, LANES), const),
            pl.BlockSpec((1, LANES), const),
            pl.BlockSpec((d // N_WSTEPS, n_ff), w_idx),
            pl.BlockSpec((d // N_WSTEPS, n_ff), w_idx),
            pl.BlockSpec((n_ff // N_WSTEPS, d), w_idx),
            pl.BlockSpec((d // N_WSTEPS, o_end), w_idx),
        ],
        out_specs=[pl.BlockSpec((t, s.shape[1]), idx) for s, idx in outs],
        out_shape=[s for s, _ in outs],
        scratch_shapes=[
            pltpu.VMEM((d, n_ff), BF16),
            pltpu.VMEM((d, n_ff), BF16),
            pltpu.VMEM((n_ff, d), BF16),
            pltpu.VMEM((d, o_end), BF16),
            pltpu.VMEM((t, n_ff), BF16),
            pltpu.VMEM((t, d), BF16),
            pltpu.VMEM((t, d), BF16),
        ],
        compiler_params=pltpu.CompilerParams(
            dimension_semantics=("arbitrary",), vmem_limit_bytes=VMEM_LIMIT),
        name="ffn1_inproj",
    )(x_main, x_main, x_small, g1, gm, gq2, gk2, w1, w3, w2, w_in)


def _prompt_attn_body(rb_ref, sink_ref, q_ref, kvc_ref, kvp_ref, kvm_ref, bidx_ref, *rest,
                      tiles_per_seq, n_meta, n_heads, n_weights):
    w_refs, attn_ref, wb_refs, bias_sc = rest[:n_weights], rest[n_weights], rest[n_weights + 1:-1], rest[-1]
    for w_ref, wb_ref in zip(w_refs, wb_refs, strict=True):
        wb_ref[...] = w_ref[...].astype(BF16)

    i = pl.program_id(0)
    t = q_ref.shape[0]
    n_kv = 2
    grp = n_heads // n_kv

    @pl.when(i == 0)
    def _():
        col = lax.broadcasted_iota(jnp.int32, (WINDOW, 2 * WINDOW), 1)
        for hd in range(n_heads):
            tab = _bias_table(bidx_ref[...], rb_ref, hd)
            bias_sc[0, hd] = tab
            bias_sc[1, hd] = jnp.where(col < WINDOW - n_meta, NEG, tab)

    first = (i % tiles_per_seq) == 0
    lead = jnp.where(first, 1, 0)
    lo_mask = lax.broadcasted_iota(jnp.int32, (1, LANES), 1) < HEAD_DIM
    kv_prev = jnp.where(first, kvm_ref[...], kvp_ref[...])
    kv = jnp.concatenate([kv_prev, kvc_ref[...]], axis=0)
    k, v = kv[:, :LANES], kv[:, LANES:]
    kb, vb = k.astype(BF16), v.astype(BF16)
    kr = pltpu.roll(k, HEAD_DIM, axis=1).astype(BF16)
    vr = pltpu.roll(v, HEAD_DIM, axis=1).astype(BF16)
    zero = jnp.zeros((), BF16)
    k_lo = [jnp.where(lo_mask, kb, zero), jnp.where(lo_mask, kr, zero)]
    k_hi = [jnp.where(lo_mask, zero, kr), jnp.where(lo_mask, zero, kb)]
    v_lo = [jnp.where(lo_mask, vb, zero), jnp.where(lo_mask, vr, zero)]
    v_hi = [jnp.where(lo_mask, zero, vr), jnp.where(lo_mask, zero, vb)]

    for qb in range(t // WINDOW):
        rows = slice(qb * WINDOW, (qb + 1) * WINDOW)
        keys = slice(qb * WINDOW, qb * WINDOW + 2 * WINDOW)
        tab = lead if qb == 0 else 0
        for h in range(n_kv):
            k_st = jnp.concatenate([k_lo[h][keys], k_hi[h][keys]], axis=0)
            v_st = jnp.concatenate([v_lo[h][keys], v_hi[h][keys]], axis=0)
            c0 = h * grp * HEAD_DIM
            qq = jnp.concatenate([q_ref[rows, c0:c0 + LANES], q_ref[rows, c0 + LANES:c0 + 2 * LANES]], axis=0)
            s = _dot_t(qq, k_st)
            p_parts, inv_parts = [], []
            for g2 in range(2):
                p_row, inv_row = [], []
                for par in range(2):
                    hd = h * grp + 2 * g2 + par
                    sq = s[g2 * WINDOW:(g2 + 1) * WINDOW, par * 2 * WINDOW:(par + 1) * 2 * WINDOW] + bias_sc[tab, hd]
                    sink = sink_ref[0, hd]
                    m = jnp.maximum(jnp.max(sq, axis=-1, keepdims=True), sink)
                    p = jnp.exp(sq - m)
                    den = jnp.sum(p, axis=-1, keepdims=True) + jnp.exp(sink - m)
                    p_row.append(p.astype(BF16))
                    inv_row.append(1.0 / den)
                p_parts.append(jnp.concatenate(p_row, axis=1))
                inv_parts.append(jnp.where(lo_mask, inv_row[0], inv_row[1]))
            pm = jnp.concatenate(p_parts, axis=0)
            o = _dot(pm, v_st)
            for g2 in range(2):
                c = c0 + g2 * LANES
                attn_ref[rows, c:c + LANES] = (o[g2 * WINDOW:(g2 + 1) * WINDOW] * inv_parts[g2]).astype(attn_ref.dtype)


def _prompt_attn(rel_bias, sinks2, q_all, kv_all, kv_meta, bidx, weights, m_rows, seq, n_meta):
    t = ATTN_TILE
    n_steps = m_rows // t
    n_heads = sinks2.shape[1]
    qw, kvw = q_all.shape[1], kv_all.shape[1]
    const = lambda i: (0, 0)
    row = lambda i: (i, 0)
    smem = pl.BlockSpec(memory_space=pltpu.SMEM)

    def w_spec(w):
        share = 1 if (w.shape[0] // n_steps) % 16 == 0 else 2
        assert w.shape[0] % (n_steps // share) == 0 and (w.shape[0] * share // n_steps) % 16 == 0
        return pl.BlockSpec((w.shape[0] * share // n_steps, w.shape[1]), lambda i: (i // share, 0))

    w_specs = [w_spec(w) for w in weights]
    outs = pl.pallas_call(
        functools.partial(_prompt_attn_body, tiles_per_seq=seq // t, n_meta=n_meta, n_heads=n_heads,
                          n_weights=len(weights)),
        grid=(n_steps,),
        in_specs=[
            smem, smem,
            pl.BlockSpec((t, qw), row),
            pl.BlockSpec((t, kvw), row),
            pl.BlockSpec((WINDOW, kvw), lambda i: (jnp.maximum(i * (t // WINDOW) - 1, 0), 0)),
            pl.BlockSpec((WINDOW, kvw), const),
            pl.BlockSpec((WINDOW, 2 * WINDOW), const),
        ] + w_specs,
        out_specs=[pl.BlockSpec((t, qw), row)] + w_specs,
        out_shape=[jax.ShapeDtypeStruct((m_rows, qw), BF16)] + [jax.ShapeDtypeStruct(w.shape, BF16) for w in weights],
        scratch_shapes=[pltpu.VMEM((2, n_heads, WINDOW, 2 * WINDOW), F32)],
        compiler_params=pltpu.CompilerParams(dimension_semantics=("arbitrary",)),
        name="prompt_attn",
    )(rel_bias, sinks2, q_all, kv_all, kv_all, kv_meta, bidx, *weights)
    return outs[0], outs[1:]


def _sample_mix_body(rb_ref, sink_ref, qx_ref, kvn_ref, glun_ref, ck_ref, cv_ref, st_ref, bidx_ref,
                     wdw_ref, bdw_ref, lng_ref, lnb_ref,
                     o_ref, conv_ref, nk_ref, nv_ref, nst_ref, bias_sc, *, n_heads):
    sb, w_buf = ck_ref.shape[0], ck_ref.shape[2]
    n_st = st_ref.shape[0]
    n_rows = sb * n_heads

    @pl.when(pl.program_id(0) == 0)
    def _():
        own = jnp.concatenate([_bias_table(bidx_ref[...], rb_ref, hd) for hd in range(n_heads)], axis=0)
        tiled = jnp.concatenate([jnp.concatenate([own] * sb, axis=1)] * sb, axis=0)
        row_seq = lax.broadcasted_iota(jnp.int32, tiled.shape, 0) // n_heads
        col_seq = lax.broadcasted_iota(jnp.int32, tiled.shape, 1) // w_buf
        bias_sc[...] = jnp.where(row_seq == col_seq, tiled, NEG)

    hrow = lax.broadcasted_iota(jnp.int32, (n_heads, 1), 0)
    sink = jnp.zeros((n_heads, 1), F32)
    bias_new = jnp.zeros((n_heads, 1), F32)
    for hd in range(n_heads):
        sink = jnp.where(hrow == hd, sink_ref[0, hd], sink)
        bias_new = jnp.where(hrow == hd, rb_ref[0, hd], bias_new)
    sink = jnp.concatenate([sink] * sb, axis=0)
    bias_new = jnp.concatenate([bias_new] * sb, axis=0)

    q = qx_ref[...].reshape(n_rows, LANES)
    kvn = kvn_ref[...]
    kn_rows = jnp.broadcast_to(kvn[:, :, :LANES], (sb, n_heads, LANES)).reshape(n_rows, LANES).astype(BF16)
    vn_rows = jnp.broadcast_to(kvn[:, :, LANES:], (sb, n_heads, LANES)).reshape(n_rows, LANES).astype(BF16)
    ck_all = jnp.concatenate([ck_ref[bb] for bb in range(sb)], axis=1).astype(BF16)
    cv_all = jnp.concatenate([cv_ref[bb] for bb in range(sb)], axis=1).astype(BF16)
    s_c = _dot(q, ck_all) + bias_sc[...]
    s_n = jnp.sum(q.astype(F32) * kn_rows.astype(F32), axis=-1, keepdims=True) + bias_new
    m = jnp.maximum(jnp.maximum(jnp.max(s_c, axis=-1, keepdims=True), s_n), sink)
    p_c = jnp.exp(s_c - m)
    p_n = jnp.exp(s_n - m)
    den = jnp.sum(p_c, axis=-1, keepdims=True) + p_n + jnp.exp(sink - m)
    o = _dot_t(p_c.astype(BF16), cv_all) + p_n.astype(BF16).astype(F32) * vn_rows.astype(F32)
    o_ref[...] = (o / den).reshape(sb, n_heads, LANES)

    for bb in range(sb):
        kn = kvn_ref[bb][:, :LANES]
        vn = kvn_ref[bb][:, LANES:]
        nk_ref[bb, 0:w_buf - 1, :] = ck_ref[bb].T[1:w_buf, :]
        nk_ref[bb, w_buf - 1:w_buf, :] = kn
        nv_ref[bb, 0:w_buf - 1, :] = cv_ref[bb].T[1:w_buf, :]
        nv_ref[bb, w_buf - 1:w_buf, :] = vn
        g_new = glun_ref[bb]
        st = st_ref[:, bb, :]
        y = (jnp.sum(st * wdw_ref[0:n_st, :], axis=0, keepdims=True)
             + g_new * wdw_ref[n_st:n_st + 1, :] + bdw_ref[...])
        conv_ref[bb] = _ln_swish(y, lng_ref[...], lnb_ref[...])
        nst_ref[bb, 0:n_st - 1, :] = st[1:n_st, :]
        nst_ref[bb, n_st - 1:n_st, :] = g_new


def _sample_mix(rel_bias, sinks2, qx, kv_new, glu_new, cache_kt, cache_vt, state_t, bidx_s, w_dw, b_dw, ln_g, ln_b):
    db, kw, w_buf = cache_kt.shape
    n_st, ch = state_t.shape[0], state_t.shape[2]
    n_heads = sinks2.shape[1]
    sb = SAMPLE_BLOCK
    blk = lambda i: (i, 0, 0)
    const = lambda i: (0, 0)
    smem = pl.BlockSpec(memory_space=pltpu.SMEM)
    return pl.pallas_call(
        functools.partial(_sample_mix_body, n_heads=n_heads),
        grid=(db // sb,),
        in_specs=[
            smem, smem,
            pl.BlockSpec((sb, n_heads, kw), blk),
            pl.BlockSpec((sb, 1, 2 * kw), blk),
            pl.BlockSpec((sb, 1, ch), blk),
            pl.BlockSpec((sb, kw, w_buf), blk),
            pl.BlockSpec((sb, kw, w_buf), blk),
            pl.BlockSpec((n_st, sb, ch), lambda i: (0, i, 0)),
            pl.BlockSpec((1, w_buf), const),
            pl.BlockSpec(w_dw.shape, const),
            pl.BlockSpec((1, ch), const),
            pl.BlockSpec((1, ch), const),
            pl.BlockSpec((1, ch), const),
        ],
        out_specs=[
            pl.BlockSpec((sb, n_heads, kw), blk),
            pl.BlockSpec((sb, 1, ch), blk),
            pl.BlockSpec((sb, w_buf, kw), blk),
            pl.BlockSpec((sb, w_buf, kw), blk),
            pl.BlockSpec((sb, n_st, ch), blk),
        ],
        out_shape=[
            jax.ShapeDtypeStruct((db, n_heads, kw), F32),
            jax.ShapeDtypeStruct((db, 1, ch), F32),
            jax.ShapeDtypeStruct((db, w_buf, kw), F32),
            jax.ShapeDtypeStruct((db, w_buf, kw), F32),
            jax.ShapeDtypeStruct((db, n_st, ch), F32),
        ],
        scratch_shapes=[pltpu.VMEM((sb * n_heads, sb * w_buf), F32)],
        compiler_params=pltpu.CompilerParams(dimension_semantics=("arbitrary",)),
        name="sample_mix",
    )(rel_bias, sinks2, qx, kv_new, glu_new, cache_kt, cache_vt, state_t, bidx_s, w_dw, b_dw, ln_g, ln_b)


def _conv_runs(cb_st, glu_ref, glum_ref, y_sc, wdw_ref, bdw_ref, first, n_runs):
    t = y_sc.shape[0]
    conv_w = wdw_ref.shape[0]
    n_ch = y_sc.shape[1]
    off = HALO - (conv_w - 1)
    n_a = (off + conv_w - 1) // 8 + 1
    units = [(lc, g) for lc in range(n_ch // LANES) for g in range(t // 8)]
    sizes = [len(units) // n_runs + (1 if r < len(units) % n_runs else 0) for r in range(n_runs)]
    staged = []

    def run(mine):
        if not staged:
            cb_st[0:HALO, :] = jnp.where(first, glum_ref[...], cb_st[0:HALO, :])
            cb_st[HALO:HALO + t, :] = glu_ref[...]
            staged.append(True)
        zs = {}

        def z(lc, s, g):
            if (lc, s, g) not in zs:
                ls = slice(lc * LANES, (lc + 1) * LANES)
                acc = None
                for a in range(n_a):
                    w = 8 * a + s - off
                    if 0 <= w < conv_w:
                        term = cb_st[8 * (g + a):8 * (g + a) + 8, ls] * wdw_ref[w:w + 1, ls]
                        acc = term if acc is None else acc + term
                zs[(lc, s, g)] = acc
            return zs[(lc, s, g)]

        dep = None
        for lc, g in mine:
            ls = slice(lc * LANES, (lc + 1) * LANES)
            acc = jnp.broadcast_to(bdw_ref[:, ls], (8, LANES)) + z(lc, 0, g)
            if dep is not None:
                acc = acc + dep
            for s in range(1, 8):
                acc = acc + jnp.concatenate([z(lc, s, g), z(lc, s, g + 1)], axis=0)[s:s + 8, :]
            y_sc[8 * g:8 * g + 8, ls] = acc
            dep = _dep_zero(acc)
        return dep[0:1, :]

    out, k = [], 0
    for n in sizes:
        out.append(functools.partial(run, units[k:k + n]))
        k += n
    return out


def _post_ffn_body(h_ref, am_ref, as_ref, cs_ref, ga_ref, gc_ref, g2_ref, glu_ref, glum_ref,
                   wdw_ref, bdw_ref, lng_ref, lnb_ref,
                   waob, wcob, woutb, w1b, w3b, w2b,
                   ym_ref, ys_ref,
                   hid_ref, cb_st, y_sc, conv_st, *, n_main, tiles_per_seq):
    i = pl.program_id(0)
    t_rows = y_sc.shape[0]

    @pl.when(i == 0)
    def _():
        cb_st[...] = jnp.zeros_like(cb_st)

    t = i - 1
    tn = t + 1
    first = (tn < n_main) & (lax.rem(tn, tiles_per_seq) == 0)
    n_chunks = w1b.shape[1] // FF_CHUNK

    def finish_conv():
        conv_st[...] = _ln_swish(y_sc[...], lng_ref[...], lnb_ref[...]).astype(conv_st.dtype)
        cb_st[0:HALO, :] = cb_st[t_rows:t_rows + HALO, :]

    @pl.when(t == -1)
    def _():
        for f in _conv_runs(cb_st, glu_ref, glum_ref, y_sc, wdw_ref, bdw_ref, first, n_chunks):
            f()
        finish_conv()

    @pl.when(t >= 0)
    def _():
        on_main = t < n_main
        conv_f = _conv_runs(cb_st, glu_ref, glum_ref, y_sc, wdw_ref, bdw_ref, first, n_chunks)
        fillers = [(c, c + 1, f) for c, f in enumerate(conv_f)]

        at = jnp.where(on_main, am_ref[...], as_ref[...])
        cv = jnp.where(on_main, conv_st[...], cs_ref[...])
        a = _dot(at, waob[...])
        c = _dot(cv, wcob[...])
        mix = (ga_ref[...].astype(F32) * a + gc_ref[...].astype(F32) * c).astype(BF16)
        h2 = h_ref[...] + _dot(mix, woutb[...])
        xn = _rms(h2, g2_ref[...]).astype(BF16)
        y = h2 + 0.5 * _swiglu(xn, w1b, w3b, w2b, hid_ref, fillers)
        finish_conv()

        @pl.when(on_main)
        def _():
            ym_ref[...] = y

        @pl.when(t == n_main)
        def _():
            ys_ref[...] = y


def _post_ffn(h_all, attn_main, attn_small, conv_small, ga_all, gc_all, g2, glu_all, glu_meta,
              w_dw, b_dw, ln_g, ln_b, w_ao, w_co, w_out, w1, w3, w2, seq):
    m_rows, aw = attn_main.shape
    ch = glu_all.shape[1]
    d = h_all.shape[1]
    n_ff = w1.shape[1]
    t = ROW_TILE
    n_main = m_rows // t

    def main_idx(i):
        return (jnp.clip(i - 1, 0, n_main - 1), 0)

    def next_idx(i):
        return (jnp.minimum(i, n_main - 1), 0)

    def row_idx(i):
        return (jnp.maximum(i - 1, 0), 0)

    const = lambda i: (0, 0)
    resident = lambda w: pl.BlockSpec(w.shape, const, pipeline_mode=pl.Buffered(1))
    return pl.pallas_call(
        functools.partial(_post_ffn_body, n_main=n_main, tiles_per_seq=seq // t),
        grid=(n_main + 2,),
        in_specs=[
            pl.BlockSpec((t, d), row_idx),
            pl.BlockSpec((t, aw), main_idx),
            pl.BlockSpec((t, aw), const),
            pl.BlockSpec((t, ch), const),
            pl.BlockSpec((t, d), row_idx),
            pl.BlockSpec((t, d), row_idx),
            pl.BlockSpec((1, d), const),
            pl.BlockSpec((t, ch), next_idx),
            pl.BlockSpec((HALO, ch), const),
            pl.BlockSpec(w_dw.shape, const),
            pl.BlockSpec((1, ch), const),
            pl.BlockSpec((1, ch), const),
            pl.BlockSpec((1, ch), const),
            resident(w_ao), resident(w_co), resident(w_out), resident(w1), resident(w3), resident(w2),
        ],
        out_specs=[pl.BlockSpec((t, d), main_idx), pl.BlockSpec((t, d), const)],
        out_shape=[jax.ShapeDtypeStruct((m_rows, d), F32), jax.ShapeDtypeStruct((t, d), F32)],
        scratch_shapes=[
            pltpu.VMEM((t, n_ff), BF16),
            pltpu.VMEM((HALO + t, ch), F32),
            pltpu.VMEM((t, ch), F32),
            pltpu.VMEM((t, ch), BF16),
        ],
        compiler_params=pltpu.CompilerParams(
            dimension_semantics=("arbitrary",), vmem_limit_bytes=VMEM_LIMIT),
        name="post_ffn2",
    )(h_all, attn_main, attn_small, conv_small, ga_all, gc_all, g2, glu_all, glu_meta,
      w_dw, b_dw, ln_g, ln_b, w_ao, w_co, w_out, w1, w3, w2)


def _t5_bucket(dist, n_buckets):
    max_exact = n_buckets // 2
    d = np.maximum(dist, 0)
    ratio = (np.log(np.maximum(d, 1).astype(np.float32) / np.float32(max_exact))
             / np.float32(math.log(REL_MAX_DIST / max_exact)))
    large = np.minimum(max_exact + (ratio * np.float32(n_buckets - max_exact)).astype(np.int32), n_buckets - 1)
    return np.where(d < max_exact, d, large).astype(np.int32)


def _bucket_or_masked(dist, n_buckets):
    ok = (dist >= 0) & (dist < WINDOW)
    return np.where(ok, _t5_bucket(dist, n_buckets), -1).astype(np.int32)


def _rows(x, start, n):
    return lax.slice_in_dim(x, start, start + n, axis=0)


def kernel(x_prompt, x_sample, cache_k, cache_v, state_conv, meta_tokens, ffn1_norm, ffn1_w1, ffn1_w3, ffn1_w2, mix_norm, w_in, q_norm, k_norm, rel_bias, sinks, w_attn_out, w_dw, b_dw, conv_ln_g, conv_ln_b, w_conv_out, w_out, ffn2_norm, ffn2_w1, ffn2_w3, ffn2_w2):
    n_b, seq, d = x_prompt.shape
    db = x_sample.shape[0]
    n_meta = meta_tokens.shape[0]
    n_heads = sinks.shape[0]
    w_buf, n_kv, hd = cache_k.shape[1], cache_k.shape[2], cache_k.shape[3]
    ch = w_dw.shape[1]
    n_st = state_conv.shape[1]
    n_buckets = rel_bias.shape[0]
    aw, kvw = n_heads * hd, n_kv * hd
    t = ROW_TILE
    m_rows = n_b * seq
    assert hd == HEAD_DIM and kvw == LANES and n_kv == 2 and n_heads == 8 and w_buf == WINDOW
    assert x_sample.shape[1] == 1 and seq % t == 0 and seq % ATTN_TILE == 0
    assert n_meta + db <= t and db % SAMPLE_BLOCK == 0
    assert n_meta <= HALO and n_meta <= WINDOW and n_st == w_dw.shape[0] - 1 and n_st <= HALO
    splits = tuple(int(v) for v in np.cumsum([0, aw, kvw, kvw, ch, ch, d, d]))
    assert splits[-1] == w_in.shape[1]

    row = lambda v: v.reshape(1, -1)
    x_main = x_prompt.reshape(m_rows, d)
    x_small = jnp.concatenate(
        [meta_tokens, x_sample.reshape(db, d), jnp.zeros((t - n_meta - db, d), F32)], axis=0)
    gq2 = row(jnp.concatenate([q_norm, q_norm]))
    gk2 = row(jnp.concatenate([k_norm, k_norm]))

    h_all, q_all, kv_all, glu_all, ga_all, gc_all = _ffn_inproj(
        x_main, x_small, row(ffn1_norm), row(mix_norm), gq2, gk2, ffn1_w1, ffn1_w3, ffn1_w2, w_in, splits)

    sinks2 = row(sinks)
    b_dw2, ln_g2, ln_b2 = row(b_dw), row(conv_ln_g), row(conv_ln_b)

    kv_meta = jnp.concatenate([jnp.zeros((WINDOW - n_meta, 2 * kvw), F32), _rows(kv_all, m_rows, n_meta)], axis=0)
    glu_meta = jnp.concatenate([jnp.zeros((HALO - n_meta, ch), F32), _rows(glu_all, m_rows, n_meta)], axis=0)
    dist = np.arange(WINDOW)[:, None] + WINDOW - np.arange(2 * WINDOW)[None, :]
    bidx = jnp.asarray(_bucket_or_masked(dist, n_buckets))
    attn_main, w_bf16 = _prompt_attn(
        rel_bias, sinks2, q_all, kv_all, kv_meta, bidx,
        (w_attn_out, w_conv_out, w_out, ffn2_w1, ffn2_w3, ffn2_w2), m_rows, seq, n_meta)

    s0 = m_rows + n_meta
    q_s = _rows(q_all, s0, db).reshape(db, n_heads, hd)
    lane_half = (np.arange(kvw)[None, :] // hd) == (np.arange(n_heads)[:, None] // (n_heads // n_kv))
    qx = jnp.where(jnp.asarray(lane_half)[None], jnp.concatenate([q_s, q_s], axis=-1), 0).astype(BF16)
    kv_new = _rows(kv_all, s0, db).reshape(db, 1, 2 * kvw)
    glu_new = _rows(glu_all, s0, db).reshape(db, 1, ch)
    bidx_s = jnp.asarray(_bucket_or_masked(w_buf - np.arange(w_buf)[None, :], n_buckets))
    o_s, conv_s, new_k_s, new_v_s, new_conv_s = _sample_mix(
        rel_bias, sinks2, qx, kv_new, glu_new,
        jnp.transpose(cache_k, (0, 2, 3, 1)).reshape(db, kvw, w_buf),
        jnp.transpose(cache_v, (0, 2, 3, 1)).reshape(db, kvw, w_buf),
        jnp.transpose(state_conv, (1, 0, 2)), bidx_s, w_dw, b_dw2, ln_g2, ln_b2)
    o_s = jnp.where(jnp.asarray(lane_half)[None], o_s, 0.0).reshape(db, n_heads, n_kv, hd).sum(axis=2)
    attn_small = jnp.concatenate(
        [jnp.zeros((n_meta, aw), BF16), o_s.reshape(db, aw).astype(BF16), jnp.zeros((t - n_meta - db, aw), BF16)],
        axis=0)
    conv_small = jnp.concatenate(
        [jnp.zeros((n_meta, ch), BF16), conv_s.reshape(db, ch).astype(BF16), jnp.zeros((t - n_meta - db, ch), BF16)],
        axis=0)

    y_main, y_small = _post_ffn(h_all, attn_main, attn_small, conv_small, ga_all, gc_all, row(ffn2_norm),
                                glu_all, glu_meta, w_dw, b_dw2, ln_g2, ln_b2, *w_bf16, seq)

    kv_t = jnp.stack([_rows(kv_all, (s + 1) * seq - WINDOW, WINDOW) for s in range(n_b)])
    glu_t = jnp.stack([_rows(glu_all, (s + 1) * seq - n_st, n_st) for s in range(n_b)])
    return (
        y_main.reshape(n_b, seq, d),
        _rows(y_small, n_meta, db).reshape(db, 1, d),
        kv_t[..., :kvw].reshape(n_b, WINDOW, n_kv, hd),
        kv_t[..., kvw:].reshape(n_b, WINDOW, n_kv, hd),
        glu_t,
        new_k_s.reshape(db, w_buf, n_kv, hd),
        new_v_s.reshape(db, w_buf, n_kv, hd),
        new_conv_s,
    )
```

```python
import functools
import math

import jax
import jax.numpy as jnp
import numpy as np
from jax import lax
from jax.experimental import pallas as pl
from jax.experimental.pallas import tpu as pltpu

F32 = jnp.float32
BF16 = jnp.bfloat16

EPS = 1e-6
NEG = -1e30
WINDOW = 128
REL_MAX_DIST = 128
HEAD_DIM = 64
LANES = 128
ROW_TILE = 256
ATTN_TILE = 1024
FF_CHUNK = 256
N_WSTEPS = 8
HALO = 32
SAMPLE_BLOCK = 16
VMEM_LIMIT = 56 * 1024 * 1024


def _dot(a, b):
    return jnp.dot(a, b, preferred_element_type=F32)


def _dot_t(a, b):
    return lax.dot_general(a, b, (((1,), (1,)), ((), ())), preferred_element_type=F32)


def _rms(x, g):
    return x * lax.rsqrt(jnp.mean(x * x, axis=-1, keepdims=True) + EPS) * g


def _pair_rms(x, g2, lo_mask):
    sq = x * x
    lo = jnp.sum(jnp.where(lo_mask, sq, 0.0), axis=-1, keepdims=True) * (1.0 / HEAD_DIM)
    hi = jnp.sum(jnp.where(lo_mask, 0.0, sq), axis=-1, keepdims=True) * (1.0 / HEAD_DIM)
    r = jnp.where(lo_mask, lax.rsqrt(lo + EPS), lax.rsqrt(hi + EPS))
    return x * r * g2


def _dep_zero(row):
    u = pltpu.bitcast(row, jnp.uint32)
    return pltpu.bitcast((u >> 16) >> 16, F32)


def _swiglu(xn, w1b, w3b, w2b, hid_ref, fillers=()):
    n_ff = w1b.shape[1]
    n_chunks = n_ff // FF_CHUNK
    due = {}
    for c in range(n_chunks):
        sl = slice(c * FF_CHUNK, (c + 1) * FF_CHUNK)
        a = _dot(xn, w1b[:, sl])
        b = _dot(xn, w3b[:, sl])
        for z in due.pop(c, ()):
            b = b + jnp.concatenate([z] * (FF_CHUNK // LANES), axis=1)
        for issue, when, thunk in fillers:
            if issue == c:
                due.setdefault(min(when, n_chunks), []).append(_dep_zero(thunk()))
        hid_ref[:, sl] = (a * jax.nn.sigmoid(a) * b).astype(BF16)
    out = _dot(hid_ref[...], w2b[...])
    for z in due.pop(n_chunks, ()):
        out = out + jnp.concatenate([z] * (out.shape[1] // LANES), axis=1)
    assert not due
    return out


def _store_chunk(dst, src, i):
    rows = src.shape[0]
    r = pl.multiple_of(i * rows, 16)
    dst[pl.ds(r, rows), :] = src[...].astype(BF16)


def _bias_table(bidx, rb_ref, head):
    tab = jnp.full(bidx.shape, NEG, F32)
    for b in range(rb_ref.shape[0]):
        tab = jnp.where(bidx == b, rb_ref[b, head], tab)
    return tab


def _ln_swish(y, g, b):
    mu = jnp.mean(y, axis=-1, keepdims=True)
    yc = y - mu
    var = jnp.mean(yc * yc, axis=-1, keepdims=True)
    z = yc * lax.rsqrt(var + EPS) * g + b
    return z * jax.nn.sigmoid(z)


def _ffn_inproj_body(xc_ref, xn_ref, xs_ref, g1_ref, gm_ref, gq_ref, gk_ref, w1c, w3c, w2c, winc,
                     h_ref, q_ref, kv_ref, glu_ref, ga_ref, gc_ref,
                     w1b, w3b, w2b, winb, hid_ref, xn_st, u_st, *, n_main, splits):
    i = pl.program_id(0)

    @pl.when(i < N_WSTEPS)
    def _():
        _store_chunk(w1b, w1c, i)
        _store_chunk(w3b, w3c, i)
        _store_chunk(w2b, w2c, i)
        _store_chunk(winb, winc, i)

    s = i - N_WSTEPS

    def project():
        u = u_st[...]
        o_q, o_k, o_v, o_a, o_b, o_ga, o_gc, o_end = splits
        lo_mask = lax.broadcasted_iota(jnp.int32, (1, LANES), 1) < HEAD_DIM
        gq2 = jnp.concatenate([gq_ref[...], gq_ref[...]], axis=1)
        gk2 = jnp.concatenate([gk_ref[...], gk_ref[...]], axis=1)
        zq = _dot(u, winb[:, o_q:o_k])
        for p in range((o_k - o_q) // LANES):
            sl = slice(p * LANES, (p + 1) * LANES)
            q_ref[:, sl] = (_pair_rms(zq[:, sl], gq2, lo_mask) * (HEAD_DIM ** -0.5)).astype(BF16)
        zkv = _dot(u, winb[:, o_k:o_a])
        kv_ref[:, :LANES] = _pair_rms(zkv[:, :LANES], gk2, lo_mask)
        kv_ref[:, LANES:] = zkv[:, LANES:]
        za = _dot(u, winb[:, o_a:o_b])
        zb = _dot(u, winb[:, o_b:o_ga])
        glu_ref[...] = za * jax.nn.sigmoid(zb)
        ga_ref[...] = jax.nn.sigmoid(_dot(u, winb[:, o_ga:o_gc]))
        gc_ref[...] = jax.nn.sigmoid(_dot(u, winb[:, o_gc:o_end]))

    def normalise_next():
        x_next = jnp.where(s + 1 < n_main, xn_ref[...], xs_ref[...])
        xn_st[...] = _rms(x_next, g1_ref[...]).astype(BF16)

    @pl.when(i == 0)
    def _():
        u_st[...] = jnp.zeros_like(u_st)

    @pl.when(s == -1)
    def _():
        normalise_next()

    @pl.when((s >= 0) & (s <= n_main))
    def _():
        project()
        x = jnp.where(s < n_main, xc_ref[...], xs_ref[...])
        h = x + 0.5 * _swiglu(xn_st[...], w1b, w3b, w2b, hid_ref)
        h_ref[...] = h
        u_st[...] = _rms(h, gm_ref[...]).astype(BF16)
        normalise_next()

    @pl.when(s == n_main + 1)
    def _():
        project()


def _ffn_inproj(x_main, x_small, g1, gm, gq, gk, w1, w3, w2, w_in, splits):
    m_rows, d = x_main.shape
    n_ff = w1.shape[1]
    n_main = m_rows // ROW_TILE
    r_rows = m_rows + ROW_TILE
    t = ROW_TILE
    o_q, o_k, o_v, o_a, o_b, o_ga, o_gc, o_end = splits
    assert o_v - o_k == LANES and o_a - o_v == LANES and (o_k - o_q) % LANES == 0

    def cur_idx(i):
        return (jnp.clip(i - N_WSTEPS, 0, n_main - 1), 0)

    def next_idx(i):
        return (jnp.clip(i - N_WSTEPS + 1, 0, n_main - 1), 0)

    def h_idx(i):
        return (jnp.clip(i - N_WSTEPS, 0, n_main), 0)

    def proj_idx(i):
        return (jnp.clip(i - N_WSTEPS - 1, 0, n_main), 0)

    def w_idx(i):
        return (jnp.minimum(i, N_WSTEPS - 1), 0)

    const = lambda i: (0, 0)
    outs = [
        (jax.ShapeDtypeStruct((r_rows, d), F32), h_idx),
        (jax.ShapeDtypeStruct((r_rows, o_k - o_q), BF16), proj_idx),
        (jax.ShapeDtypeStruct((r_rows, o_a - o_k), F32), proj_idx),
        (jax.ShapeDtypeStruct((r_rows, o_b - o_a), F32), proj_idx),
        (jax.ShapeDtypeStruct((r_rows, o_gc - o_ga), F32), proj_idx),
        (jax.ShapeDtypeStruct((r_rows, o_end - o_gc), F32), proj_idx),
    ]
    return pl.pallas_call(
        functools.partial(_ffn_inproj_body, n_main=n_main, splits=splits),
        grid=(N_WSTEPS + n_main + 2,),
        in_specs=[
            pl.BlockSpec((t, d), cur_idx),
            pl.BlockSpec((t, d), next_idx),
            pl.BlockSpec((t, d), const),
            pl.BlockSpec((1, d), const),
            pl.BlockSpec((1, d), const),
            pl.BlockSpec((1, HEAD_DIM), const),
            pl.BlockSpec((1, HEAD_DIM), const),
            pl.BlockSpec((d // N_WSTEPS, n_ff), w_idx),
            pl.BlockSpec((d // N_WSTEPS, n_ff), w_idx),
            pl.BlockSpec((n_ff // N_WSTEPS, d), w_idx),
            pl.BlockSpec((d // N_WSTEPS, o_end), w_idx),
        ],
        out_specs=[pl.BlockSpec((t, s.shape[1]), idx) for s, idx in outs],
        out_shape=[s for s, _ in outs],
        scratch_shapes=[
            pltpu.VMEM((d, n_ff), BF16),
            pltpu.VMEM((d, n_ff), BF16),
            pltpu.VMEM((n_ff, d), BF16),
            pltpu.VMEM((d, o_end), BF16),
            pltpu.VMEM((t, n_ff), BF16),
            pltpu.VMEM((t, d), BF16),
            pltpu.VMEM((t, d), BF16),
        ],
        compiler_params=pltpu.CompilerParams(
            dimension_semantics=("arbitrary",), vmem_limit_bytes=VMEM_LIMIT),
        name="ffn1_inproj",
    )(x_main, x_main, x_small, g1, gm, gq, gk, w1, w3, w2, w_in)


def _prompt_attn_body(rb_ref, sink_ref, q_ref, kvc_ref, kvp_ref, kvs_ref, bidx_ref, *rest,
                      tiles_per_seq, n_meta, n_heads, n_weights):
    w_refs, attn_ref, wb_refs = rest[:n_weights], rest[n_weights], rest[n_weights + 1:2 * n_weights + 1]
    bias_sc, kvm_ref = rest[2 * n_weights + 1:]
    for w_ref, wb_ref in zip(w_refs, wb_refs, strict=True):
        wb_ref[...] = w_ref[...].astype(BF16)

    i = pl.program_id(0)
    t = q_ref.shape[0]
    n_kv = 2
    grp = n_heads // n_kv

    @pl.when(i == 0)
    def _():
        kvm_ref[...] = jnp.zeros_like(kvm_ref)
        kvm_ref[WINDOW - n_meta:WINDOW, :] = kvs_ref[0:n_meta, :]
        col = lax.broadcasted_iota(jnp.int32, (WINDOW, 2 * WINDOW), 1)
        for hd in range(n_heads):
            tab = _bias_table(bidx_ref[...], rb_ref, hd)
            bias_sc[0, hd] = tab
            bias_sc[1, hd] = jnp.where(col < WINDOW - n_meta, NEG, tab)

    first = (i % tiles_per_seq) == 0
    lead = jnp.where(first, 1, 0)
    lo_mask = lax.broadcasted_iota(jnp.int32, (1, LANES), 1) < HEAD_DIM
    kv_prev = jnp.where(first, kvm_ref[...], kvp_ref[...])
    kv = jnp.concatenate([kv_prev, kvc_ref[...]], axis=0)
    k, v = kv[:, :LANES], kv[:, LANES:]
    kb, vb = k.astype(BF16), v.astype(BF16)
    kr = pltpu.roll(k, HEAD_DIM, axis=1).astype(BF16)
    vr = pltpu.roll(v, HEAD_DIM, axis=1).astype(BF16)
    zero = jnp.zeros((), BF16)
    k_lo = [jnp.where(lo_mask, kb, zero), jnp.where(lo_mask, kr, zero)]
    k_hi = [jnp.where(lo_mask, zero, kr), jnp.where(lo_mask, zero, kb)]
    v_lo = [jnp.where(lo_mask, vb, zero), jnp.where(lo_mask, vr, zero)]
    v_hi = [jnp.where(lo_mask, zero, vr), jnp.where(lo_mask, zero, vb)]

    for qb in range(t // WINDOW):
        rows = slice(qb * WINDOW, (qb + 1) * WINDOW)
        keys = slice(qb * WINDOW, qb * WINDOW + 2 * WINDOW)
        tab = lead if qb == 0 else 0
        for h in range(n_kv):
            k_st = jnp.concatenate([k_lo[h][keys], k_hi[h][keys]], axis=0)
            v_st = jnp.concatenate([v_lo[h][keys], v_hi[h][keys]], axis=0)
            c0 = h * grp * HEAD_DIM
            qq = jnp.concatenate([q_ref[rows, c0:c0 + LANES], q_ref[rows, c0 + LANES:c0 + 2 * LANES]], axis=0)
            s = _dot_t(qq, k_st)
            p_parts, inv_parts = [], []
            for g2 in range(2):
                p_row, inv_row = [], []
                for par in range(2):
                    hd = h * grp + 2 * g2 + par
                    sq = s[g2 * WINDOW:(g2 + 1) * WINDOW, par * 2 * WINDOW:(par + 1) * 2 * WINDOW] + bias_sc[tab, hd]
                    sink = sink_ref[0, hd]
                    m = jnp.maximum(jnp.max(sq, axis=-1, keepdims=True), sink)
                    p = jnp.exp(sq - m)
                    den = jnp.sum(p, axis=-1, keepdims=True) + jnp.exp(sink - m)
                    p_row.append(p.astype(BF16))
                    inv_row.append(1.0 / den)
                p_parts.append(jnp.concatenate(p_row, axis=1))
                inv_parts.append(jnp.where(lo_mask, inv_row[0], inv_row[1]))
            pm = jnp.concatenate(p_parts, axis=0)
            o = _dot(pm, v_st)
            for g2 in range(2):
                c = c0 + g2 * LANES
                attn_ref[rows, c:c + LANES] = (o[g2 * WINDOW:(g2 + 1) * WINDOW] * inv_parts[g2]).astype(attn_ref.dtype)


def _prompt_attn(rel_bias, sinks2, q_all, kv_all, bidx, weights, m_rows, seq, n_meta):
    t = ATTN_TILE
    n_steps = m_rows // t
    assert m_rows % WINDOW == 0 and n_meta <= WINDOW
    n_heads = sinks2.shape[1]
    qw, kvw = q_all.shape[1], kv_all.shape[1]
    const = lambda i: (0, 0)
    row = lambda i: (i, 0)
    smem = pl.BlockSpec(memory_space=pltpu.SMEM)

    def w_spec(w):
        share = 1 if (w.shape[0] // n_steps) % 16 == 0 else 2
        assert w.shape[0] % (n_steps // share) == 0 and (w.shape[0] * share // n_steps) % 16 == 0
        return pl.BlockSpec((w.shape[0] * share // n_steps, w.shape[1]), lambda i: (i // share, 0))

    w_specs = [w_spec(w) for w in weights]
    outs = pl.pallas_call(
        functools.partial(_prompt_attn_body, tiles_per_seq=seq // t, n_meta=n_meta, n_heads=n_heads,
                          n_weights=len(weights)),
        grid=(n_steps,),
        in_specs=[
            smem, smem,
            pl.BlockSpec((t, qw), row),
            pl.BlockSpec((t, kvw), row),
            pl.BlockSpec((WINDOW, kvw), lambda i: (jnp.maximum(i * (t // WINDOW) - 1, 0), 0)),
            pl.BlockSpec((WINDOW, kvw), lambda i: (m_rows // WINDOW, 0)),
            pl.BlockSpec((WINDOW, 2 * WINDOW), const),
        ] + w_specs,
        out_specs=[pl.BlockSpec((t, qw), row)] + w_specs,
        out_shape=[jax.ShapeDtypeStruct((m_rows, qw), BF16)] + [jax.ShapeDtypeStruct(w.shape, BF16) for w in weights],
        scratch_shapes=[
            pltpu.VMEM((2, n_heads, WINDOW, 2 * WINDOW), F32),
            pltpu.VMEM((WINDOW, kvw), F32),
        ],
        compiler_params=pltpu.CompilerParams(dimension_semantics=("arbitrary",)),
        name="prompt_attn",
    )(rel_bias, sinks2, q_all, kv_all, kv_all, kv_all, bidx, *weights)
    return outs[0], outs[1:]


def _sample_mix_body(rb_ref, sink_ref, q_ref, kvn_ref, glun_ref, ck_ref, cv_ref, st_ref, bidx_ref,
                     wdw_ref, bdw_ref, lng_ref, lnb_ref,
                     as_ref, cs_ref, nk_ref, nv_ref, nst_ref, bias_sc, *, n_heads, n_meta):
    i = pl.program_id(0)
    sb, w_buf = ck_ref.shape[0], ck_ref.shape[2]
    n_st = st_ref.shape[0]
    n_rows = sb * n_heads
    per_kv = n_heads // (LANES // HEAD_DIM)

    @pl.when(i == 0)
    def _():
        as_ref[...] = jnp.zeros_like(as_ref)
        cs_ref[...] = jnp.zeros_like(cs_ref)
        tiled = jnp.concatenate(
            [jnp.broadcast_to(jnp.concatenate([_bias_table(bidx_ref[...], rb_ref, hd)] * sb, axis=1),
                              (sb, sb * w_buf)) for hd in range(n_heads)], axis=0)
        row_seq = lax.rem(lax.broadcasted_iota(jnp.int32, tiled.shape, 0), sb)
        col_seq = lax.broadcasted_iota(jnp.int32, tiled.shape, 1) // w_buf
        bias_sc[...] = jnp.where(row_seq == col_seq, tiled, NEG)

    hrow = lax.broadcasted_iota(jnp.int32, (n_rows, 1), 0) // sb
    sink = jnp.zeros((n_rows, 1), F32)
    bias_new = jnp.zeros((n_rows, 1), F32)
    for hd in range(n_heads):
        sink = jnp.where(hrow == hd, sink_ref[0, hd], sink)
        bias_new = jnp.where(hrow == hd, rb_ref[0, hd], bias_new)

    lo_mask = lax.broadcasted_iota(jnp.int32, (1, LANES), 1) < HEAD_DIM

    def half(x, src_hi, dst_hi):
        if src_hi != dst_hi:
            x = pltpu.roll(x, HEAD_DIM, 1)
        return jnp.where(lo_mask, 0.0, x) if dst_hi else jnp.where(lo_mask, x, 0.0)

    q = jnp.concatenate(
        [half(q_ref[:, (hd // 2) * LANES:(hd // 2 + 1) * LANES].astype(F32), hd % 2 == 1, hd // per_kv == 1)
         for hd in range(n_heads)], axis=0).astype(BF16)
    kn_rows = jnp.concatenate([kvn_ref[:, :LANES]] * n_heads, axis=0).astype(BF16)
    vn_rows = jnp.concatenate([kvn_ref[:, LANES:]] * n_heads, axis=0).astype(BF16)
    ck_all = jnp.concatenate([ck_ref[bb] for bb in range(sb)], axis=1).astype(BF16)
    cv_all = jnp.concatenate([cv_ref[bb] for bb in range(sb)], axis=1).astype(BF16)
    s_c = _dot(q, ck_all) + bias_sc[...]
    s_n = jnp.sum(q.astype(F32) * kn_rows.astype(F32), axis=-1, keepdims=True) + bias_new
    m = jnp.maximum(jnp.maximum(jnp.max(s_c, axis=-1, keepdims=True), s_n), sink)
    p_c = jnp.exp(s_c - m)
    p_n = jnp.exp(s_n - m)
    den = jnp.sum(p_c, axis=-1, keepdims=True) + p_n + jnp.exp(sink - m)
    o = _dot_t(p_c.astype(BF16), cv_all) + p_n.astype(BF16).astype(F32) * vn_rows.astype(F32)
    o = o / den
    pairs = []
    for p in range(n_heads // 2):
        lo_head = o[(2 * p) * sb:(2 * p + 1) * sb]
        hi_head = o[(2 * p + 1) * sb:(2 * p + 2) * sb]
        pairs.append(half(lo_head, (2 * p) // per_kv == 1, False) + half(hi_head, (2 * p + 1) // per_kv == 1, True))
    rows = pl.ds(pl.multiple_of(n_meta + i * sb, sb), sb)
    as_ref[rows, :] = jnp.concatenate(pairs, axis=1).astype(as_ref.dtype)

    g_new = glun_ref[...]
    y = g_new * wdw_ref[n_st:n_st + 1, :] + bdw_ref[...]
    for k in range(n_st):
        y = y + st_ref[k] * wdw_ref[k:k + 1, :]
    cs_ref[rows, :] = _ln_swish(y, lng_ref[...], lnb_ref[...]).astype(cs_ref.dtype)

    for bb in range(sb):
        nk_ref[bb, 0:w_buf - 1, :] = ck_ref[bb].T[1:w_buf, :]
        nk_ref[bb, w_buf - 1:w_buf, :] = kvn_ref[bb:bb + 1, :LANES]
        nv_ref[bb, 0:w_buf - 1, :] = cv_ref[bb].T[1:w_buf, :]
        nv_ref[bb, w_buf - 1:w_buf, :] = kvn_ref[bb:bb + 1, LANES:]
        nst_ref[bb, 0:n_st - 1, :] = st_ref[1:n_st, bb, :]
        nst_ref[bb, n_st - 1:n_st, :] = glun_ref[bb:bb + 1, :]


def _sample_mix(rel_bias, sinks2, q_all, kv_all, glu_all, cache_kt, cache_vt, state_t, bidx_s, w_dw, b_dw, ln_g, ln_b,
                s0, n_meta, t_rows):
    db, kw, w_buf = cache_kt.shape
    n_st, ch = state_t.shape[0], state_t.shape[2]
    n_heads = sinks2.shape[1]
    aw = q_all.shape[1]
    sb = SAMPLE_BLOCK
    assert s0 % sb == 0 and n_meta % sb == 0 and n_meta + db <= t_rows
    blk = lambda i: (i, 0, 0)
    const = lambda i: (0, 0)
    tok = lambda i: (s0 // sb + i, 0)
    smem = pl.BlockSpec(memory_space=pltpu.SMEM)
    return pl.pallas_call(
        functools.partial(_sample_mix_body, n_heads=n_heads, n_meta=n_meta),
        grid=(db // sb,),
        in_specs=[
            smem, smem,
            pl.BlockSpec((sb, aw), tok),
            pl.BlockSpec((sb, 2 * kw), tok),
            pl.BlockSpec((sb, ch), tok),
            pl.BlockSpec((sb, kw, w_buf), blk),
            pl.BlockSpec((sb, kw, w_buf), blk),
            pl.BlockSpec((n_st, sb, ch), lambda i: (0, i, 0)),
            pl.BlockSpec((1, w_buf), const),
            pl.BlockSpec(w_dw.shape, const),
            pl.BlockSpec((1, ch), const),
            pl.BlockSpec((1, ch), const),
            pl.BlockSpec((1, ch), const),
        ],
        out_specs=[
            pl.BlockSpec((t_rows, aw), const),
            pl.BlockSpec((t_rows, ch), const),
            pl.BlockSpec((sb, w_buf, kw), blk),
            pl.BlockSpec((sb, w_buf, kw), blk),
            pl.BlockSpec((sb, n_st, ch), blk),
        ],
        out_shape=[
            jax.ShapeDtypeStruct((t_rows, aw), BF16),
            jax.ShapeDtypeStruct((t_rows, ch), BF16),
            jax.ShapeDtypeStruct((db, w_buf, kw), F32),
            jax.ShapeDtypeStruct((db, w_buf, kw), F32),
            jax.ShapeDtypeStruct((db, n_st, ch), F32),
        ],
        scratch_shapes=[pltpu.VMEM((sb * n_heads, sb * w_buf), F32)],
        compiler_params=pltpu.CompilerParams(dimension_semantics=("arbitrary",)),
        name="sample_mix",
    )(rel_bias, sinks2, q_all, kv_all, glu_all, cache_kt, cache_vt, state_t, bidx_s, w_dw, b_dw, ln_g, ln_b)


def _conv_runs(cb_st, glu_ref, glum_ref, y_sc, wdw_ref, bdw_ref, first, n_runs):
    t = y_sc.shape[0]
    conv_w = wdw_ref.shape[0]
    n_ch = y_sc.shape[1]
    off = HALO - (conv_w - 1)
    n_a = (off + conv_w - 1) // 8 + 1
    units = [(lc, g) for lc in range(n_ch // LANES) for g in range(t // 8)]
    sizes = [len(units) // n_runs + (1 if r < len(units) % n_runs else 0) for r in range(n_runs)]
    staged = []

    def run(mine):
        if not staged:
            cb_st[0:HALO, :] = jnp.where(first, glum_ref[...], cb_st[0:HALO, :])
            cb_st[HALO:HALO + t, :] = glu_ref[...]
            staged.append(True)
        zs = {}

        def z(lc, s, g):
            if (lc, s, g) not in zs:
                ls = slice(lc * LANES, (lc + 1) * LANES)
                acc = None
                for a in range(n_a):
                    w = 8 * a + s - off
                    if 0 <= w < conv_w:
                        term = cb_st[8 * (g + a):8 * (g + a) + 8, ls] * wdw_ref[w:w + 1, ls]
                        acc = term if acc is None else acc + term
                zs[(lc, s, g)] = acc
            return zs[(lc, s, g)]

        dep = None
        for lc, g in mine:
            ls = slice(lc * LANES, (lc + 1) * LANES)
            acc = jnp.broadcast_to(bdw_ref[:, ls], (8, LANES)) + z(lc, 0, g)
            if dep is not None:
                acc = acc + dep
            for s in range(1, 8):
                acc = acc + jnp.concatenate([z(lc, s, g), z(lc, s, g + 1)], axis=0)[s:s + 8, :]
            y_sc[8 * g:8 * g + 8, ls] = acc
            dep = _dep_zero(acc)
        return dep[0:1, :]

    out, k = [], 0
    for n in sizes:
        out.append(functools.partial(run, units[k:k + n]))
        k += n
    return out


def _post_ffn_body(h_ref, am_ref, as_ref, cs_ref, ga_ref, gc_ref, g2_ref, glu_ref, glus_ref,
                   wdw_ref, bdw_ref, lng_ref, lnb_ref,
                   waob, wcob, woutb, w1b, w3b, w2b,
                   ym_ref, ys_ref,
                   hid_ref, cb_st, y_sc, conv_st, glum_ref, *, n_main, tiles_per_seq, n_meta):
    i = pl.program_id(0)
    t_rows = y_sc.shape[0]

    @pl.when(i == 0)
    def _():
        cb_st[...] = jnp.zeros_like(cb_st)
        glum_ref[...] = jnp.zeros_like(glum_ref)
        glum_ref[HALO - n_meta:HALO, :] = glus_ref[0:n_meta, :]

    t = i - 1
    tn = t + 1
    first = (tn < n_main) & (lax.rem(tn, tiles_per_seq) == 0)
    n_chunks = w1b.shape[1] // FF_CHUNK

    def finish_conv():
        conv_st[...] = _ln_swish(y_sc[...], lng_ref[...], lnb_ref[...]).astype(conv_st.dtype)
        cb_st[0:HALO, :] = cb_st[t_rows:t_rows + HALO, :]

    @pl.when(t == -1)
    def _():
        for f in _conv_runs(cb_st, glu_ref, glum_ref, y_sc, wdw_ref, bdw_ref, first, n_chunks):
            f()
        finish_conv()

    @pl.when(t >= 0)
    def _():
        on_main = t < n_main
        conv_f = _conv_runs(cb_st, glu_ref, glum_ref, y_sc, wdw_ref, bdw_ref, first, n_chunks)
        fillers = [(c, c + 1, f) for c, f in enumerate(conv_f)]

        at = jnp.where(on_main, am_ref[...], as_ref[...])
        cv = jnp.where(on_main, conv_st[...], cs_ref[...])
        a = _dot(at, waob[...])
        c = _dot(cv, wcob[...])
        mix = (ga_ref[...] * a + gc_ref[...] * c).astype(BF16)
        h2 = h_ref[...] + _dot(mix, woutb[...])
        xn = _rms(h2, g2_ref[...]).astype(BF16)
        y = h2 + 0.5 * _swiglu(xn, w1b, w3b, w2b, hid_ref, fillers)
        finish_conv()

        @pl.when(on_main)
        def _():
            ym_ref[...] = y

        @pl.when(t == n_main)
        def _():
            ys_ref[...] = y


def _post_ffn(h_all, attn_main, attn_small, conv_small, ga_all, gc_all, g2, glu_all,
              w_dw, b_dw, ln_g, ln_b, w_ao, w_co, w_out, w1, w3, w2, seq, n_meta):
    m_rows, aw = attn_main.shape
    assert m_rows % HALO == 0 and n_meta <= HALO
    ch = glu_all.shape[1]
    d = h_all.shape[1]
    n_ff = w1.shape[1]
    t = ROW_TILE
    n_main = m_rows // t

    def main_idx(i):
        return (jnp.clip(i - 1, 0, n_main - 1), 0)

    def next_idx(i):
        return (jnp.minimum(i, n_main - 1), 0)

    def row_idx(i):
        return (jnp.maximum(i - 1, 0), 0)

    const = lambda i: (0, 0)
    resident = lambda w: pl.BlockSpec(w.shape, const, pipeline_mode=pl.Buffered(1))
    return pl.pallas_call(
        functools.partial(_post_ffn_body, n_main=n_main, tiles_per_seq=seq // t, n_meta=n_meta),
        grid=(n_main + 2,),
        in_specs=[
            pl.BlockSpec((t, d), row_idx),
            pl.BlockSpec((t, aw), main_idx),
            pl.BlockSpec((t, aw), const),
            pl.BlockSpec((t, ch), const),
            pl.BlockSpec((t, d), row_idx),
            pl.BlockSpec((t, d), row_idx),
            pl.BlockSpec((1, d), const),
            pl.BlockSpec((t, ch), next_idx),
            pl.BlockSpec((HALO, ch), lambda i: (m_rows // HALO, 0)),
            pl.BlockSpec(w_dw.shape, const),
            pl.BlockSpec((1, ch), const),
            pl.BlockSpec((1, ch), const),
            pl.BlockSpec((1, ch), const),
            resident(w_ao), resident(w_co), resident(w_out), resident(w1), resident(w3), resident(w2),
        ],
        out_specs=[pl.BlockSpec((t, d), main_idx), pl.BlockSpec((t, d), const)],
        out_shape=[jax.ShapeDtypeStruct((m_rows, d), F32), jax.ShapeDtypeStruct((t, d), F32)],
        scratch_shapes=[
            pltpu.VMEM((t, n_ff), BF16),
            pltpu.VMEM((HALO + t, ch), F32),
            pltpu.VMEM((t, ch), F32),
            pltpu.VMEM((t, ch), BF16),
            pltpu.VMEM((HALO, ch), F32),
        ],
        compiler_params=pltpu.CompilerParams(
            dimension_semantics=("arbitrary",), vmem_limit_bytes=VMEM_LIMIT),
        name="post_ffn2",
    )(h_all, attn_main, attn_small, conv_small, ga_all, gc_all, g2, glu_all, glu_all,
      w_dw, b_dw, ln_g, ln_b, w_ao, w_co, w_out, w1, w3, w2)


def _t5_bucket(dist, n_buckets):
    max_exact = n_buckets // 2
    d = np.maximum(dist, 0)
    ratio = (np.log(np.maximum(d, 1).astype(np.float32) / np.float32(max_exact))
             / np.float32(math.log(REL_MAX_DIST / max_exact)))
    large = np.minimum(max_exact + (ratio * np.float32(n_buckets - max_exact)).astype(np.int32), n_buckets - 1)
    return np.where(d < max_exact, d, large).astype(np.int32)


def _bucket_or_masked(dist, n_buckets):
    ok = (dist >= 0) & (dist < WINDOW)
    return np.where(ok, _t5_bucket(dist, n_buckets), -1).astype(np.int32)


def _rows(x, start, n):
    return lax.slice_in_dim(x, start, start + n, axis=0)


def kernel(x_prompt, x_sample, cache_k, cache_v, state_conv, meta_tokens, ffn1_norm, ffn1_w1, ffn1_w3, ffn1_w2, mix_norm, w_in, q_norm, k_norm, rel_bias, sinks, w_attn_out, w_dw, b_dw, conv_ln_g, conv_ln_b, w_conv_out, w_out, ffn2_norm, ffn2_w1, ffn2_w3, ffn2_w2):
    n_b, seq, d = x_prompt.shape
    db = x_sample.shape[0]
    n_meta = meta_tokens.shape[0]
    n_heads = sinks.shape[0]
    w_buf, n_kv, hd = cache_k.shape[1], cache_k.shape[2], cache_k.shape[3]
    ch = w_dw.shape[1]
    n_st = state_conv.shape[1]
    n_buckets = rel_bias.shape[0]
    aw, kvw = n_heads * hd, n_kv * hd
    t = ROW_TILE
    m_rows = n_b * seq
    assert hd == HEAD_DIM and kvw == LANES and n_kv == 2 and n_heads == 8 and w_buf == WINDOW
    assert x_sample.shape[1] == 1 and seq % t == 0 and seq % ATTN_TILE == 0
    assert n_meta + db <= t and db % SAMPLE_BLOCK == 0
    assert n_meta <= HALO and n_meta <= WINDOW and n_st == w_dw.shape[0] - 1 and n_st <= HALO
    splits = tuple(int(v) for v in np.cumsum([0, aw, kvw, kvw, ch, ch, d, d]))
    assert splits[-1] == w_in.shape[1]

    row = lambda v: v.reshape(1, -1)
    x_main = x_prompt.reshape(m_rows, d)
    x_small = jnp.concatenate(
        [meta_tokens, x_sample.reshape(db, d), jnp.zeros((t - n_meta - db, d), F32)], axis=0)

    h_all, q_all, kv_all, glu_all, ga_all, gc_all = _ffn_inproj(
        x_main, x_small, row(ffn1_norm), row(mix_norm), row(q_norm), row(k_norm),
        ffn1_w1, ffn1_w3, ffn1_w2, w_in, splits)

    sinks2 = row(sinks)
    b_dw2, ln_g2, ln_b2 = row(b_dw), row(conv_ln_g), row(conv_ln_b)

    dist = np.arange(WINDOW)[:, None] + WINDOW - np.arange(2 * WINDOW)[None, :]
    bidx = jnp.asarray(_bucket_or_masked(dist, n_buckets))
    attn_main, w_bf16 = _prompt_attn(
        rel_bias, sinks2, q_all, kv_all, bidx,
        (w_attn_out, w_conv_out, w_out, ffn2_w1, ffn2_w3, ffn2_w2), m_rows, seq, n_meta)

    bidx_s = jnp.asarray(_bucket_or_masked(w_buf - np.arange(w_buf)[None, :], n_buckets))
    attn_small, conv_small, new_k_s, new_v_s, new_conv_s = _sample_mix(
        rel_bias, sinks2, q_all, kv_all, glu_all,
        jnp.transpose(cache_k, (0, 2, 3, 1)).reshape(db, kvw, w_buf),
        jnp.transpose(cache_v, (0, 2, 3, 1)).reshape(db, kvw, w_buf),
        jnp.transpose(state_conv, (1, 0, 2)), bidx_s, w_dw, b_dw2, ln_g2, ln_b2, m_rows + n_meta, n_meta, t)

    y_main, y_small = _post_ffn(h_all, attn_main, attn_small, conv_small, ga_all, gc_all, row(ffn2_norm),
                                glu_all, w_dw, b_dw2, ln_g2, ln_b2, *w_bf16, seq, n_meta)

    kv_t = jnp.stack([_rows(kv_all, (s + 1) * seq - WINDOW, WINDOW) for s in range(n_b)])
    glu_t = jnp.stack([_rows(glu_all, (s + 1) * seq - n_st, n_st) for s in range(n_b)])
    return (
        y_main.reshape(n_b, seq, d),
        _rows(y_small, n_meta, db).reshape(db, 1, d),
        kv_t[..., :kvw].reshape(n_b, WINDOW, n_kv, hd),
        kv_t[..., kvw:].reshape(n_b, WINDOW, n_kv, hd),
        glu_t,
        new_k_s.reshape(db, w_buf, n_kv, hd),
        new_v_s.reshape(db, w_buf, n_kv, hd),
        new_conv_s,
    )
```

```python
import functools
import math

import jax
import jax.numpy as jnp
import numpy as np
from jax import lax
from jax.experimental import pallas as pl
from jax.experimental.pallas import tpu as pltpu

F32 = jnp.float32
BF16 = jnp.bfloat16

EPS = 1e-6
NEG = -1e30
WINDOW = 128
REL_MAX_DIST = 128
HEAD_DIM = 64
LANES = 128
ROW_TILE = 256
ATTN_TILE = 1024
FF_CHUNK = 256
N_WSTEPS = 8
HALO = 32
SAMPLE_BLOCK = 16
VMEM_LIMIT = 56 * 1024 * 1024


def _dot(a, b):
    return jnp.dot(a, b, preferred_element_type=F32)


def _dot_t(a, b):
    return lax.dot_general(a, b, (((1,), (1,)), ((), ())), preferred_element_type=F32)


def _rms(x, g):
    return x * lax.rsqrt(jnp.mean(x * x, axis=-1, keepdims=True) + EPS) * g


def _pair_rms(x, g2, lo_mask):
    sq = x * x
    lo = jnp.sum(jnp.where(lo_mask, sq, 0.0), axis=-1, keepdims=True) * (1.0 / HEAD_DIM)
    hi = jnp.sum(jnp.where(lo_mask, 0.0, sq), axis=-1, keepdims=True) * (1.0 / HEAD_DIM)
    r = jnp.where(lo_mask, lax.rsqrt(lo + EPS), lax.rsqrt(hi + EPS))
    return x * r * g2


def _dep_zero(row):
    u = pltpu.bitcast(row, jnp.uint32)
    return pltpu.bitcast((u >> 16) >> 16, F32)


def _swiglu(xn, w1b, w3b, w2b, hid_ref, fillers=()):
    n_ff = w1b.shape[1]
    n_chunks = n_ff // FF_CHUNK
    due = {}
    for c in range(n_chunks):
        sl = slice(c * FF_CHUNK, (c + 1) * FF_CHUNK)
        a = _dot(xn, w1b[:, sl])
        b = _dot(xn, w3b[:, sl])
        for z in due.pop(c, ()):
            b = b + jnp.concatenate([z] * (FF_CHUNK // LANES), axis=1)
        for issue, when, thunk in fillers:
            if issue == c:
                due.setdefault(min(when, n_chunks), []).append(_dep_zero(thunk()))
        hid_ref[:, sl] = (a * jax.nn.sigmoid(a) * b).astype(BF16)
    out = _dot(hid_ref[...], w2b[...])
    for z in due.pop(n_chunks, ()):
        out = out + jnp.concatenate([z] * (out.shape[1] // LANES), axis=1)
    assert not due
    return out


def _store_chunk(dst, src, i):
    rows = src.shape[0]
    r = pl.multiple_of(i * rows, 16)
    dst[pl.ds(r, rows), :] = src[...].astype(BF16)


def _bias_table(bidx, rb_ref, head):
    tab = jnp.full(bidx.shape, NEG, F32)
    for b in range(rb_ref.shape[0]):
        tab = jnp.where(bidx == b, rb_ref[b, head], tab)
    return tab


def _ln_swish(y, g, b):
    mu = jnp.mean(y, axis=-1, keepdims=True)
    yc = y - mu
    var = jnp.mean(yc * yc, axis=-1, keepdims=True)
    z = yc * lax.rsqrt(var + EPS) * g + b
    return z * jax.nn.sigmoid(z)


def _ffn_inproj_body(xc_ref, xn_ref, meta_ref, xsm_ref, g1_ref, gm_ref, gq_ref, gk_ref, w1c, w3c, w2c, winc,
                     h_ref, q_ref, kv_ref, glu_ref, ga_ref, gc_ref,
                     w1b, w3b, w2b, winb, hid_ref, xn_st, u_st, *, n_main, splits):
    i = pl.program_id(0)

    def small_tile():
        pad = xc_ref.shape[0] - meta_ref.shape[0] - xsm_ref.shape[0]
        return jnp.concatenate([meta_ref[...], xsm_ref[...], jnp.zeros((pad, xc_ref.shape[1]), F32)], axis=0)

    @pl.when(i < N_WSTEPS)
    def _():
        _store_chunk(w1b, w1c, i)
        _store_chunk(w3b, w3c, i)
        _store_chunk(w2b, w2c, i)
        _store_chunk(winb, winc, i)

    s = i - N_WSTEPS

    def project():
        u = u_st[...]
        o_q, o_k, o_v, o_a, o_b, o_ga, o_gc, o_end = splits
        lo_mask = lax.broadcasted_iota(jnp.int32, (1, LANES), 1) < HEAD_DIM
        gq2 = jnp.concatenate([gq_ref[...], gq_ref[...]], axis=1)
        gk2 = jnp.concatenate([gk_ref[...], gk_ref[...]], axis=1)
        zq = _dot(u, winb[:, o_q:o_k])
        for p in range((o_k - o_q) // LANES):
            sl = slice(p * LANES, (p + 1) * LANES)
            q_ref[:, sl] = (_pair_rms(zq[:, sl], gq2, lo_mask) * (HEAD_DIM ** -0.5)).astype(BF16)
        zkv = _dot(u, winb[:, o_k:o_a])
        kv_ref[:, :LANES] = _pair_rms(zkv[:, :LANES], gk2, lo_mask)
        kv_ref[:, LANES:] = zkv[:, LANES:]
        za = _dot(u, winb[:, o_a:o_b])
        zb = _dot(u, winb[:, o_b:o_ga])
        glu_ref[...] = za * jax.nn.sigmoid(zb)
        ga_ref[...] = jax.nn.sigmoid(_dot(u, winb[:, o_ga:o_gc]))
        gc_ref[...] = jax.nn.sigmoid(_dot(u, winb[:, o_gc:o_end]))

    def normalise_next():
        x_next = jnp.where(s + 1 < n_main, xn_ref[...], small_tile())
        xn_st[...] = _rms(x_next, g1_ref[...]).astype(BF16)

    @pl.when(i == 0)
    def _():
        u_st[...] = jnp.zeros_like(u_st)

    @pl.when(s == -1)
    def _():
        normalise_next()

    @pl.when((s >= 0) & (s <= n_main))
    def _():
        project()
        x = jnp.where(s < n_main, xc_ref[...], small_tile())
        h = x + 0.5 * _swiglu(xn_st[...], w1b, w3b, w2b, hid_ref)
        h_ref[...] = h
        u_st[...] = _rms(h, gm_ref[...]).astype(BF16)
        normalise_next()

    @pl.when(s == n_main + 1)
    def _():
        project()


def _ffn_inproj(x_main, x_meta, x_sample, g1, gm, gq, gk, w1, w3, w2, w_in, splits):
    m_rows, d = x_main.shape
    assert x_meta.shape[0] % 8 == 0 and x_sample.shape[0] % 8 == 0
    assert x_meta.shape[0] + x_sample.shape[0] <= ROW_TILE
    n_ff = w1.shape[1]
    n_main = m_rows // ROW_TILE
    r_rows = m_rows + ROW_TILE
    t = ROW_TILE
    o_q, o_k, o_v, o_a, o_b, o_ga, o_gc, o_end = splits
    assert o_v - o_k == LANES and o_a - o_v == LANES and (o_k - o_q) % LANES == 0

    def cur_idx(i):
        return (jnp.clip(i - N_WSTEPS, 0, n_main - 1), 0)

    def next_idx(i):
        return (jnp.clip(i - N_WSTEPS + 1, 0, n_main - 1), 0)

    def h_idx(i):
        return (jnp.clip(i - N_WSTEPS, 0, n_main), 0)

    def proj_idx(i):
        return (jnp.clip(i - N_WSTEPS - 1, 0, n_main), 0)

    def w_idx(i):
        return (jnp.minimum(i, N_WSTEPS - 1), 0)

    const = lambda i: (0, 0)
    outs = [
        (jax.ShapeDtypeStruct((r_rows, d), F32), h_idx),
        (jax.ShapeDtypeStruct((r_rows, o_k - o_q), BF16), proj_idx),
        (jax.ShapeDtypeStruct((r_rows, o_a - o_k), F32), proj_idx),
        (jax.ShapeDtypeStruct((r_rows, o_b - o_a), F32), proj_idx),
        (jax.ShapeDtypeStruct((r_rows, o_gc - o_ga), F32), proj_idx),
        (jax.ShapeDtypeStruct((r_rows, o_end - o_gc), F32), proj_idx),
    ]
    return pl.pallas_call(
        functools.partial(_ffn_inproj_body, n_main=n_main, splits=splits),
        grid=(N_WSTEPS + n_main + 2,),
        in_specs=[
            pl.BlockSpec((t, d), cur_idx),
            pl.BlockSpec((t, d), next_idx),
            pl.BlockSpec(x_meta.shape, const),
            pl.BlockSpec(x_sample.shape, const),
            pl.BlockSpec((1, d), const),
            pl.BlockSpec((1, d), const),
            pl.BlockSpec((1, HEAD_DIM), const),
            pl.BlockSpec((1, HEAD_DIM), const),
            pl.BlockSpec((d // N_WSTEPS, n_ff), w_idx),
            pl.BlockSpec((d // N_WSTEPS, n_ff), w_idx),
            pl.BlockSpec((n_ff // N_WSTEPS, d), w_idx),
            pl.BlockSpec((d // N_WSTEPS, o_end), w_idx),
        ],
        out_specs=[pl.BlockSpec((t, s.shape[1]), idx) for s, idx in outs],
        out_shape=[s for s, _ in outs],
        scratch_shapes=[
            pltpu.VMEM((d, n_ff), BF16),
            pltpu.VMEM((d, n_ff), BF16),
            pltpu.VMEM((n_ff, d), BF16),
            pltpu.VMEM((d, o_end), BF16),
            pltpu.VMEM((t, n_ff), BF16),
            pltpu.VMEM((t, d), BF16),
            pltpu.VMEM((t, d), BF16),
        ],
        compiler_params=pltpu.CompilerParams(
            dimension_semantics=("arbitrary",), vmem_limit_bytes=VMEM_LIMIT),
        name="ffn1_inproj",
    )(x_main, x_main, x_meta, x_sample, g1, gm, gq, gk, w1, w3, w2, w_in)


def _prompt_attn_body(rb_ref, sink_ref, q_ref, kvc_ref, kvp_ref, kvs_ref, bidx_ref, *rest,
                      tiles_per_seq, n_meta, n_heads, n_weights):
    w_refs, attn_ref, wb_refs = rest[:n_weights], rest[n_weights], rest[n_weights + 1:2 * n_weights + 1]
    kt_ref, vt_ref, bias_sc, kvm_ref = rest[2 * n_weights + 1:]
    for w_ref, wb_ref in zip(w_refs, wb_refs, strict=True):
        wb_ref[...] = w_ref[...].astype(BF16)

    i = pl.program_id(0)
    t = q_ref.shape[0]
    n_kv = 2
    grp = n_heads // n_kv

    @pl.when(i == 0)
    def _():
        kvm_ref[...] = jnp.zeros_like(kvm_ref)
        kvm_ref[WINDOW - n_meta:WINDOW, :] = kvs_ref[0:n_meta, :]
        col = lax.broadcasted_iota(jnp.int32, (WINDOW, 2 * WINDOW), 1)
        for hd in range(n_heads):
            tab = _bias_table(bidx_ref[...], rb_ref, hd)
            bias_sc[0, hd] = tab
            bias_sc[1, hd] = jnp.where(col < WINDOW - n_meta, NEG, tab)

    first = (i % tiles_per_seq) == 0
    lead = jnp.where(first, 1, 0)
    lo_mask = lax.broadcasted_iota(jnp.int32, (1, LANES), 1) < HEAD_DIM
    kv_prev = jnp.where(first, kvm_ref[...], kvp_ref[...])
    kv = jnp.concatenate([kv_prev, kvc_ref[...]], axis=0)
    k, v = kv[:, :LANES], kv[:, LANES:]
    kb, vb = k.astype(BF16), v.astype(BF16)
    kr = pltpu.roll(k, HEAD_DIM, axis=1).astype(BF16)
    vr = pltpu.roll(v, HEAD_DIM, axis=1).astype(BF16)
    zero = jnp.zeros((), BF16)
    k_lo = [jnp.where(lo_mask, kb, zero), jnp.where(lo_mask, kr, zero)]
    k_hi = [jnp.where(lo_mask, zero, kr), jnp.where(lo_mask, zero, kb)]
    v_lo = [jnp.where(lo_mask, vb, zero), jnp.where(lo_mask, vr, zero)]
    v_hi = [jnp.where(lo_mask, zero, vr), jnp.where(lo_mask, zero, vb)]

    for qb in range(t // WINDOW):
        rows = slice(qb * WINDOW, (qb + 1) * WINDOW)
        keys = slice(qb * WINDOW, qb * WINDOW + 2 * WINDOW)
        tab = lead if qb == 0 else 0
        for h in range(n_kv):
            k_st = jnp.concatenate([k_lo[h][keys], k_hi[h][keys]], axis=0)
            v_st = jnp.concatenate([v_lo[h][keys], v_hi[h][keys]], axis=0)
            c0 = h * grp * HEAD_DIM
            qq = jnp.concatenate([q_ref[rows, c0:c0 + LANES], q_ref[rows, c0 + LANES:c0 + 2 * LANES]], axis=0)
            s = _dot_t(qq, k_st)
            p_parts, inv_parts = [], []
            for g2 in range(2):
                p_row, inv_row = [], []
                for par in range(2):
                    hd = h * grp + 2 * g2 + par
                    sq = s[g2 * WINDOW:(g2 + 1) * WINDOW, par * 2 * WINDOW:(par + 1) * 2 * WINDOW] + bias_sc[tab, hd]
                    sink = sink_ref[0, hd]
                    m = jnp.maximum(jnp.max(sq, axis=-1, keepdims=True), sink)
                    p = jnp.exp(sq - m)
                    den = jnp.sum(p, axis=-1, keepdims=True) + jnp.exp(sink - m)
                    p_row.append(p.astype(BF16))
                    inv_row.append(1.0 / den)
                p_parts.append(jnp.concatenate(p_row, axis=1))
                inv_parts.append(jnp.where(lo_mask, inv_row[0], inv_row[1]))
            pm = jnp.concatenate(p_parts, axis=0)
            o = _dot(pm, v_st)
            for g2 in range(2):
                c = c0 + g2 * LANES
                attn_ref[rows, c:c + LANES] = (o[g2 * WINDOW:(g2 + 1) * WINDOW] * inv_parts[g2]).astype(attn_ref.dtype)

    @pl.when((i + 1) % tiles_per_seq == 0)
    def _():
        kt_ref[0] = kvc_ref[t - WINDOW:t, :LANES]
        vt_ref[0] = kvc_ref[t - WINDOW:t, LANES:]


def _prompt_attn(rel_bias, sinks2, q_all, kv_all, bidx, weights, m_rows, seq, n_meta):
    t = ATTN_TILE
    n_steps = m_rows // t
    assert m_rows % WINDOW == 0 and n_meta <= WINDOW
    n_heads = sinks2.shape[1]
    qw, kvw = q_all.shape[1], kv_all.shape[1]
    const = lambda i: (0, 0)
    row = lambda i: (i, 0)
    per_seq = lambda i: (i // (seq // t), 0, 0)
    smem = pl.BlockSpec(memory_space=pltpu.SMEM)

    def w_spec(w):
        share = 1 if (w.shape[0] // n_steps) % 16 == 0 else 2
        assert w.shape[0] % (n_steps // share) == 0 and (w.shape[0] * share // n_steps) % 16 == 0
        return pl.BlockSpec((w.shape[0] * share // n_steps, w.shape[1]), lambda i: (i // share, 0))

    w_specs = [w_spec(w) for w in weights]
    outs = pl.pallas_call(
        functools.partial(_prompt_attn_body, tiles_per_seq=seq // t, n_meta=n_meta, n_heads=n_heads,
                          n_weights=len(weights)),
        grid=(n_steps,),
        in_specs=[
            smem, smem,
            pl.BlockSpec((t, qw), row),
            pl.BlockSpec((t, kvw), row),
            pl.BlockSpec((WINDOW, kvw), lambda i: (jnp.maximum(i * (t // WINDOW) - 1, 0), 0)),
            pl.BlockSpec((WINDOW, kvw), lambda i: (m_rows // WINDOW, 0)),
            pl.BlockSpec((WINDOW, 2 * WINDOW), const),
        ] + w_specs,
        out_specs=[pl.BlockSpec((t, qw), row)] + w_specs + [pl.BlockSpec((1, WINDOW, LANES), per_seq)] * 2,
        out_shape=[jax.ShapeDtypeStruct((m_rows, qw), BF16)] + [jax.ShapeDtypeStruct(w.shape, BF16) for w in weights]
        + [jax.ShapeDtypeStruct((m_rows // seq, WINDOW, LANES), F32)] * 2,
        scratch_shapes=[
            pltpu.VMEM((2, n_heads, WINDOW, 2 * WINDOW), F32),
            pltpu.VMEM((WINDOW, kvw), F32),
        ],
        compiler_params=pltpu.CompilerParams(dimension_semantics=("arbitrary",)),
        name="prompt_attn",
    )(rel_bias, sinks2, q_all, kv_all, kv_all, kv_all, bidx, *weights)
    return outs[0], outs[1:1 + len(weights)], outs[1 + len(weights)], outs[2 + len(weights)]


def _sample_mix_body(rb_ref, sink_ref, q_ref, kvn_ref, glun_ref, ck_ref, cv_ref, st_ref, bidx_ref,
                     wdw_ref, bdw_ref, lng_ref, lnb_ref,
                     as_ref, cs_ref, nk_ref, nv_ref, nst_ref, bias_sc, *, n_heads, n_meta):
    i = pl.program_id(0)
    sb, w_buf = ck_ref.shape[0], ck_ref.shape[2]
    n_st = st_ref.shape[0]
    n_rows = sb * n_heads
    per_kv = n_heads // (LANES // HEAD_DIM)

    @pl.when(i == 0)
    def _():
        as_ref[...] = jnp.zeros_like(as_ref)
        cs_ref[...] = jnp.zeros_like(cs_ref)
        tiled = jnp.concatenate(
            [jnp.broadcast_to(jnp.concatenate([_bias_table(bidx_ref[...], rb_ref, hd)] * sb, axis=1),
                              (sb, sb * w_buf)) for hd in range(n_heads)], axis=0)
        row_seq = lax.rem(lax.broadcasted_iota(jnp.int32, tiled.shape, 0), sb)
        col_seq = lax.broadcasted_iota(jnp.int32, tiled.shape, 1) // w_buf
        bias_sc[...] = jnp.where(row_seq == col_seq, tiled, NEG)

    hrow = lax.broadcasted_iota(jnp.int32, (n_rows, 1), 0) // sb
    sink = jnp.zeros((n_rows, 1), F32)
    bias_new = jnp.zeros((n_rows, 1), F32)
    for hd in range(n_heads):
        sink = jnp.where(hrow == hd, sink_ref[0, hd], sink)
        bias_new = jnp.where(hrow == hd, rb_ref[0, hd], bias_new)

    lo_mask = lax.broadcasted_iota(jnp.int32, (1, LANES), 1) < HEAD_DIM

    def half(x, src_hi, dst_hi):
        if src_hi != dst_hi:
            x = pltpu.roll(x, HEAD_DIM, 1)
        return jnp.where(lo_mask, 0.0, x) if dst_hi else jnp.where(lo_mask, x, 0.0)

    q = jnp.concatenate(
        [half(q_ref[:, (hd // 2) * LANES:(hd // 2 + 1) * LANES].astype(F32), hd % 2 == 1, hd // per_kv == 1)
         for hd in range(n_heads)], axis=0).astype(BF16)
    kn_rows = jnp.concatenate([kvn_ref[:, :LANES]] * n_heads, axis=0).astype(BF16)
    vn_rows = jnp.concatenate([kvn_ref[:, LANES:]] * n_heads, axis=0).astype(BF16)
    ck_all = jnp.concatenate([ck_ref[bb] for bb in range(sb)], axis=1).astype(BF16)
    cv_all = jnp.concatenate([cv_ref[bb] for bb in range(sb)], axis=1).astype(BF16)
    s_c = _dot(q, ck_all) + bias_sc[...]
    s_n = jnp.sum(q.astype(F32) * kn_rows.astype(F32), axis=-1, keepdims=True) + bias_new
    m = jnp.maximum(jnp.maximum(jnp.max(s_c, axis=-1, keepdims=True), s_n), sink)
    p_c = jnp.exp(s_c - m)
    p_n = jnp.exp(s_n - m)
    den = jnp.sum(p_c, axis=-1, keepdims=True) + p_n + jnp.exp(sink - m)
    o = _dot_t(p_c.astype(BF16), cv_all) + p_n.astype(BF16).astype(F32) * vn_rows.astype(F32)
    o = o / den
    pairs = []
    for p in range(n_heads // 2):
        lo_head = o[(2 * p) * sb:(2 * p + 1) * sb]
        hi_head = o[(2 * p + 1) * sb:(2 * p + 2) * sb]
        pairs.append(half(lo_head, (2 * p) // per_kv == 1, False) + half(hi_head, (2 * p + 1) // per_kv == 1, True))
    rows = pl.ds(pl.multiple_of(n_meta + i * sb, sb), sb)
    as_ref[rows, :] = jnp.concatenate(pairs, axis=1).astype(as_ref.dtype)

    g_new = glun_ref[...]
    y = g_new * wdw_ref[n_st:n_st + 1, :] + bdw_ref[...]
    for k in range(n_st):
        y = y + st_ref[k] * wdw_ref[k:k + 1, :]
    cs_ref[rows, :] = _ln_swish(y, lng_ref[...], lnb_ref[...]).astype(cs_ref.dtype)

    for bb in range(sb):
        nk_ref[bb, 0:w_buf - 1, :] = ck_ref[bb].T[1:w_buf, :]
        nk_ref[bb, w_buf - 1:w_buf, :] = kvn_ref[bb:bb + 1, :LANES]
        nv_ref[bb, 0:w_buf - 1, :] = cv_ref[bb].T[1:w_buf, :]
        nv_ref[bb, w_buf - 1:w_buf, :] = kvn_ref[bb:bb + 1, LANES:]
        nst_ref[bb, 0:n_st - 1, :] = st_ref[1:n_st, bb, :]
        nst_ref[bb, n_st - 1:n_st, :] = glun_ref[bb:bb + 1, :]


def _sample_mix(rel_bias, sinks2, q_all, kv_all, glu_all, cache_kt, cache_vt, state_t, bidx_s, w_dw, b_dw, ln_g, ln_b,
                s0, n_meta, t_rows):
    db, kw, w_buf = cache_kt.shape
    n_st, ch = state_t.shape[0], state_t.shape[2]
    n_heads = sinks2.shape[1]
    aw = q_all.shape[1]
    sb = SAMPLE_BLOCK
    assert s0 % sb == 0 and n_meta % sb == 0 and n_meta + db <= t_rows
    blk = lambda i: (i, 0, 0)
    const = lambda i: (0, 0)
    tok = lambda i: (s0 // sb + i, 0)
    smem = pl.BlockSpec(memory_space=pltpu.SMEM)
    return pl.pallas_call(
        functools.partial(_sample_mix_body, n_heads=n_heads, n_meta=n_meta),
        grid=(db // sb,),
        in_specs=[
            smem, smem,
            pl.BlockSpec((sb, aw), tok),
            pl.BlockSpec((sb, 2 * kw), tok),
            pl.BlockSpec((sb, ch), tok),
            pl.BlockSpec((sb, kw, w_buf), blk),
            pl.BlockSpec((sb, kw, w_buf), blk),
            pl.BlockSpec((n_st, sb, ch), lambda i: (0, i, 0)),
            pl.BlockSpec((1, w_buf), const),
            pl.BlockSpec(w_dw.shape, const),
            pl.BlockSpec((1, ch), const),
            pl.BlockSpec((1, ch), const),
            pl.BlockSpec((1, ch), const),
        ],
        out_specs=[
            pl.BlockSpec((t_rows, aw), const),
            pl.BlockSpec((t_rows, ch), const),
            pl.BlockSpec((sb, w_buf, kw), blk),
            pl.BlockSpec((sb, w_buf, kw), blk),
            pl.BlockSpec((sb, n_st, ch), blk),
        ],
        out_shape=[
            jax.ShapeDtypeStruct((t_rows, aw), BF16),
            jax.ShapeDtypeStruct((t_rows, ch), BF16),
            jax.ShapeDtypeStruct((db, w_buf, kw), F32),
            jax.ShapeDtypeStruct((db, w_buf, kw), F32),
            jax.ShapeDtypeStruct((db, n_st, ch), F32),
        ],
        scratch_shapes=[pltpu.VMEM((sb * n_heads, sb * w_buf), F32)],
        compiler_params=pltpu.CompilerParams(dimension_semantics=("arbitrary",)),
        name="sample_mix",
    )(rel_bias, sinks2, q_all, kv_all, glu_all, cache_kt, cache_vt, state_t, bidx_s, w_dw, b_dw, ln_g, ln_b)


def _conv_runs(cb_st, glu_ref, glum_ref, y_sc, wdw_ref, bdw_ref, first, n_runs):
    t = y_sc.shape[0]
    conv_w = wdw_ref.shape[0]
    n_ch = y_sc.shape[1]
    off = HALO - (conv_w - 1)
    n_a = (off + conv_w - 1) // 8 + 1
    units = [(lc, g) for lc in range(n_ch // LANES) for g in range(t // 8)]
    sizes = [len(units) // n_runs + (1 if r < len(units) % n_runs else 0) for r in range(n_runs)]
    staged = []

    def run(mine):
        if not staged:
            cb_st[0:HALO, :] = jnp.where(first, glum_ref[...], cb_st[0:HALO, :])
            cb_st[HALO:HALO + t, :] = glu_ref[...]
            staged.append(True)
        zs = {}

        def z(lc, s, g):
            if (lc, s, g) not in zs:
                ls = slice(lc * LANES, (lc + 1) * LANES)
                acc = None
                for a in range(n_a):
                    w = 8 * a + s - off
                    if 0 <= w < conv_w:
                        term = cb_st[8 * (g + a):8 * (g + a) + 8, ls] * wdw_ref[w:w + 1, ls]
                        acc = term if acc is None else acc + term
                zs[(lc, s, g)] = acc
            return zs[(lc, s, g)]

        dep = None
        for lc, g in mine:
            ls = slice(lc * LANES, (lc + 1) * LANES)
            acc = jnp.broadcast_to(bdw_ref[:, ls], (8, LANES)) + z(lc, 0, g)
            if dep is not None:
                acc = acc + dep
            for s in range(1, 8):
                acc = acc + jnp.concatenate([z(lc, s, g), z(lc, s, g + 1)], axis=0)[s:s + 8, :]
            y_sc[8 * g:8 * g + 8, ls] = acc
            dep = _dep_zero(acc)
        return dep[0:1, :]

    out, k = [], 0
    for n in sizes:
        out.append(functools.partial(run, units[k:k + n]))
        k += n
    return out


def _post_ffn_body(h_ref, am_ref, as_ref, cs_ref, ga_ref, gc_ref, g2_ref, glu_ref, glus_ref,
                   wdw_ref, bdw_ref, lng_ref, lnb_ref,
                   waob, wcob, woutb, w1b, w3b, w2b,
                   ym_ref, ys_ref, gt_ref,
                   hid_ref, cb_st, y_sc, conv_st, glum_ref, *, n_main, tiles_per_seq, n_meta):
    i = pl.program_id(0)
    t_rows = y_sc.shape[0]

    @pl.when(i == 0)
    def _():
        cb_st[...] = jnp.zeros_like(cb_st)
        glum_ref[...] = jnp.zeros_like(glum_ref)
        glum_ref[HALO - n_meta:HALO, :] = glus_ref[0:n_meta, :]

    t = i - 1
    tn = t + 1
    first = (tn < n_main) & (lax.rem(tn, tiles_per_seq) == 0)
    n_chunks = w1b.shape[1] // FF_CHUNK

    def finish_conv():
        conv_st[...] = _ln_swish(y_sc[...], lng_ref[...], lnb_ref[...]).astype(conv_st.dtype)
        cb_st[0:HALO, :] = cb_st[t_rows:t_rows + HALO, :]

    @pl.when(t == -1)
    def _():
        for f in _conv_runs(cb_st, glu_ref, glum_ref, y_sc, wdw_ref, bdw_ref, first, n_chunks):
            f()
        finish_conv()

    @pl.when(t >= 0)
    def _():
        on_main = t < n_main
        conv_f = _conv_runs(cb_st, glu_ref, glum_ref, y_sc, wdw_ref, bdw_ref, first, n_chunks)
        fillers = [(c, c + 1, f) for c, f in enumerate(conv_f)]

        at = jnp.where(on_main, am_ref[...], as_ref[...])
        cv = jnp.where(on_main, conv_st[...], cs_ref[...])
        a = _dot(at, waob[...])
        c = _dot(cv, wcob[...])
        mix = (ga_ref[...] * a + gc_ref[...] * c).astype(BF16)
        h2 = h_ref[...] + _dot(mix, woutb[...])
        xn = _rms(h2, g2_ref[...]).astype(BF16)
        y = h2 + 0.5 * _swiglu(xn, w1b, w3b, w2b, hid_ref, fillers)
        finish_conv()

        @pl.when(on_main)
        def _():
            ym_ref[...] = y

        @pl.when(t == n_main)
        def _():
            ys_ref[...] = y

    @pl.when((tn < n_main) & (lax.rem(tn + 1, tiles_per_seq) == 0))
    def _():
        n_st = gt_ref.shape[1]
        gt_ref[0] = glu_ref[t_rows - n_st:t_rows, :]


def _post_ffn(h_all, attn_main, attn_small, conv_small, ga_all, gc_all, g2, glu_all,
              w_dw, b_dw, ln_g, ln_b, w_ao, w_co, w_out, w1, w3, w2, seq, n_meta):
    m_rows, aw = attn_main.shape
    assert m_rows % HALO == 0 and n_meta <= HALO
    ch = glu_all.shape[1]
    d = h_all.shape[1]
    n_ff = w1.shape[1]
    t = ROW_TILE
    n_main = m_rows // t
    n_b = m_rows // seq
    n_st = w_dw.shape[0] - 1

    def main_idx(i):
        return (jnp.clip(i - 1, 0, n_main - 1), 0)

    def next_idx(i):
        return (jnp.minimum(i, n_main - 1), 0)

    def row_idx(i):
        return (jnp.maximum(i - 1, 0), 0)

    const = lambda i: (0, 0)
    resident = lambda w: pl.BlockSpec(w.shape, const, pipeline_mode=pl.Buffered(1))
    return pl.pallas_call(
        functools.partial(_post_ffn_body, n_main=n_main, tiles_per_seq=seq // t, n_meta=n_meta),
        grid=(n_main + 2,),
        in_specs=[
            pl.BlockSpec((t, d), row_idx),
            pl.BlockSpec((t, aw), main_idx),
            pl.BlockSpec((t, aw), const),
            pl.BlockSpec((t, ch), const),
            pl.BlockSpec((t, d), row_idx),
            pl.BlockSpec((t, d), row_idx),
            pl.BlockSpec((1, d), const),
            pl.BlockSpec((t, ch), next_idx),
            pl.BlockSpec((HALO, ch), lambda i: (m_rows // HALO, 0)),
            pl.BlockSpec(w_dw.shape, const),
            pl.BlockSpec((1, ch), const),
            pl.BlockSpec((1, ch), const),
            pl.BlockSpec((1, ch), const),
            resident(w_ao), resident(w_co), resident(w_out), resident(w1), resident(w3), resident(w2),
        ],
        out_specs=[pl.BlockSpec((t, d), main_idx), pl.BlockSpec((t, d), const),
                   pl.BlockSpec((1, n_st, ch), lambda i: (jnp.minimum(i // (seq // t), n_b - 1), 0, 0))],
        out_shape=[jax.ShapeDtypeStruct((m_rows, d), F32), jax.ShapeDtypeStruct((t, d), F32),
                   jax.ShapeDtypeStruct((n_b, n_st, ch), F32)],
        scratch_shapes=[
            pltpu.VMEM((t, n_ff), BF16),
            pltpu.VMEM((HALO + t, ch), F32),
            pltpu.VMEM((t, ch), F32),
            pltpu.VMEM((t, ch), BF16),
            pltpu.VMEM((HALO, ch), F32),
        ],
        compiler_params=pltpu.CompilerParams(
            dimension_semantics=("arbitrary",), vmem_limit_bytes=VMEM_LIMIT),
        name="post_ffn2",
    )(h_all, attn_main, attn_small, conv_small, ga_all, gc_all, g2, glu_all, glu_all,
      w_dw, b_dw, ln_g, ln_b, w_ao, w_co, w_out, w1, w3, w2)


def _t5_bucket(dist, n_buckets):
    max_exact = n_buckets // 2
    d = np.maximum(dist, 0)
    ratio = (np.log(np.maximum(d, 1).astype(np.float32) / np.float32(max_exact))
             / np.float32(math.log(REL_MAX_DIST / max_exact)))
    large = np.minimum(max_exact + (ratio * np.float32(n_buckets - max_exact)).astype(np.int32), n_buckets - 1)
    return np.where(d < max_exact, d, large).astype(np.int32)


def _bucket_or_masked(dist, n_buckets):
    ok = (dist >= 0) & (dist < WINDOW)
    return np.where(ok, _t5_bucket(dist, n_buckets), -1).astype(np.int32)


def _rows(x, start, n):
    return lax.slice_in_dim(x, start, start + n, axis=0)


def kernel(x_prompt, x_sample, cache_k, cache_v, state_conv, meta_tokens, ffn1_norm, ffn1_w1, ffn1_w3, ffn1_w2, mix_norm, w_in, q_norm, k_norm, rel_bias, sinks, w_attn_out, w_dw, b_dw, conv_ln_g, conv_ln_b, w_conv_out, w_out, ffn2_norm, ffn2_w1, ffn2_w3, ffn2_w2):
    n_b, seq, d = x_prompt.shape
    db = x_sample.shape[0]
    n_meta = meta_tokens.shape[0]
    n_heads = sinks.shape[0]
    w_buf, n_kv, hd = cache_k.shape[1], cache_k.shape[2], cache_k.shape[3]
    ch = w_dw.shape[1]
    n_st = state_conv.shape[1]
    n_buckets = rel_bias.shape[0]
    aw, kvw = n_heads * hd, n_kv * hd
    t = ROW_TILE
    m_rows = n_b * seq
    assert hd == HEAD_DIM and kvw == LANES and n_kv == 2 and n_heads == 8 and w_buf == WINDOW
    assert x_sample.shape[1] == 1 and seq % t == 0 and seq % ATTN_TILE == 0
    assert n_meta + db <= t and db % SAMPLE_BLOCK == 0
    assert n_meta <= HALO and n_meta <= WINDOW and n_st == w_dw.shape[0] - 1 and n_st <= HALO
    splits = tuple(int(v) for v in np.cumsum([0, aw, kvw, kvw, ch, ch, d, d]))
    assert splits[-1] == w_in.shape[1]

    row = lambda v: v.reshape(1, -1)
    x_main = x_prompt.reshape(m_rows, d)

    h_all, q_all, kv_all, glu_all, ga_all, gc_all = _ffn_inproj(
        x_main, meta_tokens, x_sample.reshape(db, d), row(ffn1_norm), row(mix_norm), row(q_norm), row(k_norm),
        ffn1_w1, ffn1_w3, ffn1_w2, w_in, splits)

    sinks2 = row(sinks)
    b_dw2, ln_g2, ln_b2 = row(b_dw), row(conv_ln_g), row(conv_ln_b)

    dist = np.arange(WINDOW)[:, None] + WINDOW - np.arange(2 * WINDOW)[None, :]
    bidx = jnp.asarray(_bucket_or_masked(dist, n_buckets))
    attn_main, w_bf16, k_t, v_t = _prompt_attn(
        rel_bias, sinks2, q_all, kv_all, bidx,
        (w_attn_out, w_conv_out, w_out, ffn2_w1, ffn2_w3, ffn2_w2), m_rows, seq, n_meta)

    bidx_s = jnp.asarray(_bucket_or_masked(w_buf - np.arange(w_buf)[None, :], n_buckets))
    attn_small, conv_small, new_k_s, new_v_s, new_conv_s = _sample_mix(
        rel_bias, sinks2, q_all, kv_all, glu_all,
        jnp.transpose(cache_k, (0, 2, 3, 1)).reshape(db, kvw, w_buf),
        jnp.transpose(cache_v, (0, 2, 3, 1)).reshape(db, kvw, w_buf),
        jnp.transpose(state_conv, (1, 0, 2)), bidx_s, w_dw, b_dw2, ln_g2, ln_b2, m_rows + n_meta, n_meta, t)

    y_main, y_small, glu_t = _post_ffn(h_all, attn_main, attn_small, conv_small, ga_all, gc_all, row(ffn2_norm),
                                       glu_all, w_dw, b_dw2, ln_g2, ln_b2, *w_bf16, seq, n_meta)

    return (
        y_main.reshape(n_b, seq, d),
        _rows(y_small, n_meta, db).reshape(db, 1, d),
        k_t.reshape(n_b, WINDOW, n_kv, hd),
        v_t.reshape(n_b, WINDOW, n_kv, hd),
        glu_t,
        new_k_s.reshape(db, w_buf, n_kv, hd),
        new_v_s.reshape(db, w_buf, n_kv, hd),
        new_conv_s,
    )
```

```python
import functools
import math

import jax
import jax.numpy as jnp
import numpy as np
from jax import lax
from jax.experimental import pallas as pl
from jax.experimental.pallas import tpu as pltpu

F32 = jnp.float32
BF16 = jnp.bfloat16

EPS = 1e-6
NEG = -1e30
WINDOW = 128
REL_MAX_DIST = 128
HEAD_DIM = 64
LANES = 128
ROW_TILE = 256
ATTN_TILE = 1024
FF_CHUNK = 256
N_WSTEPS = 8
HALO = 32
SAMPLE_BLOCK = 16
VMEM_LIMIT = 56 * 1024 * 1024


def _dot(a, b):
    return jnp.dot(a, b, preferred_element_type=F32)


def _dot_t(a, b):
    return lax.dot_general(a, b, (((1,), (1,)), ((), ())), preferred_element_type=F32)


def _rms(x, g):
    return x * lax.rsqrt(jnp.mean(x * x, axis=-1, keepdims=True) + EPS) * g


def _pair_rms(x, g2, lo_mask):
    sq = x * x
    lo = jnp.sum(jnp.where(lo_mask, sq, 0.0), axis=-1, keepdims=True) * (1.0 / HEAD_DIM)
    hi = jnp.sum(jnp.where(lo_mask, 0.0, sq), axis=-1, keepdims=True) * (1.0 / HEAD_DIM)
    r = jnp.where(lo_mask, lax.rsqrt(lo + EPS), lax.rsqrt(hi + EPS))
    return x * r * g2


def _dep_zero(row):
    u = pltpu.bitcast(row, jnp.uint32)
    return pltpu.bitcast((u >> 16) >> 16, F32)


def _swiglu(xn, w1b, w3b, w2b, hid_ref, fillers=()):
    n_ff = w1b.shape[1]
    n_chunks = n_ff // FF_CHUNK
    due = {}
    for c in range(n_chunks):
        sl = slice(c * FF_CHUNK, (c + 1) * FF_CHUNK)
        a = _dot(xn, w1b[:, sl])
        b = _dot(xn, w3b[:, sl])
        for z in due.pop(c, ()):
            b = b + jnp.concatenate([z] * (FF_CHUNK // LANES), axis=1)
        for issue, when, thunk in fillers:
            if issue == c:
                due.setdefault(min(when, n_chunks), []).append(_dep_zero(thunk()))
        hid_ref[:, sl] = (a * jax.nn.sigmoid(a) * b).astype(BF16)
    out = _dot(hid_ref[...], w2b[...])
    for z in due.pop(n_chunks, ()):
        out = out + jnp.concatenate([z] * (out.shape[1] // LANES), axis=1)
    assert not due
    return out


def _store_chunk(dst, src, i):
    rows = src.shape[0]
    r = pl.multiple_of(i * rows, 16)
    dst[pl.ds(r, rows), :] = src[...].astype(BF16)


def _bias_table(bidx, rb_ref, head):
    tab = jnp.full(bidx.shape, NEG, F32)
    for b in range(rb_ref.shape[0]):
        tab = jnp.where(bidx == b, rb_ref[b, head], tab)
    return tab


def _ln_swish(y, g, b):
    mu = jnp.mean(y, axis=-1, keepdims=True)
    yc = y - mu
    var = jnp.mean(yc * yc, axis=-1, keepdims=True)
    z = yc * lax.rsqrt(var + EPS) * g + b
    return z * jax.nn.sigmoid(z)


def _ffn_inproj_body(xc_ref, xn_ref, meta_ref, xsm_ref, g1_ref, gm_ref, gq_ref, gk_ref, w1c, w3c, w2c, winc,
                     h_ref, q_ref, kv_ref, glu_ref, ga_ref, gc_ref,
                     w1b, w3b, w2b, winb, hid_ref, xn_st, u_st, *, n_main, splits):
    i = pl.program_id(0)

    def small_tile():
        pad = xc_ref.shape[0] - meta_ref.shape[0] - xsm_ref.shape[0]
        return jnp.concatenate([meta_ref[...], xsm_ref[...], jnp.zeros((pad, xc_ref.shape[1]), F32)], axis=0)

    @pl.when(i < N_WSTEPS)
    def _():
        _store_chunk(w1b, w1c, i)
        _store_chunk(w3b, w3c, i)
        _store_chunk(w2b, w2c, i)
        _store_chunk(winb, winc, i)

    s = i - N_WSTEPS

    def project():
        u = u_st[...]
        o_q, o_k, o_v, o_a, o_b, o_ga, o_gc, o_end = splits
        lo_mask = lax.broadcasted_iota(jnp.int32, (1, LANES), 1) < HEAD_DIM
        gq2 = jnp.concatenate([gq_ref[...], gq_ref[...]], axis=1)
        gk2 = jnp.concatenate([gk_ref[...], gk_ref[...]], axis=1)
        zq = _dot(u, winb[:, o_q:o_k])
        for p in range((o_k - o_q) // LANES):
            sl = slice(p * LANES, (p + 1) * LANES)
            q_ref[:, sl] = (_pair_rms(zq[:, sl], gq2, lo_mask) * (HEAD_DIM ** -0.5)).astype(BF16)
        zkv = _dot(u, winb[:, o_k:o_a])
        kv_ref[:, :LANES] = _pair_rms(zkv[:, :LANES], gk2, lo_mask)
        kv_ref[:, LANES:] = zkv[:, LANES:]
        za = _dot(u, winb[:, o_a:o_b])
        zb = _dot(u, winb[:, o_b:o_ga])
        glu_ref[...] = za * jax.nn.sigmoid(zb)
        ga_ref[...] = jax.nn.sigmoid(_dot(u, winb[:, o_ga:o_gc]))
        gc_ref[...] = jax.nn.sigmoid(_dot(u, winb[:, o_gc:o_end]))

    def normalise_next():
        x_next = jnp.where(s + 1 < n_main, xn_ref[...], small_tile())
        xn_st[...] = _rms(x_next, g1_ref[...]).astype(BF16)

    @pl.when(i == 0)
    def _():
        u_st[...] = jnp.zeros_like(u_st)

    @pl.when(s == -1)
    def _():
        normalise_next()

    @pl.when((s >= 0) & (s <= n_main))
    def _():
        project()
        x = jnp.where(s < n_main, xc_ref[...], small_tile())
        h = x + 0.5 * _swiglu(xn_st[...], w1b, w3b, w2b, hid_ref)
        h_ref[...] = h
        u_st[...] = _rms(h, gm_ref[...]).astype(BF16)
        normalise_next()

    @pl.when(s == n_main + 1)
    def _():
        project()


def _ffn_inproj(x_main, x_meta, x_sample, g1, gm, gq, gk, w1, w3, w2, w_in, splits):
    m_rows, d = x_main.shape
    assert x_meta.shape[0] % 8 == 0 and x_sample.shape[0] % 8 == 0
    assert x_meta.shape[0] + x_sample.shape[0] <= ROW_TILE
    n_ff = w1.shape[1]
    n_main = m_rows // ROW_TILE
    r_rows = m_rows + ROW_TILE
    t = ROW_TILE
    o_q, o_k, o_v, o_a, o_b, o_ga, o_gc, o_end = splits
    assert o_v - o_k == LANES and o_a - o_v == LANES and (o_k - o_q) % LANES == 0

    def cur_idx(i):
        return (jnp.clip(i - N_WSTEPS, 0, n_main - 1), 0)

    def next_idx(i):
        return (jnp.clip(i - N_WSTEPS + 1, 0, n_main - 1), 0)

    def h_idx(i):
        return (jnp.clip(i - N_WSTEPS, 0, n_main), 0)

    def proj_idx(i):
        return (jnp.clip(i - N_WSTEPS - 1, 0, n_main), 0)

    def w_idx(i):
        return (jnp.minimum(i, N_WSTEPS - 1), 0)

    const = lambda i: (0, 0)
    outs = [
        (jax.ShapeDtypeStruct((r_rows, d), F32), h_idx),
        (jax.ShapeDtypeStruct((r_rows, o_k - o_q), BF16), proj_idx),
        (jax.ShapeDtypeStruct((r_rows, o_a - o_k), F32), proj_idx),
        (jax.ShapeDtypeStruct((r_rows, o_b - o_a), F32), proj_idx),
        (jax.ShapeDtypeStruct((r_rows, o_gc - o_ga), F32), proj_idx),
        (jax.ShapeDtypeStruct((r_rows, o_end - o_gc), F32), proj_idx),
    ]
    return pl.pallas_call(
        functools.partial(_ffn_inproj_body, n_main=n_main, splits=splits),
        grid=(N_WSTEPS + n_main + 2,),
        in_specs=[
            pl.BlockSpec((t, d), cur_idx),
            pl.BlockSpec((t, d), next_idx),
            pl.BlockSpec(x_meta.shape, const),
            pl.BlockSpec(x_sample.shape, const),
            pl.BlockSpec((1, d), const),
            pl.BlockSpec((1, d), const),
            pl.BlockSpec((1, HEAD_DIM), const),
            pl.BlockSpec((1, HEAD_DIM), const),
            pl.BlockSpec((d // N_WSTEPS, n_ff), w_idx),
            pl.BlockSpec((d // N_WSTEPS, n_ff), w_idx),
            pl.BlockSpec((n_ff // N_WSTEPS, d), w_idx),
            pl.BlockSpec((d // N_WSTEPS, o_end), w_idx),
        ],
        out_specs=[pl.BlockSpec((t, s.shape[1]), idx) for s, idx in outs],
        out_shape=[s for s, _ in outs],
        scratch_shapes=[
            pltpu.VMEM((d, n_ff), BF16),
            pltpu.VMEM((d, n_ff), BF16),
            pltpu.VMEM((n_ff, d), BF16),
            pltpu.VMEM((d, o_end), BF16),
            pltpu.VMEM((t, n_ff), BF16),
            pltpu.VMEM((t, d), BF16),
            pltpu.VMEM((t, d), BF16),
        ],
        compiler_params=pltpu.CompilerParams(
            dimension_semantics=("arbitrary",), vmem_limit_bytes=VMEM_LIMIT),
        name="ffn1_inproj",
    )(x_main, x_main, x_meta, x_sample, g1, gm, gq, gk, w1, w3, w2, w_in)


def _prompt_attn_body(rb_ref, sink_ref, q_ref, kvc_ref, kvp_ref, kvs_ref, bidx_ref, *rest,
                      tiles_per_seq, n_meta, n_heads, n_weights, n_cast):
    w_refs, attn_ref, wb_refs = rest[:n_weights], rest[n_weights], rest[n_weights + 1:2 * n_weights + 1]
    kt_ref, vt_ref, bias_sc, kvm_ref = rest[2 * n_weights + 1:]
    i = pl.program_id(0)

    @pl.when(i >= pl.num_programs(0) - n_cast)
    def _():
        for w_ref, wb_ref in zip(w_refs, wb_refs, strict=True):
            wb_ref[...] = w_ref[...].astype(BF16)

    t = q_ref.shape[0]
    n_kv = 2
    grp = n_heads // n_kv

    @pl.when(i == 0)
    def _():
        kvm_ref[...] = jnp.zeros_like(kvm_ref)
        kvm_ref[WINDOW - n_meta:WINDOW, :] = kvs_ref[0:n_meta, :]
        col = lax.broadcasted_iota(jnp.int32, (WINDOW, 2 * WINDOW), 1)
        for hd in range(n_heads):
            tab = _bias_table(bidx_ref[...], rb_ref, hd)
            bias_sc[0, hd] = tab
            bias_sc[1, hd] = jnp.where(col < WINDOW - n_meta, NEG, tab)

    first = (i % tiles_per_seq) == 0
    lead = jnp.where(first, 1, 0)
    lo_mask = lax.broadcasted_iota(jnp.int32, (1, LANES), 1) < HEAD_DIM
    kv_prev = jnp.where(first, kvm_ref[...], kvp_ref[...])
    kv = jnp.concatenate([kv_prev, kvc_ref[...]], axis=0)
    k, v = kv[:, :LANES], kv[:, LANES:]
    kb, vb = k.astype(BF16), v.astype(BF16)
    kr = pltpu.roll(k, HEAD_DIM, axis=1).astype(BF16)
    vr = pltpu.roll(v, HEAD_DIM, axis=1).astype(BF16)
    zero = jnp.zeros((), BF16)
    k_lo = [jnp.where(lo_mask, kb, zero), jnp.where(lo_mask, kr, zero)]
    k_hi = [jnp.where(lo_mask, zero, kr), jnp.where(lo_mask, zero, kb)]
    v_lo = [jnp.where(lo_mask, vb, zero), jnp.where(lo_mask, vr, zero)]
    v_hi = [jnp.where(lo_mask, zero, vr), jnp.where(lo_mask, zero, vb)]

    for qb in range(t // WINDOW):
        rows = slice(qb * WINDOW, (qb + 1) * WINDOW)
        keys = slice(qb * WINDOW, qb * WINDOW + 2 * WINDOW)
        tab = lead if qb == 0 else 0
        for h in range(n_kv):
            k_st = jnp.concatenate([k_lo[h][keys], k_hi[h][keys]], axis=0)
            v_st = jnp.concatenate([v_lo[h][keys], v_hi[h][keys]], axis=0)
            c0 = h * grp * HEAD_DIM
            qq = jnp.concatenate([q_ref[rows, c0:c0 + LANES], q_ref[rows, c0 + LANES:c0 + 2 * LANES]], axis=0)
            s = _dot_t(qq, k_st)
            p_parts, inv_parts = [], []
            for g2 in range(2):
                p_row, inv_row = [], []
                for par in range(2):
                    hd = h * grp + 2 * g2 + par
                    sq = s[g2 * WINDOW:(g2 + 1) * WINDOW, par * 2 * WINDOW:(par + 1) * 2 * WINDOW] + bias_sc[tab, hd]
                    sink = sink_ref[0, hd]
                    m = jnp.maximum(jnp.max(sq, axis=-1, keepdims=True), sink)
                    p = jnp.exp(sq - m)
                    den = jnp.sum(p, axis=-1, keepdims=True) + jnp.exp(sink - m)
                    p_row.append(p.astype(BF16))
                    inv_row.append(1.0 / den)
                p_parts.append(jnp.concatenate(p_row, axis=1))
                inv_parts.append(jnp.where(lo_mask, inv_row[0], inv_row[1]))
            pm = jnp.concatenate(p_parts, axis=0)
            o = _dot(pm, v_st)
            for g2 in range(2):
                c = c0 + g2 * LANES
                attn_ref[rows, c:c + LANES] = (o[g2 * WINDOW:(g2 + 1) * WINDOW] * inv_parts[g2]).astype(attn_ref.dtype)

    @pl.when((i + 1) % tiles_per_seq == 0)
    def _():
        kt_ref[0] = kvc_ref[t - WINDOW:t, :LANES].T
        vt_ref[0] = kvc_ref[t - WINDOW:t, LANES:].T


def _prompt_attn(rel_bias, sinks2, q_all, kv_all, bidx, weights, m_rows, seq, n_meta):
    t = ATTN_TILE
    n_steps = m_rows // t
    assert m_rows % WINDOW == 0 and n_meta <= WINDOW
    n_heads = sinks2.shape[1]
    qw, kvw = q_all.shape[1], kv_all.shape[1]
    const = lambda i: (0, 0)
    row = lambda i: (i, 0)
    per_seq = lambda i: (i // (seq // t), 0, 0)
    smem = pl.BlockSpec(memory_space=pltpu.SMEM)

    n_cast = max(n_steps // 2, 1)

    def w_spec(w):
        assert w.shape[0] % (16 * n_cast) == 0
        return pl.BlockSpec((w.shape[0] // n_cast, w.shape[1]), lambda i: (jnp.maximum(i - (n_steps - n_cast), 0), 0))

    w_specs = [w_spec(w) for w in weights]
    outs = pl.pallas_call(
        functools.partial(_prompt_attn_body, tiles_per_seq=seq // t, n_meta=n_meta, n_heads=n_heads,
                          n_weights=len(weights), n_cast=n_cast),
        grid=(n_steps,),
        in_specs=[
            smem, smem,
            pl.BlockSpec((t, qw), row),
            pl.BlockSpec((t, kvw), row),
            pl.BlockSpec((WINDOW, kvw), lambda i: (jnp.maximum(i * (t // WINDOW) - 1, 0), 0)),
            pl.BlockSpec((WINDOW, kvw), lambda i: (m_rows // WINDOW, 0)),
            pl.BlockSpec((WINDOW, 2 * WINDOW), const),
        ] + w_specs,
        out_specs=[pl.BlockSpec((t, qw), row)] + w_specs + [pl.BlockSpec((1, WINDOW, LANES), per_seq)] * 2,
        out_shape=[jax.ShapeDtypeStruct((m_rows, qw), BF16)] + [jax.ShapeDtypeStruct(w.shape, BF16) for w in weights]
        + [jax.ShapeDtypeStruct((m_rows // seq, WINDOW, LANES), F32)] * 2,
        scratch_shapes=[
            pltpu.VMEM((2, n_heads, WINDOW, 2 * WINDOW), F32),
            pltpu.VMEM((WINDOW, kvw), F32),
        ],
        compiler_params=pltpu.CompilerParams(
            dimension_semantics=("arbitrary",), vmem_limit_bytes=VMEM_LIMIT),
        name="prompt_attn",
    )(rel_bias, sinks2, q_all, kv_all, kv_all, kv_all, bidx, *weights)
    return outs[0], outs[1:1 + len(weights)], outs[1 + len(weights)], outs[2 + len(weights)]


def _sample_mix_body(rb_ref, sink_ref, q_ref, kvn_ref, glun_ref, ck_ref, cv_ref, st_ref, bidx_ref,
                     wdw_ref, bdw_ref, lng_ref, lnb_ref,
                     as_ref, cs_ref, nk_ref, nv_ref, nst_ref, bias_sc, *, n_heads, n_meta):
    i = pl.program_id(0)
    sb, w_buf = ck_ref.shape[0], ck_ref.shape[2]
    n_st = st_ref.shape[0]
    n_rows = sb * n_heads
    per_kv = n_heads // (LANES // HEAD_DIM)

    @pl.when(i == 0)
    def _():
        as_ref[...] = jnp.zeros_like(as_ref)
        cs_ref[...] = jnp.zeros_like(cs_ref)
        tiled = jnp.concatenate(
            [jnp.broadcast_to(jnp.concatenate([_bias_table(bidx_ref[...], rb_ref, hd)] * sb, axis=1),
                              (sb, sb * w_buf)) for hd in range(n_heads)], axis=0)
        row_seq = lax.rem(lax.broadcasted_iota(jnp.int32, tiled.shape, 0), sb)
        col_seq = lax.broadcasted_iota(jnp.int32, tiled.shape, 1) // w_buf
        bias_sc[...] = jnp.where(row_seq == col_seq, tiled, NEG)

    hrow = lax.broadcasted_iota(jnp.int32, (n_rows, 1), 0) // sb
    sink = jnp.zeros((n_rows, 1), F32)
    bias_new = jnp.zeros((n_rows, 1), F32)
    for hd in range(n_heads):
        sink = jnp.where(hrow == hd, sink_ref[0, hd], sink)
        bias_new = jnp.where(hrow == hd, rb_ref[0, hd], bias_new)

    lo_mask = lax.broadcasted_iota(jnp.int32, (1, LANES), 1) < HEAD_DIM

    def half(x, src_hi, dst_hi):
        if src_hi != dst_hi:
            x = pltpu.roll(x, HEAD_DIM, 1)
        return jnp.where(lo_mask, 0.0, x) if dst_hi else jnp.where(lo_mask, x, 0.0)

    q = jnp.concatenate(
        [half(q_ref[:, (hd // 2) * LANES:(hd // 2 + 1) * LANES].astype(F32), hd % 2 == 1, hd // per_kv == 1)
         for hd in range(n_heads)], axis=0).astype(BF16)
    kn_rows = jnp.concatenate([kvn_ref[:, :LANES]] * n_heads, axis=0).astype(BF16)
    vn_rows = jnp.concatenate([kvn_ref[:, LANES:]] * n_heads, axis=0).astype(BF16)
    ck_all = jnp.concatenate([ck_ref[bb] for bb in range(sb)], axis=1).astype(BF16)
    cv_all = jnp.concatenate([cv_ref[bb] for bb in range(sb)], axis=1).astype(BF16)
    s_c = _dot(q, ck_all) + bias_sc[...]
    s_n = jnp.sum(q.astype(F32) * kn_rows.astype(F32), axis=-1, keepdims=True) + bias_new
    m = jnp.maximum(jnp.maximum(jnp.max(s_c, axis=-1, keepdims=True), s_n), sink)
    p_c = jnp.exp(s_c - m)
    p_n = jnp.exp(s_n - m)
    den = jnp.sum(p_c, axis=-1, keepdims=True) + p_n + jnp.exp(sink - m)
    o = _dot_t(p_c.astype(BF16), cv_all) + p_n.astype(BF16).astype(F32) * vn_rows.astype(F32)
    o = o / den
    pairs = []
    for p in range(n_heads // 2):
        lo_head = o[(2 * p) * sb:(2 * p + 1) * sb]
        hi_head = o[(2 * p + 1) * sb:(2 * p + 2) * sb]
        pairs.append(half(lo_head, (2 * p) // per_kv == 1, False) + half(hi_head, (2 * p + 1) // per_kv == 1, True))
    rows = pl.ds(pl.multiple_of(n_meta + i * sb, sb), sb)
    as_ref[rows, :] = jnp.concatenate(pairs, axis=1).astype(as_ref.dtype)

    g_new = glun_ref[...]
    y = g_new * wdw_ref[n_st:n_st + 1, :] + bdw_ref[...]
    for k in range(n_st):
        y = y + st_ref[k] * wdw_ref[k:k + 1, :]
    cs_ref[rows, :] = _ln_swish(y, lng_ref[...], lnb_ref[...]).astype(cs_ref.dtype)

    for bb in range(sb):
        nk_ref[bb, 0:w_buf - 1, :] = ck_ref[bb].T[1:w_buf, :]
        nk_ref[bb, w_buf - 1:w_buf, :] = kvn_ref[bb:bb + 1, :LANES]
        nv_ref[bb, 0:w_buf - 1, :] = cv_ref[bb].T[1:w_buf, :]
        nv_ref[bb, w_buf - 1:w_buf, :] = kvn_ref[bb:bb + 1, LANES:]
        nst_ref[bb, 0:n_st - 1, :] = st_ref[1:n_st, bb, :]
        nst_ref[bb, n_st - 1:n_st, :] = glun_ref[bb:bb + 1, :]


def _sample_mix(rel_bias, sinks2, q_all, kv_all, glu_all, cache_kt, cache_vt, state_t, bidx_s, w_dw, b_dw, ln_g, ln_b,
                s0, n_meta, t_rows):
    db, kw, w_buf = cache_kt.shape
    n_st, ch = state_t.shape[0], state_t.shape[2]
    n_heads = sinks2.shape[1]
    aw = q_all.shape[1]
    sb = SAMPLE_BLOCK
    assert s0 % sb == 0 and n_meta % sb == 0 and n_meta + db <= t_rows
    blk = lambda i: (i, 0, 0)
    const = lambda i: (0, 0)
    tok = lambda i: (s0 // sb + i, 0)
    smem = pl.BlockSpec(memory_space=pltpu.SMEM)
    return pl.pallas_call(
        functools.partial(_sample_mix_body, n_heads=n_heads, n_meta=n_meta),
        grid=(db // sb,),
        in_specs=[
            smem, smem,
            pl.BlockSpec((sb, aw), tok),
            pl.BlockSpec((sb, 2 * kw), tok),
            pl.BlockSpec((sb, ch), tok),
            pl.BlockSpec((sb, kw, w_buf), blk),
            pl.BlockSpec((sb, kw, w_buf), blk),
            pl.BlockSpec((n_st, sb, ch), lambda i: (0, i, 0)),
            pl.BlockSpec((1, w_buf), const),
            pl.BlockSpec(w_dw.shape, const),
            pl.BlockSpec((1, ch), const),
            pl.BlockSpec((1, ch), const),
            pl.BlockSpec((1, ch), const),
        ],
        out_specs=[
            pl.BlockSpec((t_rows, aw), const),
            pl.BlockSpec((t_rows, ch), const),
            pl.BlockSpec((sb, w_buf, kw), blk),
            pl.BlockSpec((sb, w_buf, kw), blk),
            pl.BlockSpec((sb, n_st, ch), blk),
        ],
        out_shape=[
            jax.ShapeDtypeStruct((t_rows, aw), BF16),
            jax.ShapeDtypeStruct((t_rows, ch), BF16),
            jax.ShapeDtypeStruct((db, w_buf, kw), F32),
            jax.ShapeDtypeStruct((db, w_buf, kw), F32),
            jax.ShapeDtypeStruct((db, n_st, ch), F32),
        ],
        scratch_shapes=[pltpu.VMEM((sb * n_heads, sb * w_buf), F32)],
        compiler_params=pltpu.CompilerParams(dimension_semantics=("arbitrary",)),
        name="sample_mix",
    )(rel_bias, sinks2, q_all, kv_all, glu_all, cache_kt, cache_vt, state_t, bidx_s, w_dw, b_dw, ln_g, ln_b)


def _conv_runs(cb_st, glu_ref, glum_ref, y_sc, wdw_ref, bdw_ref, first, n_runs):
    t = y_sc.shape[0]
    conv_w = wdw_ref.shape[0]
    n_ch = y_sc.shape[1]
    off = HALO - (conv_w - 1)
    n_a = (off + conv_w - 1) // 8 + 1
    units = [(lc, g) for lc in range(n_ch // LANES) for g in range(t // 8)]
    sizes = [len(units) // n_runs + (1 if r < len(units) % n_runs else 0) for r in range(n_runs)]
    staged = []

    def run(mine):
        if not staged:
            cb_st[0:HALO, :] = jnp.where(first, glum_ref[...], cb_st[0:HALO, :])
            cb_st[HALO:HALO + t, :] = glu_ref[...]
            staged.append(True)
        zs = {}

        def z(lc, s, g):
            if (lc, s, g) not in zs:
                ls = slice(lc * LANES, (lc + 1) * LANES)
                acc = None
                for a in range(n_a):
                    w = 8 * a + s - off
                    if 0 <= w < conv_w:
                        term = cb_st[8 * (g + a):8 * (g + a) + 8, ls] * wdw_ref[w:w + 1, ls]
                        acc = term if acc is None else acc + term
                zs[(lc, s, g)] = acc
            return zs[(lc, s, g)]

        dep = None
        for lc, g in mine:
            ls = slice(lc * LANES, (lc + 1) * LANES)
            acc = jnp.broadcast_to(bdw_ref[:, ls], (8, LANES)) + z(lc, 0, g)
            if dep is not None:
                acc = acc + dep
            for s in range(1, 8):
                acc = acc + jnp.concatenate([z(lc, s, g), z(lc, s, g + 1)], axis=0)[s:s + 8, :]
            y_sc[8 * g:8 * g + 8, ls] = acc
            dep = _dep_zero(acc)
        return dep[0:1, :]

    out, k = [], 0
    for n in sizes:
        out.append(functools.partial(run, units[k:k + n]))
        k += n
    return out


def _post_ffn_body(h_ref, am_ref, as_ref, cs_ref, ga_ref, gc_ref, g2_ref, glu_ref, glus_ref,
                   wdw_ref, bdw_ref, lng_ref, lnb_ref,
                   waob, wcob, woutb, w1b, w3b, w2b,
                   ym_ref, ys_ref, gt_ref,
                   hid_ref, cb_st, y_sc, conv_st, glum_ref, *, n_main, tiles_per_seq, n_meta):
    i = pl.program_id(0)
    t_rows = y_sc.shape[0]

    @pl.when(i == 0)
    def _():
        cb_st[...] = jnp.zeros_like(cb_st)
        glum_ref[...] = jnp.zeros_like(glum_ref)
        glum_ref[HALO - n_meta:HALO, :] = glus_ref[0:n_meta, :]

    t = i - 1
    tn = t + 1
    first = (tn < n_main) & (lax.rem(tn, tiles_per_seq) == 0)
    n_chunks = w1b.shape[1] // FF_CHUNK

    def finish_conv():
        conv_st[...] = _ln_swish(y_sc[...], lng_ref[...], lnb_ref[...]).astype(conv_st.dtype)
        cb_st[0:HALO, :] = cb_st[t_rows:t_rows + HALO, :]

    @pl.when(t == -1)
    def _():
        for f in _conv_runs(cb_st, glu_ref, glum_ref, y_sc, wdw_ref, bdw_ref, first, n_chunks):
            f()
        finish_conv()

    @pl.when(t >= 0)
    def _():
        on_main = t < n_main
        conv_f = _conv_runs(cb_st, glu_ref, glum_ref, y_sc, wdw_ref, bdw_ref, first, n_chunks)
        fillers = [(c, c + 1, f) for c, f in enumerate(conv_f)]

        at = jnp.where(on_main, am_ref[...], as_ref[...])
        cv = jnp.where(on_main, conv_st[...], cs_ref[...])
        a = _dot(at, waob[...])
        c = _dot(cv, wcob[...])
        mix = (ga_ref[...] * a + gc_ref[...] * c).astype(BF16)
        h2 = h_ref[...] + _dot(mix, woutb[...])
        xn = _rms(h2, g2_ref[...]).astype(BF16)
        y = h2 + 0.5 * _swiglu(xn, w1b, w3b, w2b, hid_ref, fillers)
        finish_conv()

        @pl.when(on_main)
        def _():
            ym_ref[...] = y

        @pl.when(t == n_main)
        def _():
            ys_ref[...] = y

    n_st = gt_ref.shape[0]
    for sq in range(gt_ref.shape[1]):
        @pl.when(tn == (sq + 1) * tiles_per_seq - 1)
        def _():
            for r in range(n_st):
                gt_ref[r, sq:sq + 1, :] = glu_ref[t_rows - n_st + r:t_rows - n_st + r + 1, :]


def _post_ffn(h_all, attn_main, attn_small, conv_small, ga_all, gc_all, g2, glu_all,
              w_dw, b_dw, ln_g, ln_b, w_ao, w_co, w_out, w1, w3, w2, seq, n_meta):
    m_rows, aw = attn_main.shape
    assert m_rows % HALO == 0 and n_meta <= HALO
    ch = glu_all.shape[1]
    d = h_all.shape[1]
    n_ff = w1.shape[1]
    t = ROW_TILE
    n_main = m_rows // t
    n_b = m_rows // seq
    n_st = w_dw.shape[0] - 1

    def main_idx(i):
        return (jnp.clip(i - 1, 0, n_main - 1), 0)

    def next_idx(i):
        return (jnp.minimum(i, n_main - 1), 0)

    def row_idx(i):
        return (jnp.maximum(i - 1, 0), 0)

    const = lambda i: (0, 0)
    resident = lambda w: pl.BlockSpec(w.shape, const, pipeline_mode=pl.Buffered(1))
    return pl.pallas_call(
        functools.partial(_post_ffn_body, n_main=n_main, tiles_per_seq=seq // t, n_meta=n_meta),
        grid=(n_main + 2,),
        in_specs=[
            pl.BlockSpec((t, d), row_idx),
            pl.BlockSpec((t, aw), main_idx),
            pl.BlockSpec((t, aw), const),
            pl.BlockSpec((t, ch), const),
            pl.BlockSpec((t, d), row_idx),
            pl.BlockSpec((t, d), row_idx),
            pl.BlockSpec((1, d), const),
            pl.BlockSpec((t, ch), next_idx),
            pl.BlockSpec((HALO, ch), lambda i: (m_rows // HALO, 0)),
            pl.BlockSpec(w_dw.shape, const),
            pl.BlockSpec((1, ch), const),
            pl.BlockSpec((1, ch), const),
            pl.BlockSpec((1, ch), const),
            resident(w_ao), resident(w_co), resident(w_out), resident(w1), resident(w3), resident(w2),
        ],
        out_specs=[pl.BlockSpec((t, d), main_idx), pl.BlockSpec((t, d), const),
                   pl.BlockSpec((n_st, n_b, ch), lambda i: (0, 0, 0))],
        out_shape=[jax.ShapeDtypeStruct((m_rows, d), F32), jax.ShapeDtypeStruct((t, d), F32),
                   jax.ShapeDtypeStruct((n_st, n_b, ch), F32)],
        scratch_shapes=[
            pltpu.VMEM((t, n_ff), BF16),
            pltpu.VMEM((HALO + t, ch), F32),
            pltpu.VMEM((t, ch), F32),
            pltpu.VMEM((t, ch), BF16),
            pltpu.VMEM((HALO, ch), F32),
        ],
        compiler_params=pltpu.CompilerParams(
            dimension_semantics=("arbitrary",), vmem_limit_bytes=VMEM_LIMIT),
        name="post_ffn2",
    )(h_all, attn_main, attn_small, conv_small, ga_all, gc_all, g2, glu_all, glu_all,
      w_dw, b_dw, ln_g, ln_b, w_ao, w_co, w_out, w1, w3, w2)


def _t5_bucket(dist, n_buckets):
    max_exact = n_buckets // 2
    d = np.maximum(dist, 0)
    ratio = (np.log(np.maximum(d, 1).astype(np.float32) / np.float32(max_exact))
             / np.float32(math.log(REL_MAX_DIST / max_exact)))
    large = np.minimum(max_exact + (ratio * np.float32(n_buckets - max_exact)).astype(np.int32), n_buckets - 1)
    return np.where(d < max_exact, d, large).astype(np.int32)


def _bucket_or_masked(dist, n_buckets):
    ok = (dist >= 0) & (dist < WINDOW)
    return np.where(ok, _t5_bucket(dist, n_buckets), -1).astype(np.int32)


def _rows(x, start, n):
    return lax.slice_in_dim(x, start, start + n, axis=0)


def kernel(x_prompt, x_sample, cache_k, cache_v, state_conv, meta_tokens, ffn1_norm, ffn1_w1, ffn1_w3, ffn1_w2, mix_norm, w_in, q_norm, k_norm, rel_bias, sinks, w_attn_out, w_dw, b_dw, conv_ln_g, conv_ln_b, w_conv_out, w_out, ffn2_norm, ffn2_w1, ffn2_w3, ffn2_w2):
    n_b, seq, d = x_prompt.shape
    db = x_sample.shape[0]
    n_meta = meta_tokens.shape[0]
    n_heads = sinks.shape[0]
    w_buf, n_kv, hd = cache_k.shape[1], cache_k.shape[2], cache_k.shape[3]
    ch = w_dw.shape[1]
    n_st = state_conv.shape[1]
    n_buckets = rel_bias.shape[0]
    aw, kvw = n_heads * hd, n_kv * hd
    t = ROW_TILE
    m_rows = n_b * seq
    assert hd == HEAD_DIM and kvw == LANES and n_kv == 2 and n_heads == 8 and w_buf == WINDOW
    assert x_sample.shape[1] == 1 and seq % t == 0 and seq % ATTN_TILE == 0
    assert n_meta + db <= t and db % SAMPLE_BLOCK == 0
    assert n_meta <= HALO and n_meta <= WINDOW and n_st == w_dw.shape[0] - 1 and n_st <= HALO
    splits = tuple(int(v) for v in np.cumsum([0, aw, kvw, kvw, ch, ch, d, d]))
    assert splits[-1] == w_in.shape[1]

    row = lambda v: v.reshape(1, -1)
    x_main = x_prompt.reshape(m_rows, d)

    h_all, q_all, kv_all, glu_all, ga_all, gc_all = _ffn_inproj(
        x_main, meta_tokens, x_sample.reshape(db, d), row(ffn1_norm), row(mix_norm), row(q_norm), row(k_norm),
        ffn1_w1, ffn1_w3, ffn1_w2, w_in, splits)

    sinks2 = row(sinks)
    b_dw2, ln_g2, ln_b2 = row(b_dw), row(conv_ln_g), row(conv_ln_b)

    dist = np.arange(WINDOW)[:, None] + WINDOW - np.arange(2 * WINDOW)[None, :]
    bidx = jnp.asarray(_bucket_or_masked(dist, n_buckets))
    attn_main, w_bf16, k_t, v_t = _prompt_attn(
        rel_bias, sinks2, q_all, kv_all, bidx,
        (w_attn_out, w_conv_out, w_out, ffn2_w1, ffn2_w3, ffn2_w2), m_rows, seq, n_meta)

    bidx_s = jnp.asarray(_bucket_or_masked(w_buf - np.arange(w_buf)[None, :], n_buckets))
    attn_small, conv_small, new_k_s, new_v_s, new_conv_s = _sample_mix(
        rel_bias, sinks2, q_all, kv_all, glu_all,
        jnp.transpose(cache_k, (0, 2, 3, 1)).reshape(db, kvw, w_buf),
        jnp.transpose(cache_v, (0, 2, 3, 1)).reshape(db, kvw, w_buf),
        jnp.transpose(state_conv, (1, 0, 2)), bidx_s, w_dw, b_dw2, ln_g2, ln_b2, m_rows + n_meta, n_meta, t)

    y_main, y_small, glu_t = _post_ffn(h_all, attn_main, attn_small, conv_small, ga_all, gc_all, row(ffn2_norm),
                                       glu_all, w_dw, b_dw2, ln_g2, ln_b2, *w_bf16, seq, n_meta)

    return (
        y_main.reshape(n_b, seq, d),
        _rows(y_small, n_meta, db).reshape(db, 1, d),
        jnp.transpose(k_t.reshape(n_b, n_kv, hd, WINDOW), (0, 3, 1, 2)),
        jnp.transpose(v_t.reshape(n_b, n_kv, hd, WINDOW), (0, 3, 1, 2)),
        jnp.transpose(glu_t, (1, 0, 2)),
        new_k_s.reshape(db, w_buf, n_kv, hd),
        new_v_s.reshape(db, w_buf, n_kv, hd),
        new_conv_s,
    )
```

```python
import functools
import math

import jax
import jax.numpy as jnp
import numpy as np
from jax import lax
from jax.experimental import pallas as pl
from jax.experimental.pallas import tpu as pltpu

F32 = jnp.float32
BF16 = jnp.bfloat16

EPS = 1e-6
NEG = -1e30
WINDOW = 128
REL_MAX_DIST = 128
HEAD_DIM = 64
LANES = 128
ROW_TILE = 256
ATTN_TILE = 1024
FF_CHUNK = 256
N_WSTEPS = 8
HALO = 32
SAMPLE_BLOCK = 16
VMEM_LIMIT = 56 * 1024 * 1024


def _dot(a, b):
    return jnp.dot(a, b, preferred_element_type=F32)


def _dot_t(a, b):
    return lax.dot_general(a, b, (((1,), (1,)), ((), ())), preferred_element_type=F32)


def _rms(x, g):
    return x * lax.rsqrt(jnp.mean(x * x, axis=-1, keepdims=True) + EPS) * g


def _pair_rms(x, g2, lo_mask):
    sq = x * x
    lo = jnp.sum(jnp.where(lo_mask, sq, 0.0), axis=-1, keepdims=True) * (1.0 / HEAD_DIM)
    hi = jnp.sum(jnp.where(lo_mask, 0.0, sq), axis=-1, keepdims=True) * (1.0 / HEAD_DIM)
    r = jnp.where(lo_mask, lax.rsqrt(lo + EPS), lax.rsqrt(hi + EPS))
    return x * r * g2


def _dep_zero(row):
    u = pltpu.bitcast(row, jnp.uint32)
    return pltpu.bitcast((u >> 16) >> 16, F32)


def _swiglu(xn, w1b, w3b, w2b, hid_ref, fillers=()):
    n_ff = w1b.shape[1]
    n_chunks = n_ff // FF_CHUNK
    due = {}
    for c in range(n_chunks):
        sl = slice(c * FF_CHUNK, (c + 1) * FF_CHUNK)
        a = _dot(xn, w1b[:, sl])
        b = _dot(xn, w3b[:, sl])
        for z in due.pop(c, ()):
            b = b + jnp.concatenate([z] * (FF_CHUNK // LANES), axis=1)
        for issue, when, thunk in fillers:
            if issue == c:
                due.setdefault(min(when, n_chunks), []).append(_dep_zero(thunk()))
        hid_ref[:, sl] = (a * jax.nn.sigmoid(a) * b).astype(BF16)
    out = _dot(hid_ref[...], w2b[...])
    for z in due.pop(n_chunks, ()):
        out = out + jnp.concatenate([z] * (out.shape[1] // LANES), axis=1)
    assert not due
    return out


def _store_chunk(dst, src, i):
    rows = src.shape[0]
    r = pl.multiple_of(i * rows, 16)
    dst[pl.ds(r, rows), :] = src[...].astype(BF16)


def _bias_table(bidx, rb_ref, head):
    tab = jnp.full(bidx.shape, NEG, F32)
    for b in range(rb_ref.shape[0]):
        tab = jnp.where(bidx == b, rb_ref[b, head], tab)
    return tab


def _ln_swish(y, g, b):
    mu = jnp.mean(y, axis=-1, keepdims=True)
    yc = y - mu
    var = jnp.mean(yc * yc, axis=-1, keepdims=True)
    z = yc * lax.rsqrt(var + EPS) * g + b
    return z * jax.nn.sigmoid(z)


def _ffn_inproj_body(xc_ref, xn_ref, meta_ref, xsm_ref, g1_ref, gm_ref, gq_ref, gk_ref, w1c, w3c, w2c, winc,
                     h_ref, q_ref, kv_ref, glu_ref, ga_ref, gc_ref,
                     w1b, w3b, w2b, winb, hid_ref, xn_st, u_st, *, n_main, splits):
    i = pl.program_id(0)

    def small_tile():
        pad = xc_ref.shape[0] - meta_ref.shape[0] - xsm_ref.shape[0]
        return jnp.concatenate([meta_ref[...], xsm_ref[...], jnp.zeros((pad, xc_ref.shape[1]), F32)], axis=0)

    @pl.when(i < N_WSTEPS)
    def _():
        _store_chunk(w1b, w1c, i)
        _store_chunk(w3b, w3c, i)
        _store_chunk(w2b, w2c, i)
        _store_chunk(winb, winc, i)

    s = i - N_WSTEPS

    def project():
        u = u_st[...]
        o_q, o_k, o_v, o_a, o_b, o_ga, o_gc, o_end = splits
        lo_mask = lax.broadcasted_iota(jnp.int32, (1, LANES), 1) < HEAD_DIM
        gq2 = jnp.concatenate([gq_ref[...], gq_ref[...]], axis=1)
        gk2 = jnp.concatenate([gk_ref[...], gk_ref[...]], axis=1)
        zq = _dot(u, winb[:, o_q:o_k])
        for p in range((o_k - o_q) // LANES):
            sl = slice(p * LANES, (p + 1) * LANES)
            q_ref[:, sl] = (_pair_rms(zq[:, sl], gq2, lo_mask) * (HEAD_DIM ** -0.5)).astype(BF16)
        zkv = _dot(u, winb[:, o_k:o_a])
        kv_ref[:, :LANES] = _pair_rms(zkv[:, :LANES], gk2, lo_mask)
        kv_ref[:, LANES:] = zkv[:, LANES:]
        za = _dot(u, winb[:, o_a:o_b])
        zb = _dot(u, winb[:, o_b:o_ga])
        glu_ref[...] = za * jax.nn.sigmoid(zb)
        ga_ref[...] = jax.nn.sigmoid(_dot(u, winb[:, o_ga:o_gc]))
        gc_ref[...] = jax.nn.sigmoid(_dot(u, winb[:, o_gc:o_end]))

    def normalise_next():
        x_next = jnp.where(s + 1 < n_main, xn_ref[...], small_tile())
        xn_st[...] = _rms(x_next, g1_ref[...]).astype(BF16)

    @pl.when(i == 0)
    def _():
        u_st[...] = jnp.zeros_like(u_st)

    @pl.when(s == -1)
    def _():
        normalise_next()

    @pl.when((s >= 0) & (s <= n_main))
    def _():
        project()
        x = jnp.where(s < n_main, xc_ref[...], small_tile())
        h = x + 0.5 * _swiglu(xn_st[...], w1b, w3b, w2b, hid_ref)
        h_ref[...] = h
        u_st[...] = _rms(h, gm_ref[...]).astype(BF16)
        normalise_next()

    @pl.when(s == n_main + 1)
    def _():
        project()


def _ffn_inproj(x_main, x_meta, x_sample, g1, gm, gq, gk, w1, w3, w2, w_in, splits):
    m_rows, d = x_main.shape
    assert x_meta.shape[0] % 8 == 0 and x_sample.shape[0] % 8 == 0
    assert x_meta.shape[0] + x_sample.shape[0] <= ROW_TILE
    n_ff = w1.shape[1]
    n_main = m_rows // ROW_TILE
    r_rows = m_rows + ROW_TILE
    t = ROW_TILE
    o_q, o_k, o_v, o_a, o_b, o_ga, o_gc, o_end = splits
    assert o_v - o_k == LANES and o_a - o_v == LANES and (o_k - o_q) % LANES == 0

    def cur_idx(i):
        return (jnp.clip(i - N_WSTEPS, 0, n_main - 1), 0)

    def next_idx(i):
        return (jnp.clip(i - N_WSTEPS + 1, 0, n_main - 1), 0)

    def h_idx(i):
        return (jnp.clip(i - N_WSTEPS, 0, n_main), 0)

    def proj_idx(i):
        return (jnp.clip(i - N_WSTEPS - 1, 0, n_main), 0)

    def w_idx(i):
        return (jnp.minimum(i, N_WSTEPS - 1), 0)

    const = lambda i: (0, 0)
    outs = [
        (jax.ShapeDtypeStruct((r_rows, d), F32), h_idx),
        (jax.ShapeDtypeStruct((r_rows, o_k - o_q), BF16), proj_idx),
        (jax.ShapeDtypeStruct((r_rows, o_a - o_k), F32), proj_idx),
        (jax.ShapeDtypeStruct((r_rows, o_b - o_a), F32), proj_idx),
        (jax.ShapeDtypeStruct((r_rows, o_gc - o_ga), F32), proj_idx),
        (jax.ShapeDtypeStruct((r_rows, o_end - o_gc), F32), proj_idx),
    ]
    return pl.pallas_call(
        functools.partial(_ffn_inproj_body, n_main=n_main, splits=splits),
        grid=(N_WSTEPS + n_main + 2,),
        in_specs=[
            pl.BlockSpec((t, d), cur_idx),
            pl.BlockSpec((t, d), next_idx),
            pl.BlockSpec(x_meta.shape, const),
            pl.BlockSpec(x_sample.shape, const),
            pl.BlockSpec((1, d), const),
            pl.BlockSpec((1, d), const),
            pl.BlockSpec((1, HEAD_DIM), const),
            pl.BlockSpec((1, HEAD_DIM), const),
            pl.BlockSpec((d // N_WSTEPS, n_ff), w_idx),
            pl.BlockSpec((d // N_WSTEPS, n_ff), w_idx),
            pl.BlockSpec((n_ff // N_WSTEPS, d), w_idx),
            pl.BlockSpec((d // N_WSTEPS, o_end), w_idx),
        ],
        out_specs=[pl.BlockSpec((t, s.shape[1]), idx) for s, idx in outs],
        out_shape=[s for s, _ in outs],
        scratch_shapes=[
            pltpu.VMEM((d, n_ff), BF16),
            pltpu.VMEM((d, n_ff), BF16),
            pltpu.VMEM((n_ff, d), BF16),
            pltpu.VMEM((d, o_end), BF16),
            pltpu.VMEM((t, n_ff), BF16),
            pltpu.VMEM((t, d), BF16),
            pltpu.VMEM((t, d), BF16),
        ],
        compiler_params=pltpu.CompilerParams(
            dimension_semantics=("arbitrary",), vmem_limit_bytes=VMEM_LIMIT),
        name="ffn1_inproj",
    )(x_main, x_main, x_meta, x_sample, g1, gm, gq, gk, w1, w3, w2, w_in)


def _prompt_attn_body(rb_ref, sink_ref, q_ref, kvc_ref, kvp_ref, kvs_ref, bidx_ref, *rest,
                      tiles_per_seq, n_meta, n_heads, n_weights):
    w_refs, attn_ref, wb_refs = rest[:n_weights], rest[n_weights], rest[n_weights + 1:2 * n_weights + 1]
    kt_ref, vt_ref, bias_sc, kvm_ref = rest[2 * n_weights + 1:]
    for w_ref, wb_ref in zip(w_refs, wb_refs, strict=True):
        wb_ref[...] = w_ref[...].astype(BF16)

    i = pl.program_id(0)
    t = q_ref.shape[0]
    n_kv = 2
    grp = n_heads // n_kv

    @pl.when(i == 0)
    def _():
        kvm_ref[...] = jnp.zeros_like(kvm_ref)
        kvm_ref[WINDOW - n_meta:WINDOW, :] = kvs_ref[0:n_meta, :]
        col = lax.broadcasted_iota(jnp.int32, (WINDOW, 2 * WINDOW), 1)
        for hd in range(n_heads):
            tab = _bias_table(bidx_ref[...], rb_ref, hd)
            bias_sc[0, hd] = tab
            bias_sc[1, hd] = jnp.where(col < WINDOW - n_meta, NEG, tab)

    first = (i % tiles_per_seq) == 0
    lead = jnp.where(first, 1, 0)
    lo_mask = lax.broadcasted_iota(jnp.int32, (1, LANES), 1) < HEAD_DIM
    kv_prev = jnp.where(first, kvm_ref[...], kvp_ref[...])
    kv = jnp.concatenate([kv_prev, kvc_ref[...]], axis=0)
    k, v = kv[:, :LANES], kv[:, LANES:]
    kb, vb = k.astype(BF16), v.astype(BF16)
    kr = pltpu.roll(k, HEAD_DIM, axis=1).astype(BF16)
    vr = pltpu.roll(v, HEAD_DIM, axis=1).astype(BF16)
    zero = jnp.zeros((), BF16)
    k_lo = [jnp.where(lo_mask, kb, zero), jnp.where(lo_mask, kr, zero)]
    k_hi = [jnp.where(lo_mask, zero, kr), jnp.where(lo_mask, zero, kb)]
    v_lo = [jnp.where(lo_mask, vb, zero), jnp.where(lo_mask, vr, zero)]
    v_hi = [jnp.where(lo_mask, zero, vr), jnp.where(lo_mask, zero, vb)]

    for qb in range(t // WINDOW):
        rows = slice(qb * WINDOW, (qb + 1) * WINDOW)
        keys = slice(qb * WINDOW, qb * WINDOW + 2 * WINDOW)
        tab = lead if qb == 0 else 0
        for h in range(n_kv):
            k_st = jnp.concatenate([k_lo[h][keys], k_hi[h][keys]], axis=0)
            v_st = jnp.concatenate([v_lo[h][keys], v_hi[h][keys]], axis=0)
            c0 = h * grp * HEAD_DIM
            qq = jnp.concatenate([q_ref[rows, c0:c0 + LANES], q_ref[rows, c0 + LANES:c0 + 2 * LANES]], axis=0)
            s = _dot_t(qq, k_st)
            p_parts, inv_parts = [], []
            for g2 in range(2):
                p_row, inv_row = [], []
                for par in range(2):
                    hd = h * grp + 2 * g2 + par
                    sq = s[g2 * WINDOW:(g2 + 1) * WINDOW, par * 2 * WINDOW:(par + 1) * 2 * WINDOW] + bias_sc[tab, hd]
                    sink = sink_ref[0, hd]
                    m = jnp.maximum(jnp.max(sq, axis=-1, keepdims=True), sink)
                    p = jnp.exp(sq - m)
                    den = jnp.sum(p, axis=-1, keepdims=True) + jnp.exp(sink - m)
                    p_row.append(p.astype(BF16))
                    inv_row.append(1.0 / den)
                p_parts.append(jnp.concatenate(p_row, axis=1))
                inv_parts.append(jnp.where(lo_mask, inv_row[0], inv_row[1]))
            pm = jnp.concatenate(p_parts, axis=0)
            o = _dot(pm, v_st)
            for g2 in range(2):
                c = c0 + g2 * LANES
                attn_ref[rows, c:c + LANES] = (o[g2 * WINDOW:(g2 + 1) * WINDOW] * inv_parts[g2]).astype(attn_ref.dtype)

    @pl.when((i + 1) % tiles_per_seq == 0)
    def _():
        kt_ref[0] = kvc_ref[t - WINDOW:t, :LANES].T
        vt_ref[0] = kvc_ref[t - WINDOW:t, LANES:].T


def _prompt_attn(rel_bias, sinks2, q_all, kv_all, bidx, weights, m_rows, seq, n_meta):
    t = ATTN_TILE
    n_steps = m_rows // t
    assert m_rows % WINDOW == 0 and n_meta <= WINDOW
    n_heads = sinks2.shape[1]
    qw, kvw = q_all.shape[1], kv_all.shape[1]
    const = lambda i: (0, 0)
    row = lambda i: (i, 0)
    per_seq = lambda i: (i // (seq // t), 0, 0)
    smem = pl.BlockSpec(memory_space=pltpu.SMEM)

    def w_spec(w):
        share = 1 if (w.shape[0] // n_steps) % 16 == 0 else 2
        assert w.shape[0] % (n_steps // share) == 0 and (w.shape[0] * share // n_steps) % 16 == 0
        return pl.BlockSpec((w.shape[0] * share // n_steps, w.shape[1]), lambda i: (i // share, 0))

    w_specs = [w_spec(w) for w in weights]
    outs = pl.pallas_call(
        functools.partial(_prompt_attn_body, tiles_per_seq=seq // t, n_meta=n_meta, n_heads=n_heads,
                          n_weights=len(weights)),
        grid=(n_steps,),
        in_specs=[
            smem, smem,
            pl.BlockSpec((t, qw), row),
            pl.BlockSpec((t, kvw), row),
            pl.BlockSpec((WINDOW, kvw), lambda i: (jnp.maximum(i * (t // WINDOW) - 1, 0), 0)),
            pl.BlockSpec((WINDOW, kvw), lambda i: (m_rows // WINDOW, 0)),
            pl.BlockSpec((WINDOW, 2 * WINDOW), const),
        ] + w_specs,
        out_specs=[pl.BlockSpec((t, qw), row)] + w_specs + [pl.BlockSpec((1, WINDOW, LANES), per_seq)] * 2,
        out_shape=[jax.ShapeDtypeStruct((m_rows, qw), BF16)] + [jax.ShapeDtypeStruct(w.shape, BF16) for w in weights]
        + [jax.ShapeDtypeStruct((m_rows // seq, WINDOW, LANES), F32)] * 2,
        scratch_shapes=[
            pltpu.VMEM((2, n_heads, WINDOW, 2 * WINDOW), F32),
            pltpu.VMEM((WINDOW, kvw), F32),
        ],
        compiler_params=pltpu.CompilerParams(dimension_semantics=("arbitrary",)),
        name="prompt_attn",
    )(rel_bias, sinks2, q_all, kv_all, kv_all, kv_all, bidx, *weights)
    return outs[0], outs[1:1 + len(weights)], outs[1 + len(weights)], outs[2 + len(weights)]


def _sample_mix_body(rb_ref, sink_ref, q_ref, kvn_ref, glun_ref, ck_ref, cv_ref, st_ref, bidx_ref,
                     wdw_ref, bdw_ref, lng_ref, lnb_ref, after_ref,
                     as_ref, cs_ref, nk_ref, nv_ref, nst_ref, bias_sc, *, n_heads, n_meta):
    i = pl.program_id(0)
    sb, w_buf = ck_ref.shape[0], ck_ref.shape[2]
    n_st = st_ref.shape[0]
    n_rows = sb * n_heads
    per_kv = n_heads // (LANES // HEAD_DIM)

    @pl.when(i == 0)
    def _():
        as_ref[...] = jnp.zeros_like(as_ref)
        cs_ref[...] = jnp.zeros_like(cs_ref)
        tiled = jnp.concatenate(
            [jnp.broadcast_to(jnp.concatenate([_bias_table(bidx_ref[...], rb_ref, hd)] * sb, axis=1),
                              (sb, sb * w_buf)) for hd in range(n_heads)], axis=0)
        row_seq = lax.rem(lax.broadcasted_iota(jnp.int32, tiled.shape, 0), sb)
        col_seq = lax.broadcasted_iota(jnp.int32, tiled.shape, 1) // w_buf
        bias_sc[...] = jnp.where(row_seq == col_seq, tiled, NEG)

    hrow = lax.broadcasted_iota(jnp.int32, (n_rows, 1), 0) // sb
    sink = jnp.zeros((n_rows, 1), F32)
    bias_new = jnp.zeros((n_rows, 1), F32)
    for hd in range(n_heads):
        sink = jnp.where(hrow == hd, sink_ref[0, hd], sink)
        bias_new = jnp.where(hrow == hd, rb_ref[0, hd], bias_new)

    lo_mask = lax.broadcasted_iota(jnp.int32, (1, LANES), 1) < HEAD_DIM

    def half(x, src_hi, dst_hi):
        if src_hi != dst_hi:
            x = pltpu.roll(x, HEAD_DIM, 1)
        return jnp.where(lo_mask, 0.0, x) if dst_hi else jnp.where(lo_mask, x, 0.0)

    q = jnp.concatenate(
        [half(q_ref[:, (hd // 2) * LANES:(hd // 2 + 1) * LANES].astype(F32), hd % 2 == 1, hd // per_kv == 1)
         for hd in range(n_heads)], axis=0).astype(BF16)
    kn_rows = jnp.concatenate([kvn_ref[:, :LANES]] * n_heads, axis=0).astype(BF16)
    vn_rows = jnp.concatenate([kvn_ref[:, LANES:]] * n_heads, axis=0).astype(BF16)
    ck_all = jnp.concatenate([ck_ref[bb] for bb in range(sb)], axis=1).astype(BF16)
    cv_all = jnp.concatenate([cv_ref[bb] for bb in range(sb)], axis=1).astype(BF16)
    s_c = _dot(q, ck_all) + bias_sc[...]
    s_n = jnp.sum(q.astype(F32) * kn_rows.astype(F32), axis=-1, keepdims=True) + bias_new
    m = jnp.maximum(jnp.maximum(jnp.max(s_c, axis=-1, keepdims=True), s_n), sink)
    p_c = jnp.exp(s_c - m)
    p_n = jnp.exp(s_n - m)
    den = jnp.sum(p_c, axis=-1, keepdims=True) + p_n + jnp.exp(sink - m)
    o = _dot_t(p_c.astype(BF16), cv_all) + p_n.astype(BF16).astype(F32) * vn_rows.astype(F32)
    o = o / den
    pairs = []
    for p in range(n_heads // 2):
        lo_head = o[(2 * p) * sb:(2 * p + 1) * sb]
        hi_head = o[(2 * p + 1) * sb:(2 * p + 2) * sb]
        pairs.append(half(lo_head, (2 * p) // per_kv == 1, False) + half(hi_head, (2 * p + 1) // per_kv == 1, True))
    rows = pl.ds(pl.multiple_of(n_meta + i * sb, sb), sb)
    as_ref[rows, :] = jnp.concatenate(pairs, axis=1).astype(as_ref.dtype)

    g_new = glun_ref[...]
    y = g_new * wdw_ref[n_st:n_st + 1, :] + bdw_ref[...]
    for k in range(n_st):
        y = y + st_ref[k] * wdw_ref[k:k + 1, :]
    cs_ref[rows, :] = _ln_swish(y, lng_ref[...], lnb_ref[...]).astype(cs_ref.dtype)

    for bb in range(sb):
        nk_ref[bb, 0:w_buf - 1, :] = ck_ref[bb].T[1:w_buf, :]
        nk_ref[bb, w_buf - 1:w_buf, :] = kvn_ref[bb:bb + 1, :LANES]
        nv_ref[bb, 0:w_buf - 1, :] = cv_ref[bb].T[1:w_buf, :]
        nv_ref[bb, w_buf - 1:w_buf, :] = kvn_ref[bb:bb + 1, LANES:]
        nst_ref[bb, 0:n_st - 1, :] = st_ref[1:n_st, bb, :]
        nst_ref[bb, n_st - 1:n_st, :] = glun_ref[bb:bb + 1, :]


def _sample_mix(rel_bias, sinks2, q_all, kv_all, glu_all, cache_kt, cache_vt, state_t, bidx_s, w_dw, b_dw, ln_g, ln_b,
                s0, n_meta, t_rows, after):
    db, kw, w_buf = cache_kt.shape
    n_st, ch = state_t.shape[0], state_t.shape[2]
    n_heads = sinks2.shape[1]
    aw = q_all.shape[1]
    sb = SAMPLE_BLOCK
    assert s0 % sb == 0 and n_meta % sb == 0 and n_meta + db <= t_rows
    blk = lambda i: (i, 0, 0)
    const = lambda i: (0, 0)
    tok = lambda i: (s0 // sb + i, 0)
    smem = pl.BlockSpec(memory_space=pltpu.SMEM)
    return pl.pallas_call(
        functools.partial(_sample_mix_body, n_heads=n_heads, n_meta=n_meta),
        grid=(db // sb,),
        in_specs=[
            smem, smem,
            pl.BlockSpec((sb, aw), tok),
            pl.BlockSpec((sb, 2 * kw), tok),
            pl.BlockSpec((sb, ch), tok),
            pl.BlockSpec((sb, kw, w_buf), blk),
            pl.BlockSpec((sb, kw, w_buf), blk),
            pl.BlockSpec((n_st, sb, ch), lambda i: (0, i, 0)),
            pl.BlockSpec((1, w_buf), const),
            pl.BlockSpec(w_dw.shape, const),
            pl.BlockSpec((1, ch), const),
            pl.BlockSpec((1, ch), const),
            pl.BlockSpec((1, ch), const),
            pl.BlockSpec(memory_space=pl.ANY),
        ],
        out_specs=[
            pl.BlockSpec((t_rows, aw), const),
            pl.BlockSpec((t_rows, ch), const),
            pl.BlockSpec((sb, w_buf, kw), blk),
            pl.BlockSpec((sb, w_buf, kw), blk),
            pl.BlockSpec((sb, n_st, ch), blk),
        ],
        out_shape=[
            jax.ShapeDtypeStruct((t_rows, aw), BF16),
            jax.ShapeDtypeStruct((t_rows, ch), BF16),
            jax.ShapeDtypeStruct((db, w_buf, kw), F32),
            jax.ShapeDtypeStruct((db, w_buf, kw), F32),
            jax.ShapeDtypeStruct((db, n_st, ch), F32),
        ],
        scratch_shapes=[pltpu.VMEM((sb * n_heads, sb * w_buf), F32)],
        compiler_params=pltpu.CompilerParams(dimension_semantics=("arbitrary",)),
        name="sample_mix",
    )(rel_bias, sinks2, q_all, kv_all, glu_all, cache_kt, cache_vt, state_t, bidx_s, w_dw, b_dw, ln_g, ln_b, after)


def _conv_runs(cb_st, glu_ref, glum_ref, y_sc, wdw_ref, bdw_ref, first, n_runs):
    t = y_sc.shape[0]
    conv_w = wdw_ref.shape[0]
    n_ch = y_sc.shape[1]
    off = HALO - (conv_w - 1)
    n_a = (off + conv_w - 1) // 8 + 1
    units = [(lc, g) for lc in range(n_ch // LANES) for g in range(t // 8)]
    sizes = [len(units) // n_runs + (1 if r < len(units) % n_runs else 0) for r in range(n_runs)]
    staged = []

    def run(mine):
        if not staged:
            cb_st[0:HALO, :] = jnp.where(first, glum_ref[...], cb_st[0:HALO, :])
            cb_st[HALO:HALO + t, :] = glu_ref[...]
            staged.append(True)
        zs = {}

        def z(lc, s, g):
            if (lc, s, g) not in zs:
                ls = slice(lc * LANES, (lc + 1) * LANES)
                acc = None
                for a in range(n_a):
                    w = 8 * a + s - off
                    if 0 <= w < conv_w:
                        term = cb_st[8 * (g + a):8 * (g + a) + 8, ls] * wdw_ref[w:w + 1, ls]
                        acc = term if acc is None else acc + term
                zs[(lc, s, g)] = acc
            return zs[(lc, s, g)]

        dep = None
        for lc, g in mine:
            ls = slice(lc * LANES, (lc + 1) * LANES)
            acc = jnp.broadcast_to(bdw_ref[:, ls], (8, LANES)) + z(lc, 0, g)
            if dep is not None:
                acc = acc + dep
            for s in range(1, 8):
                acc = acc + jnp.concatenate([z(lc, s, g), z(lc, s, g + 1)], axis=0)[s:s + 8, :]
            y_sc[8 * g:8 * g + 8, ls] = acc
            dep = _dep_zero(acc)
        return dep[0:1, :]

    out, k = [], 0
    for n in sizes:
        out.append(functools.partial(run, units[k:k + n]))
        k += n
    return out


def _post_ffn_body(h_ref, am_ref, as_ref, cs_ref, ga_ref, gc_ref, g2_ref, glu_ref, glus_ref,
                   wdw_ref, bdw_ref, lng_ref, lnb_ref,
                   waob, wcob, woutb, w1b, w3b, w2b,
                   ym_ref, ys_ref, gt_ref,
                   hid_ref, cb_st, y_sc, conv_st, glum_ref, *, n_main, tiles_per_seq, n_meta):
    i = pl.program_id(0)
    t_rows = y_sc.shape[0]

    @pl.when(i == 0)
    def _():
        cb_st[...] = jnp.zeros_like(cb_st)
        glum_ref[...] = jnp.zeros_like(glum_ref)
        glum_ref[HALO - n_meta:HALO, :] = glus_ref[0:n_meta, :]

    t = i - 1
    tn = t + 1
    first = (tn < n_main) & (lax.rem(tn, tiles_per_seq) == 0)
    n_chunks = w1b.shape[1] // FF_CHUNK

    def finish_conv():
        conv_st[...] = _ln_swish(y_sc[...], lng_ref[...], lnb_ref[...]).astype(conv_st.dtype)
        cb_st[0:HALO, :] = cb_st[t_rows:t_rows + HALO, :]

    @pl.when(t == -1)
    def _():
        for f in _conv_runs(cb_st, glu_ref, glum_ref, y_sc, wdw_ref, bdw_ref, first, n_chunks):
            f()
        finish_conv()

    @pl.when(t >= 0)
    def _():
        on_main = t < n_main
        conv_f = _conv_runs(cb_st, glu_ref, glum_ref, y_sc, wdw_ref, bdw_ref, first, n_chunks)
        fillers = [(c, c + 1, f) for c, f in enumerate(conv_f)]

        at = jnp.where(on_main, am_ref[...], as_ref[...])
        cv = jnp.where(on_main, conv_st[...], cs_ref[...])
        a = _dot(at, waob[...])
        c = _dot(cv, wcob[...])
        mix = (ga_ref[...] * a + gc_ref[...] * c).astype(BF16)
        h2 = h_ref[...] + _dot(mix, woutb[...])
        xn = _rms(h2, g2_ref[...]).astype(BF16)
        y = h2 + 0.5 * _swiglu(xn, w1b, w3b, w2b, hid_ref, fillers)
        finish_conv()

        @pl.when(on_main)
        def _():
            ym_ref[...] = y

        @pl.when(t == n_main)
        def _():
            ys_ref[...] = y

    n_st = gt_ref.shape[0]
    for sq in range(gt_ref.shape[1]):
        @pl.when(tn == (sq + 1) * tiles_per_seq - 1)
        def _():
            for r in range(n_st):
                gt_ref[r, sq:sq + 1, :] = glu_ref[t_rows - n_st + r:t_rows - n_st + r + 1, :]


def _post_ffn(h_all, attn_main, attn_small, conv_small, ga_all, gc_all, g2, glu_all,
              w_dw, b_dw, ln_g, ln_b, w_ao, w_co, w_out, w1, w3, w2, seq, n_meta):
    m_rows, aw = attn_main.shape
    assert m_rows % HALO == 0 and n_meta <= HALO
    ch = glu_all.shape[1]
    d = h_all.shape[1]
    n_ff = w1.shape[1]
    t = ROW_TILE
    n_main = m_rows // t
    n_b = m_rows // seq
    n_st = w_dw.shape[0] - 1

    def main_idx(i):
        return (jnp.clip(i - 1, 0, n_main - 1), 0)

    def next_idx(i):
        return (jnp.minimum(i, n_main - 1), 0)

    def row_idx(i):
        return (jnp.maximum(i - 1, 0), 0)

    const = lambda i: (0, 0)
    resident = lambda w: pl.BlockSpec(w.shape, const, pipeline_mode=pl.Buffered(1))
    return pl.pallas_call(
        functools.partial(_post_ffn_body, n_main=n_main, tiles_per_seq=seq // t, n_meta=n_meta),
        grid=(n_main + 2,),
        in_specs=[
            pl.BlockSpec((t, d), row_idx),
            pl.BlockSpec((t, aw), main_idx),
            pl.BlockSpec((t, aw), const),
            pl.BlockSpec((t, ch), const),
            pl.BlockSpec((t, d), row_idx),
            pl.BlockSpec((t, d), row_idx),
            pl.BlockSpec((1, d), const),
            pl.BlockSpec((t, ch), next_idx),
            pl.BlockSpec((HALO, ch), lambda i: (m_rows // HALO, 0)),
            pl.BlockSpec(w_dw.shape, const),
            pl.BlockSpec((1, ch), const),
            pl.BlockSpec((1, ch), const),
            pl.BlockSpec((1, ch), const),
            resident(w_ao), resident(w_co), resident(w_out), resident(w1), resident(w3), resident(w2),
        ],
        out_specs=[pl.BlockSpec((t, d), main_idx), pl.BlockSpec((t, d), const),
                   pl.BlockSpec((n_st, n_b, ch), lambda i: (0, 0, 0))],
        out_shape=[jax.ShapeDtypeStruct((m_rows, d), F32), jax.ShapeDtypeStruct((t, d), F32),
                   jax.ShapeDtypeStruct((n_st, n_b, ch), F32)],
        scratch_shapes=[
            pltpu.VMEM((t, n_ff), BF16),
            pltpu.VMEM((HALO + t, ch), F32),
            pltpu.VMEM((t, ch), F32),
            pltpu.VMEM((t, ch), BF16),
            pltpu.VMEM((HALO, ch), F32),
        ],
        compiler_params=pltpu.CompilerParams(
            dimension_semantics=("arbitrary",), vmem_limit_bytes=VMEM_LIMIT),
        name="post_ffn2",
    )(h_all, attn_main, attn_small, conv_small, ga_all, gc_all, g2, glu_all, glu_all,
      w_dw, b_dw, ln_g, ln_b, w_ao, w_co, w_out, w1, w3, w2)


def _t5_bucket(dist, n_buckets):
    max_exact = n_buckets // 2
    d = np.maximum(dist, 0)
    ratio = (np.log(np.maximum(d, 1).astype(np.float32) / np.float32(max_exact))
             / np.float32(math.log(REL_MAX_DIST / max_exact)))
    large = np.minimum(max_exact + (ratio * np.float32(n_buckets - max_exact)).astype(np.int32), n_buckets - 1)
    return np.where(d < max_exact, d, large).astype(np.int32)


def _bucket_or_masked(dist, n_buckets):
    ok = (dist >= 0) & (dist < WINDOW)
    return np.where(ok, _t5_bucket(dist, n_buckets), -1).astype(np.int32)


def _rows(x, start, n):
    return lax.slice_in_dim(x, start, start + n, axis=0)


def kernel(x_prompt, x_sample, cache_k, cache_v, state_conv, meta_tokens, ffn1_norm, ffn1_w1, ffn1_w3, ffn1_w2, mix_norm, w_in, q_norm, k_norm, rel_bias, sinks, w_attn_out, w_dw, b_dw, conv_ln_g, conv_ln_b, w_conv_out, w_out, ffn2_norm, ffn2_w1, ffn2_w3, ffn2_w2):
    n_b, seq, d = x_prompt.shape
    db = x_sample.shape[0]
    n_meta = meta_tokens.shape[0]
    n_heads = sinks.shape[0]
    w_buf, n_kv, hd = cache_k.shape[1], cache_k.shape[2], cache_k.shape[3]
    ch = w_dw.shape[1]
    n_st = state_conv.shape[1]
    n_buckets = rel_bias.shape[0]
    aw, kvw = n_heads * hd, n_kv * hd
    t = ROW_TILE
    m_rows = n_b * seq
    assert hd == HEAD_DIM and kvw == LANES and n_kv == 2 and n_heads == 8 and w_buf == WINDOW
    assert x_sample.shape[1] == 1 and seq % t == 0 and seq % ATTN_TILE == 0
    assert n_meta + db <= t and db % SAMPLE_BLOCK == 0
    assert n_meta <= HALO and n_meta <= WINDOW and n_st == w_dw.shape[0] - 1 and n_st <= HALO
    splits = tuple(int(v) for v in np.cumsum([0, aw, kvw, kvw, ch, ch, d, d]))
    assert splits[-1] == w_in.shape[1]

    row = lambda v: v.reshape(1, -1)
    x_main = x_prompt.reshape(m_rows, d)

    h_all, q_all, kv_all, glu_all, ga_all, gc_all = _ffn_inproj(
        x_main, meta_tokens, x_sample.reshape(db, d), row(ffn1_norm), row(mix_norm), row(q_norm), row(k_norm),
        ffn1_w1, ffn1_w3, ffn1_w2, w_in, splits)

    sinks2 = row(sinks)
    b_dw2, ln_g2, ln_b2 = row(b_dw), row(conv_ln_g), row(conv_ln_b)

    dist = np.arange(WINDOW)[:, None] + WINDOW - np.arange(2 * WINDOW)[None, :]
    bidx = jnp.asarray(_bucket_or_masked(dist, n_buckets))
    attn_main, w_bf16, k_t, v_t = _prompt_attn(
        rel_bias, sinks2, q_all, kv_all, bidx,
        (w_attn_out, w_conv_out, w_out, ffn2_w1, ffn2_w3, ffn2_w2), m_rows, seq, n_meta)

    bidx_s = jnp.asarray(_bucket_or_masked(w_buf - np.arange(w_buf)[None, :], n_buckets))
    attn_small, conv_small, new_k_s, new_v_s, new_conv_s = _sample_mix(
        rel_bias, sinks2, q_all, kv_all, glu_all,
        jnp.transpose(cache_k, (0, 2, 3, 1)).reshape(db, kvw, w_buf),
        jnp.transpose(cache_v, (0, 2, 3, 1)).reshape(db, kvw, w_buf),
        jnp.transpose(state_conv, (1, 0, 2)), bidx_s, w_dw, b_dw2, ln_g2, ln_b2, m_rows + n_meta, n_meta, t,
        after=k_t)

    y_main, y_small, glu_t = _post_ffn(h_all, attn_main, attn_small, conv_small, ga_all, gc_all, row(ffn2_norm),
                                       glu_all, w_dw, b_dw2, ln_g2, ln_b2, *w_bf16, seq, n_meta)

    return (
        y_main.reshape(n_b, seq, d),
        _rows(y_small, n_meta, db).reshape(db, 1, d),
        jnp.transpose(k_t.reshape(n_b, n_kv, hd, WINDOW), (0, 3, 1, 2)),
        jnp.transpose(v_t.reshape(n_b, n_kv, hd, WINDOW), (0, 3, 1, 2)),
        jnp.transpose(glu_t, (1, 0, 2)),
        new_k_s.reshape(db, w_buf, n_kv, hd),
        new_v_s.reshape(db, w_buf, n_kv, hd),
        new_conv_s,
    )
```

```python
import functools
import math

import jax
import jax.numpy as jnp
import numpy as np
from jax import lax
from jax.experimental import pallas as pl
from jax.experimental.pallas import tpu as pltpu

F32 = jnp.float32
BF16 = jnp.bfloat16

EPS = 1e-6
NEG = -1e30
WINDOW = 128
REL_MAX_DIST = 128
HEAD_DIM = 64
LANES = 128
ROW_TILE = 256
ATTN_TILE = 1024
FF_CHUNK = 256
N_WSTEPS = 8
HALO = 32
SAMPLE_BLOCK = 16
VMEM_LIMIT = 56 * 1024 * 1024


def _dot(a, b):
    return jnp.dot(a, b, preferred_element_type=F32)


def _dot_t(a, b):
    return lax.dot_general(a, b, (((1,), (1,)), ((), ())), preferred_element_type=F32)


def _rms(x, g):
    return x * lax.rsqrt(jnp.mean(x * x, axis=-1, keepdims=True) + EPS) * g


def _pair_rms(x, g2, lo_mask):
    sq = x * x
    lo = jnp.sum(jnp.where(lo_mask, sq, 0.0), axis=-1, keepdims=True) * (1.0 / HEAD_DIM)
    hi = jnp.sum(jnp.where(lo_mask, 0.0, sq), axis=-1, keepdims=True) * (1.0 / HEAD_DIM)
    r = jnp.where(lo_mask, lax.rsqrt(lo + EPS), lax.rsqrt(hi + EPS))
    return x * r * g2


def _dep_zero(row):
    u = pltpu.bitcast(row, jnp.uint32)
    return pltpu.bitcast((u >> 16) >> 16, F32)


def _swiglu(xn, w1b, w3b, w2b, hid_ref, fillers=()):
    n_ff = w1b.shape[1]
    n_chunks = n_ff // FF_CHUNK
    due = {}
    for c in range(n_chunks):
        sl = slice(c * FF_CHUNK, (c + 1) * FF_CHUNK)
        a = _dot(xn, w1b[:, sl])
        b = _dot(xn, w3b[:, sl])
        for z in due.pop(c, ()):
            b = b + jnp.concatenate([z] * (FF_CHUNK // LANES), axis=1)
        for issue, when, thunk in fillers:
            if issue == c:
                due.setdefault(min(when, n_chunks), []).append(_dep_zero(thunk()))
        hid_ref[:, sl] = (a * jax.nn.sigmoid(a) * b).astype(BF16)
    out = _dot(hid_ref[...], w2b[...])
    for z in due.pop(n_chunks, ()):
        out = out + jnp.concatenate([z] * (out.shape[1] // LANES), axis=1)
    assert not due
    return out


def _store_chunk(dst, src, i):
    rows = src.shape[0]
    r = pl.multiple_of(i * rows, 16)
    dst[pl.ds(r, rows), :] = src[...].astype(BF16)


def _bias_table(bidx, rb_ref, head):
    tab = jnp.full(bidx.shape, NEG, F32)
    for b in range(rb_ref.shape[0]):
        tab = jnp.where(bidx == b, rb_ref[b, head], tab)
    return tab


def _ln_swish(y, g, b):
    mu = jnp.mean(y, axis=-1, keepdims=True)
    yc = y - mu
    var = jnp.mean(yc * yc, axis=-1, keepdims=True)
    z = yc * lax.rsqrt(var + EPS) * g + b
    return z * jax.nn.sigmoid(z)


def _ffn_inproj_body(xc_ref, xn_ref, meta_ref, xsm_ref, g1_ref, gm_ref, gq_ref, gk_ref, w1c, w3c, w2c, winc,
                     h_ref, q_ref, kv_ref, glu_ref, ga_ref, gc_ref,
                     w1b, w3b, w2b, winb, hid_ref, xn_st, u_st, *, n_main, splits):
    i = pl.program_id(0)

    def small_tile():
        pad = xc_ref.shape[0] - meta_ref.shape[0] - xsm_ref.shape[0]
        return jnp.concatenate([meta_ref[...], xsm_ref[...], jnp.zeros((pad, xc_ref.shape[1]), F32)], axis=0)

    @pl.when(i < N_WSTEPS)
    def _():
        _store_chunk(w1b, w1c, i)
        _store_chunk(w3b, w3c, i)
        _store_chunk(w2b, w2c, i)
        _store_chunk(winb, winc, i)

    s = i - N_WSTEPS

    def project():
        u = u_st[...]
        o_q, o_k, o_v, o_a, o_b, o_ga, o_gc, o_end = splits
        lo_mask = lax.broadcasted_iota(jnp.int32, (1, LANES), 1) < HEAD_DIM
        gq2 = jnp.concatenate([gq_ref[...], gq_ref[...]], axis=1)
        gk2 = jnp.concatenate([gk_ref[...], gk_ref[...]], axis=1)
        zq = _dot(u, winb[:, o_q:o_k])
        for p in range((o_k - o_q) // LANES):
            sl = slice(p * LANES, (p + 1) * LANES)
            q_ref[:, sl] = (_pair_rms(zq[:, sl], gq2, lo_mask) * (HEAD_DIM ** -0.5)).astype(BF16)
        zkv = _dot(u, winb[:, o_k:o_a])
        kv_ref[:, :LANES] = _pair_rms(zkv[:, :LANES], gk2, lo_mask)
        kv_ref[:, LANES:] = zkv[:, LANES:]
        za = _dot(u, winb[:, o_a:o_b])
        zb = _dot(u, winb[:, o_b:o_ga])
        glu_ref[...] = za * jax.nn.sigmoid(zb)
        ga_ref[...] = jax.nn.sigmoid(_dot(u, winb[:, o_ga:o_gc]))
        gc_ref[...] = jax.nn.sigmoid(_dot(u, winb[:, o_gc:o_end]))

    def normalise_next():
        x_next = jnp.where(s + 1 < n_main, xn_ref[...], small_tile())
        xn_st[...] = _rms(x_next, g1_ref[...]).astype(BF16)

    @pl.when(i == 0)
    def _():
        u_st[...] = jnp.zeros_like(u_st)

    @pl.when(s == -1)
    def _():
        normalise_next()

    @pl.when((s >= 0) & (s <= n_main))
    def _():
        project()
        x = jnp.where(s < n_main, xc_ref[...], small_tile())
        h = x + 0.5 * _swiglu(xn_st[...], w1b, w3b, w2b, hid_ref)
        h_ref[...] = h
        u_st[...] = _rms(h, gm_ref[...]).astype(BF16)
        normalise_next()

    @pl.when(s == n_main + 1)
    def _():
        project()


def _ffn_inproj(x_main, x_meta, x_sample, g1, gm, gq, gk, w1, w3, w2, w_in, splits):
    m_rows, d = x_main.shape
    assert x_meta.shape[0] % 8 == 0 and x_sample.shape[0] % 8 == 0
    assert x_meta.shape[0] + x_sample.shape[0] <= ROW_TILE
    n_ff = w1.shape[1]
    n_main = m_rows // ROW_TILE
    r_rows = m_rows + ROW_TILE
    t = ROW_TILE
    o_q, o_k, o_v, o_a, o_b, o_ga, o_gc, o_end = splits
    assert o_v - o_k == LANES and o_a - o_v == LANES and (o_k - o_q) % LANES == 0

    def cur_idx(i):
        return (jnp.clip(i - N_WSTEPS, 0, n_main - 1), 0)

    def next_idx(i):
        return (jnp.clip(i - N_WSTEPS + 1, 0, n_main - 1), 0)

    def h_idx(i):
        return (jnp.clip(i - N_WSTEPS, 0, n_main), 0)

    def proj_idx(i):
        return (jnp.clip(i - N_WSTEPS - 1, 0, n_main), 0)

    def w_idx(i):
        return (jnp.minimum(i, N_WSTEPS - 1), 0)

    const = lambda i: (0, 0)
    outs = [
        (jax.ShapeDtypeStruct((r_rows, d), F32), h_idx),
        (jax.ShapeDtypeStruct((r_rows, o_k - o_q), BF16), proj_idx),
        (jax.ShapeDtypeStruct((r_rows, o_a - o_k), F32), proj_idx),
        (jax.ShapeDtypeStruct((r_rows, o_b - o_a), F32), proj_idx),
        (jax.ShapeDtypeStruct((r_rows, o_gc - o_ga), F32), proj_idx),
        (jax.ShapeDtypeStruct((r_rows, o_end - o_gc), F32), proj_idx),
    ]
    return pl.pallas_call(
        functools.partial(_ffn_inproj_body, n_main=n_main, splits=splits),
        grid=(N_WSTEPS + n_main + 2,),
        in_specs=[
            pl.BlockSpec((t, d), cur_idx),
            pl.BlockSpec((t, d), next_idx),
            pl.BlockSpec(x_meta.shape, const),
            pl.BlockSpec(x_sample.shape, const),
            pl.BlockSpec((1, d), const),
            pl.BlockSpec((1, d), const),
            pl.BlockSpec((1, HEAD_DIM), const),
            pl.BlockSpec((1, HEAD_DIM), const),
            pl.BlockSpec((d // N_WSTEPS, n_ff), w_idx),
            pl.BlockSpec((d // N_WSTEPS, n_ff), w_idx),
            pl.BlockSpec((n_ff // N_WSTEPS, d), w_idx),
            pl.BlockSpec((d // N_WSTEPS, o_end), w_idx),
        ],
        out_specs=[pl.BlockSpec((t, s.shape[1]), idx) for s, idx in outs],
        out_shape=[s for s, _ in outs],
        scratch_shapes=[
            pltpu.VMEM((d, n_ff), BF16),
            pltpu.VMEM((d, n_ff), BF16),
            pltpu.VMEM((n_ff, d), BF16),
            pltpu.VMEM((d, o_end), BF16),
            pltpu.VMEM((t, n_ff), BF16),
            pltpu.VMEM((t, d), BF16),
            pltpu.VMEM((t, d), BF16),
        ],
        compiler_params=pltpu.CompilerParams(
            dimension_semantics=("arbitrary",), vmem_limit_bytes=VMEM_LIMIT),
        name="ffn1_inproj",
    )(x_main, x_main, x_meta, x_sample, g1, gm, gq, gk, w1, w3, w2, w_in)


def _prompt_attn_body(rb_ref, sink_ref, q_ref, kvc_ref, kvp_ref, kvs_ref, bidx_ref, *rest,
                      tiles_per_seq, n_meta, n_heads, n_weights):
    w_refs, attn_ref, wb_refs = rest[:n_weights], rest[n_weights], rest[n_weights + 1:2 * n_weights + 1]
    kt_ref, vt_ref, bias_sc, kvm_ref = rest[2 * n_weights + 1:]
    for w_ref, wb_ref in zip(w_refs, wb_refs, strict=True):
        wb_ref[...] = w_ref[...].astype(BF16)

    i = pl.program_id(0)
    t = q_ref.shape[0]
    n_kv = 2
    grp = n_heads // n_kv

    @pl.when(i == 0)
    def _():
        kvm_ref[...] = jnp.zeros_like(kvm_ref)
        kvm_ref[WINDOW - n_meta:WINDOW, :] = kvs_ref[0:n_meta, :]
        col = lax.broadcasted_iota(jnp.int32, (WINDOW, 2 * WINDOW), 1)
        for hd in range(n_heads):
            tab = _bias_table(bidx_ref[...], rb_ref, hd)
            bias_sc[0, hd] = tab
            bias_sc[1, hd] = jnp.where(col < WINDOW - n_meta, NEG, tab)

    first = (i % tiles_per_seq) == 0
    lead = jnp.where(first, 1, 0)
    lo_mask = lax.broadcasted_iota(jnp.int32, (1, LANES), 1) < HEAD_DIM
    kv_prev = jnp.where(first, kvm_ref[...], kvp_ref[...])
    kv = jnp.concatenate([kv_prev, kvc_ref[...]], axis=0)
    k, v = kv[:, :LANES], kv[:, LANES:]
    kb, vb = k.astype(BF16), v.astype(BF16)
    kr = pltpu.roll(k, HEAD_DIM, axis=1).astype(BF16)
    vr = pltpu.roll(v, HEAD_DIM, axis=1).astype(BF16)
    zero = jnp.zeros((), BF16)
    k_lo = [jnp.where(lo_mask, kb, zero), jnp.where(lo_mask, kr, zero)]
    k_hi = [jnp.where(lo_mask, zero, kr), jnp.where(lo_mask, zero, kb)]
    v_lo = [jnp.where(lo_mask, vb, zero), jnp.where(lo_mask, vr, zero)]
    v_hi = [jnp.where(lo_mask, zero, vr), jnp.where(lo_mask, zero, vb)]

    for qb in range(t // WINDOW):
        rows = slice(qb * WINDOW, (qb + 1) * WINDOW)
        keys = slice(qb * WINDOW, qb * WINDOW + 2 * WINDOW)
        tab = lead if qb == 0 else 0
        for h in range(n_kv):
            k_st = jnp.concatenate([k_lo[h][keys], k_hi[h][keys]], axis=0)
            v_st = jnp.concatenate([v_lo[h][keys], v_hi[h][keys]], axis=0)
            c0 = h * grp * HEAD_DIM
            qq = jnp.concatenate([q_ref[rows, c0:c0 + LANES], q_ref[rows, c0 + LANES:c0 + 2 * LANES]], axis=0)
            s = _dot_t(qq, k_st)
            p_parts, inv_parts = [], []
            for g2 in range(2):
                p_row, inv_row = [], []
                for par in range(2):
                    hd = h * grp + 2 * g2 + par
                    sq = s[g2 * WINDOW:(g2 + 1) * WINDOW, par * 2 * WINDOW:(par + 1) * 2 * WINDOW] + bias_sc[tab, hd]
                    sink = sink_ref[0, hd]
                    m = jnp.maximum(jnp.max(sq, axis=-1, keepdims=True), sink)
                    p = jnp.exp(sq - m)
                    den = jnp.sum(p, axis=-1, keepdims=True) + jnp.exp(sink - m)
                    p_row.append(p.astype(BF16))
                    inv_row.append(1.0 / den)
                p_parts.append(jnp.concatenate(p_row, axis=1))
                inv_parts.append(jnp.where(lo_mask, inv_row[0], inv_row[1]))
            pm = jnp.concatenate(p_parts, axis=0)
            o = _dot(pm, v_st)
            for g2 in range(2):
                c = c0 + g2 * LANES
                attn_ref[rows, c:c + LANES] = (o[g2 * WINDOW:(g2 + 1) * WINDOW] * inv_parts[g2]).astype(attn_ref.dtype)

    @pl.when((i + 1) % tiles_per_seq == 0)
    def _():
        kt_ref[0] = kvc_ref[t - WINDOW:t, :LANES].T
        vt_ref[0] = kvc_ref[t - WINDOW:t, LANES:].T


def _prompt_attn(rel_bias, sinks2, q_all, kv_all, bidx, weights, m_rows, seq, n_meta):
    t = ATTN_TILE
    n_steps = m_rows // t
    assert m_rows % WINDOW == 0 and n_meta <= WINDOW
    n_heads = sinks2.shape[1]
    qw, kvw = q_all.shape[1], kv_all.shape[1]
    const = lambda i: (0, 0)
    row = lambda i: (i, 0)
    per_seq = lambda i: (i // (seq // t), 0, 0)
    smem = pl.BlockSpec(memory_space=pltpu.SMEM)

    def w_spec(w):
        share = 1 if (w.shape[0] // n_steps) % 16 == 0 else 2
        assert w.shape[0] % (n_steps // share) == 0 and (w.shape[0] * share // n_steps) % 16 == 0
        return pl.BlockSpec((w.shape[0] * share // n_steps, w.shape[1]), lambda i: (i // share, 0))

    w_specs = [w_spec(w) for w in weights]
    outs = pl.pallas_call(
        functools.partial(_prompt_attn_body, tiles_per_seq=seq // t, n_meta=n_meta, n_heads=n_heads,
                          n_weights=len(weights)),
        grid=(n_steps,),
        in_specs=[
            smem, smem,
            pl.BlockSpec((t, qw), row),
            pl.BlockSpec((t, kvw), row),
            pl.BlockSpec((WINDOW, kvw), lambda i: (jnp.maximum(i * (t // WINDOW) - 1, 0), 0)),
            pl.BlockSpec((WINDOW, kvw), lambda i: (m_rows // WINDOW, 0)),
            pl.BlockSpec((WINDOW, 2 * WINDOW), const),
        ] + w_specs,
        out_specs=[pl.BlockSpec((t, qw), row)] + w_specs + [pl.BlockSpec((1, WINDOW, LANES), per_seq)] * 2,
        out_shape=[jax.ShapeDtypeStruct((m_rows, qw), BF16)] + [jax.ShapeDtypeStruct(w.shape, BF16) for w in weights]
        + [jax.ShapeDtypeStruct((m_rows // seq, WINDOW, LANES), F32)] * 2,
        scratch_shapes=[
            pltpu.VMEM((2, n_heads, WINDOW, 2 * WINDOW), F32),
            pltpu.VMEM((WINDOW, kvw), F32),
        ],
        compiler_params=pltpu.CompilerParams(dimension_semantics=("arbitrary",)),
        name="prompt_attn",
    )(rel_bias, sinks2, q_all, kv_all, kv_all, kv_all, bidx, *weights)
    return outs[0], outs[1:1 + len(weights)], outs[1 + len(weights)], outs[2 + len(weights)]


def _sample_mix_body(rb_ref, sink_ref, q_ref, kvn_ref, glun_ref, ck_ref, cv_ref, st_ref, bidx_ref,
                     wdw_ref, bdw_ref, lng_ref, lnb_ref, after_ref,
                     as_ref, cs_ref, nk_ref, nv_ref, nst_ref, bias_sc, *, n_heads, n_meta):
    i = pl.program_id(0)
    sb, w_buf = ck_ref.shape[0], ck_ref.shape[2]
    n_st = st_ref.shape[0]
    n_rows = sb * n_heads
    per_kv = n_heads // (LANES // HEAD_DIM)

    @pl.when(i == 0)
    def _():
        as_ref[...] = jnp.zeros_like(as_ref)
        cs_ref[...] = jnp.zeros_like(cs_ref)
        tiled = jnp.concatenate(
            [jnp.broadcast_to(jnp.concatenate([_bias_table(bidx_ref[...], rb_ref, hd)] * sb, axis=1),
                              (sb, sb * w_buf)) for hd in range(n_heads)], axis=0)
        row_seq = lax.rem(lax.broadcasted_iota(jnp.int32, tiled.shape, 0), sb)
        col_seq = lax.broadcasted_iota(jnp.int32, tiled.shape, 1) // w_buf
        bias_sc[...] = jnp.where(row_seq == col_seq, tiled, NEG)

    hrow = lax.broadcasted_iota(jnp.int32, (n_rows, 1), 0) // sb
    sink = jnp.zeros((n_rows, 1), F32)
    bias_new = jnp.zeros((n_rows, 1), F32)
    for hd in range(n_heads):
        sink = jnp.where(hrow == hd, sink_ref[0, hd], sink)
        bias_new = jnp.where(hrow == hd, rb_ref[0, hd], bias_new)

    lo_mask = lax.broadcasted_iota(jnp.int32, (1, LANES), 1) < HEAD_DIM

    def half(x, src_hi, dst_hi):
        if src_hi != dst_hi:
            x = pltpu.roll(x, HEAD_DIM, 1)
        return jnp.where(lo_mask, 0.0, x) if dst_hi else jnp.where(lo_mask, x, 0.0)

    q = jnp.concatenate(
        [half(q_ref[:, (hd // 2) * LANES:(hd // 2 + 1) * LANES].astype(F32), hd % 2 == 1, hd // per_kv == 1)
         for hd in range(n_heads)], axis=0).astype(BF16)
    kn_rows = jnp.concatenate([kvn_ref[:, :LANES]] * n_heads, axis=0).astype(BF16)
    vn_rows = jnp.concatenate([kvn_ref[:, LANES:]] * n_heads, axis=0).astype(BF16)
    ck_all = jnp.concatenate([ck_ref[bb] for bb in range(sb)], axis=1).astype(BF16)
    cv_all = jnp.concatenate([cv_ref[bb] for bb in range(sb)], axis=1).astype(BF16)
    s_c = _dot(q, ck_all) + bias_sc[...]
    s_n = jnp.sum(q.astype(F32) * kn_rows.astype(F32), axis=-1, keepdims=True) + bias_new
    m = jnp.maximum(jnp.maximum(jnp.max(s_c, axis=-1, keepdims=True), s_n), sink)
    p_c = jnp.exp(s_c - m)
    p_n = jnp.exp(s_n - m)
    den = jnp.sum(p_c, axis=-1, keepdims=True) + p_n + jnp.exp(sink - m)
    o = _dot_t(p_c.astype(BF16), cv_all) + p_n.astype(BF16).astype(F32) * vn_rows.astype(F32)
    o = o / den
    pairs = []
    for p in range(n_heads // 2):
        lo_head = o[(2 * p) * sb:(2 * p + 1) * sb]
        hi_head = o[(2 * p + 1) * sb:(2 * p + 2) * sb]
        pairs.append(half(lo_head, (2 * p) // per_kv == 1, False) + half(hi_head, (2 * p + 1) // per_kv == 1, True))
    rows = pl.ds(pl.multiple_of(n_meta + i * sb, sb), sb)
    as_ref[rows, :] = jnp.concatenate(pairs, axis=1).astype(as_ref.dtype)

    g_new = glun_ref[...]
    y = g_new * wdw_ref[n_st:n_st + 1, :] + bdw_ref[...]
    for k in range(n_st):
        y = y + st_ref[k] * wdw_ref[k:k + 1, :]
    cs_ref[rows, :] = _ln_swish(y, lng_ref[...], lnb_ref[...]).astype(cs_ref.dtype)

    fill = jnp.zeros((w_buf - sb, LANES), F32)
    kn_t = jnp.concatenate([kvn_ref[:, :LANES], fill], axis=0).T
    vn_t = jnp.concatenate([kvn_ref[:, LANES:], fill], axis=0).T
    newest = lax.broadcasted_iota(jnp.int32, (LANES, w_buf), 1) == w_buf - 1
    for bb in range(sb):
        nk_ref[bb] = jnp.where(newest, kn_t[:, bb:bb + 1], pltpu.roll(ck_ref[bb], w_buf - 1, 1))
        nv_ref[bb] = jnp.where(newest, vn_t[:, bb:bb + 1], pltpu.roll(cv_ref[bb], w_buf - 1, 1))
    nst_ref[0:n_st - 1] = st_ref[1:n_st]
    nst_ref[n_st - 1] = g_new


def _sample_mix(rel_bias, sinks2, q_all, kv_all, glu_all, cache_kt, cache_vt, state_t, bidx_s, w_dw, b_dw, ln_g, ln_b,
                s0, n_meta, t_rows, after):
    db, kw, w_buf = cache_kt.shape
    n_st, ch = state_t.shape[0], state_t.shape[2]
    n_heads = sinks2.shape[1]
    aw = q_all.shape[1]
    sb = SAMPLE_BLOCK
    assert s0 % sb == 0 and n_meta % sb == 0 and n_meta + db <= t_rows
    blk = lambda i: (i, 0, 0)
    const = lambda i: (0, 0)
    tok = lambda i: (s0 // sb + i, 0)
    smem = pl.BlockSpec(memory_space=pltpu.SMEM)
    return pl.pallas_call(
        functools.partial(_sample_mix_body, n_heads=n_heads, n_meta=n_meta),
        grid=(db // sb,),
        in_specs=[
            smem, smem,
            pl.BlockSpec((sb, aw), tok),
            pl.BlockSpec((sb, 2 * kw), tok),
            pl.BlockSpec((sb, ch), tok),
            pl.BlockSpec((sb, kw, w_buf), blk),
            pl.BlockSpec((sb, kw, w_buf), blk),
            pl.BlockSpec((n_st, sb, ch), lambda i: (0, i, 0)),
            pl.BlockSpec((1, w_buf), const),
            pl.BlockSpec(w_dw.shape, const),
            pl.BlockSpec((1, ch), const),
            pl.BlockSpec((1, ch), const),
            pl.BlockSpec((1, ch), const),
            pl.BlockSpec(memory_space=pl.ANY),
        ],
        out_specs=[
            pl.BlockSpec((t_rows, aw), const),
            pl.BlockSpec((t_rows, ch), const),
            pl.BlockSpec((sb, kw, w_buf), blk),
            pl.BlockSpec((sb, kw, w_buf), blk),
            pl.BlockSpec((n_st, sb, ch), lambda i: (0, i, 0)),
        ],
        out_shape=[
            jax.ShapeDtypeStruct((t_rows, aw), BF16),
            jax.ShapeDtypeStruct((t_rows, ch), BF16),
            jax.ShapeDtypeStruct((db, kw, w_buf), F32),
            jax.ShapeDtypeStruct((db, kw, w_buf), F32),
            jax.ShapeDtypeStruct((n_st, db, ch), F32),
        ],
        scratch_shapes=[pltpu.VMEM((sb * n_heads, sb * w_buf), F32)],
        compiler_params=pltpu.CompilerParams(dimension_semantics=("arbitrary",)),
        name="sample_mix",
    )(rel_bias, sinks2, q_all, kv_all, glu_all, cache_kt, cache_vt, state_t, bidx_s, w_dw, b_dw, ln_g, ln_b, after)


def _conv_runs(cb_st, glu_ref, glum_ref, y_sc, wdw_ref, bdw_ref, first, n_runs):
    t = y_sc.shape[0]
    conv_w = wdw_ref.shape[0]
    n_ch = y_sc.shape[1]
    off = HALO - (conv_w - 1)
    n_a = (off + conv_w - 1) // 8 + 1
    units = [(lc, g) for lc in range(n_ch // LANES) for g in range(t // 8)]
    sizes = [len(units) // n_runs + (1 if r < len(units) % n_runs else 0) for r in range(n_runs)]
    staged = []

    def run(mine):
        if not staged:
            cb_st[0:HALO, :] = jnp.where(first, glum_ref[...], cb_st[0:HALO, :])
            cb_st[HALO:HALO + t, :] = glu_ref[...]
            staged.append(True)
        zs = {}

        def z(lc, s, g):
            if (lc, s, g) not in zs:
                ls = slice(lc * LANES, (lc + 1) * LANES)
                acc = None
                for a in range(n_a):
                    w = 8 * a + s - off
                    if 0 <= w < conv_w:
                        term = cb_st[8 * (g + a):8 * (g + a) + 8, ls] * wdw_ref[w:w + 1, ls]
                        acc = term if acc is None else acc + term
                zs[(lc, s, g)] = acc
            return zs[(lc, s, g)]

        dep = None
        for lc, g in mine:
            ls = slice(lc * LANES, (lc + 1) * LANES)
            acc = jnp.broadcast_to(bdw_ref[:, ls], (8, LANES)) + z(lc, 0, g)
            if dep is not None:
                acc = acc + dep
            for s in range(1, 8):
                acc = acc + jnp.concatenate([z(lc, s, g), z(lc, s, g + 1)], axis=0)[s:s + 8, :]
            y_sc[8 * g:8 * g + 8, ls] = acc
            dep = _dep_zero(acc)
        return dep[0:1, :]

    out, k = [], 0
    for n in sizes:
        out.append(functools.partial(run, units[k:k + n]))
        k += n
    return out


def _post_ffn_body(h_ref, am_ref, as_ref, cs_ref, ga_ref, gc_ref, g2_ref, glu_ref, glus_ref,
                   wdw_ref, bdw_ref, lng_ref, lnb_ref,
                   waob, wcob, woutb, w1b, w3b, w2b,
                   ym_ref, ys_ref, gt_ref,
                   hid_ref, cb_st, y_sc, conv_st, glum_ref, *, n_main, tiles_per_seq, n_meta):
    i = pl.program_id(0)
    t_rows = y_sc.shape[0]

    @pl.when(i == 0)
    def _():
        cb_st[...] = jnp.zeros_like(cb_st)
        glum_ref[...] = jnp.zeros_like(glum_ref)
        glum_ref[HALO - n_meta:HALO, :] = glus_ref[0:n_meta, :]

    t = i - 1
    tn = t + 1
    first = (tn < n_main) & (lax.rem(tn, tiles_per_seq) == 0)
    n_chunks = w1b.shape[1] // FF_CHUNK

    def finish_conv():
        conv_st[...] = _ln_swish(y_sc[...], lng_ref[...], lnb_ref[...]).astype(conv_st.dtype)
        cb_st[0:HALO, :] = cb_st[t_rows:t_rows + HALO, :]

    @pl.when(t == -1)
    def _():
        for f in _conv_runs(cb_st, glu_ref, glum_ref, y_sc, wdw_ref, bdw_ref, first, n_chunks):
            f()
        finish_conv()

    @pl.when(t >= 0)
    def _():
        on_main = t < n_main
        conv_f = _conv_runs(cb_st, glu_ref, glum_ref, y_sc, wdw_ref, bdw_ref, first, n_chunks)
        fillers = [(c, c + 1, f) for c, f in enumerate(conv_f)]

        at = jnp.where(on_main, am_ref[...], as_ref[...])
        cv = jnp.where(on_main, conv_st[...], cs_ref[...])
        a = _dot(at, waob[...])
        c = _dot(cv, wcob[...])
        mix = (ga_ref[...] * a + gc_ref[...] * c).astype(BF16)
        h2 = h_ref[...] + _dot(mix, woutb[...])
        xn = _rms(h2, g2_ref[...]).astype(BF16)
        y = h2 + 0.5 * _swiglu(xn, w1b, w3b, w2b, hid_ref, fillers)
        finish_conv()

        @pl.when(on_main)
        def _():
            ym_ref[...] = y

        @pl.when(t == n_main)
        def _():
            ys_ref[...] = y

    n_st = gt_ref.shape[0]
    for sq in range(gt_ref.shape[1]):
        @pl.when(tn == (sq + 1) * tiles_per_seq - 1)
        def _():
            for r in range(n_st):
                gt_ref[r, sq:sq + 1, :] = glu_ref[t_rows - n_st + r:t_rows - n_st + r + 1, :]


def _post_ffn(h_all, attn_main, attn_small, conv_small, ga_all, gc_all, g2, glu_all,
              w_dw, b_dw, ln_g, ln_b, w_ao, w_co, w_out, w1, w3, w2, seq, n_meta):
    m_rows, aw = attn_main.shape
    assert m_rows % HALO == 0 and n_meta <= HALO
    ch = glu_all.shape[1]
    d = h_all.shape[1]
    n_ff = w1.shape[1]
    t = ROW_TILE
    n_main = m_rows // t
    n_b = m_rows // seq
    n_st = w_dw.shape[0] - 1

    def main_idx(i):
        return (jnp.clip(i - 1, 0, n_main - 1), 0)

    def next_idx(i):
        return (jnp.minimum(i, n_main - 1), 0)

    def row_idx(i):
        return (jnp.maximum(i - 1, 0), 0)

    const = lambda i: (0, 0)
    resident = lambda w: pl.BlockSpec(w.shape, const, pipeline_mode=pl.Buffered(1))
    return pl.pallas_call(
        functools.partial(_post_ffn_body, n_main=n_main, tiles_per_seq=seq // t, n_meta=n_meta),
        grid=(n_main + 2,),
        in_specs=[
            pl.BlockSpec((t, d), row_idx),
            pl.BlockSpec((t, aw), main_idx),
            pl.BlockSpec((t, aw), const),
            pl.BlockSpec((t, ch), const),
            pl.BlockSpec((t, d), row_idx),
            pl.BlockSpec((t, d), row_idx),
            pl.BlockSpec((1, d), const),
            pl.BlockSpec((t, ch), next_idx),
            pl.BlockSpec((HALO, ch), lambda i: (m_rows // HALO, 0)),
            pl.BlockSpec(w_dw.shape, const),
            pl.BlockSpec((1, ch), const),
            pl.BlockSpec((1, ch), const),
            pl.BlockSpec((1, ch), const),
            resident(w_ao), resident(w_co), resident(w_out), resident(w1), resident(w3), resident(w2),
        ],
        out_specs=[pl.BlockSpec((t, d), main_idx), pl.BlockSpec((t, d), const),
                   pl.BlockSpec((n_st, n_b, ch), lambda i: (0, 0, 0))],
        out_shape=[jax.ShapeDtypeStruct((m_rows, d), F32), jax.ShapeDtypeStruct((t, d), F32),
                   jax.ShapeDtypeStruct((n_st, n_b, ch), F32)],
        scratch_shapes=[
            pltpu.VMEM((t, n_ff), BF16),
            pltpu.VMEM((HALO + t, ch), F32),
            pltpu.VMEM((t, ch), F32),
            pltpu.VMEM((t, ch), BF16),
            pltpu.VMEM((HALO, ch), F32),
        ],
        compiler_params=pltpu.CompilerParams(
            dimension_semantics=("arbitrary",), vmem_limit_bytes=VMEM_LIMIT),
        name="post_ffn2",
    )(h_all, attn_main, attn_small, conv_small, ga_all, gc_all, g2, glu_all, glu_all,
      w_dw, b_dw, ln_g, ln_b, w_ao, w_co, w_out, w1, w3, w2)


def _t5_bucket(dist, n_buckets):
    max_exact = n_buckets // 2
    d = np.maximum(dist, 0)
    ratio = (np.log(np.maximum(d, 1).astype(np.float32) / np.float32(max_exact))
             / np.float32(math.log(REL_MAX_DIST / max_exact)))
    large = np.minimum(max_exact + (ratio * np.float32(n_buckets - max_exact)).astype(np.int32), n_buckets - 1)
    return np.where(d < max_exact, d, large).astype(np.int32)


def _bucket_or_masked(dist, n_buckets):
    ok = (dist >= 0) & (dist < WINDOW)
    return np.where(ok, _t5_bucket(dist, n_buckets), -1).astype(np.int32)


def _rows(x, start, n):
    return lax.slice_in_dim(x, start, start + n, axis=0)


def kernel(x_prompt, x_sample, cache_k, cache_v, state_conv, meta_tokens, ffn1_norm, ffn1_w1, ffn1_w3, ffn1_w2, mix_norm, w_in, q_norm, k_norm, rel_bias, sinks, w_attn_out, w_dw, b_dw, conv_ln_g, conv_ln_b, w_conv_out, w_out, ffn2_norm, ffn2_w1, ffn2_w3, ffn2_w2):
    n_b, seq, d = x_prompt.shape
    db = x_sample.shape[0]
    n_meta = meta_tokens.shape[0]
    n_heads = sinks.shape[0]
    w_buf, n_kv, hd = cache_k.shape[1], cache_k.shape[2], cache_k.shape[3]
    ch = w_dw.shape[1]
    n_st = state_conv.shape[1]
    n_buckets = rel_bias.shape[0]
    aw, kvw = n_heads * hd, n_kv * hd
    t = ROW_TILE
    m_rows = n_b * seq
    assert hd == HEAD_DIM and kvw == LANES and n_kv == 2 and n_heads == 8 and w_buf == WINDOW
    assert x_sample.shape[1] == 1 and seq % t == 0 and seq % ATTN_TILE == 0
    assert n_meta + db <= t and db % SAMPLE_BLOCK == 0
    assert n_meta <= HALO and n_meta <= WINDOW and n_st == w_dw.shape[0] - 1 and n_st <= HALO
    splits = tuple(int(v) for v in np.cumsum([0, aw, kvw, kvw, ch, ch, d, d]))
    assert splits[-1] == w_in.shape[1]

    row = lambda v: v.reshape(1, -1)
    x_main = x_prompt.reshape(m_rows, d)

    h_all, q_all, kv_all, glu_all, ga_all, gc_all = _ffn_inproj(
        x_main, meta_tokens, x_sample.reshape(db, d), row(ffn1_norm), row(mix_norm), row(q_norm), row(k_norm),
        ffn1_w1, ffn1_w3, ffn1_w2, w_in, splits)

    sinks2 = row(sinks)
    b_dw2, ln_g2, ln_b2 = row(b_dw), row(conv_ln_g), row(conv_ln_b)

    dist = np.arange(WINDOW)[:, None] + WINDOW - np.arange(2 * WINDOW)[None, :]
    bidx = jnp.asarray(_bucket_or_masked(dist, n_buckets))
    attn_main, w_bf16, k_t, v_t = _prompt_attn(
        rel_bias, sinks2, q_all, kv_all, bidx,
        (w_attn_out, w_conv_out, w_out, ffn2_w1, ffn2_w3, ffn2_w2), m_rows, seq, n_meta)

    bidx_s = jnp.asarray(_bucket_or_masked(w_buf - np.arange(w_buf)[None, :], n_buckets))
    attn_small, conv_small, new_k_s, new_v_s, new_conv_s = _sample_mix(
        rel_bias, sinks2, q_all, kv_all, glu_all,
        jnp.transpose(cache_k, (0, 2, 3, 1)).reshape(db, kvw, w_buf),
        jnp.transpose(cache_v, (0, 2, 3, 1)).reshape(db, kvw, w_buf),
        jnp.transpose(state_conv, (1, 0, 2)), bidx_s, w_dw, b_dw2, ln_g2, ln_b2, m_rows + n_meta, n_meta, t,
        after=k_t)

    y_main, y_small, glu_t = _post_ffn(h_all, attn_main, attn_small, conv_small, ga_all, gc_all, row(ffn2_norm),
                                       glu_all, w_dw, b_dw2, ln_g2, ln_b2, *w_bf16, seq, n_meta)

    return (
        y_main.reshape(n_b, seq, d),
        _rows(y_small, n_meta, db).reshape(db, 1, d),
        jnp.transpose(k_t.reshape(n_b, n_kv, hd, WINDOW), (0, 3, 1, 2)),
        jnp.transpose(v_t.reshape(n_b, n_kv, hd, WINDOW), (0, 3, 1, 2)),
        jnp.transpose(glu_t, (1, 0, 2)),
        jnp.transpose(new_k_s.reshape(db, n_kv, hd, w_buf), (0, 3, 1, 2)),
        jnp.transpose(new_v_s.reshape(db, n_kv, hd, w_buf), (0, 3, 1, 2)),
        jnp.transpose(new_conv_s, (1, 0, 2)),
    )
```

```python
import functools
import math

import jax
import jax.numpy as jnp
import numpy as np
from jax import lax
from jax.experimental import pallas as pl
from jax.experimental.pallas import tpu as pltpu

F32 = jnp.float32
BF16 = jnp.bfloat16

EPS = 1e-6
NEG = -1e30
WINDOW = 128
REL_MAX_DIST = 128
HEAD_DIM = 64
LANES = 128
ROW_TILE = 256
ATTN_TILE = 1024
FF_CHUNK = 256
N_WSTEPS = 8
HALO = 32
SAMPLE_BLOCK = 16
VMEM_LIMIT = 56 * 1024 * 1024


def _dot(a, b):
    return jnp.dot(a, b, preferred_element_type=F32)


def _dot_t(a, b):
    return lax.dot_general(a, b, (((1,), (1,)), ((), ())), preferred_element_type=F32)


def _rms(x, g):
    return x * lax.rsqrt(jnp.mean(x * x, axis=-1, keepdims=True) + EPS) * g


def _pair_rms(x, g2, lo_mask):
    sq = x * x
    lo = jnp.sum(jnp.where(lo_mask, sq, 0.0), axis=-1, keepdims=True) * (1.0 / HEAD_DIM)
    hi = jnp.sum(jnp.where(lo_mask, 0.0, sq), axis=-1, keepdims=True) * (1.0 / HEAD_DIM)
    r = jnp.where(lo_mask, lax.rsqrt(lo + EPS), lax.rsqrt(hi + EPS))
    return x * r * g2


def _dep_zero(row):
    u = pltpu.bitcast(row, jnp.uint32)
    return pltpu.bitcast((u >> 16) >> 16, F32)


def _swiglu(xn, w1b, w3b, w2b, hid_ref, fillers=()):
    n_ff = w1b.shape[1]
    n_chunks = n_ff // FF_CHUNK
    due = {}
    for c in range(n_chunks):
        sl = slice(c * FF_CHUNK, (c + 1) * FF_CHUNK)
        a = _dot(xn, w1b[:, sl])
        b = _dot(xn, w3b[:, sl])
        for z in due.pop(c, ()):
            b = b + jnp.concatenate([z] * (FF_CHUNK // LANES), axis=1)
        for issue, when, thunk in fillers:
            if issue == c:
                due.setdefault(min(when, n_chunks), []).append(_dep_zero(thunk()))
        hid_ref[:, sl] = (a * jax.nn.sigmoid(a) * b).astype(BF16)
    out = _dot(hid_ref[...], w2b[...])
    for z in due.pop(n_chunks, ()):
        out = out + jnp.concatenate([z] * (out.shape[1] // LANES), axis=1)
    assert not due
    return out


def _store_chunk(dst, src, i):
    rows = src.shape[0]
    r = pl.multiple_of(i * rows, 16)
    dst[pl.ds(r, rows), :] = src[...].astype(BF16)


def _bias_table(bidx, rb_ref, head):
    tab = jnp.full(bidx.shape, NEG, F32)
    for b in range(rb_ref.shape[1]):
        tab = jnp.where(bidx == b, rb_ref[head, b], tab)
    return tab


def _ln_swish(y, g, b):
    mu = jnp.mean(y, axis=-1, keepdims=True)
    yc = y - mu
    var = jnp.mean(yc * yc, axis=-1, keepdims=True)
    z = yc * lax.rsqrt(var + EPS) * g + b
    return z * jax.nn.sigmoid(z)


def _ffn_inproj_body(xc_ref, xn_ref, meta_ref, xsm_ref, g1_ref, gm_ref, gq_ref, gk_ref, w1c, w3c, w2c, winc,
                     h_ref, q_ref, kv_ref, glu_ref, ga_ref, gc_ref,
                     w1b, w3b, w2b, winb, hid_ref, xn_st, u_st, xs_st, *, n_main, splits):
    i = pl.program_id(0)

    @pl.when(i == 0)
    def _():
        n_meta, n_c = meta_ref.shape[0], xs_st.shape[1] // LANES
        n_s = xsm_ref.shape[0] // n_c
        xs_st[...] = jnp.zeros_like(xs_st)
        xs_st[0:n_meta, :] = meta_ref[...]
        for c in range(n_c):
            xs_st[n_meta:n_meta + n_s, c * LANES:(c + 1) * LANES] = xsm_ref[pl.ds(c, n_s, stride=n_c), :]

    def small_tile():
        return xs_st[...]

    @pl.when(i < N_WSTEPS)
    def _():
        _store_chunk(w1b, w1c, i)
        _store_chunk(w3b, w3c, i)
        _store_chunk(w2b, w2c, i)
        _store_chunk(winb, winc, i)

    s = i - N_WSTEPS

    def project():
        u = u_st[...]
        o_q, o_k, o_v, o_a, o_b, o_ga, o_gc, o_end = splits
        lo_mask = lax.broadcasted_iota(jnp.int32, (1, LANES), 1) < HEAD_DIM
        gq2 = jnp.concatenate([gq_ref[...], gq_ref[...]], axis=1)
        gk2 = jnp.concatenate([gk_ref[...], gk_ref[...]], axis=1)
        zq = _dot(u, winb[:, o_q:o_k])
        for p in range((o_k - o_q) // LANES):
            sl = slice(p * LANES, (p + 1) * LANES)
            q_ref[:, sl] = (_pair_rms(zq[:, sl], gq2, lo_mask) * (HEAD_DIM ** -0.5)).astype(BF16)
        zkv = _dot(u, winb[:, o_k:o_a])
        kv_ref[:, :LANES] = _pair_rms(zkv[:, :LANES], gk2, lo_mask)
        kv_ref[:, LANES:] = zkv[:, LANES:]
        za = _dot(u, winb[:, o_a:o_b])
        zb = _dot(u, winb[:, o_b:o_ga])
        glu_ref[...] = za * jax.nn.sigmoid(zb)
        ga_ref[...] = jax.nn.sigmoid(_dot(u, winb[:, o_ga:o_gc]))
        gc_ref[...] = jax.nn.sigmoid(_dot(u, winb[:, o_gc:o_end]))

    def normalise_next():
        x_next = jnp.where(s + 1 < n_main, xn_ref[...], small_tile())
        xn_st[...] = _rms(x_next, g1_ref[...]).astype(BF16)

    @pl.when(i == 0)
    def _():
        u_st[...] = jnp.zeros_like(u_st)

    @pl.when(s == -1)
    def _():
        normalise_next()

    @pl.when((s >= 0) & (s <= n_main))
    def _():
        project()
        x = jnp.where(s < n_main, xc_ref[...], small_tile())
        h = x + 0.5 * _swiglu(xn_st[...], w1b, w3b, w2b, hid_ref)
        h_ref[...] = h
        u_st[...] = _rms(h, gm_ref[...]).astype(BF16)
        normalise_next()

    @pl.when(s == n_main + 1)
    def _():
        project()


def _ffn_inproj(x_main, x_meta, x_sample, g1, gm, gq, gk, w1, w3, w2, w_in, splits):
    m_rows, d = x_main.shape
    n_sample = x_sample.shape[0] * LANES // d
    assert x_meta.shape[0] % 8 == 0 and n_sample % 8 == 0 and x_sample.shape[1] == LANES
    assert x_meta.shape[0] + n_sample <= ROW_TILE
    n_ff = w1.shape[1]
    n_main = m_rows // ROW_TILE
    r_rows = m_rows + ROW_TILE
    t = ROW_TILE
    o_q, o_k, o_v, o_a, o_b, o_ga, o_gc, o_end = splits
    assert o_v - o_k == LANES and o_a - o_v == LANES and (o_k - o_q) % LANES == 0

    def cur_idx(i):
        return (jnp.clip(i - N_WSTEPS, 0, n_main - 1), 0)

    def next_idx(i):
        return (jnp.clip(i - N_WSTEPS + 1, 0, n_main - 1), 0)

    def h_idx(i):
        return (jnp.clip(i - N_WSTEPS, 0, n_main), 0)

    def proj_idx(i):
        return (jnp.clip(i - N_WSTEPS - 1, 0, n_main), 0)

    def w_idx(i):
        return (jnp.minimum(i, N_WSTEPS - 1), 0)

    const = lambda i: (0, 0)
    outs = [
        (jax.ShapeDtypeStruct((r_rows, d), F32), h_idx),
        (jax.ShapeDtypeStruct((r_rows, o_k - o_q), BF16), proj_idx),
        (jax.ShapeDtypeStruct((r_rows, o_a - o_k), F32), proj_idx),
        (jax.ShapeDtypeStruct((r_rows, o_b - o_a), F32), proj_idx),
        (jax.ShapeDtypeStruct((r_rows, o_gc - o_ga), F32), proj_idx),
        (jax.ShapeDtypeStruct((r_rows, o_end - o_gc), F32), proj_idx),
    ]
    return pl.pallas_call(
        functools.partial(_ffn_inproj_body, n_main=n_main, splits=splits),
        grid=(N_WSTEPS + n_main + 2,),
        in_specs=[
            pl.BlockSpec((t, d), cur_idx),
            pl.BlockSpec((t, d), next_idx),
            pl.BlockSpec(x_meta.shape, const),
            pl.BlockSpec(x_sample.shape, const),
            pl.BlockSpec((1, d), const),
            pl.BlockSpec((1, d), const),
            pl.BlockSpec((1, HEAD_DIM), const),
            pl.BlockSpec((1, HEAD_DIM), const),
            pl.BlockSpec((d // N_WSTEPS, n_ff), w_idx),
            pl.BlockSpec((d // N_WSTEPS, n_ff), w_idx),
            pl.BlockSpec((n_ff // N_WSTEPS, d), w_idx),
            pl.BlockSpec((d // N_WSTEPS, o_end), w_idx),
        ],
        out_specs=[pl.BlockSpec((t, s.shape[1]), idx) for s, idx in outs],
        out_shape=[s for s, _ in outs],
        scratch_shapes=[
            pltpu.VMEM((d, n_ff), BF16),
            pltpu.VMEM((d, n_ff), BF16),
            pltpu.VMEM((n_ff, d), BF16),
            pltpu.VMEM((d, o_end), BF16),
            pltpu.VMEM((t, n_ff), BF16),
            pltpu.VMEM((t, d), BF16),
            pltpu.VMEM((t, d), BF16),
            pltpu.VMEM((t, d), F32),
        ],
        compiler_params=pltpu.CompilerParams(
            dimension_semantics=("arbitrary",), vmem_limit_bytes=VMEM_LIMIT),
        name="ffn1_inproj",
    )(x_main, x_main, x_meta, x_sample, g1, gm, gq, gk, w1, w3, w2, w_in)


def _prompt_attn_body(rb_ref, sink_ref, q_ref, kvc_ref, kvp_ref, kvs_ref, bidx_ref, *rest,
                      tiles_per_seq, n_meta, n_heads, n_weights):
    w_refs, attn_ref, wb_refs = rest[:n_weights], rest[n_weights], rest[n_weights + 1:2 * n_weights + 1]
    kt_ref, vt_ref, bias_sc, kvm_ref = rest[2 * n_weights + 1:]
    for w_ref, wb_ref in zip(w_refs, wb_refs, strict=True):
        wb_ref[...] = w_ref[...].astype(BF16)

    i = pl.program_id(0)
    t = q_ref.shape[0]
    n_kv = 2
    grp = n_heads // n_kv

    @pl.when(i == 0)
    def _():
        kvm_ref[...] = jnp.zeros_like(kvm_ref)
        kvm_ref[WINDOW - n_meta:WINDOW, :] = kvs_ref[0:n_meta, :]
        col = lax.broadcasted_iota(jnp.int32, (WINDOW, 2 * WINDOW), 1)
        for hd in range(n_heads):
            tab = _bias_table(bidx_ref[...], rb_ref, hd)
            bias_sc[0, hd] = tab
            bias_sc[1, hd] = jnp.where(col < WINDOW - n_meta, NEG, tab)

    first = (i % tiles_per_seq) == 0
    lead = jnp.where(first, 1, 0)
    lo_mask = lax.broadcasted_iota(jnp.int32, (1, LANES), 1) < HEAD_DIM
    kv_prev = jnp.where(first, kvm_ref[...], kvp_ref[...])
    kv = jnp.concatenate([kv_prev, kvc_ref[...]], axis=0)
    k, v = kv[:, :LANES], kv[:, LANES:]
    kb, vb = k.astype(BF16), v.astype(BF16)
    kr = pltpu.roll(k, HEAD_DIM, axis=1).astype(BF16)
    vr = pltpu.roll(v, HEAD_DIM, axis=1).astype(BF16)
    zero = jnp.zeros((), BF16)
    k_lo = [jnp.where(lo_mask, kb, zero), jnp.where(lo_mask, kr, zero)]
    k_hi = [jnp.where(lo_mask, zero, kr), jnp.where(lo_mask, zero, kb)]
    v_lo = [jnp.where(lo_mask, vb, zero), jnp.where(lo_mask, vr, zero)]
    v_hi = [jnp.where(lo_mask, zero, vr), jnp.where(lo_mask, zero, vb)]

    for qb in range(t // WINDOW):
        rows = slice(qb * WINDOW, (qb + 1) * WINDOW)
        keys = slice(qb * WINDOW, qb * WINDOW + 2 * WINDOW)
        tab = lead if qb == 0 else 0
        for h in range(n_kv):
            k_st = jnp.concatenate([k_lo[h][keys], k_hi[h][keys]], axis=0)
            v_st = jnp.concatenate([v_lo[h][keys], v_hi[h][keys]], axis=0)
            c0 = h * grp * HEAD_DIM
            qq = jnp.concatenate([q_ref[rows, c0:c0 + LANES], q_ref[rows, c0 + LANES:c0 + 2 * LANES]], axis=0)
            s = _dot_t(qq, k_st)
            p_parts, inv_parts = [], []
            for g2 in range(2):
                p_row, inv_row = [], []
                for par in range(2):
                    hd = h * grp + 2 * g2 + par
                    sq = s[g2 * WINDOW:(g2 + 1) * WINDOW, par * 2 * WINDOW:(par + 1) * 2 * WINDOW] + bias_sc[tab, hd]
                    sink = sink_ref[0, hd]
                    m = jnp.maximum(jnp.max(sq, axis=-1, keepdims=True), sink)
                    p = jnp.exp(sq - m)
                    den = jnp.sum(p, axis=-1, keepdims=True) + jnp.exp(sink - m)
                    p_row.append(p.astype(BF16))
                    inv_row.append(1.0 / den)
                p_parts.append(jnp.concatenate(p_row, axis=1))
                inv_parts.append(jnp.where(lo_mask, inv_row[0], inv_row[1]))
            pm = jnp.concatenate(p_parts, axis=0)
            o = _dot(pm, v_st)
            for g2 in range(2):
                c = c0 + g2 * LANES
                attn_ref[rows, c:c + LANES] = (o[g2 * WINDOW:(g2 + 1) * WINDOW] * inv_parts[g2]).astype(attn_ref.dtype)

    @pl.when((i + 1) % tiles_per_seq == 0)
    def _():
        kt_ref[0] = kvc_ref[t - WINDOW:t, :LANES].T
        vt_ref[0] = kvc_ref[t - WINDOW:t, LANES:].T


def _prompt_attn(rel_bias, sinks2, q_all, kv_all, bidx, weights, m_rows, seq, n_meta):
    t = ATTN_TILE
    n_steps = m_rows // t
    assert m_rows % WINDOW == 0 and n_meta <= WINDOW
    n_heads = sinks2.shape[1]
    qw, kvw = q_all.shape[1], kv_all.shape[1]
    const = lambda i: (0, 0)
    row = lambda i: (i, 0)
    per_seq = lambda i: (i // (seq // t), 0, 0)
    smem = pl.BlockSpec(memory_space=pltpu.SMEM)

    def w_spec(w):
        share = 1 if (w.shape[0] // n_steps) % 16 == 0 else 2
        assert w.shape[0] % (n_steps // share) == 0 and (w.shape[0] * share // n_steps) % 16 == 0
        return pl.BlockSpec((w.shape[0] * share // n_steps, w.shape[1]), lambda i: (i // share, 0))

    w_specs = [w_spec(w) for w in weights]
    outs = pl.pallas_call(
        functools.partial(_prompt_attn_body, tiles_per_seq=seq // t, n_meta=n_meta, n_heads=n_heads,
                          n_weights=len(weights)),
        grid=(n_steps,),
        in_specs=[
            smem, smem,
            pl.BlockSpec((t, qw), row),
            pl.BlockSpec((t, kvw), row),
            pl.BlockSpec((WINDOW, kvw), lambda i: (jnp.maximum(i * (t // WINDOW) - 1, 0), 0)),
            pl.BlockSpec((WINDOW, kvw), lambda i: (m_rows // WINDOW, 0)),
            pl.BlockSpec((WINDOW, 2 * WINDOW), const),
        ] + w_specs,
        out_specs=[pl.BlockSpec((t, qw), row)] + w_specs + [pl.BlockSpec((1, WINDOW, LANES), per_seq)] * 2,
        out_shape=[jax.ShapeDtypeStruct((m_rows, qw), BF16)] + [jax.ShapeDtypeStruct(w.shape, BF16) for w in weights]
        + [jax.ShapeDtypeStruct((m_rows // seq, WINDOW, LANES), F32)] * 2,
        scratch_shapes=[
            pltpu.VMEM((2, n_heads, WINDOW, 2 * WINDOW), F32),
            pltpu.VMEM((WINDOW, kvw), F32),
        ],
        compiler_params=pltpu.CompilerParams(dimension_semantics=("arbitrary",)),
        name="prompt_attn",
    )(rel_bias, sinks2, q_all, kv_all, kv_all, kv_all, bidx, *weights)
    return outs[0], outs[1:1 + len(weights)], outs[1 + len(weights)], outs[2 + len(weights)]


def _sample_mix_body(rb_ref, sink_ref, q_ref, kvn_ref, glun_ref, ck_ref, cv_ref, st_ref, bidx_ref,
                     wdw_ref, bdw_ref, lng_ref, lnb_ref, after_ref,
                     as_ref, cs_ref, nk_ref, nv_ref, nst_ref, bias_sc, *, n_heads, n_meta):
    i = pl.program_id(0)
    sb, w_buf = ck_ref.shape[0], ck_ref.shape[2]
    n_st = st_ref.shape[0]
    n_rows = sb * n_heads
    per_kv = n_heads // (LANES // HEAD_DIM)

    @pl.when(i == 0)
    def _():
        as_ref[...] = jnp.zeros_like(as_ref)
        cs_ref[...] = jnp.zeros_like(cs_ref)
        tiled = jnp.concatenate(
            [jnp.broadcast_to(jnp.concatenate([_bias_table(bidx_ref[...], rb_ref, hd)] * sb, axis=1),
                              (sb, sb * w_buf)) for hd in range(n_heads)], axis=0)
        row_seq = lax.rem(lax.broadcasted_iota(jnp.int32, tiled.shape, 0), sb)
        col_seq = lax.broadcasted_iota(jnp.int32, tiled.shape, 1) // w_buf
        bias_sc[...] = jnp.where(row_seq == col_seq, tiled, NEG)

    hrow = lax.broadcasted_iota(jnp.int32, (n_rows, 1), 0) // sb
    sink = jnp.zeros((n_rows, 1), F32)
    bias_new = jnp.zeros((n_rows, 1), F32)
    for hd in range(n_heads):
        sink = jnp.where(hrow == hd, sink_ref[0, hd], sink)
        bias_new = jnp.where(hrow == hd, rb_ref[hd, 0], bias_new)

    lo_mask = lax.broadcasted_iota(jnp.int32, (1, LANES), 1) < HEAD_DIM

    def half(x, src_hi, dst_hi):
        if src_hi != dst_hi:
            x = pltpu.roll(x, HEAD_DIM, 1)
        return jnp.where(lo_mask, 0.0, x) if dst_hi else jnp.where(lo_mask, x, 0.0)

    q = jnp.concatenate(
        [half(q_ref[:, (hd // 2) * LANES:(hd // 2 + 1) * LANES].astype(F32), hd % 2 == 1, hd // per_kv == 1)
         for hd in range(n_heads)], axis=0).astype(BF16)
    kn_rows = jnp.concatenate([kvn_ref[:, :LANES]] * n_heads, axis=0).astype(BF16)
    vn_rows = jnp.concatenate([kvn_ref[:, LANES:]] * n_heads, axis=0).astype(BF16)
    ck_all = jnp.concatenate([ck_ref[bb] for bb in range(sb)], axis=1).astype(BF16)
    cv_all = jnp.concatenate([cv_ref[bb] for bb in range(sb)], axis=1).astype(BF16)
    s_c = _dot(q, ck_all) + bias_sc[...]
    s_n = jnp.sum(q.astype(F32) * kn_rows.astype(F32), axis=-1, keepdims=True) + bias_new
    m = jnp.maximum(jnp.maximum(jnp.max(s_c, axis=-1, keepdims=True), s_n), sink)
    p_c = jnp.exp(s_c - m)
    p_n = jnp.exp(s_n - m)
    den = jnp.sum(p_c, axis=-1, keepdims=True) + p_n + jnp.exp(sink - m)
    o = _dot_t(p_c.astype(BF16), cv_all) + p_n.astype(BF16).astype(F32) * vn_rows.astype(F32)
    o = o / den
    pairs = []
    for p in range(n_heads // 2):
        lo_head = o[(2 * p) * sb:(2 * p + 1) * sb]
        hi_head = o[(2 * p + 1) * sb:(2 * p + 2) * sb]
        pairs.append(half(lo_head, (2 * p) // per_kv == 1, False) + half(hi_head, (2 * p + 1) // per_kv == 1, True))
    rows = pl.ds(pl.multiple_of(n_meta + i * sb, sb), sb)
    as_ref[rows, :] = jnp.concatenate(pairs, axis=1).astype(as_ref.dtype)

    g_new = glun_ref[...]
    y = g_new * wdw_ref[n_st:n_st + 1, :] + bdw_ref[...]
    for k in range(n_st):
        y = y + st_ref[k] * wdw_ref[k:k + 1, :]
    cs_ref[rows, :] = _ln_swish(y, lng_ref[...], lnb_ref[...]).astype(cs_ref.dtype)

    fill = jnp.zeros((w_buf - sb, LANES), F32)
    kn_t = jnp.concatenate([kvn_ref[:, :LANES], fill], axis=0).T
    vn_t = jnp.concatenate([kvn_ref[:, LANES:], fill], axis=0).T
    newest = lax.broadcasted_iota(jnp.int32, (LANES, w_buf), 1) == w_buf - 1
    for bb in range(sb):
        nk_ref[bb] = jnp.where(newest, kn_t[:, bb:bb + 1], pltpu.roll(ck_ref[bb], w_buf - 1, 1))
        nv_ref[bb] = jnp.where(newest, vn_t[:, bb:bb + 1], pltpu.roll(cv_ref[bb], w_buf - 1, 1))
    nst_ref[0:n_st - 1] = st_ref[1:n_st]
    nst_ref[n_st - 1] = g_new


def _sample_mix(rel_bias, sinks2, q_all, kv_all, glu_all, cache_kt, cache_vt, state_t, bidx_s, w_dw, b_dw, ln_g, ln_b,
                s0, n_meta, t_rows, after):
    db, kw, w_buf = cache_kt.shape
    n_st, ch = state_t.shape[0], state_t.shape[2]
    n_heads = sinks2.shape[1]
    aw = q_all.shape[1]
    sb = SAMPLE_BLOCK
    assert s0 % sb == 0 and n_meta % sb == 0 and n_meta + db <= t_rows
    blk = lambda i: (i, 0, 0)
    const = lambda i: (0, 0)
    tok = lambda i: (s0 // sb + i, 0)
    smem = pl.BlockSpec(memory_space=pltpu.SMEM)
    return pl.pallas_call(
        functools.partial(_sample_mix_body, n_heads=n_heads, n_meta=n_meta),
        grid=(db // sb,),
        in_specs=[
            smem, smem,
            pl.BlockSpec((sb, aw), tok),
            pl.BlockSpec((sb, 2 * kw), tok),
            pl.BlockSpec((sb, ch), tok),
            pl.BlockSpec((sb, kw, w_buf), blk),
            pl.BlockSpec((sb, kw, w_buf), blk),
            pl.BlockSpec((n_st, sb, ch), lambda i: (0, i, 0)),
            pl.BlockSpec((1, w_buf), const),
            pl.BlockSpec(w_dw.shape, const),
            pl.BlockSpec((1, ch), const),
            pl.BlockSpec((1, ch), const),
            pl.BlockSpec((1, ch), const),
            pl.BlockSpec(memory_space=pl.ANY),
        ],
        out_specs=[
            pl.BlockSpec((t_rows, aw), const),
            pl.BlockSpec((t_rows, ch), const),
            pl.BlockSpec((sb, kw, w_buf), blk),
            pl.BlockSpec((sb, kw, w_buf), blk),
            pl.BlockSpec((n_st, sb, ch), lambda i: (0, i, 0)),
        ],
        out_shape=[
            jax.ShapeDtypeStruct((t_rows, aw), BF16),
            jax.ShapeDtypeStruct((t_rows, ch), BF16),
            jax.ShapeDtypeStruct((db, kw, w_buf), F32),
            jax.ShapeDtypeStruct((db, kw, w_buf), F32),
            jax.ShapeDtypeStruct((n_st, db, ch), F32),
        ],
        scratch_shapes=[pltpu.VMEM((sb * n_heads, sb * w_buf), F32)],
        compiler_params=pltpu.CompilerParams(dimension_semantics=("arbitrary",)),
        name="sample_mix",
    )(rel_bias, sinks2, q_all, kv_all, glu_all, cache_kt, cache_vt, state_t, bidx_s, w_dw, b_dw, ln_g, ln_b, after)


def _conv_runs(cb_st, glu_ref, glum_ref, y_sc, wdw_ref, bdw_ref, first, n_runs):
    t = y_sc.shape[0]
    conv_w = wdw_ref.shape[0]
    n_ch = y_sc.shape[1]
    off = HALO - (conv_w - 1)
    n_a = (off + conv_w - 1) // 8 + 1
    units = [(lc, g) for lc in range(n_ch // LANES) for g in range(t // 8)]
    sizes = [len(units) // n_runs + (1 if r < len(units) % n_runs else 0) for r in range(n_runs)]
    staged = []

    def run(mine):
        if not staged:
            cb_st[0:HALO, :] = jnp.where(first, glum_ref[...], cb_st[0:HALO, :])
            cb_st[HALO:HALO + t, :] = glu_ref[...]
            staged.append(True)
        zs = {}

        def z(lc, s, g):
            if (lc, s, g) not in zs:
                ls = slice(lc * LANES, (lc + 1) * LANES)
                acc = None
                for a in range(n_a):
                    w = 8 * a + s - off
                    if 0 <= w < conv_w:
                        term = cb_st[8 * (g + a):8 * (g + a) + 8, ls] * wdw_ref[w:w + 1, ls]
                        acc = term if acc is None else acc + term
                zs[(lc, s, g)] = acc
            return zs[(lc, s, g)]

        dep = None
        for lc, g in mine:
            ls = slice(lc * LANES, (lc + 1) * LANES)
            acc = jnp.broadcast_to(bdw_ref[:, ls], (8, LANES)) + z(lc, 0, g)
            if dep is not None:
                acc = acc + dep
            for s in range(1, 8):
                acc = acc + jnp.concatenate([z(lc, s, g), z(lc, s, g + 1)], axis=0)[s:s + 8, :]
            y_sc[8 * g:8 * g + 8, ls] = acc
            dep = _dep_zero(acc)
        return dep[0:1, :]

    out, k = [], 0
    for n in sizes:
        out.append(functools.partial(run, units[k:k + n]))
        k += n
    return out


def _post_ffn_body(h_ref, am_ref, as_ref, cs_ref, ga_ref, gc_ref, g2_ref, glu_ref, glus_ref,
                   wdw_ref, bdw_ref, lng_ref, lnb_ref,
                   waob, wcob, woutb, w1b, w3b, w2b,
                   ym_ref, ys_ref, gt_ref,
                   hid_ref, cb_st, y_sc, conv_st, glum_ref, *, n_main, tiles_per_seq, n_meta):
    i = pl.program_id(0)
    t_rows = y_sc.shape[0]

    @pl.when(i == 0)
    def _():
        cb_st[...] = jnp.zeros_like(cb_st)
        glum_ref[...] = jnp.zeros_like(glum_ref)
        glum_ref[HALO - n_meta:HALO, :] = glus_ref[0:n_meta, :]

    t = i - 1
    tn = t + 1
    first = (tn < n_main) & (lax.rem(tn, tiles_per_seq) == 0)
    n_chunks = w1b.shape[1] // FF_CHUNK

    def finish_conv():
        conv_st[...] = _ln_swish(y_sc[...], lng_ref[...], lnb_ref[...]).astype(conv_st.dtype)
        cb_st[0:HALO, :] = cb_st[t_rows:t_rows + HALO, :]

    @pl.when(t == -1)
    def _():
        for f in _conv_runs(cb_st, glu_ref, glum_ref, y_sc, wdw_ref, bdw_ref, first, n_chunks):
            f()
        finish_conv()

    @pl.when(t >= 0)
    def _():
        on_main = t < n_main
        conv_f = _conv_runs(cb_st, glu_ref, glum_ref, y_sc, wdw_ref, bdw_ref, first, n_chunks)
        fillers = [(c, c + 1, f) for c, f in enumerate(conv_f)]

        at = jnp.where(on_main, am_ref[...], as_ref[...])
        cv = jnp.where(on_main, conv_st[...], cs_ref[...])
        a = _dot(at, waob[...])
        c = _dot(cv, wcob[...])
        mix = (ga_ref[...] * a + gc_ref[...] * c).astype(BF16)
        h2 = h_ref[...] + _dot(mix, woutb[...])
        xn = _rms(h2, g2_ref[...]).astype(BF16)
        y = h2 + 0.5 * _swiglu(xn, w1b, w3b, w2b, hid_ref, fillers)
        finish_conv()

        @pl.when(on_main)
        def _():
            ym_ref[...] = y

        @pl.when(t == n_main)
        def _():
            n_c = y.shape[1] // LANES
            n_s = ys_ref.shape[0] // n_c
            for c in range(n_c):
                ys_ref[pl.ds(c, n_s, stride=n_c), :] = y[n_meta:n_meta + n_s, c * LANES:(c + 1) * LANES]

    n_st = gt_ref.shape[0]
    for sq in range(gt_ref.shape[1]):
        @pl.when(tn == (sq + 1) * tiles_per_seq - 1)
        def _():
            for r in range(n_st):
                gt_ref[r, sq:sq + 1, :] = glu_ref[t_rows - n_st + r:t_rows - n_st + r + 1, :]


def _post_ffn(h_all, attn_main, attn_small, conv_small, ga_all, gc_all, g2, glu_all,
              w_dw, b_dw, ln_g, ln_b, w_ao, w_co, w_out, w1, w3, w2, seq, n_meta, n_sample):
    m_rows, aw = attn_main.shape
    assert m_rows % HALO == 0 and n_meta <= HALO and n_meta % 8 == 0 and n_sample % 8 == 0
    ch = glu_all.shape[1]
    d = h_all.shape[1]
    n_ff = w1.shape[1]
    t = ROW_TILE
    n_main = m_rows // t
    n_b = m_rows // seq
    n_st = w_dw.shape[0] - 1

    def main_idx(i):
        return (jnp.clip(i - 1, 0, n_main - 1), 0)

    def next_idx(i):
        return (jnp.minimum(i, n_main - 1), 0)

    def row_idx(i):
        return (jnp.maximum(i - 1, 0), 0)

    const = lambda i: (0, 0)
    resident = lambda w: pl.BlockSpec(w.shape, const, pipeline_mode=pl.Buffered(1))
    return pl.pallas_call(
        functools.partial(_post_ffn_body, n_main=n_main, tiles_per_seq=seq // t, n_meta=n_meta),
        grid=(n_main + 2,),
        in_specs=[
            pl.BlockSpec((t, d), row_idx),
            pl.BlockSpec((t, aw), main_idx),
            pl.BlockSpec((t, aw), const),
            pl.BlockSpec((t, ch), const),
            pl.BlockSpec((t, d), row_idx),
            pl.BlockSpec((t, d), row_idx),
            pl.BlockSpec((1, d), const),
            pl.BlockSpec((t, ch), next_idx),
            pl.BlockSpec((HALO, ch), lambda i: (m_rows // HALO, 0)),
            pl.BlockSpec(w_dw.shape, const),
            pl.BlockSpec((1, ch), const),
            pl.BlockSpec((1, ch), const),
            pl.BlockSpec((1, ch), const),
            resident(w_ao), resident(w_co), resident(w_out), resident(w1), resident(w3), resident(w2),
        ],
        out_specs=[pl.BlockSpec((t, d), main_idx), pl.BlockSpec((n_sample * (d // LANES), LANES), const),
                   pl.BlockSpec((n_st, n_b, ch), lambda i: (0, 0, 0))],
        out_shape=[jax.ShapeDtypeStruct((m_rows, d), F32),
                   jax.ShapeDtypeStruct((n_sample * (d // LANES), LANES), F32),
                   jax.ShapeDtypeStruct((n_st, n_b, ch), F32)],
        scratch_shapes=[
            pltpu.VMEM((t, n_ff), BF16),
            pltpu.VMEM((HALO + t, ch), F32),
            pltpu.VMEM((t, ch), F32),
            pltpu.VMEM((t, ch), BF16),
            pltpu.VMEM((HALO, ch), F32),
        ],
        compiler_params=pltpu.CompilerParams(
            dimension_semantics=("arbitrary",), vmem_limit_bytes=VMEM_LIMIT),
        name="post_ffn2",
    )(h_all, attn_main, attn_small, conv_small, ga_all, gc_all, g2, glu_all, glu_all,
      w_dw, b_dw, ln_g, ln_b, w_ao, w_co, w_out, w1, w3, w2)


def _t5_bucket(dist, n_buckets):
    max_exact = n_buckets // 2
    d = np.maximum(dist, 0)
    ratio = (np.log(np.maximum(d, 1).astype(np.float32) / np.float32(max_exact))
             / np.float32(math.log(REL_MAX_DIST / max_exact)))
    large = np.minimum(max_exact + (ratio * np.float32(n_buckets - max_exact)).astype(np.int32), n_buckets - 1)
    return np.where(d < max_exact, d, large).astype(np.int32)


def _bucket_or_masked(dist, n_buckets):
    ok = (dist >= 0) & (dist < WINDOW)
    return np.where(ok, _t5_bucket(dist, n_buckets), -1).astype(np.int32)


def _rows(x, start, n):
    return lax.slice_in_dim(x, start, start + n, axis=0)


def kernel(x_prompt, x_sample, cache_k, cache_v, state_conv, meta_tokens, ffn1_norm, ffn1_w1, ffn1_w3, ffn1_w2, mix_norm, w_in, q_norm, k_norm, rel_bias, sinks, w_attn_out, w_dw, b_dw, conv_ln_g, conv_ln_b, w_conv_out, w_out, ffn2_norm, ffn2_w1, ffn2_w3, ffn2_w2):
    n_b, seq, d = x_prompt.shape
    db = x_sample.shape[0]
    n_meta = meta_tokens.shape[0]
    n_heads = sinks.shape[0]
    w_buf, n_kv, hd = cache_k.shape[1], cache_k.shape[2], cache_k.shape[3]
    ch = w_dw.shape[1]
    n_st = state_conv.shape[1]
    n_buckets = rel_bias.shape[0]
    aw, kvw = n_heads * hd, n_kv * hd
    t = ROW_TILE
    m_rows = n_b * seq
    assert hd == HEAD_DIM and kvw == LANES and n_kv == 2 and n_heads == 8 and w_buf == WINDOW
    assert x_sample.shape[1] == 1 and seq % t == 0 and seq % ATTN_TILE == 0
    assert n_meta + db <= t and db % SAMPLE_BLOCK == 0
    assert n_meta <= HALO and n_meta <= WINDOW and n_st == w_dw.shape[0] - 1 and n_st <= HALO
    splits = tuple(int(v) for v in np.cumsum([0, aw, kvw, kvw, ch, ch, d, d]))
    assert splits[-1] == w_in.shape[1]

    row = lambda v: v.reshape(1, -1)
    x_main = x_prompt.reshape(m_rows, d)

    h_all, q_all, kv_all, glu_all, ga_all, gc_all = _ffn_inproj(
        x_main, meta_tokens, x_sample.reshape(db * (d // LANES), LANES), row(ffn1_norm), row(mix_norm),
        row(q_norm), row(k_norm), ffn1_w1, ffn1_w3, ffn1_w2, w_in, splits)

    sinks2 = row(sinks)
    rel_bias_t = rel_bias.T
    b_dw2, ln_g2, ln_b2 = row(b_dw), row(conv_ln_g), row(conv_ln_b)

    dist = np.arange(WINDOW)[:, None] + WINDOW - np.arange(2 * WINDOW)[None, :]
    bidx = jnp.asarray(_bucket_or_masked(dist, n_buckets))
    attn_main, w_bf16, k_t, v_t = _prompt_attn(
        rel_bias_t, sinks2, q_all, kv_all, bidx,
        (w_attn_out, w_conv_out, w_out, ffn2_w1, ffn2_w3, ffn2_w2), m_rows, seq, n_meta)

    bidx_s = jnp.asarray(_bucket_or_masked(w_buf - np.arange(w_buf)[None, :], n_buckets))
    attn_small, conv_small, new_k_s, new_v_s, new_conv_s = _sample_mix(
        rel_bias_t, sinks2, q_all, kv_all, glu_all,
        jnp.transpose(cache_k, (0, 2, 3, 1)).reshape(db, kvw, w_buf),
        jnp.transpose(cache_v, (0, 2, 3, 1)).reshape(db, kvw, w_buf),
        jnp.transpose(state_conv, (1, 0, 2)), bidx_s, w_dw, b_dw2, ln_g2, ln_b2, m_rows + n_meta, n_meta, t,
        after=k_t)

    y_main, y_small, glu_t = _post_ffn(h_all, attn_main, attn_small, conv_small, ga_all, gc_all, row(ffn2_norm),
                                       glu_all, w_dw, b_dw2, ln_g2, ln_b2, *w_bf16, seq, n_meta, db)

    return (
        y_main.reshape(n_b, seq, d),
        y_small.reshape(db, 1, d),
        jnp.transpose(k_t.reshape(n_b, n_kv, hd, WINDOW), (0, 3, 1, 2)),
        jnp.transpose(v_t.reshape(n_b, n_kv, hd, WINDOW), (0, 3, 1, 2)),
        jnp.transpose(glu_t, (1, 0, 2)),
        jnp.transpose(new_k_s.reshape(db, n_kv, hd, w_buf), (0, 3, 1, 2)),
        jnp.transpose(new_v_s.reshape(db, n_kv, hd, w_buf), (0, 3, 1, 2)),
        jnp.transpose(new_conv_s, (1, 0, 2)),
    )
```

```python
import functools
import math

import jax
import jax.numpy as jnp
import numpy as np
from jax import lax
from jax.experimental import pallas as pl
from jax.experimental.pallas import tpu as pltpu

F32 = jnp.float32
BF16 = jnp.bfloat16

EPS = 1e-6
NEG = -1e30
WINDOW = 128
REL_MAX_DIST = 128
HEAD_DIM = 64
LANES = 128
ROW_TILE = 256
ATTN_TILE = 2048
FF_CHUNK = 256
N_WSTEPS = 8
HALO = 32
SAMPLE_BLOCK = 16
VMEM_LIMIT = 56 * 1024 * 1024


def _dot(a, b):
    return jnp.dot(a, b, preferred_element_type=F32)


def _dot_t(a, b):
    return lax.dot_general(a, b, (((1,), (1,)), ((), ())), preferred_element_type=F32)


def _rms(x, g):
    return x * lax.rsqrt(jnp.mean(x * x, axis=-1, keepdims=True) + EPS) * g


def _pair_rms(x, g2, lo_mask):
    sq = x * x
    lo = jnp.sum(jnp.where(lo_mask, sq, 0.0), axis=-1, keepdims=True) * (1.0 / HEAD_DIM)
    hi = jnp.sum(jnp.where(lo_mask, 0.0, sq), axis=-1, keepdims=True) * (1.0 / HEAD_DIM)
    r = jnp.where(lo_mask, lax.rsqrt(lo + EPS), lax.rsqrt(hi + EPS))
    return x * r * g2


def _dep_zero(row):
    u = pltpu.bitcast(row, jnp.uint32)
    return pltpu.bitcast((u >> 16) >> 16, F32)


def _swiglu(xn, w1b, w3b, w2b, hid_ref, fillers=()):
    n_ff = w1b.shape[1]
    n_chunks = n_ff // FF_CHUNK
    due = {}
    for c in range(n_chunks):
        sl = slice(c * FF_CHUNK, (c + 1) * FF_CHUNK)
        a = _dot(xn, w1b[:, sl])
        b = _dot(xn, w3b[:, sl])
        for z in due.pop(c, ()):
            b = b + jnp.concatenate([z] * (FF_CHUNK // LANES), axis=1)
        for issue, when, thunk in fillers:
            if issue == c:
                due.setdefault(min(when, n_chunks), []).append(_dep_zero(thunk()))
        hid_ref[:, sl] = (a * jax.nn.sigmoid(a) * b).astype(BF16)
    out = _dot(hid_ref[...], w2b[...])
    for z in due.pop(n_chunks, ()):
        out = out + jnp.concatenate([z] * (out.shape[1] // LANES), axis=1)
    assert not due
    return out


def _store_chunk(dst, src, i):
    rows = src.shape[0]
    r = pl.multiple_of(i * rows, 16)
    dst[pl.ds(r, rows), :] = src[...].astype(BF16)


def _bias_table(bidx, rb_ref, head):
    tab = jnp.full(bidx.shape, NEG, F32)
    for b in range(rb_ref.shape[1]):
        tab = jnp.where(bidx == b, rb_ref[head, b], tab)
    return tab


def _ln_swish(y, g, b):
    mu = jnp.mean(y, axis=-1, keepdims=True)
    yc = y - mu
    var = jnp.mean(yc * yc, axis=-1, keepdims=True)
    z = yc * lax.rsqrt(var + EPS) * g + b
    return z * jax.nn.sigmoid(z)


def _ffn_inproj_body(xc_ref, xn_ref, meta_ref, xsm_ref, g1_ref, gm_ref, gq_ref, gk_ref, w1c, w3c, w2c, winc,
                     h_ref, q_ref, kv_ref, glu_ref, ga_ref, gc_ref,
                     w1b, w3b, w2b, winb, hid_ref, xn_st, u_st, xs_st, *, n_main, splits):
    i = pl.program_id(0)

    @pl.when(i == 0)
    def _():
        n_meta, n_c = meta_ref.shape[0], xs_st.shape[1] // LANES
        n_s = xsm_ref.shape[0] // n_c
        xs_st[...] = jnp.zeros_like(xs_st)
        xs_st[0:n_meta, :] = meta_ref[...]
        for c in range(n_c):
            xs_st[n_meta:n_meta + n_s, c * LANES:(c + 1) * LANES] = xsm_ref[pl.ds(c, n_s, stride=n_c), :]

    def small_tile():
        return xs_st[...]

    @pl.when(i < N_WSTEPS)
    def _():
        _store_chunk(w1b, w1c, i)
        _store_chunk(w3b, w3c, i)
        _store_chunk(w2b, w2c, i)
        _store_chunk(winb, winc, i)

    s = i - N_WSTEPS

    def project():
        u = u_st[...]
        o_q, o_k, o_v, o_a, o_b, o_ga, o_gc, o_end = splits
        lo_mask = lax.broadcasted_iota(jnp.int32, (1, LANES), 1) < HEAD_DIM
        gq2 = jnp.concatenate([gq_ref[...], gq_ref[...]], axis=1)
        gk2 = jnp.concatenate([gk_ref[...], gk_ref[...]], axis=1)
        zq = _dot(u, winb[:, o_q:o_k])
        for p in range((o_k - o_q) // LANES):
            sl = slice(p * LANES, (p + 1) * LANES)
            q_ref[:, sl] = (_pair_rms(zq[:, sl], gq2, lo_mask) * (HEAD_DIM ** -0.5)).astype(BF16)
        zkv = _dot(u, winb[:, o_k:o_a])
        kv_ref[:, :LANES] = _pair_rms(zkv[:, :LANES], gk2, lo_mask)
        kv_ref[:, LANES:] = zkv[:, LANES:]
        za = _dot(u, winb[:, o_a:o_b])
        zb = _dot(u, winb[:, o_b:o_ga])
        glu_ref[...] = za * jax.nn.sigmoid(zb)
        ga_ref[...] = jax.nn.sigmoid(_dot(u, winb[:, o_ga:o_gc]))
        gc_ref[...] = jax.nn.sigmoid(_dot(u, winb[:, o_gc:o_end]))

    def normalise_next():
        x_next = jnp.where(s + 1 < n_main, xn_ref[...], small_tile())
        xn_st[...] = _rms(x_next, g1_ref[...]).astype(BF16)

    @pl.when(i == 0)
    def _():
        u_st[...] = jnp.zeros_like(u_st)

    @pl.when(s == -1)
    def _():
        normalise_next()

    @pl.when((s >= 0) & (s <= n_main))
    def _():
        project()
        x = jnp.where(s < n_main, xc_ref[...], small_tile())
        h = x + 0.5 * _swiglu(xn_st[...], w1b, w3b, w2b, hid_ref)
        h_ref[...] = h
        u_st[...] = _rms(h, gm_ref[...]).astype(BF16)
        normalise_next()

    @pl.when(s == n_main + 1)
    def _():
        project()


def _ffn_inproj(x_main, x_meta, x_sample, g1, gm, gq, gk, w1, w3, w2, w_in, splits):
    m_rows, d = x_main.shape
    n_sample = x_sample.shape[0] * LANES // d
    assert x_meta.shape[0] % 8 == 0 and n_sample % 8 == 0 and x_sample.shape[1] == LANES
    assert x_meta.shape[0] + n_sample <= ROW_TILE
    n_ff = w1.shape[1]
    n_main = m_rows // ROW_TILE
    r_rows = m_rows + ROW_TILE
    t = ROW_TILE
    o_q, o_k, o_v, o_a, o_b, o_ga, o_gc, o_end = splits
    assert o_v - o_k == LANES and o_a - o_v == LANES and (o_k - o_q) % LANES == 0

    def cur_idx(i):
        return (jnp.clip(i - N_WSTEPS, 0, n_main - 1), 0)

    def next_idx(i):
        return (jnp.clip(i - N_WSTEPS + 1, 0, n_main - 1), 0)

    def h_idx(i):
        return (jnp.clip(i - N_WSTEPS, 0, n_main), 0)

    def proj_idx(i):
        return (jnp.clip(i - N_WSTEPS - 1, 0, n_main), 0)

    def w_idx(i):
        return (jnp.minimum(i, N_WSTEPS - 1), 0)

    const = lambda i: (0, 0)
    outs = [
        (jax.ShapeDtypeStruct((r_rows, d), F32), h_idx),
        (jax.ShapeDtypeStruct((r_rows, o_k - o_q), BF16), proj_idx),
        (jax.ShapeDtypeStruct((r_rows, o_a - o_k), F32), proj_idx),
        (jax.ShapeDtypeStruct((r_rows, o_b - o_a), F32), proj_idx),
        (jax.ShapeDtypeStruct((r_rows, o_gc - o_ga), F32), proj_idx),
        (jax.ShapeDtypeStruct((r_rows, o_end - o_gc), F32), proj_idx),
    ]
    return pl.pallas_call(
        functools.partial(_ffn_inproj_body, n_main=n_main, splits=splits),
        grid=(N_WSTEPS + n_main + 2,),
        in_specs=[
            pl.BlockSpec((t, d), cur_idx),
            pl.BlockSpec((t, d), next_idx),
            pl.BlockSpec(x_meta.shape, const),
            pl.BlockSpec(x_sample.shape, const),
            pl.BlockSpec((1, d), const),
            pl.BlockSpec((1, d), const),
            pl.BlockSpec((1, HEAD_DIM), const),
            pl.BlockSpec((1, HEAD_DIM), const),
            pl.BlockSpec((d // N_WSTEPS, n_ff), w_idx),
            pl.BlockSpec((d // N_WSTEPS, n_ff), w_idx),
            pl.BlockSpec((n_ff // N_WSTEPS, d), w_idx),
            pl.BlockSpec((d // N_WSTEPS, o_end), w_idx),
        ],
        out_specs=[pl.BlockSpec((t, s.shape[1]), idx) for s, idx in outs],
        out_shape=[s for s, _ in outs],
        scratch_shapes=[
            pltpu.VMEM((d, n_ff), BF16),
            pltpu.VMEM((d, n_ff), BF16),
            pltpu.VMEM((n_ff, d), BF16),
            pltpu.VMEM((d, o_end), BF16),
            pltpu.VMEM((t, n_ff), BF16),
            pltpu.VMEM((t, d), BF16),
            pltpu.VMEM((t, d), BF16),
            pltpu.VMEM((t, d), F32),
        ],
        compiler_params=pltpu.CompilerParams(
            dimension_semantics=("arbitrary",), vmem_limit_bytes=VMEM_LIMIT),
        name="ffn1_inproj",
    )(x_main, x_main, x_meta, x_sample, g1, gm, gq, gk, w1, w3, w2, w_in)


def _prompt_attn_body(rb_ref, sink_ref, q_ref, kvc_ref, kvp_ref, kvs_ref, bidx_ref, *rest,
                      tiles_per_seq, n_meta, n_heads, n_weights):
    w_refs, attn_ref, wb_refs = rest[:n_weights], rest[n_weights], rest[n_weights + 1:2 * n_weights + 1]
    kt_ref, vt_ref, bias_sc, kvm_ref = rest[2 * n_weights + 1:]
    for w_ref, wb_ref in zip(w_refs, wb_refs, strict=True):
        wb_ref[...] = w_ref[...].astype(BF16)

    i = pl.program_id(0)
    t = q_ref.shape[0]
    n_kv = 2
    grp = n_heads // n_kv

    @pl.when(i == 0)
    def _():
        kvm_ref[...] = jnp.zeros_like(kvm_ref)
        kvm_ref[WINDOW - n_meta:WINDOW, :] = kvs_ref[0:n_meta, :]
        col = lax.broadcasted_iota(jnp.int32, (WINDOW, 2 * WINDOW), 1)
        for hd in range(n_heads):
            tab = _bias_table(bidx_ref[...], rb_ref, hd)
            bias_sc[0, hd] = tab
            bias_sc[1, hd] = jnp.where(col < WINDOW - n_meta, NEG, tab)

    first = (i % tiles_per_seq) == 0
    lead = jnp.where(first, 1, 0)
    lo_mask = lax.broadcasted_iota(jnp.int32, (1, LANES), 1) < HEAD_DIM
    kv_prev = jnp.where(first, kvm_ref[...], kvp_ref[...])
    kv = jnp.concatenate([kv_prev, kvc_ref[...]], axis=0)
    k, v = kv[:, :LANES], kv[:, LANES:]
    kb, vb = k.astype(BF16), v.astype(BF16)
    kr = pltpu.roll(k, HEAD_DIM, axis=1).astype(BF16)
    vr = pltpu.roll(v, HEAD_DIM, axis=1).astype(BF16)
    zero = jnp.zeros((), BF16)
    k_lo = [jnp.where(lo_mask, kb, zero), jnp.where(lo_mask, kr, zero)]
    k_hi = [jnp.where(lo_mask, zero, kr), jnp.where(lo_mask, zero, kb)]
    v_lo = [jnp.where(lo_mask, vb, zero), jnp.where(lo_mask, vr, zero)]
    v_hi = [jnp.where(lo_mask, zero, vr), jnp.where(lo_mask, zero, vb)]

    for qb in range(t // WINDOW):
        rows = slice(qb * WINDOW, (qb + 1) * WINDOW)
        keys = slice(qb * WINDOW, qb * WINDOW + 2 * WINDOW)
        tab = lead if qb == 0 else 0
        for h in range(n_kv):
            k_st = jnp.concatenate([k_lo[h][keys], k_hi[h][keys]], axis=0)
            v_st = jnp.concatenate([v_lo[h][keys], v_hi[h][keys]], axis=0)
            c0 = h * grp * HEAD_DIM
            qq = jnp.concatenate([q_ref[rows, c0:c0 + LANES], q_ref[rows, c0 + LANES:c0 + 2 * LANES]], axis=0)
            s = _dot_t(qq, k_st)
            p_parts, inv_parts = [], []
            for g2 in range(2):
                p_row, inv_row = [], []
                for par in range(2):
                    hd = h * grp + 2 * g2 + par
                    sq = s[g2 * WINDOW:(g2 + 1) * WINDOW, par * 2 * WINDOW:(par + 1) * 2 * WINDOW] + bias_sc[tab, hd]
                    sink = sink_ref[0, hd]
                    m = jnp.maximum(jnp.max(sq, axis=-1, keepdims=True), sink)
                    p = jnp.exp(sq - m)
                    den = jnp.sum(p, axis=-1, keepdims=True) + jnp.exp(sink - m)
                    p_row.append(p.astype(BF16))
                    inv_row.append(1.0 / den)
                p_parts.append(jnp.concatenate(p_row, axis=1))
                inv_parts.append(jnp.where(lo_mask, inv_row[0], inv_row[1]))
            pm = jnp.concatenate(p_parts, axis=0)
            o = _dot(pm, v_st)
            for g2 in range(2):
                c = c0 + g2 * LANES
                attn_ref[rows, c:c + LANES] = (o[g2 * WINDOW:(g2 + 1) * WINDOW] * inv_parts[g2]).astype(attn_ref.dtype)

    @pl.when((i + 1) % tiles_per_seq == 0)
    def _():
        kt_ref[0] = kvc_ref[t - WINDOW:t, :LANES].T
        vt_ref[0] = kvc_ref[t - WINDOW:t, LANES:].T


def _prompt_attn(rel_bias, sinks2, q_all, kv_all, bidx, weights, m_rows, seq, n_meta):
    t = ATTN_TILE
    n_steps = m_rows // t
    assert m_rows % WINDOW == 0 and n_meta <= WINDOW
    n_heads = sinks2.shape[1]
    qw, kvw = q_all.shape[1], kv_all.shape[1]
    const = lambda i: (0, 0)
    row = lambda i: (i, 0)
    per_seq = lambda i: (i // (seq // t), 0, 0)
    smem = pl.BlockSpec(memory_space=pltpu.SMEM)

    def w_spec(w):
        share = 1 if (w.shape[0] // n_steps) % 16 == 0 else 2
        assert w.shape[0] % (n_steps // share) == 0 and (w.shape[0] * share // n_steps) % 16 == 0
        return pl.BlockSpec((w.shape[0] * share // n_steps, w.shape[1]), lambda i: (i // share, 0))

    w_specs = [w_spec(w) for w in weights]
    outs = pl.pallas_call(
        functools.partial(_prompt_attn_body, tiles_per_seq=seq // t, n_meta=n_meta, n_heads=n_heads,
                          n_weights=len(weights)),
        grid=(n_steps,),
        in_specs=[
            smem, smem,
            pl.BlockSpec((t, qw), row),
            pl.BlockSpec((t, kvw), row),
            pl.BlockSpec((WINDOW, kvw), lambda i: (jnp.maximum(i * (t // WINDOW) - 1, 0), 0)),
            pl.BlockSpec((WINDOW, kvw), lambda i: (m_rows // WINDOW, 0)),
            pl.BlockSpec((WINDOW, 2 * WINDOW), const),
        ] + w_specs,
        out_specs=[pl.BlockSpec((t, qw), row)] + w_specs + [pl.BlockSpec((1, WINDOW, LANES), per_seq)] * 2,
        out_shape=[jax.ShapeDtypeStruct((m_rows, qw), BF16)] + [jax.ShapeDtypeStruct(w.shape, BF16) for w in weights]
        + [jax.ShapeDtypeStruct((m_rows // seq, WINDOW, LANES), F32)] * 2,
        scratch_shapes=[
            pltpu.VMEM((2, n_heads, WINDOW, 2 * WINDOW), F32),
            pltpu.VMEM((WINDOW, kvw), F32),
        ],
        compiler_params=pltpu.CompilerParams(
            dimension_semantics=("arbitrary",), vmem_limit_bytes=VMEM_LIMIT),
        name="prompt_attn",
    )(rel_bias, sinks2, q_all, kv_all, kv_all, kv_all, bidx, *weights)
    return outs[0], outs[1:1 + len(weights)], outs[1 + len(weights)], outs[2 + len(weights)]


def _sample_mix_body(rb_ref, sink_ref, q_ref, kvn_ref, glun_ref, ck_ref, cv_ref, st_ref, bidx_ref,
                     wdw_ref, bdw_ref, lng_ref, lnb_ref, after_ref,
                     as_ref, cs_ref, nk_ref, nv_ref, nst_ref, bias_sc, *, n_heads, n_meta):
    i = pl.program_id(0)
    sb, w_buf = ck_ref.shape[0], ck_ref.shape[2]
    n_st = st_ref.shape[0]
    n_rows = sb * n_heads
    per_kv = n_heads // (LANES // HEAD_DIM)

    @pl.when(i == 0)
    def _():
        as_ref[...] = jnp.zeros_like(as_ref)
        cs_ref[...] = jnp.zeros_like(cs_ref)
        tiled = jnp.concatenate(
            [jnp.broadcast_to(jnp.concatenate([_bias_table(bidx_ref[...], rb_ref, hd)] * sb, axis=1),
                              (sb, sb * w_buf)) for hd in range(n_heads)], axis=0)
        row_seq = lax.rem(lax.broadcasted_iota(jnp.int32, tiled.shape, 0), sb)
        col_seq = lax.broadcasted_iota(jnp.int32, tiled.shape, 1) // w_buf
        bias_sc[...] = jnp.where(row_seq == col_seq, tiled, NEG)

    hrow = lax.broadcasted_iota(jnp.int32, (n_rows, 1), 0) // sb
    sink = jnp.zeros((n_rows, 1), F32)
    bias_new = jnp.zeros((n_rows, 1), F32)
    for hd in range(n_heads):
        sink = jnp.where(hrow == hd, sink_ref[0, hd], sink)
        bias_new = jnp.where(hrow == hd, rb_ref[hd, 0], bias_new)

    lo_mask = lax.broadcasted_iota(jnp.int32, (1, LANES), 1) < HEAD_DIM

    def half(x, src_hi, dst_hi):
        if src_hi != dst_hi:
            x = pltpu.roll(x, HEAD_DIM, 1)
        return jnp.where(lo_mask, 0.0, x) if dst_hi else jnp.where(lo_mask, x, 0.0)

    q = jnp.concatenate(
        [half(q_ref[:, (hd // 2) * LANES:(hd // 2 + 1) * LANES].astype(F32), hd % 2 == 1, hd // per_kv == 1)
         for hd in range(n_heads)], axis=0).astype(BF16)
    kn_rows = jnp.concatenate([kvn_ref[:, :LANES]] * n_heads, axis=0).astype(BF16)
    vn_rows = jnp.concatenate([kvn_ref[:, LANES:]] * n_heads, axis=0).astype(BF16)
    ck_all = jnp.concatenate([ck_ref[bb] for bb in range(sb)], axis=1).astype(BF16)
    cv_all = jnp.concatenate([cv_ref[bb] for bb in range(sb)], axis=1).astype(BF16)
    s_c = _dot(q, ck_all) + bias_sc[...]
    s_n = jnp.sum(q.astype(F32) * kn_rows.astype(F32), axis=-1, keepdims=True) + bias_new
    m = jnp.maximum(jnp.maximum(jnp.max(s_c, axis=-1, keepdims=True), s_n), sink)
    p_c = jnp.exp(s_c - m)
    p_n = jnp.exp(s_n - m)
    den = jnp.sum(p_c, axis=-1, keepdims=True) + p_n + jnp.exp(sink - m)
    o = _dot_t(p_c.astype(BF16), cv_all) + p_n.astype(BF16).astype(F32) * vn_rows.astype(F32)
    o = o / den
    pairs = []
    for p in range(n_heads // 2):
        lo_head = o[(2 * p) * sb:(2 * p + 1) * sb]
        hi_head = o[(2 * p + 1) * sb:(2 * p + 2) * sb]
        pairs.append(half(lo_head, (2 * p) // per_kv == 1, False) + half(hi_head, (2 * p + 1) // per_kv == 1, True))
    rows = pl.ds(pl.multiple_of(n_meta + i * sb, sb), sb)
    as_ref[rows, :] = jnp.concatenate(pairs, axis=1).astype(as_ref.dtype)

    g_new = glun_ref[...]
    y = g_new * wdw_ref[n_st:n_st + 1, :] + bdw_ref[...]
    for k in range(n_st):
        y = y + st_ref[k] * wdw_ref[k:k + 1, :]
    cs_ref[rows, :] = _ln_swish(y, lng_ref[...], lnb_ref[...]).astype(cs_ref.dtype)

    fill = jnp.zeros((w_buf - sb, LANES), F32)
    kn_t = jnp.concatenate([kvn_ref[:, :LANES], fill], axis=0).T
    vn_t = jnp.concatenate([kvn_ref[:, LANES:], fill], axis=0).T
    newest = lax.broadcasted_iota(jnp.int32, (LANES, w_buf), 1) == w_buf - 1
    for bb in range(sb):
        nk_ref[bb] = jnp.where(newest, kn_t[:, bb:bb + 1], pltpu.roll(ck_ref[bb], w_buf - 1, 1))
        nv_ref[bb] = jnp.where(newest, vn_t[:, bb:bb + 1], pltpu.roll(cv_ref[bb], w_buf - 1, 1))
    nst_ref[0:n_st - 1] = st_ref[1:n_st]
    nst_ref[n_st - 1] = g_new


def _sample_mix(rel_bias, sinks2, q_all, kv_all, glu_all, cache_kt, cache_vt, state_t, bidx_s, w_dw, b_dw, ln_g, ln_b,
                s0, n_meta, t_rows, after):
    db, kw, w_buf = cache_kt.shape
    n_st, ch = state_t.shape[0], state_t.shape[2]
    n_heads = sinks2.shape[1]
    aw = q_all.shape[1]
    sb = SAMPLE_BLOCK
    assert s0 % sb == 0 and n_meta % sb == 0 and n_meta + db <= t_rows
    blk = lambda i: (i, 0, 0)
    const = lambda i: (0, 0)
    tok = lambda i: (s0 // sb + i, 0)
    smem = pl.BlockSpec(memory_space=pltpu.SMEM)
    return pl.pallas_call(
        functools.partial(_sample_mix_body, n_heads=n_heads, n_meta=n_meta),
        grid=(db // sb,),
        in_specs=[
            smem, smem,
            pl.BlockSpec((sb, aw), tok),
            pl.BlockSpec((sb, 2 * kw), tok),
            pl.BlockSpec((sb, ch), tok),
            pl.BlockSpec((sb, kw, w_buf), blk),
            pl.BlockSpec((sb, kw, w_buf), blk),
            pl.BlockSpec((n_st, sb, ch), lambda i: (0, i, 0)),
            pl.BlockSpec((1, w_buf), const),
            pl.BlockSpec(w_dw.shape, const),
            pl.BlockSpec((1, ch), const),
            pl.BlockSpec((1, ch), const),
            pl.BlockSpec((1, ch), const),
            pl.BlockSpec(memory_space=pl.ANY),
        ],
        out_specs=[
            pl.BlockSpec((t_rows, aw), const),
            pl.BlockSpec((t_rows, ch), const),
            pl.BlockSpec((sb, kw, w_buf), blk),
            pl.BlockSpec((sb, kw, w_buf), blk),
            pl.BlockSpec((n_st, sb, ch), lambda i: (0, i, 0)),
        ],
        out_shape=[
            jax.ShapeDtypeStruct((t_rows, aw), BF16),
            jax.ShapeDtypeStruct((t_rows, ch), BF16),
            jax.ShapeDtypeStruct((db, kw, w_buf), F32),
            jax.ShapeDtypeStruct((db, kw, w_buf), F32),
            jax.ShapeDtypeStruct((n_st, db, ch), F32),
        ],
        scratch_shapes=[pltpu.VMEM((sb * n_heads, sb * w_buf), F32)],
        compiler_params=pltpu.CompilerParams(dimension_semantics=("arbitrary",)),
        name="sample_mix",
    )(rel_bias, sinks2, q_all, kv_all, glu_all, cache_kt, cache_vt, state_t, bidx_s, w_dw, b_dw, ln_g, ln_b, after)


def _conv_runs(cb_st, glu_ref, glum_ref, y_sc, wdw_ref, bdw_ref, first, n_runs):
    t = y_sc.shape[0]
    conv_w = wdw_ref.shape[0]
    n_ch = y_sc.shape[1]
    off = HALO - (conv_w - 1)
    n_a = (off + conv_w - 1) // 8 + 1
    units = [(lc, g) for lc in range(n_ch // LANES) for g in range(t // 8)]
    sizes = [len(units) // n_runs + (1 if r < len(units) % n_runs else 0) for r in range(n_runs)]
    staged = []

    def run(mine):
        if not staged:
            cb_st[0:HALO, :] = jnp.where(first, glum_ref[...], cb_st[0:HALO, :])
            cb_st[HALO:HALO + t, :] = glu_ref[...]
            staged.append(True)
        zs = {}

        def z(lc, s, g):
            if (lc, s, g) not in zs:
                ls = slice(lc * LANES, (lc + 1) * LANES)
                acc = None
                for a in range(n_a):
                    w = 8 * a + s - off
                    if 0 <= w < conv_w:
                        term = cb_st[8 * (g + a):8 * (g + a) + 8, ls] * wdw_ref[w:w + 1, ls]
                        acc = term if acc is None else acc + term
                zs[(lc, s, g)] = acc
            return zs[(lc, s, g)]

        dep = None
        for lc, g in mine:
            ls = slice(lc * LANES, (lc + 1) * LANES)
            acc = jnp.broadcast_to(bdw_ref[:, ls], (8, LANES)) + z(lc, 0, g)
            if dep is not None:
                acc = acc + dep
            for s in range(1, 8):
                acc = acc + jnp.concatenate([z(lc, s, g), z(lc, s, g + 1)], axis=0)[s:s + 8, :]
            y_sc[8 * g:8 * g + 8, ls] = acc
            dep = _dep_zero(acc)
        return dep[0:1, :]

    out, k = [], 0
    for n in sizes:
        out.append(functools.partial(run, units[k:k + n]))
        k += n
    return out


def _post_ffn_body(h_ref, am_ref, as_ref, cs_ref, ga_ref, gc_ref, g2_ref, glu_ref, glus_ref,
                   wdw_ref, bdw_ref, lng_ref, lnb_ref,
                   waob, wcob, woutb, w1b, w3b, w2b,
                   ym_ref, ys_ref, gt_ref,
                   hid_ref, cb_st, y_sc, conv_st, glum_ref, *, n_main, tiles_per_seq, n_meta):
    i = pl.program_id(0)
    t_rows = y_sc.shape[0]

    @pl.when(i == 0)
    def _():
        cb_st[...] = jnp.zeros_like(cb_st)
        glum_ref[...] = jnp.zeros_like(glum_ref)
        glum_ref[HALO - n_meta:HALO, :] = glus_ref[0:n_meta, :]

    t = i - 1
    tn = t + 1
    first = (tn < n_main) & (lax.rem(tn, tiles_per_seq) == 0)
    n_chunks = w1b.shape[1] // FF_CHUNK

    def finish_conv():
        conv_st[...] = _ln_swish(y_sc[...], lng_ref[...], lnb_ref[...]).astype(conv_st.dtype)
        cb_st[0:HALO, :] = cb_st[t_rows:t_rows + HALO, :]

    @pl.when(t == -1)
    def _():
        for f in _conv_runs(cb_st, glu_ref, glum_ref, y_sc, wdw_ref, bdw_ref, first, n_chunks):
            f()
        finish_conv()

    @pl.when(t >= 0)
    def _():
        on_main = t < n_main
        conv_f = _conv_runs(cb_st, glu_ref, glum_ref, y_sc, wdw_ref, bdw_ref, first, n_chunks)
        fillers = [(c, c + 1, f) for c, f in enumerate(conv_f)]

        at = jnp.where(on_main, am_ref[...], as_ref[...])
        cv = jnp.where(on_main, conv_st[...], cs_ref[...])
        a = _dot(at, waob[...])
        c = _dot(cv, wcob[...])
        mix = (ga_ref[...] * a + gc_ref[...] * c).astype(BF16)
        h2 = h_ref[...] + _dot(mix, woutb[...])
        xn = _rms(h2, g2_ref[...]).astype(BF16)
        y = h2 + 0.5 * _swiglu(xn, w1b, w3b, w2b, hid_ref, fillers)
        finish_conv()

        @pl.when(on_main)
        def _():
            ym_ref[...] = y

        @pl.when(t == n_main)
        def _():
            n_c = y.shape[1] // LANES
            n_s = ys_ref.shape[0] // n_c
            for c in range(n_c):
                ys_ref[pl.ds(c, n_s, stride=n_c), :] = y[n_meta:n_meta + n_s, c * LANES:(c + 1) * LANES]

    n_st = gt_ref.shape[0]
    for sq in range(gt_ref.shape[1]):
        @pl.when(tn == (sq + 1) * tiles_per_seq - 1)
        def _():
            for r in range(n_st):
                gt_ref[r, sq:sq + 1, :] = glu_ref[t_rows - n_st + r:t_rows - n_st + r + 1, :]


def _post_ffn(h_all, attn_main, attn_small, conv_small, ga_all, gc_all, g2, glu_all,
              w_dw, b_dw, ln_g, ln_b, w_ao, w_co, w_out, w1, w3, w2, seq, n_meta, n_sample):
    m_rows, aw = attn_main.shape
    assert m_rows % HALO == 0 and n_meta <= HALO and n_meta % 8 == 0 and n_sample % 8 == 0
    ch = glu_all.shape[1]
    d = h_all.shape[1]
    n_ff = w1.shape[1]
    t = ROW_TILE
    n_main = m_rows // t
    n_b = m_rows // seq
    n_st = w_dw.shape[0] - 1

    def main_idx(i):
        return (jnp.clip(i - 1, 0, n_main - 1), 0)

    def next_idx(i):
        return (jnp.minimum(i, n_main - 1), 0)

    def row_idx(i):
        return (jnp.maximum(i - 1, 0), 0)

    const = lambda i: (0, 0)
    resident = lambda w: pl.BlockSpec(w.shape, const, pipeline_mode=pl.Buffered(1))
    return pl.pallas_call(
        functools.partial(_post_ffn_body, n_main=n_main, tiles_per_seq=seq // t, n_meta=n_meta),
        grid=(n_main + 2,),
        in_specs=[
            pl.BlockSpec((t, d), row_idx),
            pl.BlockSpec((t, aw), main_idx),
            pl.BlockSpec((t, aw), const),
            pl.BlockSpec((t, ch), const),
            pl.BlockSpec((t, d), row_idx),
            pl.BlockSpec((t, d), row_idx),
            pl.BlockSpec((1, d), const),
            pl.BlockSpec((t, ch), next_idx),
            pl.BlockSpec((HALO, ch), lambda i: (m_rows // HALO, 0)),
            pl.BlockSpec(w_dw.shape, const),
            pl.BlockSpec((1, ch), const),
            pl.BlockSpec((1, ch), const),
            pl.BlockSpec((1, ch), const),
            resident(w_ao), resident(w_co), resident(w_out), resident(w1), resident(w3), resident(w2),
        ],
        out_specs=[pl.BlockSpec((t, d), main_idx), pl.BlockSpec((n_sample * (d // LANES), LANES), const),
                   pl.BlockSpec((n_st, n_b, ch), lambda i: (0, 0, 0))],
        out_shape=[jax.ShapeDtypeStruct((m_rows, d), F32),
                   jax.ShapeDtypeStruct((n_sample * (d // LANES), LANES), F32),
                   jax.ShapeDtypeStruct((n_st, n_b, ch), F32)],
        scratch_shapes=[
            pltpu.VMEM((t, n_ff), BF16),
            pltpu.VMEM((HALO + t, ch), F32),
            pltpu.VMEM((t, ch), F32),
            pltpu.VMEM((t, ch), BF16),
            pltpu.VMEM((HALO, ch), F32),
        ],
        compiler_params=pltpu.CompilerParams(
            dimension_semantics=("arbitrary",), vmem_limit_bytes=VMEM_LIMIT),
        name="post_ffn2",
    )(h_all, attn_main, attn_small, conv_small, ga_all, gc_all, g2, glu_all, glu_all,
      w_dw, b_dw, ln_g, ln_b, w_ao, w_co, w_out, w1, w3, w2)


def _t5_bucket(dist, n_buckets):
    max_exact = n_buckets // 2
    d = np.maximum(dist, 0)
    ratio = (np.log(np.maximum(d, 1).astype(np.float32) / np.float32(max_exact))
             / np.float32(math.log(REL_MAX_DIST / max_exact)))
    large = np.minimum(max_exact + (ratio * np.float32(n_buckets - max_exact)).astype(np.int32), n_buckets - 1)
    return np.where(d < max_exact, d, large).astype(np.int32)


def _bucket_or_masked(dist, n_buckets):
    ok = (dist >= 0) & (dist < WINDOW)
    return np.where(ok, _t5_bucket(dist, n_buckets), -1).astype(np.int32)


def _rows(x, start, n):
    return lax.slice_in_dim(x, start, start + n, axis=0)


def kernel(x_prompt, x_sample, cache_k, cache_v, state_conv, meta_tokens, ffn1_norm, ffn1_w1, ffn1_w3, ffn1_w2, mix_norm, w_in, q_norm, k_norm, rel_bias, sinks, w_attn_out, w_dw, b_dw, conv_ln_g, conv_ln_b, w_conv_out, w_out, ffn2_norm, ffn2_w1, ffn2_w3, ffn2_w2):
    n_b, seq, d = x_prompt.shape
    db = x_sample.shape[0]
    n_meta = meta_tokens.shape[0]
    n_heads = sinks.shape[0]
    w_buf, n_kv, hd = cache_k.shape[1], cache_k.shape[2], cache_k.shape[3]
    ch = w_dw.shape[1]
    n_st = state_conv.shape[1]
    n_buckets = rel_bias.shape[0]
    aw, kvw = n_heads * hd, n_kv * hd
    t = ROW_TILE
    m_rows = n_b * seq
    assert hd == HEAD_DIM and kvw == LANES and n_kv == 2 and n_heads == 8 and w_buf == WINDOW
    assert x_sample.shape[1] == 1 and seq % t == 0 and seq % ATTN_TILE == 0
    assert n_meta + db <= t and db % SAMPLE_BLOCK == 0
    assert n_meta <= HALO and n_meta <= WINDOW and n_st == w_dw.shape[0] - 1 and n_st <= HALO
    splits = tuple(int(v) for v in np.cumsum([0, aw, kvw, kvw, ch, ch, d, d]))
    assert splits[-1] == w_in.shape[1]

    row = lambda v: v.reshape(1, -1)
    x_main = x_prompt.reshape(m_rows, d)

    h_all, q_all, kv_all, glu_all, ga_all, gc_all = _ffn_inproj(
        x_main, meta_tokens, x_sample.reshape(db * (d // LANES), LANES), row(ffn1_norm), row(mix_norm),
        row(q_norm), row(k_norm), ffn1_w1, ffn1_w3, ffn1_w2, w_in, splits)

    sinks2 = row(sinks)
    rel_bias_t = rel_bias.T
    b_dw2, ln_g2, ln_b2 = row(b_dw), row(conv_ln_g), row(conv_ln_b)

    dist = np.arange(WINDOW)[:, None] + WINDOW - np.arange(2 * WINDOW)[None, :]
    bidx = jnp.asarray(_bucket_or_masked(dist, n_buckets))
    attn_main, w_bf16, k_t, v_t = _prompt_attn(
        rel_bias_t, sinks2, q_all, kv_all, bidx,
        (w_attn_out, w_conv_out, w_out, ffn2_w1, ffn2_w3, ffn2_w2), m_rows, seq, n_meta)

    bidx_s = jnp.asarray(_bucket_or_masked(w_buf - np.arange(w_buf)[None, :], n_buckets))
    attn_small, conv_small, new_k_s, new_v_s, new_conv_s = _sample_mix(
        rel_bias_t, sinks2, q_all, kv_all, glu_all,
        jnp.transpose(cache_k, (0, 2, 3, 1)).reshape(db, kvw, w_buf),
        jnp.transpose(cache_v, (0, 2, 3, 1)).reshape(db, kvw, w_buf),
        jnp.transpose(state_conv, (1, 0, 2)), bidx_s, w_dw, b_dw2, ln_g2, ln_b2, m_rows + n_meta, n_meta, t,
        after=k_t)

    y_main, y_small, glu_t = _post_ffn(h_all, attn_main, attn_small, conv_small, ga_all, gc_all, row(ffn2_norm),
                                       glu_all, w_dw, b_dw2, ln_g2, ln_b2, *w_bf16, seq, n_meta, db)

    return (
        y_main.reshape(n_b, seq, d),
        y_small.reshape(db, 1, d),
        jnp.transpose(k_t.reshape(n_b, n_kv, hd, WINDOW), (0, 3, 1, 2)),
        jnp.transpose(v_t.reshape(n_b, n_kv, hd, WINDOW), (0, 3, 1, 2)),
        jnp.transpose(glu_t, (1, 0, 2)),
        jnp.transpose(new_k_s.reshape(db, n_kv, hd, w_buf), (0, 3, 1, 2)),
        jnp.transpose(new_v_s.reshape(db, n_kv, hd, w_buf), (0, 3, 1, 2)),
        jnp.transpose(new_conv_s, (1, 0, 2)),
    )
```

```python
import functools
import math

import jax
import jax.numpy as jnp
import numpy as np
from jax import lax
from jax.experimental import pallas as pl
from jax.experimental.pallas import tpu as pltpu

F32 = jnp.float32
BF16 = jnp.bfloat16

EPS = 1e-6
NEG = -1e30
WINDOW = 128
REL_MAX_DIST = 128
HEAD_DIM = 64
LANES = 128
ROW_TILE = 256
ATTN_TILE = 1024
FF_CHUNK = 256
N_WSTEPS = 8
HALO = 32
SAMPLE_BLOCK = 16
RING = 3
VMEM_LIMIT = 56 * 1024 * 1024


def _dot(a, b):
    return jnp.dot(a, b, preferred_element_type=F32)


def _dot_t(a, b):
    return lax.dot_general(a, b, (((1,), (1,)), ((), ())), preferred_element_type=F32)


def _rms(x, g):
    return x * lax.rsqrt(jnp.mean(x * x, axis=-1, keepdims=True) + EPS) * g


def _pair_rms(x, g2, lo_mask):
    sq = x * x
    lo = jnp.sum(jnp.where(lo_mask, sq, 0.0), axis=-1, keepdims=True) * (1.0 / HEAD_DIM)
    hi = jnp.sum(jnp.where(lo_mask, 0.0, sq), axis=-1, keepdims=True) * (1.0 / HEAD_DIM)
    r = jnp.where(lo_mask, lax.rsqrt(lo + EPS), lax.rsqrt(hi + EPS))
    return x * r * g2


def _dep_zero(row):
    u = pltpu.bitcast(row, jnp.uint32)
    return pltpu.bitcast((u >> 16) >> 16, F32)


def _swiglu(xn, w1b, w3b, w2b, hid_ref, fillers=()):
    n_ff = w1b.shape[1]
    n_chunks = n_ff // FF_CHUNK
    due = {}
    for c in range(n_chunks):
        sl = slice(c * FF_CHUNK, (c + 1) * FF_CHUNK)
        a = _dot(xn, w1b[:, sl])
        b = _dot(xn, w3b[:, sl])
        for z in due.pop(c, ()):
            b = b + jnp.concatenate([z] * (FF_CHUNK // LANES), axis=1)
        for issue, when, thunk in fillers:
            if issue == c:
                due.setdefault(min(when, n_chunks), []).append(_dep_zero(thunk()))
        hid_ref[:, sl] = (a * jax.nn.sigmoid(a) * b).astype(BF16)
    out = _dot(hid_ref[...], w2b[...])
    for z in due.pop(n_chunks, ()):
        out = out + jnp.concatenate([z] * (out.shape[1] // LANES), axis=1)
    assert not due
    return out


def _store_chunk(dst, src, i):
    rows = src.shape[0]
    r = pl.multiple_of(i * rows, 16)
    dst[pl.ds(r, rows), :] = src[...].astype(BF16)


def _bias_table(bidx, rb_ref, head):
    tab = jnp.full(bidx.shape, NEG, F32)
    for b in range(rb_ref.shape[1]):
        tab = jnp.where(bidx == b, rb_ref[head, b], tab)
    return tab


def _ln_swish(y, g, b):
    mu = jnp.mean(y, axis=-1, keepdims=True)
    yc = y - mu
    var = jnp.mean(yc * yc, axis=-1, keepdims=True)
    z = yc * lax.rsqrt(var + EPS) * g + b
    return z * jax.nn.sigmoid(z)


def _ffn_inproj_body(xc_ref, xn_ref, meta_ref, xsm_ref, g1_ref, gm_ref, gq_ref, gk_ref, w1c, w3c, w2c, winc,
                     h_ref, q_ref, kv_ref, glu_ref, ga_ref, gc_ref,
                     w1b, w3b, w2b, winb, hid_ref, xn_st, u_st, xs_st, *, n_main, splits):
    i = pl.program_id(0)

    @pl.when(i == 0)
    def _():
        n_meta, n_c = meta_ref.shape[0], xs_st.shape[1] // LANES
        n_s = xsm_ref.shape[0] // n_c
        xs_st[...] = jnp.zeros_like(xs_st)
        xs_st[0:n_meta, :] = meta_ref[...]
        for c in range(n_c):
            xs_st[n_meta:n_meta + n_s, c * LANES:(c + 1) * LANES] = xsm_ref[pl.ds(c, n_s, stride=n_c), :]

    def small_tile():
        return xs_st[...]

    @pl.when(i < N_WSTEPS)
    def _():
        _store_chunk(w1b, w1c, i)
        _store_chunk(w3b, w3c, i)
        _store_chunk(w2b, w2c, i)
        _store_chunk(winb, winc, i)

    s = i - N_WSTEPS

    def project():
        u = u_st[...]
        o_q, o_k, o_v, o_a, o_b, o_ga, o_gc, o_end = splits
        lo_mask = lax.broadcasted_iota(jnp.int32, (1, LANES), 1) < HEAD_DIM
        gq2 = jnp.concatenate([gq_ref[...], gq_ref[...]], axis=1)
        gk2 = jnp.concatenate([gk_ref[...], gk_ref[...]], axis=1)
        zq = _dot(u, winb[:, o_q:o_k])
        for p in range((o_k - o_q) // LANES):
            sl = slice(p * LANES, (p + 1) * LANES)
            q_ref[:, sl] = (_pair_rms(zq[:, sl], gq2, lo_mask) * (HEAD_DIM ** -0.5)).astype(BF16)
        zkv = _dot(u, winb[:, o_k:o_a])
        kv_ref[:, :LANES] = _pair_rms(zkv[:, :LANES], gk2, lo_mask)
        kv_ref[:, LANES:] = zkv[:, LANES:]
        za = _dot(u, winb[:, o_a:o_b])
        zb = _dot(u, winb[:, o_b:o_ga])
        glu_ref[...] = za * jax.nn.sigmoid(zb)
        ga_ref[...] = jax.nn.sigmoid(_dot(u, winb[:, o_ga:o_gc]))
        gc_ref[...] = jax.nn.sigmoid(_dot(u, winb[:, o_gc:o_end]))

    def normalise_next():
        x_next = jnp.where(s + 1 < n_main, xn_ref[...], small_tile())
        xn_st[...] = _rms(x_next, g1_ref[...]).astype(BF16)

    @pl.when(i == 0)
    def _():
        u_st[...] = jnp.zeros_like(u_st)

    @pl.when(s == -1)
    def _():
        normalise_next()

    @pl.when((s >= 0) & (s <= n_main))
    def _():
        project()
        x = jnp.where(s < n_main, xc_ref[...], small_tile())
        h = x + 0.5 * _swiglu(xn_st[...], w1b, w3b, w2b, hid_ref)
        h_ref[...] = h
        u_st[...] = _rms(h, gm_ref[...]).astype(BF16)
        normalise_next()

    @pl.when(s == n_main + 1)
    def _():
        project()


def _ffn_inproj(x_main, x_meta, x_sample, g1, gm, gq, gk, w1, w3, w2, w_in, splits):
    m_rows, d = x_main.shape
    n_sample = x_sample.shape[0] * LANES // d
    assert x_meta.shape[0] % 8 == 0 and n_sample % 8 == 0 and x_sample.shape[1] == LANES
    assert x_meta.shape[0] + n_sample <= ROW_TILE
    n_ff = w1.shape[1]
    n_main = m_rows // ROW_TILE
    r_rows = m_rows + ROW_TILE
    t = ROW_TILE
    o_q, o_k, o_v, o_a, o_b, o_ga, o_gc, o_end = splits
    assert o_v - o_k == LANES and o_a - o_v == LANES and (o_k - o_q) % LANES == 0

    def cur_idx(i):
        return (jnp.clip(i - N_WSTEPS, 0, n_main - 1), 0)

    def next_idx(i):
        return (jnp.clip(i - N_WSTEPS + 1, 0, n_main - 1), 0)

    def h_idx(i):
        return (jnp.clip(i - N_WSTEPS, 0, n_main), 0)

    def proj_idx(i):
        return (jnp.clip(i - N_WSTEPS - 1, 0, n_main), 0)

    def w_idx(i):
        return (jnp.minimum(i, N_WSTEPS - 1), 0)

    const = lambda i: (0, 0)
    outs = [
        (jax.ShapeDtypeStruct((r_rows, d), F32), h_idx),
        (jax.ShapeDtypeStruct((r_rows, o_k - o_q), BF16), proj_idx),
        (jax.ShapeDtypeStruct((r_rows, o_a - o_k), F32), proj_idx),
        (jax.ShapeDtypeStruct((r_rows, o_b - o_a), F32), proj_idx),
        (jax.ShapeDtypeStruct((r_rows, o_gc - o_ga), F32), proj_idx),
        (jax.ShapeDtypeStruct((r_rows, o_end - o_gc), F32), proj_idx),
    ]
    return pl.pallas_call(
        functools.partial(_ffn_inproj_body, n_main=n_main, splits=splits),
        grid=(N_WSTEPS + n_main + 2,),
        in_specs=[
            pl.BlockSpec((t, d), cur_idx),
            pl.BlockSpec((t, d), next_idx),
            pl.BlockSpec(x_meta.shape, const),
            pl.BlockSpec(x_sample.shape, const),
            pl.BlockSpec((1, d), const),
            pl.BlockSpec((1, d), const),
            pl.BlockSpec((1, HEAD_DIM), const),
            pl.BlockSpec((1, HEAD_DIM), const),
            pl.BlockSpec((d // N_WSTEPS, n_ff), w_idx),
            pl.BlockSpec((d // N_WSTEPS, n_ff), w_idx),
            pl.BlockSpec((n_ff // N_WSTEPS, d), w_idx),
            pl.BlockSpec((d // N_WSTEPS, o_end), w_idx),
        ],
        out_specs=[pl.BlockSpec((t, s.shape[1]), idx) for s, idx in outs],
        out_shape=[s for s, _ in outs],
        scratch_shapes=[
            pltpu.VMEM((d, n_ff), BF16),
            pltpu.VMEM((d, n_ff), BF16),
            pltpu.VMEM((n_ff, d), BF16),
            pltpu.VMEM((d, o_end), BF16),
            pltpu.VMEM((t, n_ff), BF16),
            pltpu.VMEM((t, d), BF16),
            pltpu.VMEM((t, d), BF16),
            pltpu.VMEM((t, d), F32),
        ],
        compiler_params=pltpu.CompilerParams(
            dimension_semantics=("arbitrary",), vmem_limit_bytes=VMEM_LIMIT),
        name="ffn1_inproj",
    )(x_main, x_main, x_meta, x_sample, g1, gm, gq, gk, w1, w3, w2, w_in)


def _prompt_attn_body(rb_ref, sink_ref, q_ref, kvc_ref, kvp_ref, kvs_ref, bidx_ref, *rest,
                      tiles_per_seq, n_meta, n_heads, n_weights):
    w_refs, attn_ref, wb_refs = rest[:n_weights], rest[n_weights], rest[n_weights + 1:2 * n_weights + 1]
    kt_ref, vt_ref, bias_sc, kvm_ref = rest[2 * n_weights + 1:]
    for w_ref, wb_ref in zip(w_refs, wb_refs, strict=True):
        wb_ref[...] = w_ref[...].astype(BF16)

    i = pl.program_id(0)
    t = q_ref.shape[0]
    n_kv = 2
    grp = n_heads // n_kv

    @pl.when(i == 0)
    def _():
        kvm_ref[...] = jnp.zeros_like(kvm_ref)
        kvm_ref[WINDOW - n_meta:WINDOW, :] = kvs_ref[0:n_meta, :]
        col = lax.broadcasted_iota(jnp.int32, (WINDOW, 2 * WINDOW), 1)
        for hd in range(n_heads):
            tab = _bias_table(bidx_ref[...], rb_ref, hd)
            bias_sc[0, hd] = tab
            bias_sc[1, hd] = jnp.where(col < WINDOW - n_meta, NEG, tab)

    first = (i % tiles_per_seq) == 0
    lead = jnp.where(first, 1, 0)
    lo_mask = lax.broadcasted_iota(jnp.int32, (1, LANES), 1) < HEAD_DIM
    kv_prev = jnp.where(first, kvm_ref[...], kvp_ref[...])
    kv = jnp.concatenate([kv_prev, kvc_ref[...]], axis=0)
    k, v = kv[:, :LANES], kv[:, LANES:]
    kb, vb = k.astype(BF16), v.astype(BF16)
    kr = pltpu.roll(k, HEAD_DIM, axis=1).astype(BF16)
    vr = pltpu.roll(v, HEAD_DIM, axis=1).astype(BF16)
    zero = jnp.zeros((), BF16)
    k_lo = [jnp.where(lo_mask, kb, zero), jnp.where(lo_mask, kr, zero)]
    k_hi = [jnp.where(lo_mask, zero, kr), jnp.where(lo_mask, zero, kb)]
    v_lo = [jnp.where(lo_mask, vb, zero), jnp.where(lo_mask, vr, zero)]
    v_hi = [jnp.where(lo_mask, zero, vr), jnp.where(lo_mask, zero, vb)]

    for qb in range(t // WINDOW):
        rows = slice(qb * WINDOW, (qb + 1) * WINDOW)
        keys = slice(qb * WINDOW, qb * WINDOW + 2 * WINDOW)
        tab = lead if qb == 0 else 0
        for h in range(n_kv):
            k_st = jnp.concatenate([k_lo[h][keys], k_hi[h][keys]], axis=0)
            v_st = jnp.concatenate([v_lo[h][keys], v_hi[h][keys]], axis=0)
            c0 = h * grp * HEAD_DIM
            qq = jnp.concatenate([q_ref[rows, c0:c0 + LANES], q_ref[rows, c0 + LANES:c0 + 2 * LANES]], axis=0)
            s = _dot_t(qq, k_st)
            p_parts, inv_parts = [], []
            for g2 in range(2):
                p_row, inv_row = [], []
                for par in range(2):
                    hd = h * grp + 2 * g2 + par
                    sq = s[g2 * WINDOW:(g2 + 1) * WINDOW, par * 2 * WINDOW:(par + 1) * 2 * WINDOW] + bias_sc[tab, hd]
                    sink = sink_ref[0, hd]
                    m = jnp.maximum(jnp.max(sq, axis=-1, keepdims=True), sink)
                    p = jnp.exp(sq - m)
                    den = jnp.sum(p, axis=-1, keepdims=True) + jnp.exp(sink - m)
                    p_row.append(p.astype(BF16))
                    inv_row.append(1.0 / den)
                p_parts.append(jnp.concatenate(p_row, axis=1))
                inv_parts.append(jnp.where(lo_mask, inv_row[0], inv_row[1]))
            pm = jnp.concatenate(p_parts, axis=0)
            o = _dot(pm, v_st)
            for g2 in range(2):
                c = c0 + g2 * LANES
                attn_ref[rows, c:c + LANES] = (o[g2 * WINDOW:(g2 + 1) * WINDOW] * inv_parts[g2]).astype(attn_ref.dtype)

    @pl.when((i + 1) % tiles_per_seq == 0)
    def _():
        kt_ref[0] = kvc_ref[t - WINDOW:t, :LANES].T
        vt_ref[0] = kvc_ref[t - WINDOW:t, LANES:].T


def _prompt_attn(rel_bias, sinks2, q_all, kv_all, bidx, weights, m_rows, seq, n_meta):
    t = ATTN_TILE
    n_steps = m_rows // t
    assert m_rows % WINDOW == 0 and n_meta <= WINDOW
    n_heads = sinks2.shape[1]
    qw, kvw = q_all.shape[1], kv_all.shape[1]
    const = lambda i: (0, 0)
    row = lambda i: (i, 0)
    per_seq = lambda i: (i // (seq // t), 0, 0)
    smem = pl.BlockSpec(memory_space=pltpu.SMEM)

    def w_spec(w):
        share = 1 if (w.shape[0] // n_steps) % 16 == 0 else 2
        assert w.shape[0] % (n_steps // share) == 0 and (w.shape[0] * share // n_steps) % 16 == 0
        return pl.BlockSpec((w.shape[0] * share // n_steps, w.shape[1]), lambda i: (i // share, 0))

    w_specs = [w_spec(w) for w in weights]
    outs = pl.pallas_call(
        functools.partial(_prompt_attn_body, tiles_per_seq=seq // t, n_meta=n_meta, n_heads=n_heads,
                          n_weights=len(weights)),
        grid=(n_steps,),
        in_specs=[
            smem, smem,
            pl.BlockSpec((t, qw), row),
            pl.BlockSpec((t, kvw), row),
            pl.BlockSpec((WINDOW, kvw), lambda i: (jnp.maximum(i * (t // WINDOW) - 1, 0), 0)),
            pl.BlockSpec((WINDOW, kvw), lambda i: (m_rows // WINDOW, 0)),
            pl.BlockSpec((WINDOW, 2 * WINDOW), const),
        ] + w_specs,
        out_specs=[pl.BlockSpec((t, qw), row)] + w_specs + [pl.BlockSpec((1, WINDOW, LANES), per_seq)] * 2,
        out_shape=[jax.ShapeDtypeStruct((m_rows, qw), BF16)] + [jax.ShapeDtypeStruct(w.shape, BF16) for w in weights]
        + [jax.ShapeDtypeStruct((m_rows // seq, WINDOW, LANES), F32)] * 2,
        scratch_shapes=[
            pltpu.VMEM((2, n_heads, WINDOW, 2 * WINDOW), F32),
            pltpu.VMEM((WINDOW, kvw), F32),
        ],
        compiler_params=pltpu.CompilerParams(dimension_semantics=("arbitrary",)),
        name="prompt_attn",
    )(rel_bias, sinks2, q_all, kv_all, kv_all, kv_all, bidx, *weights)
    return outs[0], outs[1:1 + len(weights)], outs[1 + len(weights)], outs[2 + len(weights)]


def _sample_mix_body(rb_ref, sink_ref, q_ref, kvn_ref, glun_ref, ck_hbm, cv_hbm, st_ref, bidx_ref,
                     wdw_ref, bdw_ref, lng_ref, lnb_ref, after_ref,
                     as_ref, cs_ref, nk_ref, nv_ref, nst_ref, bias_sc, kbuf, vbuf, sems, *, n_heads, n_meta):
    i = pl.program_id(0)
    n_steps = pl.num_programs(0)
    sb, w_buf = kbuf.shape[1], kbuf.shape[3]
    n_st = st_ref.shape[0]
    n_rows = sb * n_heads
    per_kv = n_heads // (LANES // HEAD_DIM)

    def fetch(step):
        slot = step % RING
        rows = pl.ds(step * sb if isinstance(step, int) else pl.multiple_of(step * sb, sb), sb)
        return (pltpu.make_async_copy(ck_hbm.at[rows], kbuf.at[slot], sems.at[0, slot]),
                pltpu.make_async_copy(cv_hbm.at[rows], vbuf.at[slot], sems.at[1, slot]))

    @pl.when(i == 0)
    def _():
        for s0 in range(RING - 1):
            @pl.when(s0 < n_steps)
            def _():
                for c in fetch(s0):
                    c.start()

    @pl.when(i + RING - 1 < n_steps)
    def _():
        for c in fetch(i + RING - 1):
            c.start()

    for c in fetch(i):
        c.wait()
    ck_ref = kbuf.at[i % RING]
    cv_ref = vbuf.at[i % RING]

    @pl.when(i == 0)
    def _():
        as_ref[...] = jnp.zeros_like(as_ref)
        cs_ref[...] = jnp.zeros_like(cs_ref)
        tiled = jnp.concatenate(
            [jnp.broadcast_to(jnp.concatenate([_bias_table(bidx_ref[...], rb_ref, hd)] * sb, axis=1),
                              (sb, sb * w_buf)) for hd in range(n_heads)], axis=0)
        row_seq = lax.rem(lax.broadcasted_iota(jnp.int32, tiled.shape, 0), sb)
        col_seq = lax.broadcasted_iota(jnp.int32, tiled.shape, 1) // w_buf
        bias_sc[...] = jnp.where(row_seq == col_seq, tiled, NEG)

    hrow = lax.broadcasted_iota(jnp.int32, (n_rows, 1), 0) // sb
    sink = jnp.zeros((n_rows, 1), F32)
    bias_new = jnp.zeros((n_rows, 1), F32)
    for hd in range(n_heads):
        sink = jnp.where(hrow == hd, sink_ref[0, hd], sink)
        bias_new = jnp.where(hrow == hd, rb_ref[hd, 0], bias_new)

    lo_mask = lax.broadcasted_iota(jnp.int32, (1, LANES), 1) < HEAD_DIM

    def half(x, src_hi, dst_hi):
        if src_hi != dst_hi:
            x = pltpu.roll(x, HEAD_DIM, 1)
        return jnp.where(lo_mask, 0.0, x) if dst_hi else jnp.where(lo_mask, x, 0.0)

    q = jnp.concatenate(
        [half(q_ref[:, (hd // 2) * LANES:(hd // 2 + 1) * LANES].astype(F32), hd % 2 == 1, hd // per_kv == 1)
         for hd in range(n_heads)], axis=0).astype(BF16)
    kn_rows = jnp.concatenate([kvn_ref[:, :LANES]] * n_heads, axis=0).astype(BF16)
    vn_rows = jnp.concatenate([kvn_ref[:, LANES:]] * n_heads, axis=0).astype(BF16)
    ck_all = jnp.concatenate([ck_ref[bb] for bb in range(sb)], axis=1).astype(BF16)
    cv_all = jnp.concatenate([cv_ref[bb] for bb in range(sb)], axis=1).astype(BF16)
    s_c = _dot(q, ck_all) + bias_sc[...]
    s_n = jnp.sum(q.astype(F32) * kn_rows.astype(F32), axis=-1, keepdims=True) + bias_new
    m = jnp.maximum(jnp.maximum(jnp.max(s_c, axis=-1, keepdims=True), s_n), sink)
    p_c = jnp.exp(s_c - m)
    p_n = jnp.exp(s_n - m)
    den = jnp.sum(p_c, axis=-1, keepdims=True) + p_n + jnp.exp(sink - m)
    o = _dot_t(p_c.astype(BF16), cv_all) + p_n.astype(BF16).astype(F32) * vn_rows.astype(F32)
    o = o / den
    pairs = []
    for p in range(n_heads // 2):
        lo_head = o[(2 * p) * sb:(2 * p + 1) * sb]
        hi_head = o[(2 * p + 1) * sb:(2 * p + 2) * sb]
        pairs.append(half(lo_head, (2 * p) // per_kv == 1, False) + half(hi_head, (2 * p + 1) // per_kv == 1, True))
    rows = pl.ds(pl.multiple_of(n_meta + i * sb, sb), sb)
    as_ref[rows, :] = jnp.concatenate(pairs, axis=1).astype(as_ref.dtype)

    g_new = glun_ref[...]
    y = g_new * wdw_ref[n_st:n_st + 1, :] + bdw_ref[...]
    for k in range(n_st):
        y = y + st_ref[k] * wdw_ref[k:k + 1, :]
    cs_ref[rows, :] = _ln_swish(y, lng_ref[...], lnb_ref[...]).astype(cs_ref.dtype)

    fill = jnp.zeros((w_buf - sb, LANES), F32)
    kn_t = jnp.concatenate([kvn_ref[:, :LANES], fill], axis=0).T
    vn_t = jnp.concatenate([kvn_ref[:, LANES:], fill], axis=0).T
    newest = lax.broadcasted_iota(jnp.int32, (LANES, w_buf), 1) == w_buf - 1
    for bb in range(sb):
        nk_ref[bb] = jnp.where(newest, kn_t[:, bb:bb + 1], pltpu.roll(ck_ref[bb], w_buf - 1, 1))
        nv_ref[bb] = jnp.where(newest, vn_t[:, bb:bb + 1], pltpu.roll(cv_ref[bb], w_buf - 1, 1))
    nst_ref[0:n_st - 1] = st_ref[1:n_st]
    nst_ref[n_st - 1] = g_new


def _sample_mix(rel_bias, sinks2, q_all, kv_all, glu_all, cache_kt, cache_vt, state_t, bidx_s, w_dw, b_dw, ln_g, ln_b,
                s0, n_meta, t_rows, after):
    db, kw, w_buf = cache_kt.shape
    n_st, ch = state_t.shape[0], state_t.shape[2]
    n_heads = sinks2.shape[1]
    aw = q_all.shape[1]
    sb = SAMPLE_BLOCK
    assert s0 % sb == 0 and n_meta % sb == 0 and n_meta + db <= t_rows
    blk = lambda i: (i, 0, 0)
    const = lambda i: (0, 0)
    tok = lambda i: (s0 // sb + i, 0)
    smem = pl.BlockSpec(memory_space=pltpu.SMEM)
    return pl.pallas_call(
        functools.partial(_sample_mix_body, n_heads=n_heads, n_meta=n_meta),
        grid=(db // sb,),
        in_specs=[
            smem, smem,
            pl.BlockSpec((sb, aw), tok),
            pl.BlockSpec((sb, 2 * kw), tok),
            pl.BlockSpec((sb, ch), tok),
            pl.BlockSpec(memory_space=pl.ANY),
            pl.BlockSpec(memory_space=pl.ANY),
            pl.BlockSpec((n_st, sb, ch), lambda i: (0, i, 0)),
            pl.BlockSpec((1, w_buf), const),
            pl.BlockSpec(w_dw.shape, const),
            pl.BlockSpec((1, ch), const),
            pl.BlockSpec((1, ch), const),
            pl.BlockSpec((1, ch), const),
            pl.BlockSpec(memory_space=pl.ANY),
        ],
        out_specs=[
            pl.BlockSpec((t_rows, aw), const),
            pl.BlockSpec((t_rows, ch), const),
            pl.BlockSpec((sb, kw, w_buf), blk),
            pl.BlockSpec((sb, kw, w_buf), blk),
            pl.BlockSpec((n_st, sb, ch), lambda i: (0, i, 0)),
        ],
        out_shape=[
            jax.ShapeDtypeStruct((t_rows, aw), BF16),
            jax.ShapeDtypeStruct((t_rows, ch), BF16),
            jax.ShapeDtypeStruct((db, kw, w_buf), F32),
            jax.ShapeDtypeStruct((db, kw, w_buf), F32),
            jax.ShapeDtypeStruct((n_st, db, ch), F32),
        ],
        scratch_shapes=[
            pltpu.VMEM((sb * n_heads, sb * w_buf), F32),
            pltpu.VMEM((RING, sb, kw, w_buf), F32),
            pltpu.VMEM((RING, sb, kw, w_buf), F32),
            pltpu.SemaphoreType.DMA((2, RING)),
        ],
        compiler_params=pltpu.CompilerParams(dimension_semantics=("arbitrary",)),
        name="sample_mix",
    )(rel_bias, sinks2, q_all, kv_all, glu_all, cache_kt, cache_vt, state_t, bidx_s, w_dw, b_dw, ln_g, ln_b, after)


def _conv_runs(cb_st, glu_ref, glum_ref, y_sc, wdw_ref, bdw_ref, first, n_runs):
    t = y_sc.shape[0]
    conv_w = wdw_ref.shape[0]
    n_ch = y_sc.shape[1]
    off = HALO - (conv_w - 1)
    n_a = (off + conv_w - 1) // 8 + 1
    units = [(lc, g) for lc in range(n_ch // LANES) for g in range(t // 8)]
    sizes = [len(units) // n_runs + (1 if r < len(units) % n_runs else 0) for r in range(n_runs)]
    staged = []

    def run(mine):
        if not staged:
            cb_st[0:HALO, :] = jnp.where(first, glum_ref[...], cb_st[0:HALO, :])
            cb_st[HALO:HALO + t, :] = glu_ref[...]
            staged.append(True)
        zs = {}

        def z(lc, s, g):
            if (lc, s, g) not in zs:
                ls = slice(lc * LANES, (lc + 1) * LANES)
                acc = None
                for a in range(n_a):
                    w = 8 * a + s - off
                    if 0 <= w < conv_w:
                        term = cb_st[8 * (g + a):8 * (g + a) + 8, ls] * wdw_ref[w:w + 1, ls]
                        acc = term if acc is None else acc + term
                zs[(lc, s, g)] = acc
            return zs[(lc, s, g)]

        dep = None
        for lc, g in mine:
            ls = slice(lc * LANES, (lc + 1) * LANES)
            acc = jnp.broadcast_to(bdw_ref[:, ls], (8, LANES)) + z(lc, 0, g)
            if dep is not None:
                acc = acc + dep
            for s in range(1, 8):
                acc = acc + jnp.concatenate([z(lc, s, g), z(lc, s, g + 1)], axis=0)[s:s + 8, :]
            y_sc[8 * g:8 * g + 8, ls] = acc
            dep = _dep_zero(acc)
        return dep[0:1, :]

    out, k = [], 0
    for n in sizes:
        out.append(functools.partial(run, units[k:k + n]))
        k += n
    return out


def _post_ffn_body(h_ref, am_ref, as_ref, cs_ref, ga_ref, gc_ref, g2_ref, glu_ref, glus_ref,
                   wdw_ref, bdw_ref, lng_ref, lnb_ref,
                   waob, wcob, woutb, w1b, w3b, w2b,
                   ym_ref, ys_ref, gt_ref,
                   hid_ref, cb_st, y_sc, conv_st, glum_ref, *, n_main, tiles_per_seq, n_meta):
    i = pl.program_id(0)
    t_rows = y_sc.shape[0]

    @pl.when(i == 0)
    def _():
        cb_st[...] = jnp.zeros_like(cb_st)
        glum_ref[...] = jnp.zeros_like(glum_ref)
        glum_ref[HALO - n_meta:HALO, :] = glus_ref[0:n_meta, :]

    t = i - 1
    tn = t + 1
    first = (tn < n_main) & (lax.rem(tn, tiles_per_seq) == 0)
    n_chunks = w1b.shape[1] // FF_CHUNK

    def finish_conv():
        conv_st[...] = _ln_swish(y_sc[...], lng_ref[...], lnb_ref[...]).astype(conv_st.dtype)
        cb_st[0:HALO, :] = cb_st[t_rows:t_rows + HALO, :]

    @pl.when(t == -1)
    def _():
        for f in _conv_runs(cb_st, glu_ref, glum_ref, y_sc, wdw_ref, bdw_ref, first, n_chunks):
            f()
        finish_conv()

    @pl.when(t >= 0)
    def _():
        on_main = t < n_main
        conv_f = _conv_runs(cb_st, glu_ref, glum_ref, y_sc, wdw_ref, bdw_ref, first, n_chunks)
        fillers = [(c, c + 1, f) for c, f in enumerate(conv_f)]

        at = jnp.where(on_main, am_ref[...], as_ref[...])
        cv = jnp.where(on_main, conv_st[...], cs_ref[...])
        a = _dot(at, waob[...])
        c = _dot(cv, wcob[...])
        mix = (ga_ref[...] * a + gc_ref[...] * c).astype(BF16)
        h2 = h_ref[...] + _dot(mix, woutb[...])
        xn = _rms(h2, g2_ref[...]).astype(BF16)
        y = h2 + 0.5 * _swiglu(xn, w1b, w3b, w2b, hid_ref, fillers)
        finish_conv()

        @pl.when(on_main)
        def _():
            ym_ref[...] = y

        @pl.when(t == n_main)
        def _():
            n_c = y.shape[1] // LANES
            n_s = ys_ref.shape[0] // n_c
            for c in range(n_c):
                ys_ref[pl.ds(c, n_s, stride=n_c), :] = y[n_meta:n_meta + n_s, c * LANES:(c + 1) * LANES]

    n_st = gt_ref.shape[0]
    for sq in range(gt_ref.shape[1]):
        @pl.when(tn == (sq + 1) * tiles_per_seq - 1)
        def _():
            for r in range(n_st):
                gt_ref[r, sq:sq + 1, :] = glu_ref[t_rows - n_st + r:t_rows - n_st + r + 1, :]


def _post_ffn(h_all, attn_main, attn_small, conv_small, ga_all, gc_all, g2, glu_all,
              w_dw, b_dw, ln_g, ln_b, w_ao, w_co, w_out, w1, w3, w2, seq, n_meta, n_sample):
    m_rows, aw = attn_main.shape
    assert m_rows % HALO == 0 and n_meta <= HALO and n_meta % 8 == 0 and n_sample % 8 == 0
    ch = glu_all.shape[1]
    d = h_all.shape[1]
    n_ff = w1.shape[1]
    t = ROW_TILE
    n_main = m_rows // t
    n_b = m_rows // seq
    n_st = w_dw.shape[0] - 1

    def main_idx(i):
        return (jnp.clip(i - 1, 0, n_main - 1), 0)

    def next_idx(i):
        return (jnp.minimum(i, n_main - 1), 0)

    def row_idx(i):
        return (jnp.maximum(i - 1, 0), 0)

    const = lambda i: (0, 0)
    resident = lambda w: pl.BlockSpec(w.shape, const, pipeline_mode=pl.Buffered(1))
    return pl.pallas_call(
        functools.partial(_post_ffn_body, n_main=n_main, tiles_per_seq=seq // t, n_meta=n_meta),
        grid=(n_main + 2,),
        in_specs=[
            pl.BlockSpec((t, d), row_idx),
            pl.BlockSpec((t, aw), main_idx),
            pl.BlockSpec((t, aw), const),
            pl.BlockSpec((t, ch), const),
            pl.BlockSpec((t, d), row_idx),
            pl.BlockSpec((t, d), row_idx),
            pl.BlockSpec((1, d), const),
            pl.BlockSpec((t, ch), next_idx),
            pl.BlockSpec((HALO, ch), lambda i: (m_rows // HALO, 0)),
            pl.BlockSpec(w_dw.shape, const),
            pl.BlockSpec((1, ch), const),
            pl.BlockSpec((1, ch), const),
            pl.BlockSpec((1, ch), const),
            resident(w_ao), resident(w_co), resident(w_out), resident(w1), resident(w3), resident(w2),
        ],
        out_specs=[pl.BlockSpec((t, d), main_idx), pl.BlockSpec((n_sample * (d // LANES), LANES), const),
                   pl.BlockSpec((n_st, n_b, ch), lambda i: (0, 0, 0))],
        out_shape=[jax.ShapeDtypeStruct((m_rows, d), F32),
                   jax.ShapeDtypeStruct((n_sample * (d // LANES), LANES), F32),
                   jax.ShapeDtypeStruct((n_st, n_b, ch), F32)],
        scratch_shapes=[
            pltpu.VMEM((t, n_ff), BF16),
            pltpu.VMEM((HALO + t, ch), F32),
            pltpu.VMEM((t, ch), F32),
            pltpu.VMEM((t, ch), BF16),
            pltpu.VMEM((HALO, ch), F32),
        ],
        compiler_params=pltpu.CompilerParams(
            dimension_semantics=("arbitrary",), vmem_limit_bytes=VMEM_LIMIT),
        name="post_ffn2",
    )(h_all, attn_main, attn_small, conv_small, ga_all, gc_all, g2, glu_all, glu_all,
      w_dw, b_dw, ln_g, ln_b, w_ao, w_co, w_out, w1, w3, w2)


def _t5_bucket(dist, n_buckets):
    max_exact = n_buckets // 2
    d = np.maximum(dist, 0)
    ratio = (np.log(np.maximum(d, 1).astype(np.float32) / np.float32(max_exact))
             / np.float32(math.log(REL_MAX_DIST / max_exact)))
    large = np.minimum(max_exact + (ratio * np.float32(n_buckets - max_exact)).astype(np.int32), n_buckets - 1)
    return np.where(d < max_exact, d, large).astype(np.int32)


def _bucket_or_masked(dist, n_buckets):
    ok = (dist >= 0) & (dist < WINDOW)
    return np.where(ok, _t5_bucket(dist, n_buckets), -1).astype(np.int32)


def _rows(x, start, n):
    return lax.slice_in_dim(x, start, start + n, axis=0)


def kernel(x_prompt, x_sample, cache_k, cache_v, state_conv, meta_tokens, ffn1_norm, ffn1_w1, ffn1_w3, ffn1_w2, mix_norm, w_in, q_norm, k_norm, rel_bias, sinks, w_attn_out, w_dw, b_dw, conv_ln_g, conv_ln_b, w_conv_out, w_out, ffn2_norm, ffn2_w1, ffn2_w3, ffn2_w2):
    n_b, seq, d = x_prompt.shape
    db = x_sample.shape[0]
    n_meta = meta_tokens.shape[0]
    n_heads = sinks.shape[0]
    w_buf, n_kv, hd = cache_k.shape[1], cache_k.shape[2], cache_k.shape[3]
    ch = w_dw.shape[1]
    n_st = state_conv.shape[1]
    n_buckets = rel_bias.shape[0]
    aw, kvw = n_heads * hd, n_kv * hd
    t = ROW_TILE
    m_rows = n_b * seq
    assert hd == HEAD_DIM and kvw == LANES and n_kv == 2 and n_heads == 8 and w_buf == WINDOW
    assert x_sample.shape[1] == 1 and seq % t == 0 and seq % ATTN_TILE == 0
    assert n_meta + db <= t and db % SAMPLE_BLOCK == 0
    assert n_meta <= HALO and n_meta <= WINDOW and n_st == w_dw.shape[0] - 1 and n_st <= HALO
    splits = tuple(int(v) for v in np.cumsum([0, aw, kvw, kvw, ch, ch, d, d]))
    assert splits[-1] == w_in.shape[1]

    row = lambda v: v.reshape(1, -1)
    x_main = x_prompt.reshape(m_rows, d)

    h_all, q_all, kv_all, glu_all, ga_all, gc_all = _ffn_inproj(
        x_main, meta_tokens, x_sample.reshape(db * (d // LANES), LANES), row(ffn1_norm), row(mix_norm),
        row(q_norm), row(k_norm), ffn1_w1, ffn1_w3, ffn1_w2, w_in, splits)

    sinks2 = row(sinks)
    rel_bias_t = rel_bias.T
    b_dw2, ln_g2, ln_b2 = row(b_dw), row(conv_ln_g), row(conv_ln_b)

    dist = np.arange(WINDOW)[:, None] + WINDOW - np.arange(2 * WINDOW)[None, :]
    bidx = jnp.asarray(_bucket_or_masked(dist, n_buckets))
    attn_main, w_bf16, k_t, v_t = _prompt_attn(
        rel_bias_t, sinks2, q_all, kv_all, bidx,
        (w_attn_out, w_conv_out, w_out, ffn2_w1, ffn2_w3, ffn2_w2), m_rows, seq, n_meta)

    bidx_s = jnp.asarray(_bucket_or_masked(w_buf - np.arange(w_buf)[None, :], n_buckets))
    attn_small, conv_small, new_k_s, new_v_s, new_conv_s = _sample_mix(
        rel_bias_t, sinks2, q_all, kv_all, glu_all,
        jnp.transpose(cache_k, (0, 2, 3, 1)).reshape(db, kvw, w_buf),
        jnp.transpose(cache_v, (0, 2, 3, 1)).reshape(db, kvw, w_buf),
        jnp.transpose(state_conv, (1, 0, 2)), bidx_s, w_dw, b_dw2, ln_g2, ln_b2, m_rows + n_meta, n_meta, t,
        after=k_t)

    y_main, y_small, glu_t = _post_ffn(h_all, attn_main, attn_small, conv_small, ga_all, gc_all, row(ffn2_norm),
                                       glu_all, w_dw, b_dw2, ln_g2, ln_b2, *w_bf16, seq, n_meta, db)

    return (
        y_main.reshape(n_b, seq, d),
        y_small.reshape(db, 1, d),
        jnp.transpose(k_t.reshape(n_b, n_kv, hd, WINDOW), (0, 3, 1, 2)),
        jnp.transpose(v_t.reshape(n_b, n_kv, hd, WINDOW), (0, 3, 1, 2)),
        jnp.transpose(glu_t, (1, 0, 2)),
        jnp.transpose(new_k_s.reshape(db, n_kv, hd, w_buf), (0, 3, 1, 2)),
        jnp.transpose(new_v_s.reshape(db, n_kv, hd, w_buf), (0, 3, 1, 2)),
        jnp.transpose(new_conv_s, (1, 0, 2)),
    )
```

```python
import functools
import math

import jax
import jax.numpy as jnp
import numpy as np
from jax import lax
from jax.experimental import pallas as pl
from jax.experimental.pallas import tpu as pltpu

F32 = jnp.float32
BF16 = jnp.bfloat16

EPS = 1e-6
NEG = -1e30
WINDOW = 128
REL_MAX_DIST = 128
HEAD_DIM = 64
LANES = 128
ROW_TILE = 256
ATTN_TILE = 1024
FF_CHUNK = 256
N_WSTEPS = 8
HALO = 32
SAMPLE_BLOCK = 16
VMEM_LIMIT = 56 * 1024 * 1024


def _dot(a, b):
    return jnp.dot(a, b, preferred_element_type=F32)


def _dot_t(a, b):
    return lax.dot_general(a, b, (((1,), (1,)), ((), ())), preferred_element_type=F32)


def _rms(x, g):
    return x * lax.rsqrt(jnp.mean(x * x, axis=-1, keepdims=True) + EPS) * g


def _pair_rms(x, g2, lo_mask):
    sq = x * x
    lo = jnp.sum(jnp.where(lo_mask, sq, 0.0), axis=-1, keepdims=True) * (1.0 / HEAD_DIM)
    hi = jnp.sum(jnp.where(lo_mask, 0.0, sq), axis=-1, keepdims=True) * (1.0 / HEAD_DIM)
    r = jnp.where(lo_mask, lax.rsqrt(lo + EPS), lax.rsqrt(hi + EPS))
    return x * r * g2


def _dep_zero(row):
    u = pltpu.bitcast(row, jnp.uint32)
    return pltpu.bitcast((u >> 16) >> 16, F32)


def _swiglu(xn, w1b, w3b, w2b, hid_ref, fillers=()):
    n_ff = w1b.shape[1]
    n_chunks = n_ff // FF_CHUNK
    due = {}
    for c in range(n_chunks):
        sl = slice(c * FF_CHUNK, (c + 1) * FF_CHUNK)
        a = _dot(xn, w1b[:, sl])
        b = _dot(xn, w3b[:, sl])
        for z in due.pop(c, ()):
            b = b + jnp.concatenate([z] * (FF_CHUNK // LANES), axis=1)
        for issue, when, thunk in fillers:
            if issue == c:
                due.setdefault(min(when, n_chunks), []).append(_dep_zero(thunk()))
        hid_ref[:, sl] = (a * jax.nn.sigmoid(a) * b).astype(BF16)
    out = _dot(hid_ref[...], w2b[...])
    for z in due.pop(n_chunks, ()):
        out = out + jnp.concatenate([z] * (out.shape[1] // LANES), axis=1)
    assert not due
    return out


def _store_chunk(dst, src, i):
    rows = src.shape[0]
    r = pl.multiple_of(i * rows, 16)
    dst[pl.ds(r, rows), :] = src[...].astype(BF16)


def _bias_table(bidx, rb_ref, head):
    tab = jnp.full(bidx.shape, NEG, F32)
    for b in range(rb_ref.shape[1]):
        tab = jnp.where(bidx == b, rb_ref[head, b], tab)
    return tab


def _ln_swish(y, g, b):
    mu = jnp.mean(y, axis=-1, keepdims=True)
    yc = y - mu
    var = jnp.mean(yc * yc, axis=-1, keepdims=True)
    z = yc * lax.rsqrt(var + EPS) * g + b
    return z * jax.nn.sigmoid(z)


def _ffn_inproj_body(xc_ref, xn_ref, meta_ref, xsm_ref, g1_ref, gm_ref, gq_ref, gk_ref, w1c, w3c, w2c, winc,
                     h_ref, q_ref, kv_ref, glu_ref, ga_ref, gc_ref,
                     w1b, w3b, w2b, winb, hid_ref, xn_st, u_st, xs_st, *, n_main, splits):
    i = pl.program_id(0)

    @pl.when(i == 0)
    def _():
        n_meta, n_c = meta_ref.shape[0], xs_st.shape[1] // LANES
        n_s = xsm_ref.shape[0] // n_c
        xs_st[...] = jnp.zeros_like(xs_st)
        xs_st[0:n_meta, :] = meta_ref[...]
        for c in range(n_c):
            xs_st[n_meta:n_meta + n_s, c * LANES:(c + 1) * LANES] = xsm_ref[pl.ds(c, n_s, stride=n_c), :]

    def small_tile():
        return xs_st[...]

    @pl.when(i < N_WSTEPS)
    def _():
        _store_chunk(w1b, w1c, i)
        _store_chunk(w3b, w3c, i)
        _store_chunk(w2b, w2c, i)
        _store_chunk(winb, winc, i)

    s = i - N_WSTEPS

    def project():
        u = u_st[...]
        o_q, o_k, o_v, o_a, o_b, o_ga, o_gc, o_end = splits
        lo_mask = lax.broadcasted_iota(jnp.int32, (1, LANES), 1) < HEAD_DIM
        gq2 = jnp.concatenate([gq_ref[...], gq_ref[...]], axis=1)
        gk2 = jnp.concatenate([gk_ref[...], gk_ref[...]], axis=1)
        zq = _dot(u, winb[:, o_q:o_k])
        for p in range((o_k - o_q) // LANES):
            sl = slice(p * LANES, (p + 1) * LANES)
            q_ref[:, sl] = (_pair_rms(zq[:, sl], gq2, lo_mask) * (HEAD_DIM ** -0.5)).astype(BF16)
        zkv = _dot(u, winb[:, o_k:o_a])
        kv_ref[:, :LANES] = _pair_rms(zkv[:, :LANES], gk2, lo_mask)
        kv_ref[:, LANES:] = zkv[:, LANES:]
        za = _dot(u, winb[:, o_a:o_b])
        zb = _dot(u, winb[:, o_b:o_ga])
        glu_ref[...] = za * jax.nn.sigmoid(zb)
        ga_ref[...] = jax.nn.sigmoid(_dot(u, winb[:, o_ga:o_gc]))
        gc_ref[...] = jax.nn.sigmoid(_dot(u, winb[:, o_gc:o_end]))

    def normalise_next():
        x_next = jnp.where(s + 1 < n_main, xn_ref[...], small_tile())
        xn_st[...] = _rms(x_next, g1_ref[...]).astype(BF16)

    @pl.when(i == 0)
    def _():
        u_st[...] = jnp.zeros_like(u_st)

    @pl.when(s == -1)
    def _():
        normalise_next()

    @pl.when((s >= 0) & (s <= n_main))
    def _():
        project()
        x = jnp.where(s < n_main, xc_ref[...], small_tile())
        h = x + 0.5 * _swiglu(xn_st[...], w1b, w3b, w2b, hid_ref)
        h_ref[...] = h
        u_st[...] = _rms(h, gm_ref[...]).astype(BF16)
        normalise_next()

    @pl.when(s == n_main + 1)
    def _():
        project()


def _ffn_inproj(x_main, x_meta, x_sample, g1, gm, gq, gk, w1, w3, w2, w_in, splits):
    m_rows, d = x_main.shape
    n_sample = x_sample.shape[0] * LANES // d
    assert x_meta.shape[0] % 8 == 0 and n_sample % 8 == 0 and x_sample.shape[1] == LANES
    assert x_meta.shape[0] + n_sample <= ROW_TILE
    n_ff = w1.shape[1]
    n_main = m_rows // ROW_TILE
    r_rows = m_rows + ROW_TILE
    t = ROW_TILE
    o_q, o_k, o_v, o_a, o_b, o_ga, o_gc, o_end = splits
    assert o_v - o_k == LANES and o_a - o_v == LANES and (o_k - o_q) % LANES == 0

    def cur_idx(i):
        return (jnp.clip(i - N_WSTEPS, 0, n_main - 1), 0)

    def next_idx(i):
        return (jnp.clip(i - N_WSTEPS + 1, 0, n_main - 1), 0)

    def h_idx(i):
        return (jnp.clip(i - N_WSTEPS, 0, n_main), 0)

    def proj_idx(i):
        return (jnp.clip(i - N_WSTEPS - 1, 0, n_main), 0)

    def w_idx(i):
        return (jnp.minimum(i, N_WSTEPS - 1), 0)

    const = lambda i: (0, 0)
    outs = [
        (jax.ShapeDtypeStruct((r_rows, d), F32), h_idx),
        (jax.ShapeDtypeStruct((r_rows, o_k - o_q), BF16), proj_idx),
        (jax.ShapeDtypeStruct((r_rows, o_a - o_k), F32), proj_idx),
        (jax.ShapeDtypeStruct((r_rows, o_b - o_a), F32), proj_idx),
        (jax.ShapeDtypeStruct((r_rows, o_gc - o_ga), F32), proj_idx),
        (jax.ShapeDtypeStruct((r_rows, o_end - o_gc), F32), proj_idx),
    ]
    return pl.pallas_call(
        functools.partial(_ffn_inproj_body, n_main=n_main, splits=splits),
        grid=(N_WSTEPS + n_main + 2,),
        in_specs=[
            pl.BlockSpec((t, d), cur_idx),
            pl.BlockSpec((t, d), next_idx),
            pl.BlockSpec(x_meta.shape, const),
            pl.BlockSpec(x_sample.shape, const),
            pl.BlockSpec((1, d), const),
            pl.BlockSpec((1, d), const),
            pl.BlockSpec((1, HEAD_DIM), const),
            pl.BlockSpec((1, HEAD_DIM), const),
            pl.BlockSpec((d // N_WSTEPS, n_ff), w_idx),
            pl.BlockSpec((d // N_WSTEPS, n_ff), w_idx),
            pl.BlockSpec((n_ff // N_WSTEPS, d), w_idx),
            pl.BlockSpec((d // N_WSTEPS, o_end), w_idx),
        ],
        out_specs=[pl.BlockSpec((t, s.shape[1]), idx) for s, idx in outs],
        out_shape=[s for s, _ in outs],
        scratch_shapes=[
            pltpu.VMEM((d, n_ff), BF16),
            pltpu.VMEM((d, n_ff), BF16),
            pltpu.VMEM((n_ff, d), BF16),
            pltpu.VMEM((d, o_end), BF16),
            pltpu.VMEM((t, n_ff), BF16),
            pltpu.VMEM((t, d), BF16),
            pltpu.VMEM((t, d), BF16),
            pltpu.VMEM((t, d), F32),
        ],
        compiler_params=pltpu.CompilerParams(
            dimension_semantics=("arbitrary",), vmem_limit_bytes=VMEM_LIMIT),
        name="ffn1_inproj",
    )(x_main, x_main, x_meta, x_sample, g1, gm, gq, gk, w1, w3, w2, w_in)


def _prompt_attn_body(rb_ref, sink_ref, q_ref, kvc_ref, kvp_ref, kvs_ref, bidx_ref, *rest,
                      tiles_per_seq, n_meta, n_heads, n_weights):
    w_refs, attn_ref, wb_refs = rest[:n_weights], rest[n_weights], rest[n_weights + 1:2 * n_weights + 1]
    kt_ref, vt_ref, bias_sc, kvm_ref, kvs_st = rest[2 * n_weights + 1:]
    for w_ref, wb_ref in zip(w_refs, wb_refs, strict=True):
        wb_ref[...] = w_ref[...].astype(BF16)

    i = pl.program_id(0)
    t = q_ref.shape[0]
    n_kv = 2
    grp = n_heads // n_kv

    @pl.when(i == 0)
    def _():
        kvm_ref[...] = jnp.zeros_like(kvm_ref)
        kvm_ref[WINDOW - n_meta:WINDOW, :] = kvs_ref[0:n_meta, :]
        col = lax.broadcasted_iota(jnp.int32, (WINDOW, 2 * WINDOW), 1)
        for hd in range(n_heads):
            tab = _bias_table(bidx_ref[...], rb_ref, hd)
            bias_sc[0, hd] = tab
            bias_sc[1, hd] = jnp.where(col < WINDOW - n_meta, NEG, tab)

    first = (i % tiles_per_seq) == 0
    lead = jnp.where(first, 1, 0)
    lo_mask = lax.broadcasted_iota(jnp.int32, (1, LANES), 1) < HEAD_DIM
    kv_prev = jnp.where(first, kvm_ref[...], kvp_ref[...])
    kv = jnp.concatenate([kv_prev, kvc_ref[...]], axis=0)
    k, v = kv[:, :LANES], kv[:, LANES:]
    kb, vb = k.astype(BF16), v.astype(BF16)
    kr = pltpu.roll(k, HEAD_DIM, axis=1).astype(BF16)
    vr = pltpu.roll(v, HEAD_DIM, axis=1).astype(BF16)
    zero = jnp.zeros((), BF16)
    k_lo = [jnp.where(lo_mask, kb, zero), jnp.where(lo_mask, kr, zero)]
    k_hi = [jnp.where(lo_mask, zero, kr), jnp.where(lo_mask, zero, kb)]
    v_lo = [jnp.where(lo_mask, vb, zero), jnp.where(lo_mask, vr, zero)]
    v_hi = [jnp.where(lo_mask, zero, vr), jnp.where(lo_mask, zero, vb)]
    for h in range(n_kv):
        kvs_st[0, h], kvs_st[1, h], kvs_st[2, h], kvs_st[3, h] = k_lo[h], k_hi[h], v_lo[h], v_hi[h]

    for qb in range(t // WINDOW):
        rows = slice(qb * WINDOW, (qb + 1) * WINDOW)
        keys = pl.ds(pl.multiple_of(qb * WINDOW + jnp.minimum(i, 0), WINDOW), 2 * WINDOW)
        tab = lead if qb == 0 else 0
        for h in range(n_kv):
            k_st = jnp.concatenate([kvs_st[0, h, keys, :], kvs_st[1, h, keys, :]], axis=0)
            v_st = jnp.concatenate([kvs_st[2, h, keys, :], kvs_st[3, h, keys, :]], axis=0)
            c0 = h * grp * HEAD_DIM
            qq = jnp.concatenate([q_ref[rows, c0:c0 + LANES], q_ref[rows, c0 + LANES:c0 + 2 * LANES]], axis=0)
            s = _dot_t(qq, k_st)
            p_parts, inv_parts = [], []
            for g2 in range(2):
                p_row, inv_row = [], []
                for par in range(2):
                    hd = h * grp + 2 * g2 + par
                    sq = s[g2 * WINDOW:(g2 + 1) * WINDOW, par * 2 * WINDOW:(par + 1) * 2 * WINDOW] + bias_sc[tab, hd]
                    sink = sink_ref[0, hd]
                    m = jnp.maximum(jnp.max(sq, axis=-1, keepdims=True), sink)
                    p = jnp.exp(sq - m)
                    den = jnp.sum(p, axis=-1, keepdims=True) + jnp.exp(sink - m)
                    p_row.append(p.astype(BF16))
                    inv_row.append(1.0 / den)
                p_parts.append(jnp.concatenate(p_row, axis=1))
                inv_parts.append(jnp.where(lo_mask, inv_row[0], inv_row[1]))
            pm = jnp.concatenate(p_parts, axis=0)
            o = _dot(pm, v_st)
            for g2 in range(2):
                c = c0 + g2 * LANES
                attn_ref[rows, c:c + LANES] = (o[g2 * WINDOW:(g2 + 1) * WINDOW] * inv_parts[g2]).astype(attn_ref.dtype)

    @pl.when((i + 1) % tiles_per_seq == 0)
    def _():
        kt_ref[0] = kvc_ref[t - WINDOW:t, :LANES].T
        vt_ref[0] = kvc_ref[t - WINDOW:t, LANES:].T


def _prompt_attn(rel_bias, sinks2, q_all, kv_all, bidx, weights, m_rows, seq, n_meta):
    t = ATTN_TILE
    n_steps = m_rows // t
    assert m_rows % WINDOW == 0 and n_meta <= WINDOW
    n_heads = sinks2.shape[1]
    qw, kvw = q_all.shape[1], kv_all.shape[1]
    const = lambda i: (0, 0)
    row = lambda i: (i, 0)
    per_seq = lambda i: (i // (seq // t), 0, 0)
    smem = pl.BlockSpec(memory_space=pltpu.SMEM)

    def w_spec(w):
        share = 1 if (w.shape[0] // n_steps) % 16 == 0 else 2
        assert w.shape[0] % (n_steps // share) == 0 and (w.shape[0] * share // n_steps) % 16 == 0
        return pl.BlockSpec((w.shape[0] * share // n_steps, w.shape[1]), lambda i: (i // share, 0))

    w_specs = [w_spec(w) for w in weights]
    outs = pl.pallas_call(
        functools.partial(_prompt_attn_body, tiles_per_seq=seq // t, n_meta=n_meta, n_heads=n_heads,
                          n_weights=len(weights)),
        grid=(n_steps,),
        in_specs=[
            smem, smem,
            pl.BlockSpec((t, qw), row),
            pl.BlockSpec((t, kvw), row),
            pl.BlockSpec((WINDOW, kvw), lambda i: (jnp.maximum(i * (t // WINDOW) - 1, 0), 0)),
            pl.BlockSpec((WINDOW, kvw), lambda i: (m_rows // WINDOW, 0)),
            pl.BlockSpec((WINDOW, 2 * WINDOW), const),
        ] + w_specs,
        out_specs=[pl.BlockSpec((t, qw), row)] + w_specs + [pl.BlockSpec((1, WINDOW, LANES), per_seq)] * 2,
        out_shape=[jax.ShapeDtypeStruct((m_rows, qw), BF16)] + [jax.ShapeDtypeStruct(w.shape, BF16) for w in weights]
        + [jax.ShapeDtypeStruct((m_rows // seq, WINDOW, LANES), F32)] * 2,
        scratch_shapes=[
            pltpu.VMEM((2, n_heads, WINDOW, 2 * WINDOW), F32),
            pltpu.VMEM((WINDOW, kvw), F32),
            pltpu.VMEM((4, 2, WINDOW + t, LANES), BF16),
        ],
        compiler_params=pltpu.CompilerParams(dimension_semantics=("arbitrary",)),
        name="prompt_attn",
    )(rel_bias, sinks2, q_all, kv_all, kv_all, kv_all, bidx, *weights)
    return outs[0], outs[1:1 + len(weights)], outs[1 + len(weights)], outs[2 + len(weights)]


def _sample_mix_body(rb_ref, sink_ref, q_ref, kvn_ref, glun_ref, ck_ref, cv_ref, st_ref, bidx_ref,
                     wdw_ref, bdw_ref, lng_ref, lnb_ref, after_ref,
                     as_ref, cs_ref, nk_ref, nv_ref, nst_ref, bias_sc, *, n_heads, n_meta):
    i = pl.program_id(0)
    sb, w_buf = ck_ref.shape[0], ck_ref.shape[2]
    n_st = st_ref.shape[0]
    n_rows = sb * n_heads
    per_kv = n_heads // (LANES // HEAD_DIM)

    @pl.when(i == 0)
    def _():
        as_ref[...] = jnp.zeros_like(as_ref)
        cs_ref[...] = jnp.zeros_like(cs_ref)
        tiled = jnp.concatenate(
            [jnp.broadcast_to(jnp.concatenate([_bias_table(bidx_ref[...], rb_ref, hd)] * sb, axis=1),
                              (sb, sb * w_buf)) for hd in range(n_heads)], axis=0)
        row_seq = lax.rem(lax.broadcasted_iota(jnp.int32, tiled.shape, 0), sb)
        col_seq = lax.broadcasted_iota(jnp.int32, tiled.shape, 1) // w_buf
        bias_sc[...] = jnp.where(row_seq == col_seq, tiled, NEG)

    hrow = lax.broadcasted_iota(jnp.int32, (n_rows, 1), 0) // sb
    sink = jnp.zeros((n_rows, 1), F32)
    bias_new = jnp.zeros((n_rows, 1), F32)
    for hd in range(n_heads):
        sink = jnp.where(hrow == hd, sink_ref[0, hd], sink)
        bias_new = jnp.where(hrow == hd, rb_ref[hd, 0], bias_new)

    lo_mask = lax.broadcasted_iota(jnp.int32, (1, LANES), 1) < HEAD_DIM

    def half(x, src_hi, dst_hi):
        if src_hi != dst_hi:
            x = pltpu.roll(x, HEAD_DIM, 1)
        return jnp.where(lo_mask, 0.0, x) if dst_hi else jnp.where(lo_mask, x, 0.0)

    q = jnp.concatenate(
        [half(q_ref[:, (hd // 2) * LANES:(hd // 2 + 1) * LANES].astype(F32), hd % 2 == 1, hd // per_kv == 1)
         for hd in range(n_heads)], axis=0).astype(BF16)
    kn_rows = jnp.concatenate([kvn_ref[:, :LANES]] * n_heads, axis=0).astype(BF16)
    vn_rows = jnp.concatenate([kvn_ref[:, LANES:]] * n_heads, axis=0).astype(BF16)
    ck_all = jnp.concatenate([ck_ref[bb] for bb in range(sb)], axis=1).astype(BF16)
    cv_all = jnp.concatenate([cv_ref[bb] for bb in range(sb)], axis=1).astype(BF16)
    s_c = _dot(q, ck_all) + bias_sc[...]
    s_n = jnp.sum(q.astype(F32) * kn_rows.astype(F32), axis=-1, keepdims=True) + bias_new
    m = jnp.maximum(jnp.maximum(jnp.max(s_c, axis=-1, keepdims=True), s_n), sink)
    p_c = jnp.exp(s_c - m)
    p_n = jnp.exp(s_n - m)
    den = jnp.sum(p_c, axis=-1, keepdims=True) + p_n + jnp.exp(sink - m)
    o = _dot_t(p_c.astype(BF16), cv_all) + p_n.astype(BF16).astype(F32) * vn_rows.astype(F32)
    o = o / den
    pairs = []
    for p in range(n_heads // 2):
        lo_head = o[(2 * p) * sb:(2 * p + 1) * sb]
        hi_head = o[(2 * p + 1) * sb:(2 * p + 2) * sb]
        pairs.append(half(lo_head, (2 * p) // per_kv == 1, False) + half(hi_head, (2 * p + 1) // per_kv == 1, True))
    rows = pl.ds(pl.multiple_of(n_meta + i * sb, sb), sb)
    as_ref[rows, :] = jnp.concatenate(pairs, axis=1).astype(as_ref.dtype)

    g_new = glun_ref[...]
    y = g_new * wdw_ref[n_st:n_st + 1, :] + bdw_ref[...]
    for k in range(n_st):
        y = y + st_ref[k] * wdw_ref[k:k + 1, :]
    cs_ref[rows, :] = _ln_swish(y, lng_ref[...], lnb_ref[...]).astype(cs_ref.dtype)

    fill = jnp.zeros((w_buf - sb, LANES), F32)
    kn_t = jnp.concatenate([kvn_ref[:, :LANES], fill], axis=0).T
    vn_t = jnp.concatenate([kvn_ref[:, LANES:], fill], axis=0).T
    newest = lax.broadcasted_iota(jnp.int32, (LANES, w_buf), 1) == w_buf - 1
    for bb in range(sb):
        nk_ref[bb] = jnp.where(newest, kn_t[:, bb:bb + 1], pltpu.roll(ck_ref[bb], w_buf - 1, 1))
        nv_ref[bb] = jnp.where(newest, vn_t[:, bb:bb + 1], pltpu.roll(cv_ref[bb], w_buf - 1, 1))
    nst_ref[0:n_st - 1] = st_ref[1:n_st]
    nst_ref[n_st - 1] = g_new


def _sample_mix(rel_bias, sinks2, q_all, kv_all, glu_all, cache_kt, cache_vt, state_t, bidx_s, w_dw, b_dw, ln_g, ln_b,
                s0, n_meta, t_rows, after):
    db, kw, w_buf = cache_kt.shape
    n_st, ch = state_t.shape[0], state_t.shape[2]
    n_heads = sinks2.shape[1]
    aw = q_all.shape[1]
    sb = SAMPLE_BLOCK
    assert s0 % sb == 0 and n_meta % sb == 0 and n_meta + db <= t_rows
    blk = lambda i: (i, 0, 0)
    const = lambda i: (0, 0)
    tok = lambda i: (s0 // sb + i, 0)
    smem = pl.BlockSpec(memory_space=pltpu.SMEM)
    return pl.pallas_call(
        functools.partial(_sample_mix_body, n_heads=n_heads, n_meta=n_meta),
        grid=(db // sb,),
        in_specs=[
            smem, smem,
            pl.BlockSpec((sb, aw), tok),
            pl.BlockSpec((sb, 2 * kw), tok),
            pl.BlockSpec((sb, ch), tok),
            pl.BlockSpec((sb, kw, w_buf), blk),
            pl.BlockSpec((sb, kw, w_buf), blk),
            pl.BlockSpec((n_st, sb, ch), lambda i: (0, i, 0)),
            pl.BlockSpec((1, w_buf), const),
            pl.BlockSpec(w_dw.shape, const),
            pl.BlockSpec((1, ch), const),
            pl.BlockSpec((1, ch), const),
            pl.BlockSpec((1, ch), const),
            pl.BlockSpec(memory_space=pl.ANY),
        ],
        out_specs=[
            pl.BlockSpec((t_rows, aw), const),
            pl.BlockSpec((t_rows, ch), const),
            pl.BlockSpec((sb, kw, w_buf), blk),
            pl.BlockSpec((sb, kw, w_buf), blk),
            pl.BlockSpec((n_st, sb, ch), lambda i: (0, i, 0)),
        ],
        out_shape=[
            jax.ShapeDtypeStruct((t_rows, aw), BF16),
            jax.ShapeDtypeStruct((t_rows, ch), BF16),
            jax.ShapeDtypeStruct((db, kw, w_buf), F32),
            jax.ShapeDtypeStruct((db, kw, w_buf), F32),
            jax.ShapeDtypeStruct((n_st, db, ch), F32),
        ],
        scratch_shapes=[pltpu.VMEM((sb * n_heads, sb * w_buf), F32)],
        compiler_params=pltpu.CompilerParams(dimension_semantics=("arbitrary",)),
        name="sample_mix",
    )(rel_bias, sinks2, q_all, kv_all, glu_all, cache_kt, cache_vt, state_t, bidx_s, w_dw, b_dw, ln_g, ln_b, after)


def _conv_runs(cb_st, glu_ref, glum_ref, y_sc, wdw_ref, bdw_ref, first, n_runs):
    t = y_sc.shape[0]
    conv_w = wdw_ref.shape[0]
    n_ch = y_sc.shape[1]
    off = HALO - (conv_w - 1)
    n_a = (off + conv_w - 1) // 8 + 1
    units = [(lc, g) for lc in range(n_ch // LANES) for g in range(t // 8)]
    sizes = [len(units) // n_runs + (1 if r < len(units) % n_runs else 0) for r in range(n_runs)]
    staged = []

    def run(mine):
        if not staged:
            cb_st[0:HALO, :] = jnp.where(first, glum_ref[...], cb_st[0:HALO, :])
            cb_st[HALO:HALO + t, :] = glu_ref[...]
            staged.append(True)
        zs = {}

        def z(lc, s, g):
            if (lc, s, g) not in zs:
                ls = slice(lc * LANES, (lc + 1) * LANES)
                acc = None
                for a in range(n_a):
                    w = 8 * a + s - off
                    if 0 <= w < conv_w:
                        term = cb_st[8 * (g + a):8 * (g + a) + 8, ls] * wdw_ref[w:w + 1, ls]
                        acc = term if acc is None else acc + term
                zs[(lc, s, g)] = acc
            return zs[(lc, s, g)]

        dep = None
        for lc, g in mine:
            ls = slice(lc * LANES, (lc + 1) * LANES)
            acc = jnp.broadcast_to(bdw_ref[:, ls], (8, LANES)) + z(lc, 0, g)
            if dep is not None:
                acc = acc + dep
            for s in range(1, 8):
                acc = acc + jnp.concatenate([z(lc, s, g), z(lc, s, g + 1)], axis=0)[s:s + 8, :]
            y_sc[8 * g:8 * g + 8, ls] = acc
            dep = _dep_zero(acc)
        return dep[0:1, :]

    out, k = [], 0
    for n in sizes:
        out.append(functools.partial(run, units[k:k + n]))
        k += n
    return out


def _post_ffn_body(h_ref, am_ref, as_ref, cs_ref, ga_ref, gc_ref, g2_ref, glu_ref, glus_ref,
                   wdw_ref, bdw_ref, lng_ref, lnb_ref,
                   waob, wcob, woutb, w1b, w3b, w2b,
                   ym_ref, ys_ref, gt_ref,
                   hid_ref, cb_st, y_sc, conv_st, glum_ref, *, n_main, tiles_per_seq, n_meta):
    i = pl.program_id(0)
    t_rows = y_sc.shape[0]

    @pl.when(i == 0)
    def _():
        cb_st[...] = jnp.zeros_like(cb_st)
        glum_ref[...] = jnp.zeros_like(glum_ref)
        glum_ref[HALO - n_meta:HALO, :] = glus_ref[0:n_meta, :]

    t = i - 1
    tn = t + 1
    first = (tn < n_main) & (lax.rem(tn, tiles_per_seq) == 0)
    n_chunks = w1b.shape[1] // FF_CHUNK

    def finish_conv():
        conv_st[...] = _ln_swish(y_sc[...], lng_ref[...], lnb_ref[...]).astype(conv_st.dtype)
        cb_st[0:HALO, :] = cb_st[t_rows:t_rows + HALO, :]

    @pl.when(t == -1)
    def _():
        for f in _conv_runs(cb_st, glu_ref, glum_ref, y_sc, wdw_ref, bdw_ref, first, n_chunks):
            f()
        finish_conv()

    @pl.when(t >= 0)
    def _():
        on_main = t < n_main
        conv_f = _conv_runs(cb_st, glu_ref, glum_ref, y_sc, wdw_ref, bdw_ref, first, n_chunks)
        fillers = [(c, c + 1, f) for c, f in enumerate(conv_f)]

        at = jnp.where(on_main, am_ref[...], as_ref[...])
        cv = jnp.where(on_main, conv_st[...], cs_ref[...])
        a = _dot(at, waob[...])
        c = _dot(cv, wcob[...])
        mix = (ga_ref[...] * a + gc_ref[...] * c).astype(BF16)
        h2 = h_ref[...] + _dot(mix, woutb[...])
        xn = _rms(h2, g2_ref[...]).astype(BF16)
        y = h2 + 0.5 * _swiglu(xn, w1b, w3b, w2b, hid_ref, fillers)
        finish_conv()

        @pl.when(on_main)
        def _():
            ym_ref[...] = y

        @pl.when(t == n_main)
        def _():
            n_c = y.shape[1] // LANES
            n_s = ys_ref.shape[0] // n_c
            for c in range(n_c):
                ys_ref[pl.ds(c, n_s, stride=n_c), :] = y[n_meta:n_meta + n_s, c * LANES:(c + 1) * LANES]

    n_st = gt_ref.shape[0]
    for sq in range(gt_ref.shape[1]):
        @pl.when(tn == (sq + 1) * tiles_per_seq - 1)
        def _():
            for r in range(n_st):
                gt_ref[r, sq:sq + 1, :] = glu_ref[t_rows - n_st + r:t_rows - n_st + r + 1, :]


def _post_ffn(h_all, attn_main, attn_small, conv_small, ga_all, gc_all, g2, glu_all,
              w_dw, b_dw, ln_g, ln_b, w_ao, w_co, w_out, w1, w3, w2, seq, n_meta, n_sample):
    m_rows, aw = attn_main.shape
    assert m_rows % HALO == 0 and n_meta <= HALO and n_meta % 8 == 0 and n_sample % 8 == 0
    ch = glu_all.shape[1]
    d = h_all.shape[1]
    n_ff = w1.shape[1]
    t = ROW_TILE
    n_main = m_rows // t
    n_b = m_rows // seq
    n_st = w_dw.shape[0] - 1

    def main_idx(i):
        return (jnp.clip(i - 1, 0, n_main - 1), 0)

    def next_idx(i):
        return (jnp.minimum(i, n_main - 1), 0)

    def row_idx(i):
        return (jnp.maximum(i - 1, 0), 0)

    const = lambda i: (0, 0)
    resident = lambda w: pl.BlockSpec(w.shape, const, pipeline_mode=pl.Buffered(1))
    return pl.pallas_call(
        functools.partial(_post_ffn_body, n_main=n_main, tiles_per_seq=seq // t, n_meta=n_meta),
        grid=(n_main + 2,),
        in_specs=[
            pl.BlockSpec((t, d), row_idx),
            pl.BlockSpec((t, aw), main_idx),
            pl.BlockSpec((t, aw), const),
            pl.BlockSpec((t, ch), const),
            pl.BlockSpec((t, d), row_idx),
            pl.BlockSpec((t, d), row_idx),
            pl.BlockSpec((1, d), const),
            pl.BlockSpec((t, ch), next_idx),
            pl.BlockSpec((HALO, ch), lambda i: (m_rows // HALO, 0)),
            pl.BlockSpec(w_dw.shape, const),
            pl.BlockSpec((1, ch), const),
            pl.BlockSpec((1, ch), const),
            pl.BlockSpec((1, ch), const),
            resident(w_ao), resident(w_co), resident(w_out), resident(w1), resident(w3), resident(w2),
        ],
        out_specs=[pl.BlockSpec((t, d), main_idx), pl.BlockSpec((n_sample * (d // LANES), LANES), const),
                   pl.BlockSpec((n_st, n_b, ch), lambda i: (0, 0, 0))],
        out_shape=[jax.ShapeDtypeStruct((m_rows, d), F32),
                   jax.ShapeDtypeStruct((n_sample * (d // LANES), LANES), F32),
                   jax.ShapeDtypeStruct((n_st, n_b, ch), F32)],
        scratch_shapes=[
            pltpu.VMEM((t, n_ff), BF16),
            pltpu.VMEM((HALO + t, ch), F32),
            pltpu.VMEM((t, ch), F32),
            pltpu.VMEM((t, ch), BF16),
            pltpu.VMEM((HALO, ch), F32),
        ],
        compiler_params=pltpu.CompilerParams(
            dimension_semantics=("arbitrary",), vmem_limit_bytes=VMEM_LIMIT),
        name="post_ffn2",
    )(h_all, attn_main, attn_small, conv_small, ga_all, gc_all, g2, glu_all, glu_all,
      w_dw, b_dw, ln_g, ln_b, w_ao, w_co, w_out, w1, w3, w2)


def _t5_bucket(dist, n_buckets):
    max_exact = n_buckets // 2
    d = np.maximum(dist, 0)
    ratio = (np.log(np.maximum(d, 1).astype(np.float32) / np.float32(max_exact))
             / np.float32(math.log(REL_MAX_DIST / max_exact)))
    large = np.minimum(max_exact + (ratio * np.float32(n_buckets - max_exact)).astype(np.int32), n_buckets - 1)
    return np.where(d < max_exact, d, large).astype(np.int32)


def _bucket_or_masked(dist, n_buckets):
    ok = (dist >= 0) & (dist < WINDOW)
    return np.where(ok, _t5_bucket(dist, n_buckets), -1).astype(np.int32)


def _rows(x, start, n):
    return lax.slice_in_dim(x, start, start + n, axis=0)


def kernel(x_prompt, x_sample, cache_k, cache_v, state_conv, meta_tokens, ffn1_norm, ffn1_w1, ffn1_w3, ffn1_w2, mix_norm, w_in, q_norm, k_norm, rel_bias, sinks, w_attn_out, w_dw, b_dw, conv_ln_g, conv_ln_b, w_conv_out, w_out, ffn2_norm, ffn2_w1, ffn2_w3, ffn2_w2):
    n_b, seq, d = x_prompt.shape
    db = x_sample.shape[0]
    n_meta = meta_tokens.shape[0]
    n_heads = sinks.shape[0]
    w_buf, n_kv, hd = cache_k.shape[1], cache_k.shape[2], cache_k.shape[3]
    ch = w_dw.shape[1]
    n_st = state_conv.shape[1]
    n_buckets = rel_bias.shape[0]
    aw, kvw = n_heads * hd, n_kv * hd
    t = ROW_TILE
    m_rows = n_b * seq
    assert hd == HEAD_DIM and kvw == LANES and n_kv == 2 and n_heads == 8 and w_buf == WINDOW
    assert x_sample.shape[1] == 1 and seq % t == 0 and seq % ATTN_TILE == 0
    assert n_meta + db <= t and db % SAMPLE_BLOCK == 0
    assert n_meta <= HALO and n_meta <= WINDOW and n_st == w_dw.shape[0] - 1 and n_st <= HALO
    splits = tuple(int(v) for v in np.cumsum([0, aw, kvw, kvw, ch, ch, d, d]))
    assert splits[-1] == w_in.shape[1]

    row = lambda v: v.reshape(1, -1)
    x_main = x_prompt.reshape(m_rows, d)

    h_all, q_all, kv_all, glu_all, ga_all, gc_all = _ffn_inproj(
        x_main, meta_tokens, x_sample.reshape(db * (d // LANES), LANES), row(ffn1_norm), row(mix_norm),
        row(q_norm), row(k_norm), ffn1_w1, ffn1_w3, ffn1_w2, w_in, splits)

    sinks2 = row(sinks)
    rel_bias_t = rel_bias.T
    b_dw2, ln_g2, ln_b2 = row(b_dw), row(conv_ln_g), row(conv_ln_b)

    dist = np.arange(WINDOW)[:, None] + WINDOW - np.arange(2 * WINDOW)[None, :]
    bidx = jnp.asarray(_bucket_or_masked(dist, n_buckets))
    attn_main, w_bf16, k_t, v_t = _prompt_attn(
        rel_bias_t, sinks2, q_all, kv_all, bidx,
        (w_attn_out, w_conv_out, w_out, ffn2_w1, ffn2_w3, ffn2_w2), m_rows, seq, n_meta)

    bidx_s = jnp.asarray(_bucket_or_masked(w_buf - np.arange(w_buf)[None, :], n_buckets))
    attn_small, conv_small, new_k_s, new_v_s, new_conv_s = _sample_mix(
        rel_bias_t, sinks2, q_all, kv_all, glu_all,
        jnp.transpose(cache_k, (0, 2, 3, 1)).reshape(db, kvw, w_buf),
        jnp.transpose(cache_v, (0, 2, 3, 1)).reshape(db, kvw, w_buf),
        jnp.transpose(state_conv, (1, 0, 2)), bidx_s, w_dw, b_dw2, ln_g2, ln_b2, m_rows + n_meta, n_meta, t,
        after=k_t)

    y_main, y_small, glu_t = _post_ffn(h_all, attn_main, attn_small, conv_small, ga_all, gc_all, row(ffn2_norm),
                                       glu_all, w_dw, b_dw2, ln_g2, ln_b2, *w_bf16, seq, n_meta, db)

    return (
        y_main.reshape(n_b, seq, d),
        y_small.reshape(db, 1, d),
        jnp.transpose(k_t.reshape(n_b, n_kv, hd, WINDOW), (0, 3, 1, 2)),
        jnp.transpose(v_t.reshape(n_b, n_kv, hd, WINDOW), (0, 3, 1, 2)),
        jnp.transpose(glu_t, (1, 0, 2)),
        jnp.transpose(new_k_s.reshape(db, n_kv, hd, w_buf), (0, 3, 1, 2)),
        jnp.transpose(new_v_s.reshape(db, n_kv, hd, w_buf), (0, 3, 1, 2)),
        jnp.transpose(new_conv_s, (1, 0, 2)),
    )
```

```python
import functools
import math

import jax
import jax.numpy as jnp
import numpy as np
from jax import lax
from jax.experimental import pallas as pl
from jax.experimental.pallas import tpu as pltpu

F32 = jnp.float32
BF16 = jnp.bfloat16

EPS = 1e-6
NEG = -1e30
WINDOW = 128
REL_MAX_DIST = 128
HEAD_DIM = 64
LANES = 128
ROW_TILE = 256
ATTN_TILE = 1024
FF_CHUNK = 256
N_WSTEPS = 8
HALO = 32
SAMPLE_BLOCK = 16
VMEM_LIMIT = 56 * 1024 * 1024


def _dot(a, b):
    return jnp.dot(a, b, preferred_element_type=F32)


def _dot_t(a, b):
    return lax.dot_general(a, b, (((1,), (1,)), ((), ())), preferred_element_type=F32)


def _rms(x, g):
    return x * lax.rsqrt(jnp.mean(x * x, axis=-1, keepdims=True) + EPS) * g


def _pair_rms(x, g2, lo_mask):
    sq = x * x
    lo = jnp.sum(jnp.where(lo_mask, sq, 0.0), axis=-1, keepdims=True) * (1.0 / HEAD_DIM)
    hi = jnp.sum(jnp.where(lo_mask, 0.0, sq), axis=-1, keepdims=True) * (1.0 / HEAD_DIM)
    r = jnp.where(lo_mask, lax.rsqrt(lo + EPS), lax.rsqrt(hi + EPS))
    return x * r * g2


def _dep_zero(row):
    u = pltpu.bitcast(row, jnp.uint32)
    return pltpu.bitcast((u >> 16) >> 16, F32)


def _swiglu(xn, w1b, w3b, w2b, hid_ref, fillers=()):
    n_ff = w1b.shape[1]
    n_chunks = n_ff // FF_CHUNK
    due = {}
    for c in range(n_chunks):
        sl = slice(c * FF_CHUNK, (c + 1) * FF_CHUNK)
        a = _dot(xn, w1b[:, sl])
        b = _dot(xn, w3b[:, sl])
        for z in due.pop(c, ()):
            b = b + jnp.concatenate([z] * (FF_CHUNK // LANES), axis=1)
        for issue, when, thunk in fillers:
            if issue == c:
                due.setdefault(min(when, n_chunks), []).append(_dep_zero(thunk()))
        hid_ref[:, sl] = (a * jax.nn.sigmoid(a) * b).astype(BF16)
    out = _dot(hid_ref[...], w2b[...])
    for z in due.pop(n_chunks, ()):
        out = out + jnp.concatenate([z] * (out.shape[1] // LANES), axis=1)
    assert not due
    return out


def _store_chunk(dst, src, i):
    rows = src.shape[0]
    r = pl.multiple_of(i * rows, 16)
    dst[pl.ds(r, rows), :] = src[...].astype(BF16)


def _bias_table(bidx, rb_ref, head):
    tab = jnp.full(bidx.shape, NEG, F32)
    for b in range(rb_ref.shape[1]):
        tab = jnp.where(bidx == b, rb_ref[head, b], tab)
    return tab


def _ln_swish(y, g, b):
    mu = jnp.mean(y, axis=-1, keepdims=True)
    yc = y - mu
    var = jnp.mean(yc * yc, axis=-1, keepdims=True)
    z = yc * lax.rsqrt(var + EPS) * g + b
    return z * jax.nn.sigmoid(z)


def _ffn_inproj_body(xc_ref, xn_ref, meta_ref, xsm_ref, g1_ref, gm_ref, gq_ref, gk_ref, w1c, w3c, w2c, winc,
                     h_ref, q_ref, kv_ref, glu_ref, ga_ref, gc_ref,
                     w1b, w3b, w2b, winb, hid_ref, xn_st, u_st, xs_st, *, n_main, splits):
    i = pl.program_id(0)

    @pl.when(i == 0)
    def _():
        n_meta, n_c = meta_ref.shape[0], xs_st.shape[1] // LANES
        n_s = xsm_ref.shape[0] // n_c
        xs_st[...] = jnp.zeros_like(xs_st)
        xs_st[0:n_meta, :] = meta_ref[...]
        for c in range(n_c):
            xs_st[n_meta:n_meta + n_s, c * LANES:(c + 1) * LANES] = xsm_ref[pl.ds(c, n_s, stride=n_c), :]

    def small_tile():
        return xs_st[...]

    @pl.when(i < N_WSTEPS)
    def _():
        _store_chunk(w1b, w1c, i)
        _store_chunk(w3b, w3c, i)
        _store_chunk(w2b, w2c, i)
        _store_chunk(winb, winc, i)

    s = i - N_WSTEPS

    def project():
        u = u_st[...]
        o_q, o_k, o_v, o_a, o_b, o_ga, o_gc, o_end = splits
        lo_mask = lax.broadcasted_iota(jnp.int32, (1, LANES), 1) < HEAD_DIM
        gq2 = jnp.concatenate([gq_ref[...], gq_ref[...]], axis=1)
        gk2 = jnp.concatenate([gk_ref[...], gk_ref[...]], axis=1)
        zq = _dot(u, winb[:, o_q:o_k])
        for p in range((o_k - o_q) // LANES):
            sl = slice(p * LANES, (p + 1) * LANES)
            q_ref[:, sl] = (_pair_rms(zq[:, sl], gq2, lo_mask) * (HEAD_DIM ** -0.5)).astype(BF16)
        zkv = _dot(u, winb[:, o_k:o_a])
        kv_ref[:, :LANES] = _pair_rms(zkv[:, :LANES], gk2, lo_mask)
        kv_ref[:, LANES:] = zkv[:, LANES:]
        za = _dot(u, winb[:, o_a:o_b])
        zb = _dot(u, winb[:, o_b:o_ga])
        glu_ref[...] = za * jax.nn.sigmoid(zb)
        ga_ref[...] = jax.nn.sigmoid(_dot(u, winb[:, o_ga:o_gc]))
        gc_ref[...] = jax.nn.sigmoid(_dot(u, winb[:, o_gc:o_end]))

    def normalise_next():
        x_next = jnp.where(s + 1 < n_main, xn_ref[...], small_tile())
        xn_st[...] = _rms(x_next, g1_ref[...]).astype(BF16)

    @pl.when(i == 0)
    def _():
        u_st[...] = jnp.zeros_like(u_st)

    @pl.when(s == -1)
    def _():
        normalise_next()

    @pl.when((s >= 0) & (s <= n_main))
    def _():
        project()
        x = jnp.where(s < n_main, xc_ref[...], small_tile())
        h = x + 0.5 * _swiglu(xn_st[...], w1b, w3b, w2b, hid_ref)
        h_ref[...] = h
        u_st[...] = _rms(h, gm_ref[...]).astype(BF16)
        normalise_next()

    @pl.when(s == n_main + 1)
    def _():
        project()


def _ffn_inproj(x_main, x_meta, x_sample, g1, gm, gq, gk, w1, w3, w2, w_in, splits):
    m_rows, d = x_main.shape
    n_sample = x_sample.shape[0] * LANES // d
    assert x_meta.shape[0] % 8 == 0 and n_sample % 8 == 0 and x_sample.shape[1] == LANES
    assert x_meta.shape[0] + n_sample <= ROW_TILE
    n_ff = w1.shape[1]
    n_main = m_rows // ROW_TILE
    r_rows = m_rows + ROW_TILE
    t = ROW_TILE
    o_q, o_k, o_v, o_a, o_b, o_ga, o_gc, o_end = splits
    assert o_v - o_k == LANES and o_a - o_v == LANES and (o_k - o_q) % LANES == 0

    def cur_idx(i):
        return (jnp.clip(i - N_WSTEPS, 0, n_main - 1), 0)

    def next_idx(i):
        return (jnp.clip(i - N_WSTEPS + 1, 0, n_main - 1), 0)

    def h_idx(i):
        return (jnp.clip(i - N_WSTEPS, 0, n_main), 0)

    def proj_idx(i):
        return (jnp.clip(i - N_WSTEPS - 1, 0, n_main), 0)

    def w_idx(i):
        return (jnp.minimum(i, N_WSTEPS - 1), 0)

    const = lambda i: (0, 0)
    outs = [
        (jax.ShapeDtypeStruct((r_rows, d), F32), h_idx),
        (jax.ShapeDtypeStruct((r_rows, o_k - o_q), BF16), proj_idx),
        (jax.ShapeDtypeStruct((r_rows, o_a - o_k), F32), proj_idx),
        (jax.ShapeDtypeStruct((r_rows, o_b - o_a), F32), proj_idx),
        (jax.ShapeDtypeStruct((r_rows, o_gc - o_ga), F32), proj_idx),
        (jax.ShapeDtypeStruct((r_rows, o_end - o_gc), F32), proj_idx),
    ]
    return pl.pallas_call(
        functools.partial(_ffn_inproj_body, n_main=n_main, splits=splits),
        grid=(N_WSTEPS + n_main + 2,),
        in_specs=[
            pl.BlockSpec((t, d), cur_idx),
            pl.BlockSpec((t, d), next_idx),
            pl.BlockSpec(x_meta.shape, const),
            pl.BlockSpec(x_sample.shape, const),
            pl.BlockSpec((1, d), const),
            pl.BlockSpec((1, d), const),
            pl.BlockSpec((1, HEAD_DIM), const),
            pl.BlockSpec((1, HEAD_DIM), const),
            pl.BlockSpec((d // N_WSTEPS, n_ff), w_idx),
            pl.BlockSpec((d // N_WSTEPS, n_ff), w_idx),
            pl.BlockSpec((n_ff // N_WSTEPS, d), w_idx),
            pl.BlockSpec((d // N_WSTEPS, o_end), w_idx),
        ],
        out_specs=[pl.BlockSpec((t, s.shape[1]), idx) for s, idx in outs],
        out_shape=[s for s, _ in outs],
        scratch_shapes=[
            pltpu.VMEM((d, n_ff), BF16),
            pltpu.VMEM((d, n_ff), BF16),
            pltpu.VMEM((n_ff, d), BF16),
            pltpu.VMEM((d, o_end), BF16),
            pltpu.VMEM((t, n_ff), BF16),
            pltpu.VMEM((t, d), BF16),
            pltpu.VMEM((t, d), BF16),
            pltpu.VMEM((t, d), F32),
        ],
        compiler_params=pltpu.CompilerParams(
            dimension_semantics=("arbitrary",), vmem_limit_bytes=VMEM_LIMIT),
        name="ffn1_inproj",
    )(x_main, x_main, x_meta, x_sample, g1, gm, gq, gk, w1, w3, w2, w_in)


def _prompt_attn_body(rb_ref, sink_ref, q_ref, kvc_ref, kvp_ref, kvs_ref, bidx_ref, *rest,
                      tiles_per_seq, n_meta, n_heads, n_weights):
    w_refs, attn_ref, wb_refs = rest[:n_weights], rest[n_weights], rest[n_weights + 1:2 * n_weights + 1]
    kt_ref, vt_ref, bias_sc, kvm_ref = rest[2 * n_weights + 1:]
    for w_ref, wb_ref in zip(w_refs, wb_refs, strict=True):
        wb_ref[...] = w_ref[...].astype(BF16)

    i = pl.program_id(0)
    t = q_ref.shape[0]
    n_kv = 2
    grp = n_heads // n_kv

    @pl.when(i == 0)
    def _():
        kvm_ref[...] = jnp.zeros_like(kvm_ref)
        kvm_ref[WINDOW - n_meta:WINDOW, :] = kvs_ref[0:n_meta, :]
        col = lax.broadcasted_iota(jnp.int32, (WINDOW, 2 * WINDOW), 1)
        for hd in range(n_heads):
            tab = _bias_table(bidx_ref[...], rb_ref, hd)
            bias_sc[0, hd] = tab
            bias_sc[1, hd] = jnp.where(col < WINDOW - n_meta, NEG, tab)

    first = (i % tiles_per_seq) == 0
    lead = jnp.where(first, 1, 0)
    lo_mask = lax.broadcasted_iota(jnp.int32, (1, LANES), 1) < HEAD_DIM
    kv_prev = jnp.where(first, kvm_ref[...], kvp_ref[...])
    kv = jnp.concatenate([kv_prev, kvc_ref[...]], axis=0)
    k, v = kv[:, :LANES], kv[:, LANES:]
    kb, vb = k.astype(BF16), v.astype(BF16)
    kr = pltpu.roll(k, HEAD_DIM, axis=1).astype(BF16)
    vr = pltpu.roll(v, HEAD_DIM, axis=1).astype(BF16)
    zero = jnp.zeros((), BF16)
    k_lo = [jnp.where(lo_mask, kb, zero), jnp.where(lo_mask, kr, zero)]
    k_hi = [jnp.where(lo_mask, zero, kr), jnp.where(lo_mask, zero, kb)]
    v_lo = [jnp.where(lo_mask, vb, zero), jnp.where(lo_mask, vr, zero)]
    v_hi = [jnp.where(lo_mask, zero, vr), jnp.where(lo_mask, zero, vb)]

    for qb in range(t // WINDOW):
        rows = slice(qb * WINDOW, (qb + 1) * WINDOW)
        keys = slice(qb * WINDOW, qb * WINDOW + 2 * WINDOW)
        tab = lead if qb == 0 else 0
        for h in range(n_kv):
            k_st = jnp.concatenate([k_lo[h][keys], k_hi[h][keys]], axis=0)
            v_st = jnp.concatenate([v_lo[h][keys], v_hi[h][keys]], axis=0)
            c0 = h * grp * HEAD_DIM
            qq = jnp.concatenate([q_ref[rows, c0:c0 + LANES], q_ref[rows, c0 + LANES:c0 + 2 * LANES]], axis=0)
            s = _dot_t(qq, k_st)
            p_parts, inv_parts = [], []
            for g2 in range(2):
                p_row, inv_row = [], []
                for par in range(2):
                    hd = h * grp + 2 * g2 + par
                    sq = s[g2 * WINDOW:(g2 + 1) * WINDOW, par * 2 * WINDOW:(par + 1) * 2 * WINDOW] + bias_sc[tab, hd]
                    sink = sink_ref[0, hd]
                    m = jnp.maximum(jnp.max(sq, axis=-1, keepdims=True), sink)
                    p = jnp.exp(sq - m)
                    den = jnp.sum(p, axis=-1, keepdims=True) + jnp.exp(sink - m)
                    p_row.append(p.astype(BF16))
                    inv_row.append(1.0 / den)
                p_parts.append(jnp.concatenate(p_row, axis=1))
                inv_parts.append(jnp.where(lo_mask, inv_row[0], inv_row[1]))
            pm = jnp.concatenate(p_parts, axis=0)
            o = _dot(pm, v_st)
            for g2 in range(2):
                c = c0 + g2 * LANES
                attn_ref[rows, c:c + LANES] = (o[g2 * WINDOW:(g2 + 1) * WINDOW] * inv_parts[g2]).astype(attn_ref.dtype)

    @pl.when((i + 1) % tiles_per_seq == 0)
    def _():
        kt_ref[0] = kvc_ref[t - WINDOW:t, :LANES].T
        vt_ref[0] = kvc_ref[t - WINDOW:t, LANES:].T


def _prompt_attn(rel_bias, sinks2, q_all, kv_all, bidx, weights, m_rows, seq, n_meta):
    t = ATTN_TILE
    n_steps = m_rows // t
    assert m_rows % WINDOW == 0 and n_meta <= WINDOW
    n_heads = sinks2.shape[1]
    qw, kvw = q_all.shape[1], kv_all.shape[1]
    const = lambda i: (0, 0)
    row = lambda i: (i, 0)
    per_seq = lambda i: (i // (seq // t), 0, 0)
    smem = pl.BlockSpec(memory_space=pltpu.SMEM)

    def w_spec(w):
        share = 1 if (w.shape[0] // n_steps) % 16 == 0 else 2
        assert w.shape[0] % (n_steps // share) == 0 and (w.shape[0] * share // n_steps) % 16 == 0
        return pl.BlockSpec((w.shape[0] * share // n_steps, w.shape[1]), lambda i: (i // share, 0))

    w_specs = [w_spec(w) for w in weights]
    outs = pl.pallas_call(
        functools.partial(_prompt_attn_body, tiles_per_seq=seq // t, n_meta=n_meta, n_heads=n_heads,
                          n_weights=len(weights)),
        grid=(n_steps,),
        in_specs=[
            smem, smem,
            pl.BlockSpec((t, qw), row),
            pl.BlockSpec((t, kvw), row),
            pl.BlockSpec((WINDOW, kvw), lambda i: (jnp.maximum(i * (t // WINDOW) - 1, 0), 0)),
            pl.BlockSpec((WINDOW, kvw), lambda i: (m_rows // WINDOW, 0)),
            pl.BlockSpec((WINDOW, 2 * WINDOW), const),
        ] + w_specs,
        out_specs=[pl.BlockSpec((t, qw), row)] + w_specs + [pl.BlockSpec((1, WINDOW, LANES), per_seq)] * 2,
        out_shape=[jax.ShapeDtypeStruct((m_rows, qw), BF16)] + [jax.ShapeDtypeStruct(w.shape, BF16) for w in weights]
        + [jax.ShapeDtypeStruct((m_rows // seq, WINDOW, LANES), F32)] * 2,
        scratch_shapes=[
            pltpu.VMEM((2, n_heads, WINDOW, 2 * WINDOW), F32),
            pltpu.VMEM((WINDOW, kvw), F32),
        ],
        compiler_params=pltpu.CompilerParams(dimension_semantics=("arbitrary",)),
        name="prompt_attn",
    )(rel_bias, sinks2, q_all, kv_all, kv_all, kv_all, bidx, *weights)
    return outs[0], outs[1:1 + len(weights)], outs[1 + len(weights)], outs[2 + len(weights)]


def _sample_mix_body(rb_ref, sink_ref, q_ref, kvn_ref, glun_ref, ck_ref, cv_ref, st_ref, bidx_ref,
                     wdw_ref, bdw_ref, lng_ref, lnb_ref, after_ref,
                     as_ref, cs_ref, nk_ref, nv_ref, nst_ref, bias_sc, *, n_heads, n_meta):
    i = pl.program_id(0)
    sb, w_buf = ck_ref.shape[0], ck_ref.shape[2]
    n_st = st_ref.shape[0]
    n_rows = sb * n_heads
    per_kv = n_heads // (LANES // HEAD_DIM)

    @pl.when(i == 0)
    def _():
        as_ref[...] = jnp.zeros_like(as_ref)
        cs_ref[...] = jnp.zeros_like(cs_ref)
        tiled = jnp.concatenate(
            [jnp.broadcast_to(jnp.concatenate([_bias_table(bidx_ref[...], rb_ref, hd)] * sb, axis=1),
                              (sb, sb * w_buf)) for hd in range(n_heads)], axis=0)
        row_seq = lax.rem(lax.broadcasted_iota(jnp.int32, tiled.shape, 0), sb)
        col_seq = lax.broadcasted_iota(jnp.int32, tiled.shape, 1) // w_buf
        bias_sc[...] = jnp.where(row_seq == col_seq, tiled, NEG)

    hrow = lax.broadcasted_iota(jnp.int32, (n_rows, 1), 0) // sb
    sink = jnp.zeros((n_rows, 1), F32)
    bias_new = jnp.zeros((n_rows, 1), F32)
    for hd in range(n_heads):
        sink = jnp.where(hrow == hd, sink_ref[0, hd], sink)
        bias_new = jnp.where(hrow == hd, rb_ref[hd, 0], bias_new)

    lo_mask = lax.broadcasted_iota(jnp.int32, (1, LANES), 1) < HEAD_DIM

    def half(x, src_hi, dst_hi):
        if src_hi != dst_hi:
            x = pltpu.roll(x, HEAD_DIM, 1)
        return jnp.where(lo_mask, 0.0, x) if dst_hi else jnp.where(lo_mask, x, 0.0)

    q = jnp.concatenate(
        [half(q_ref[:, (hd // 2) * LANES:(hd // 2 + 1) * LANES].astype(F32), hd % 2 == 1, hd // per_kv == 1)
         for hd in range(n_heads)], axis=0).astype(BF16)
    kn_rows = jnp.concatenate([kvn_ref[:, :LANES]] * n_heads, axis=0).astype(BF16)
    vn_rows = jnp.concatenate([kvn_ref[:, LANES:]] * n_heads, axis=0).astype(BF16)
    ck_all = jnp.concatenate([ck_ref[bb] for bb in range(sb)], axis=1).astype(BF16)
    cv_all = jnp.concatenate([cv_ref[bb] for bb in range(sb)], axis=1).astype(BF16)
    s_c = _dot(q, ck_all) + bias_sc[...]
    s_n = jnp.sum(q.astype(F32) * kn_rows.astype(F32), axis=-1, keepdims=True) + bias_new
    m = jnp.maximum(jnp.maximum(jnp.max(s_c, axis=-1, keepdims=True), s_n), sink)
    p_c = jnp.exp(s_c - m)
    p_n = jnp.exp(s_n - m)
    den = jnp.sum(p_c, axis=-1, keepdims=True) + p_n + jnp.exp(sink - m)
    o = _dot_t(p_c.astype(BF16), cv_all) + p_n.astype(BF16).astype(F32) * vn_rows.astype(F32)
    o = o / den
    pairs = []
    for p in range(n_heads // 2):
        lo_head = o[(2 * p) * sb:(2 * p + 1) * sb]
        hi_head = o[(2 * p + 1) * sb:(2 * p + 2) * sb]
        pairs.append(half(lo_head, (2 * p) // per_kv == 1, False) + half(hi_head, (2 * p + 1) // per_kv == 1, True))
    rows = pl.ds(pl.multiple_of(n_meta + i * sb, sb), sb)
    as_ref[rows, :] = jnp.concatenate(pairs, axis=1).astype(as_ref.dtype)

    g_new = glun_ref[...]
    y = g_new * wdw_ref[n_st:n_st + 1, :] + bdw_ref[...]
    for k in range(n_st):
        y = y + st_ref[k] * wdw_ref[k:k + 1, :]
    cs_ref[rows, :] = _ln_swish(y, lng_ref[...], lnb_ref[...]).astype(cs_ref.dtype)

    fill = jnp.zeros((w_buf - sb, LANES), F32)
    kn_t = jnp.concatenate([kvn_ref[:, :LANES], fill], axis=0).T
    vn_t = jnp.concatenate([kvn_ref[:, LANES:], fill], axis=0).T
    newest = lax.broadcasted_iota(jnp.int32, (LANES, w_buf), 1) == w_buf - 1
    for bb in range(sb):
        nk_ref[bb] = jnp.where(newest, kn_t[:, bb:bb + 1], pltpu.roll(ck_ref[bb], w_buf - 1, 1))
        nv_ref[bb] = jnp.where(newest, vn_t[:, bb:bb + 1], pltpu.roll(cv_ref[bb], w_buf - 1, 1))
    nst_ref[0:n_st - 1] = st_ref[1:n_st]
    nst_ref[n_st - 1] = g_new


def _sample_mix(rel_bias, sinks2, q_all, kv_all, glu_all, cache_kt, cache_vt, state_t, bidx_s, w_dw, b_dw, ln_g, ln_b,
                s0, n_meta, t_rows, after):
    db, kw, w_buf = cache_kt.shape
    n_st, ch = state_t.shape[0], state_t.shape[2]
    n_heads = sinks2.shape[1]
    aw = q_all.shape[1]
    sb = SAMPLE_BLOCK
    assert s0 % sb == 0 and n_meta % sb == 0 and n_meta + db <= t_rows
    blk = lambda i: (i, 0, 0)
    const = lambda i: (0, 0)
    tok = lambda i: (s0 // sb + i, 0)
    smem = pl.BlockSpec(memory_space=pltpu.SMEM)
    return pl.pallas_call(
        functools.partial(_sample_mix_body, n_heads=n_heads, n_meta=n_meta),
        grid=(db // sb,),
        in_specs=[
            smem, smem,
            pl.BlockSpec((sb, aw), tok),
            pl.BlockSpec((sb, 2 * kw), tok),
            pl.BlockSpec((sb, ch), tok),
            pl.BlockSpec((sb, kw, w_buf), blk),
            pl.BlockSpec((sb, kw, w_buf), blk),
            pl.BlockSpec((n_st, sb, ch), lambda i: (0, i, 0)),
            pl.BlockSpec((1, w_buf), const),
            pl.BlockSpec(w_dw.shape, const),
            pl.BlockSpec((1, ch), const),
            pl.BlockSpec((1, ch), const),
            pl.BlockSpec((1, ch), const),
            pl.BlockSpec(memory_space=pl.ANY),
        ],
        out_specs=[
            pl.BlockSpec((t_rows, aw), const),
            pl.BlockSpec((t_rows, ch), const),
            pl.BlockSpec((sb, kw, w_buf), blk),
            pl.BlockSpec((sb, kw, w_buf), blk),
            pl.BlockSpec((n_st, sb, ch), lambda i: (0, i, 0)),
        ],
        out_shape=[
            jax.ShapeDtypeStruct((t_rows, aw), BF16),
            jax.ShapeDtypeStruct((t_rows, ch), BF16),
            jax.ShapeDtypeStruct((db, kw, w_buf), F32),
            jax.ShapeDtypeStruct((db, kw, w_buf), F32),
            jax.ShapeDtypeStruct((n_st, db, ch), F32),
        ],
        scratch_shapes=[pltpu.VMEM((sb * n_heads, sb * w_buf), F32)],
        compiler_params=pltpu.CompilerParams(dimension_semantics=("arbitrary",)),
        name="sample_mix",
    )(rel_bias, sinks2, q_all, kv_all, glu_all, cache_kt, cache_vt, state_t, bidx_s, w_dw, b_dw, ln_g, ln_b, after)


def _conv_runs(cb_st, glu_ref, glum_ref, y_sc, wdw_ref, bdw_ref, first, n_runs):
    t = y_sc.shape[0]
    conv_w = wdw_ref.shape[0]
    n_ch = y_sc.shape[1]
    off = HALO - (conv_w - 1)
    n_a = (off + conv_w - 1) // 8 + 1
    units = [(lc, g) for lc in range(n_ch // LANES) for g in range(t // 8)]
    sizes = [len(units) // n_runs + (1 if r < len(units) % n_runs else 0) for r in range(n_runs)]
    staged = []

    def run(mine):
        if not staged:
            cb_st[0:HALO, :] = jnp.where(first, glum_ref[...], cb_st[0:HALO, :])
            cb_st[HALO:HALO + t, :] = glu_ref[...]
            staged.append(True)
        zs = {}

        def z(lc, s, g):
            if (lc, s, g) not in zs:
                ls = slice(lc * LANES, (lc + 1) * LANES)
                acc = None
                for a in range(n_a):
                    w = 8 * a + s - off
                    if 0 <= w < conv_w:
                        term = cb_st[8 * (g + a):8 * (g + a) + 8, ls] * wdw_ref[w:w + 1, ls]
                        acc = term if acc is None else acc + term
                zs[(lc, s, g)] = acc
            return zs[(lc, s, g)]

        dep = None
        for lc, g in mine:
            ls = slice(lc * LANES, (lc + 1) * LANES)
            acc = jnp.broadcast_to(bdw_ref[:, ls], (8, LANES)) + z(lc, 0, g)
            if dep is not None:
                acc = acc + dep
            for s in range(1, 8):
                acc = acc + jnp.concatenate([z(lc, s, g), z(lc, s, g + 1)], axis=0)[s:s + 8, :]
            y_sc[8 * g:8 * g + 8, ls] = acc
            dep = _dep_zero(acc)
        return dep[0:1, :]

    out, k = [], 0
    for n in sizes:
        out.append(functools.partial(run, units[k:k + n]))
        k += n
    return out


def _post_ffn_body(h_ref, am_ref, as_ref, cs_ref, ga_ref, gc_ref, g2_ref, glu_ref, glus_ref,
                   wdw_ref, bdw_ref, lng_ref, lnb_ref,
                   waob, wcob, woutb, w1b, w3b, w2b,
                   ym_ref, ys_ref, gt_ref,
                   hid_ref, cb_st, y_sc, conv_st, glum_ref, *, n_main, tiles_per_seq, n_meta):
    i = pl.program_id(0)
    t_rows = y_sc.shape[0]

    @pl.when(i == 0)
    def _():
        cb_st[...] = jnp.zeros_like(cb_st)
        glum_ref[...] = jnp.zeros_like(glum_ref)
        glum_ref[HALO - n_meta:HALO, :] = glus_ref[0:n_meta, :]

    t = i - 1
    tn = t + 1
    first = (tn < n_main) & (lax.rem(tn, tiles_per_seq) == 0)
    n_chunks = w1b.shape[1] // FF_CHUNK

    def finish_conv():
        conv_st[...] = _ln_swish(y_sc[...], lng_ref[...], lnb_ref[...]).astype(conv_st.dtype)
        cb_st[0:HALO, :] = cb_st[t_rows:t_rows + HALO, :]

    @pl.when(t == -1)
    def _():
        for f in _conv_runs(cb_st, glu_ref, glum_ref, y_sc, wdw_ref, bdw_ref, first, n_chunks):
            f()
        finish_conv()

    @pl.when(t >= 0)
    def _():
        on_main = t < n_main
        conv_f = _conv_runs(cb_st, glu_ref, glum_ref, y_sc, wdw_ref, bdw_ref, first, n_chunks)
        fillers = [(c, c + 1, f) for c, f in enumerate(conv_f)]

        at = jnp.where(on_main, am_ref[...], as_ref[...])
        cv = jnp.where(on_main, conv_st[...], cs_ref[...])
        a = _dot(at, waob[...])
        c = _dot(cv, wcob[...])
        mix = (ga_ref[...] * a + gc_ref[...] * c).astype(BF16)
        h2 = h_ref[...] + _dot(mix, woutb[...])
        xn = _rms(h2, g2_ref[...]).astype(BF16)
        y = h2 + 0.5 * _swiglu(xn, w1b, w3b, w2b, hid_ref, fillers)
        finish_conv()

        ym_ref[...] = jnp.where(on_main, y, ym_ref[...])

        @pl.when(t == n_main)
        def _():
            n_c = y.shape[1] // LANES
            n_s = ys_ref.shape[0] // n_c
            for c in range(n_c):
                ys_ref[pl.ds(c, n_s, stride=n_c), :] = y[n_meta:n_meta + n_s, c * LANES:(c + 1) * LANES]

    n_st = gt_ref.shape[0]
    for sq in range(gt_ref.shape[1]):
        @pl.when(tn == (sq + 1) * tiles_per_seq - 1)
        def _():
            for r in range(n_st):
                gt_ref[r, sq:sq + 1, :] = glu_ref[t_rows - n_st + r:t_rows - n_st + r + 1, :]


def _post_ffn(h_all, attn_main, attn_small, conv_small, ga_all, gc_all, g2, glu_all,
              w_dw, b_dw, ln_g, ln_b, w_ao, w_co, w_out, w1, w3, w2, seq, n_meta, n_sample):
    m_rows, aw = attn_main.shape
    assert m_rows % HALO == 0 and n_meta <= HALO and n_meta % 8 == 0 and n_sample % 8 == 0
    ch = glu_all.shape[1]
    d = h_all.shape[1]
    n_ff = w1.shape[1]
    t = ROW_TILE
    n_main = m_rows // t
    n_b = m_rows // seq
    n_st = w_dw.shape[0] - 1

    def main_idx(i):
        return (jnp.clip(i - 1, 0, n_main - 1), 0)

    def next_idx(i):
        return (jnp.minimum(i, n_main - 1), 0)

    def row_idx(i):
        return (jnp.maximum(i - 1, 0), 0)

    const = lambda i: (0, 0)
    resident = lambda w: pl.BlockSpec(w.shape, const, pipeline_mode=pl.Buffered(1))
    return pl.pallas_call(
        functools.partial(_post_ffn_body, n_main=n_main, tiles_per_seq=seq // t, n_meta=n_meta),
        grid=(n_main + 2,),
        in_specs=[
            pl.BlockSpec((t, d), row_idx),
            pl.BlockSpec((t, aw), main_idx),
            pl.BlockSpec((t, aw), const),
            pl.BlockSpec((t, ch), const),
            pl.BlockSpec((t, d), row_idx),
            pl.BlockSpec((t, d), row_idx),
            pl.BlockSpec((1, d), const),
            pl.BlockSpec((t, ch), next_idx),
            pl.BlockSpec((HALO, ch), lambda i: (m_rows // HALO, 0)),
            pl.BlockSpec(w_dw.shape, const),
            pl.BlockSpec((1, ch), const),
            pl.BlockSpec((1, ch), const),
            pl.BlockSpec((1, ch), const),
            resident(w_ao), resident(w_co), resident(w_out), resident(w1), resident(w3), resident(w2),
        ],
        out_specs=[pl.BlockSpec((t, d), main_idx), pl.BlockSpec((n_sample * (d // LANES), LANES), const),
                   pl.BlockSpec((n_st, n_b, ch), lambda i: (0, 0, 0))],
        out_shape=[jax.ShapeDtypeStruct((m_rows, d), F32),
                   jax.ShapeDtypeStruct((n_sample * (d // LANES), LANES), F32),
                   jax.ShapeDtypeStruct((n_st, n_b, ch), F32)],
        scratch_shapes=[
            pltpu.VMEM((t, n_ff), BF16),
            pltpu.VMEM((HALO + t, ch), F32),
            pltpu.VMEM((t, ch), F32),
            pltpu.VMEM((t, ch), BF16),
            pltpu.VMEM((HALO, ch), F32),
        ],
        compiler_params=pltpu.CompilerParams(
            dimension_semantics=("arbitrary",), vmem_limit_bytes=VMEM_LIMIT),
        name="post_ffn2",
    )(h_all, attn_main, attn_small, conv_small, ga_all, gc_all, g2, glu_all, glu_all,
      w_dw, b_dw, ln_g, ln_b, w_ao, w_co, w_out, w1, w3, w2)


def _t5_bucket(dist, n_buckets):
    max_exact = n_buckets // 2
    d = np.maximum(dist, 0)
    ratio = (np.log(np.maximum(d, 1).astype(np.float32) / np.float32(max_exact))
             / np.float32(math.log(REL_MAX_DIST / max_exact)))
    large = np.minimum(max_exact + (ratio * np.float32(n_buckets - max_exact)).astype(np.int32), n_buckets - 1)
    return np.where(d < max_exact, d, large).astype(np.int32)


def _bucket_or_masked(dist, n_buckets):
    ok = (dist >= 0) & (dist < WINDOW)
    return np.where(ok, _t5_bucket(dist, n_buckets), -1).astype(np.int32)


def _rows(x, start, n):
    return lax.slice_in_dim(x, start, start + n, axis=0)


def kernel(x_prompt, x_sample, cache_k, cache_v, state_conv, meta_tokens, ffn1_norm, ffn1_w1, ffn1_w3, ffn1_w2, mix_norm, w_in, q_norm, k_norm, rel_bias, sinks, w_attn_out, w_dw, b_dw, conv_ln_g, conv_ln_b, w_conv_out, w_out, ffn2_norm, ffn2_w1, ffn2_w3, ffn2_w2):
    n_b, seq, d = x_prompt.shape
    db = x_sample.shape[0]
    n_meta = meta_tokens.shape[0]
    n_heads = sinks.shape[0]
    w_buf, n_kv, hd = cache_k.shape[1], cache_k.shape[2], cache_k.shape[3]
    ch = w_dw.shape[1]
    n_st = state_conv.shape[1]
    n_buckets = rel_bias.shape[0]
    aw, kvw = n_heads * hd, n_kv * hd
    t = ROW_TILE
    m_rows = n_b * seq
    assert hd == HEAD_DIM and kvw == LANES and n_kv == 2 and n_heads == 8 and w_buf == WINDOW
    assert x_sample.shape[1] == 1 and seq % t == 0 and seq % ATTN_TILE == 0
    assert n_meta + db <= t and db % SAMPLE_BLOCK == 0
    assert n_meta <= HALO and n_meta <= WINDOW and n_st == w_dw.shape[0] - 1 and n_st <= HALO
    splits = tuple(int(v) for v in np.cumsum([0, aw, kvw, kvw, ch, ch, d, d]))
    assert splits[-1] == w_in.shape[1]

    row = lambda v: v.reshape(1, -1)
    x_main = x_prompt.reshape(m_rows, d)

    h_all, q_all, kv_all, glu_all, ga_all, gc_all = _ffn_inproj(
        x_main, meta_tokens, x_sample.reshape(db * (d // LANES), LANES), row(ffn1_norm), row(mix_norm),
        row(q_norm), row(k_norm), ffn1_w1, ffn1_w3, ffn1_w2, w_in, splits)

    sinks2 = row(sinks)
    rel_bias_t = rel_bias.T
    b_dw2, ln_g2, ln_b2 = row(b_dw), row(conv_ln_g), row(conv_ln_b)

    dist = np.arange(WINDOW)[:, None] + WINDOW - np.arange(2 * WINDOW)[None, :]
    bidx = jnp.asarray(_bucket_or_masked(dist, n_buckets))
    attn_main, w_bf16, k_t, v_t = _prompt_attn(
        rel_bias_t, sinks2, q_all, kv_all, bidx,
        (w_attn_out, w_conv_out, w_out, ffn2_w1, ffn2_w3, ffn2_w2), m_rows, seq, n_meta)

    bidx_s = jnp.asarray(_bucket_or_masked(w_buf - np.arange(w_buf)[None, :], n_buckets))
    attn_small, conv_small, new_k_s, new_v_s, new_conv_s = _sample_mix(
        rel_bias_t, sinks2, q_all, kv_all, glu_all,
        jnp.transpose(cache_k, (0, 2, 3, 1)).reshape(db, kvw, w_buf),
        jnp.transpose(cache_v, (0, 2, 3, 1)).reshape(db, kvw, w_buf),
        jnp.transpose(state_conv, (1, 0, 2)), bidx_s, w_dw, b_dw2, ln_g2, ln_b2, m_rows + n_meta, n_meta, t,
        after=k_t)

    y_main, y_small, glu_t = _post_ffn(h_all, attn_main, attn_small, conv_small, ga_all, gc_all, row(ffn2_norm),
                                       glu_all, w_dw, b_dw2, ln_g2, ln_b2, *w_bf16, seq, n_meta, db)

    return (
        y_main.reshape(n_b, seq, d),
        y_small.reshape(db, 1, d),
        jnp.transpose(k_t.reshape(n_b, n_kv, hd, WINDOW), (0, 3, 1, 2)),
        jnp.transpose(v_t.reshape(n_b, n_kv, hd, WINDOW), (0, 3, 1, 2)),
        jnp.transpose(glu_t, (1, 0, 2)),
        jnp.transpose(new_k_s.reshape(db, n_kv, hd, w_buf), (0, 3, 1, 2)),
        jnp.transpose(new_v_s.reshape(db, n_kv, hd, w_buf), (0, 3, 1, 2)),
        jnp.transpose(new_conv_s, (1, 0, 2)),
    )
```
